```python
import math
import jax, jax.numpy as jnp
from jax import lax
import numpy as np

D_MODEL = 2048
BATCH = 4
SEQ = 2048
DEPTH = 4
DEC_BATCH = 8
DEC_SEQ = 4
PAST_LEN = 16384
PAGE_SIZE = 128

N_MIXERS = 3
N_SSD_LAYERS = (DEPTH + 2) // N_MIXERS
N_NSA_LAYERS = (DEPTH + 1) // N_MIXERS
N_POOL_LAYERS = DEPTH // N_MIXERS

SSD_D_INNER = 2 * D_MODEL
SSD_HEAD_DIM = 64
SSD_HEADS = SSD_D_INNER // SSD_HEAD_DIM
SSD_GROUPS = 8
SSD_STATE = 128
SSD_CONV = 4
SSD_CHUNK = 128
SSD_CONV_DIM = SSD_D_INNER + 2 * SSD_GROUPS * SSD_STATE
SSD_IN_DIM = SSD_D_INNER + SSD_CONV_DIM + SSD_HEADS

NSA_HEADS = 16
NSA_KV_HEADS = 4
NSA_HEAD_DIM = D_MODEL // NSA_HEADS
NSA_REP = NSA_HEADS // NSA_KV_HEADS
NSA_Q_DIM = NSA_HEADS * NSA_HEAD_DIM
NSA_KV_DIM = NSA_KV_HEADS * NSA_HEAD_DIM
NSA_IN_DIM = NSA_Q_DIM + 6 * NSA_KV_DIM + 3 * NSA_HEADS
CMP_LEN = 32
CMP_STRIDE = 16
CMP_HIDDEN = 256
SEL_LEN = 64
SEL_TOPK = 16
WINDOW = 512
WIN_Q_BLOCK = 128
SEL_Q_BLOCK = 64
ROPE_THETA = 10000.0
FORCE_BONUS = 1000.0

POOL_WINDOWS = (2, 4, 8, 16)
POOL_GROUPS = len(POOL_WINDOWS)
POOL_GROUP_DIM = D_MODEL // POOL_GROUPS
POOL_BUF = max(POOL_WINDOWS) - 1

D_FF = 5632
FFN_CONV = 3

EPS = 1e-6
NEG_INF = -1e30
TINY = 1e-30

kernel_name = 'hybrid_ssd_nsa_pool_decoder_step'


def rmsnorm(x, g):
    xf = x.astype(jnp.float32)
    y = xf * lax.rsqrt(jnp.mean(xf * xf, axis=-1, keepdims=True) + EPS)
    return (y * g.astype(jnp.float32)).astype(x.dtype)


def modulate(h, shift, scale):
    return h * (1 + scale[:, None, :]) + shift[:, None, :]


def rope(x, pos):
    hd = x.shape[-1]
    half = hd // 2
    inv = jnp.exp(-math.log(ROPE_THETA) * jnp.arange(half, dtype=jnp.float32) * 2.0 / hd)
    ang = pos.astype(jnp.float32)[:, None] * inv[None, :]
    cos = jnp.cos(ang)[:, None, :].astype(x.dtype)
    sin = jnp.sin(ang)[:, None, :].astype(x.dtype)
    x1, x2 = x[..., :half], x[..., half:]
    return jnp.concatenate([x1 * cos - x2 * sin, x2 * cos + x1 * sin], axis=-1)


def masked_softmax(s, mask):
    s = jnp.where(mask, s.astype(jnp.float32), NEG_INF)
    m = jnp.max(s, axis=-1, keepdims=True)
    p = jnp.where(mask, jnp.exp(s - m), 0.0)
    return p / jnp.maximum(jnp.sum(p, axis=-1, keepdims=True), TINY)


def causal_dwconv(x, buf, w, bias):
    xe = jnp.concatenate([buf.astype(x.dtype), x], axis=1)
    y = lax.conv_general_dilated(xe, w[:, None, :].astype(x.dtype), window_strides=(1,), padding='VALID',
                                 dimension_numbers=('NWC', 'WIO', 'NWC'), feature_group_count=x.shape[-1])
    return y + bias, xe[:, -(w.shape[0] - 1):]


def ssd_scan(x, dt, a_neg, bm, cm, s0):
    b, t = x.shape[:2]
    cl = min(SSD_CHUNK, t)
    pad = (-t) % cl
    if pad:
        padf = lambda a: jnp.pad(a, [(0, 0), (0, pad)] + [(0, 0)] * (a.ndim - 2))
        x, dt, bm, cm = padf(x), padf(dt), padf(bm), padf(cm)
    nc = (t + pad) // cl
    chunk = lambda a: a.reshape((b, nc, cl) + a.shape[2:])
    xdt = chunk(x * dt[..., None])
    a_cs = jnp.cumsum(chunk(dt * a_neg), axis=2)
    bc, cc = chunk(bm), chunk(cm)
    causal = jnp.tril(jnp.ones((cl, cl), bool))[:, :, None, None]
    seg = a_cs[:, :, :, None] - a_cs[:, :, None, :]
    lmat = jnp.exp(jnp.where(causal, seg, NEG_INF))
    cb = jnp.einsum('bclgn,bcsgn->bclsg', cc, bc)
    y_diag = jnp.einsum('bclsgr,bcsgrp->bclgrp', cb[..., None] * lmat, xdt)
    decay_to_end = jnp.exp(a_cs[:, :, -1:] - a_cs)
    chunk_states = jnp.einsum('bclgn,bclgr,bclgrp->bcgrpn', bc, decay_to_end, xdt)
    chunk_decay = jnp.exp(a_cs[:, :, -1])

    def step(s, inp):
        st, dc = inp
        return s * dc[..., None, None] + st, s

    s_final, s_in = lax.scan(step, s0, (jnp.moveaxis(chunk_states, 1, 0), jnp.moveaxis(chunk_decay, 1, 0)))
    s_in = jnp.moveaxis(s_in, 0, 1)
    y_off = jnp.einsum('bclgn,bcgrpn,bclgr->bclgrp', cc, s_in, jnp.exp(a_cs))
    y = (y_diag + y_off).reshape((b, nc * cl) + x.shape[2:])[:, :t]
    return y, s_final


def ssd_mixer(h, P, j, conv_buf, s0):
    b, t, _ = h.shape
    rep = SSD_HEADS // SSD_GROUPS
    gn = SSD_GROUPS * SSD_STATE
    zxbcdt = h @ P['ssd_w_in'][j]
    z = zxbcdt[..., :SSD_D_INNER]
    xbc = zxbcdt[..., SSD_D_INNER:SSD_D_INNER + SSD_CONV_DIM]
    dt_raw = zxbcdt[..., SSD_D_INNER + SSD_CONV_DIM:]
    xbc, conv_new = causal_dwconv(xbc, conv_buf, P['ssd_conv_w'][j], P['ssd_conv_b'][j])
    xbc = jax.nn.silu(xbc).astype(jnp.float32)
    xs = xbc[..., :SSD_D_INNER].reshape(b, t, SSD_GROUPS, rep, SSD_HEAD_DIM)
    bm = xbc[..., SSD_D_INNER:SSD_D_INNER + gn].reshape(b, t, SSD_GROUPS, SSD_STATE)
    cm = xbc[..., SSD_D_INNER + gn:].reshape(b, t, SSD_GROUPS, SSD_STATE)
    dt = jax.nn.softplus(dt_raw.astype(jnp.float32) + P['ssd_dt_bias'][j].astype(jnp.float32)).reshape(b, t, SSD_GROUPS, rep)
    a_neg = -jnp.exp(P['ssd_a_log'][j].astype(jnp.float32)).reshape(SSD_GROUPS, rep)
    s0 = s0.astype(jnp.float32).reshape(b, SSD_GROUPS, rep, SSD_HEAD_DIM, SSD_STATE)
    y, s_new = ssd_scan(xs, dt, a_neg, bm, cm, s0)
    y = y + P['ssd_d'][j].astype(jnp.float32).reshape(SSD_GROUPS, rep, 1) * xs
    y = y.reshape(b, t, SSD_D_INNER).astype(h.dtype)
    y = rmsnorm(y * jax.nn.silu(z), P['ssd_norm_g'][j])
    s_new = s_new.reshape(b, SSD_HEADS, SSD_HEAD_DIM, SSD_STATE).astype(h.dtype)
    return y @ P['ssd_w_out'][j], conv_new, s_new


def compress_blocks(rows, w1, w2, pe):
    b, tk = rows.shape[:2]
    ratio = CMP_LEN // CMP_STRIDE
    n_cmp = (tk - CMP_LEN) // CMP_STRIDE + 1
    n_sub = n_cmp + ratio - 1
    sub = rows[:, :n_sub * CMP_STRIDE].reshape(b, n_sub, CMP_STRIDE, NSA_KV_HEADS, NSA_HEAD_DIM)
    w1r = w1.reshape(ratio, CMP_STRIDE, NSA_HEAD_DIM, CMP_HIDDEN)
    hidden = jnp.einsum('ld,ldh->h', pe, w1)
    for r in range(ratio):
        hidden = hidden + jnp.einsum('bnsgd,sdh->bngh', sub[:, r:r + n_cmp], w1r[r])
    return jnp.einsum('bngh,hd->bngd', jax.nn.gelu(hidden), w2)


def block_coverage(n_cmp, n_sel):
    start = np.arange(n_cmp)[:, None] * CMP_STRIDE
    sel_start = np.arange(n_sel)[None, :] * SEL_LEN
    inter = np.minimum(start + CMP_LEN, sel_start + SEL_LEN) - np.maximum(start, sel_start)
    return (np.clip(inter, 0, None) / CMP_LEN).astype(np.float32)


def window_attention(q, k, v, q_pos, k_pos):
    diff = q_pos[:, None] - k_pos[None, :]
    mask = (diff >= 0) & (diff < WINDOW) & (k_pos[None, :] >= 0)
    p = masked_softmax(jnp.einsum('btgrd,bsgd->bgrts', q, k), mask)
    return jnp.einsum('bgrts,bsgd->btgrd', p.astype(v.dtype), v)


def banded_window_attention(q, k_pad, v_pad, q_pos, k_pos):
    b, t = q.shape[:2]
    qb = WIN_Q_BLOCK if t % WIN_Q_BLOCK == 0 else t
    nb = t // qb
    span = qb + WINDOW
    q_blocks = jnp.moveaxis(q.reshape((b, nb, qb) + q.shape[2:]), 1, 0)

    def one(args):
        qblk, qp, i = args
        start = i * qb
        kb = lax.dynamic_slice_in_dim(k_pad, start, span, axis=1)
        vb = lax.dynamic_slice_in_dim(v_pad, start, span, axis=1)
        kp = lax.dynamic_slice_in_dim(k_pos, start, span)
        return window_attention(qblk, kb, vb, qp, kp)

    out = lax.map(one, (q_blocks, q_pos.reshape(nb, qb), jnp.arange(nb)))
    return jnp.moveaxis(out, 0, 1).reshape(q.shape)


def selected_attention(q, k_blocks, v_blocks, sel_idx, q_pos):
    b, t = q.shape[:2]
    qb = SEL_Q_BLOCK if t % SEL_Q_BLOCK == 0 else t
    nb = t // qb
    n_top = sel_idx.shape[-1]
    gather = jax.vmap(jax.vmap(lambda blocks, ix: blocks[ix]))
    q_blocks = jnp.moveaxis(q.reshape((b, nb, qb) + q.shape[2:]), 1, 0)
    idx_blocks = jnp.moveaxis(sel_idx.reshape(b, NSA_KV_HEADS, nb, qb, n_top), 2, 0)

    def one(args):
        qblk, ix, qp = args
        kg = gather(k_blocks, ix)
        vg = gather(v_blocks, ix)
        k_pos = ix[..., None] * SEL_LEN + jnp.arange(SEL_LEN)
        mask = (k_pos <= qp[:, None, None])[:, :, None]
        s = jnp.einsum('btgrd,bgtksd->bgrtks', qblk, kg)
        shp = s.shape
        flat = shp[:4] + (-1,)
        p = masked_softmax(s.reshape(flat), jnp.broadcast_to(mask, shp).reshape(flat)).reshape(shp)
        return jnp.einsum('bgrtks,bgtksd->btgrd', p.astype(vg.dtype), vg)

    out = lax.map(one, (q_blocks, idx_blocks, q_pos.reshape(nb, qb)))
    return jnp.moveaxis(out, 0, 1).reshape(q.shape)


def nsa_mixer(h, pos0, P, j, past):
    b, t, _ = h.shape
    proj = h @ P['nsa_w_in'][j]
    q = proj[..., :NSA_Q_DIM].reshape(b, t, NSA_HEADS, NSA_HEAD_DIM)
    kv = proj[..., NSA_Q_DIM:NSA_Q_DIM + 6 * NSA_KV_DIM].reshape(b, t, 6, NSA_KV_HEADS, NSA_HEAD_DIM)
    gates = jax.nn.sigmoid(proj[..., NSA_Q_DIM + 6 * NSA_KV_DIM:].astype(jnp.float32))
    gates = gates.reshape(b, t, 3, NSA_KV_HEADS, NSA_REP, 1).astype(h.dtype)
    q_pos = pos0 + jnp.arange(t)
    q = rope(q, q_pos).reshape(b, t, NSA_KV_HEADS, NSA_REP, NSA_HEAD_DIM) * (NSA_HEAD_DIM ** -0.5)
    kc, vc = rope(kv[:, :, 0], q_pos), kv[:, :, 1]
    ks, vs = rope(kv[:, :, 2], q_pos), kv[:, :, 3]
    kw, vw = rope(kv[:, :, 4], q_pos), kv[:, :, 5]
    if past is None:
        kc_all, vc_all, ks_all, vs_all = kc, vc, ks, vs
    else:
        kc_all = jnp.concatenate([past[0].astype(h.dtype), kc], axis=1)
        vc_all = jnp.concatenate([past[1].astype(h.dtype), vc], axis=1)
        ks_all = jnp.concatenate([past[2].astype(h.dtype), ks], axis=1)
        vs_all = jnp.concatenate([past[3].astype(h.dtype), vs], axis=1)
    tk = ks_all.shape[1]

    kcmp = compress_blocks(kc_all, P['nsa_cmpk_w1'][j], P['nsa_cmpk_w2'][j], P['nsa_cmpk_pe'][j])
    vcmp = compress_blocks(vc_all, P['nsa_cmpv_w1'][j], P['nsa_cmpv_w2'][j], P['nsa_cmpv_pe'][j])
    n_cmp = kcmp.shape[1]
    cmp_end = jnp.arange(n_cmp) * CMP_STRIDE + CMP_LEN - 1
    p_cmp = masked_softmax(jnp.einsum('btgrd,bngd->bgrtn', q, kcmp), cmp_end[None, :] <= q_pos[:, None])
    o_cmp = jnp.einsum('bgrtn,bngd->btgrd', p_cmp.astype(h.dtype), vcmp)

    n_sel = -(-tk // SEL_LEN)
    imp = jnp.einsum('bgrtn,nj->bgtj', p_cmp, jnp.asarray(block_coverage(n_cmp, n_sel)))
    blk = jnp.arange(n_sel)[None, :]
    cur = (q_pos // SEL_LEN)[:, None]
    forced = (blk == 0) | (blk == cur) | (blk == cur - 1)
    score = jnp.where(blk <= cur, imp + FORCE_BONUS * forced, -1.0)
    _, sel_idx = lax.top_k(score, min(SEL_TOPK, n_sel))
    pad = n_sel * SEL_LEN - tk

    def to_blocks(a):
        a = jnp.pad(a, ((0, 0), (0, pad), (0, 0), (0, 0)))
        return a.reshape(b, n_sel, SEL_LEN, NSA_KV_HEADS, NSA_HEAD_DIM).transpose(0, 3, 1, 2, 4)

    o_slc = selected_attention(q, to_blocks(ks_all), to_blocks(vs_all), sel_idx, q_pos)

    if past is None:
        zeros = jnp.zeros((b, WINDOW, NSA_KV_HEADS, NSA_HEAD_DIM), h.dtype)
        k_pos = jnp.arange(t + WINDOW) - WINDOW
        o_win = banded_window_attention(q, jnp.concatenate([zeros, kw], axis=1),
                                        jnp.concatenate([zeros, vw], axis=1), q_pos, k_pos)
        keep = min(WINDOW, t)
        kw_new, vw_new = kw[:, t - keep:], vw[:, t - keep:]
    else:
        nbuf = past[4].shape[1]
        kw_all = jnp.concatenate([past[4].astype(h.dtype), kw], axis=1)
        vw_all = jnp.concatenate([past[5].astype(h.dtype), vw], axis=1)
        k_pos = pos0 - nbuf + jnp.arange(nbuf + t)
        o_win = window_attention(q, kw_all, vw_all, q_pos, k_pos)
        kw_new, vw_new = kw_all[:, t:], vw_all[:, t:]

    o = gates[:, :, 0] * o_cmp + gates[:, :, 1] * o_slc + gates[:, :, 2] * o_win
    out = o.reshape(b, t, NSA_Q_DIM) @ P['nsa_w_out'][j]
    return out, kc, vc, ks, vs, kw_new, vw_new


def pool_mixer(h, pos0, P, j, buf):
    b, t, _ = h.shape
    q_pos = pos0 + jnp.arange(t)
    he = jnp.concatenate([buf.astype(h.dtype), h], axis=1)
    hf = he.astype(jnp.float32)
    cs = jnp.concatenate([jnp.zeros((b, 1, D_MODEL), jnp.float32), jnp.cumsum(hf, axis=1)], axis=1)
    end = POOL_BUF + 1
    means = []
    for gi, w in enumerate(POOL_WINDOWS):
        lo, hi = gi * POOL_GROUP_DIM, (gi + 1) * POOL_GROUP_DIM
        win_sum = cs[:, end:end + t, lo:hi] - cs[:, end - w:end - w + t, lo:hi]
        count = jnp.minimum(w, q_pos + 1).astype(jnp.float32)[None, :, None]
        means.append(win_sum / count)
    mixed = (jnp.concatenate(means, axis=-1) - hf[:, POOL_BUF:]).astype(h.dtype)
    mixed = jnp.einsum('btgc,gcd->btgd', mixed.reshape(b, t, POOL_GROUPS, POOL_GROUP_DIM), P['pool_w'][j])
    return mixed.reshape(b, t, D_MODEL) * P['pool_scale'][j], he[:, -POOL_BUF:]


def conv_ffn(h, P, i, buf):
    u = h @ P['ffn_w_up'][i]
    u, buf_new = causal_dwconv(u, buf, P['ffn_conv_w'][i], P['ffn_conv_b'][i])
    a, g = jnp.split(u, 2, axis=-1)
    return (jax.nn.silu(g) * a) @ P['ffn_w_down'][i], buf_new


def run_trunk(x, c, pos0, past, P):
    b = x.shape[0]
    names = ('ssd_state', 'ssd_conv', 'cmp_k', 'cmp_v', 'slc_k', 'slc_v', 'win_k', 'win_v', 'pool', 'ffn')
    new = {n: [] for n in names}
    c_act = jax.nn.silu(c)
    for i in range(DEPTH):
        kind, j = i % N_MIXERS, i // N_MIXERS
        mod = c_act @ P['ada_w'][i] + P['ada_b'][i]
        sh1, sc1, g1, sh2, sc2, g2 = jnp.split(mod, 6, axis=-1)
        h = modulate(rmsnorm(x, P['norm1_g'][i]), sh1, sc1)
        if kind == 0:
            if past is None:
                conv_buf = jnp.zeros((b, SSD_CONV - 1, SSD_CONV_DIM), x.dtype)
                s0 = jnp.zeros((b, SSD_HEADS, SSD_HEAD_DIM, SSD_STATE), jnp.float32)
            else:
                conv_buf, s0 = past['ssd_conv'][j], past['ssd_state'][j]
            m, conv_new, s_new = ssd_mixer(h, P, j, conv_buf, s0)
            new['ssd_conv'].append(conv_new)
            new['ssd_state'].append(s_new)
        elif kind == 1:
            m, kc, vc, ks, vs, kw, vw = nsa_mixer(h, pos0, P, j, None if past is None else past['nsa'][j])
            new['cmp_k'].append(kc)
            new['cmp_v'].append(vc)
            new['slc_k'].append(ks)
            new['slc_v'].append(vs)
            new['win_k'].append(kw)
            new['win_v'].append(vw)
        else:
            buf = jnp.zeros((b, POOL_BUF, D_MODEL), x.dtype) if past is None else past['pool'][j]
            m, buf_new = pool_mixer(h, pos0, P, j, buf)
            new['pool'].append(buf_new)
        x = x + g1[:, None, :] * m
        h2 = modulate(rmsnorm(x, P['norm2_g'][i]), sh2, sc2)
        fbuf = jnp.zeros((b, FFN_CONV - 1, 2 * D_FF), x.dtype) if past is None else past['ffn'][i]
        f, fbuf_new = conv_ffn(h2, P, i, fbuf)
        new['ffn'].append(fbuf_new)
        x = x + g2[:, None, :] * f
    y = rmsnorm(x, P['final_g'])
    return y, {n: jnp.stack(v) for n, v in new.items()}


def gather_pages(pool, page_table):
    g = pool[page_table]
    return g.reshape((g.shape[0], g.shape[1] * g.shape[2]) + g.shape[3:])


def setup_inputs(seed: int = 0) -> dict:
    key = jax.random.key(seed)
    ks = iter(jax.random.split(key, 64))
    f32 = jnp.float32

    def nrm(shape, scale):
        return jax.random.normal(next(ks), shape, f32) * scale

    def gain(shape):
        return 1.0 + nrm(shape, 0.02)

    n_pages = PAST_LEN // PAGE_SIZE
    n_used = DEC_BATCH * n_pages
    n_phys = n_used + max(1, n_used // 4)
    win_buf = min(WINDOW, PAST_LEN)
    paged = (N_NSA_LAYERS, n_phys, PAGE_SIZE, NSA_KV_HEADS, NSA_HEAD_DIM)
    win = (N_NSA_LAYERS, DEC_BATCH, win_buf, NSA_KV_HEADS, NSA_HEAD_DIM)
    page_table = jax.random.permutation(next(ks), n_phys)[:n_used].reshape(DEC_BATCH, n_pages).astype(jnp.int32)
    dt0 = jnp.exp(jax.random.uniform(next(ks), (N_SSD_LAYERS, SSD_HEADS), f32, math.log(1e-3), math.log(1e-1)))
    dt_bias = dt0 + jnp.log(-jnp.expm1(-dt0))
    a_log = jnp.log(jax.random.uniform(next(ks), (N_SSD_LAYERS, SSD_HEADS), f32, 1.0, 16.0))
    return {
        'x_prompt': nrm((BATCH, SEQ, D_MODEL), 1.0),
        'x_sample': nrm((DEC_BATCH, DEC_SEQ, D_MODEL), 1.0),
        'state_ssd': nrm((N_SSD_LAYERS, DEC_BATCH, SSD_HEADS, SSD_HEAD_DIM, SSD_STATE), 0.1),
        'state_ssd_conv': nrm((N_SSD_LAYERS, DEC_BATCH, SSD_CONV - 1, SSD_CONV_DIM), 1.0),
        'cache_cmp_k': nrm(paged, 1.0),
        'cache_cmp_v': nrm(paged, 1.0),
        'cache_slc_k': nrm(paged, 1.0),
        'cache_slc_v': nrm(paged, 1.0),
        'cache_win_k': nrm(win, 1.0),
        'cache_win_v': nrm(win, 1.0),
        'state_pool': nrm((N_POOL_LAYERS, DEC_BATCH, POOL_BUF, D_MODEL), 1.0),
        'state_ffn_conv': nrm((DEPTH, DEC_BATCH, FFN_CONV - 1, 2 * D_FF), 1.0),
        'page_table': page_table,
        'c_prompt': nrm((BATCH, D_MODEL), 1.0),
        'c_sample': nrm((DEC_BATCH, D_MODEL), 1.0),
        'ada_w': nrm((DEPTH, D_MODEL, 6 * D_MODEL), 0.5 * D_MODEL ** -0.5),
        'ada_b': nrm((DEPTH, 6 * D_MODEL), 0.02),
        'norm1_g': gain((DEPTH, D_MODEL)),
        'norm2_g': gain((DEPTH, D_MODEL)),
        'final_g': gain((D_MODEL,)),
        'ssd_w_in': nrm((N_SSD_LAYERS, D_MODEL, SSD_IN_DIM), D_MODEL ** -0.5),
        'ssd_conv_w': nrm((N_SSD_LAYERS, SSD_CONV, SSD_CONV_DIM), SSD_CONV ** -0.5),
        'ssd_conv_b': nrm((N_SSD_LAYERS, SSD_CONV_DIM), 0.02),
        'ssd_dt_bias': dt_bias,
        'ssd_a_log': a_log,
        'ssd_d': 1.0 + nrm((N_SSD_LAYERS, SSD_HEADS), 0.1),
        'ssd_norm_g': gain((N_SSD_LAYERS, SSD_D_INNER)),
        'ssd_w_out': nrm((N_SSD_LAYERS, SSD_D_INNER, D_MODEL), SSD_D_INNER ** -0.5),
        'nsa_w_in': nrm((N_NSA_LAYERS, D_MODEL, NSA_IN_DIM), D_MODEL ** -0.5),
        'nsa_cmpk_w1': nrm((N_NSA_LAYERS, CMP_LEN, NSA_HEAD_DIM, CMP_HIDDEN), (CMP_LEN * NSA_HEAD_DIM) ** -0.5),
        'nsa_cmpk_w2': nrm((N_NSA_LAYERS, CMP_HIDDEN, NSA_HEAD_DIM), CMP_HIDDEN ** -0.5),
        'nsa_cmpk_pe': nrm((N_NSA_LAYERS, CMP_LEN, NSA_HEAD_DIM), 0.1),
        'nsa_cmpv_w1': nrm((N_NSA_LAYERS, CMP_LEN, NSA_HEAD_DIM, CMP_HIDDEN), (CMP_LEN * NSA_HEAD_DIM) ** -0.5),
        'nsa_cmpv_w2': nrm((N_NSA_LAYERS, CMP_HIDDEN, NSA_HEAD_DIM), CMP_HIDDEN ** -0.5),
        'nsa_cmpv_pe': nrm((N_NSA_LAYERS, CMP_LEN, NSA_HEAD_DIM), 0.1),
        'nsa_w_out': nrm((N_NSA_LAYERS, NSA_Q_DIM, D_MODEL), NSA_Q_DIM ** -0.5),
        'pool_w': nrm((N_POOL_LAYERS, POOL_GROUPS, POOL_GROUP_DIM, POOL_GROUP_DIM), POOL_GROUP_DIM ** -0.5),
        'pool_scale': gain((N_POOL_LAYERS, D_MODEL)),
        'ffn_w_up': nrm((DEPTH, D_MODEL, 2 * D_FF), D_MODEL ** -0.5),
        'ffn_conv_w': nrm((DEPTH, FFN_CONV, 2 * D_FF), FFN_CONV ** -0.5),
        'ffn_conv_b': nrm((DEPTH, 2 * D_FF), 0.02),
        'ffn_w_down': nrm((DEPTH, D_FF, D_MODEL), D_FF ** -0.5),
    }


def reference(x_prompt, x_sample, state_ssd, state_ssd_conv, cache_cmp_k, cache_cmp_v, cache_slc_k, cache_slc_v,
              cache_win_k, cache_win_v, state_pool, state_ffn_conv, page_table, c_prompt, c_sample,
              ada_w, ada_b, norm1_g, norm2_g, final_g,
              ssd_w_in, ssd_conv_w, ssd_conv_b, ssd_dt_bias, ssd_a_log, ssd_d, ssd_norm_g, ssd_w_out,
              nsa_w_in, nsa_cmpk_w1, nsa_cmpk_w2, nsa_cmpk_pe, nsa_cmpv_w1, nsa_cmpv_w2, nsa_cmpv_pe, nsa_w_out,
              pool_w, pool_scale, ffn_w_up, ffn_conv_w, ffn_conv_b, ffn_w_down):
    P = dict(ada_w=ada_w, ada_b=ada_b, norm1_g=norm1_g, norm2_g=norm2_g, final_g=final_g,
             ssd_w_in=ssd_w_in, ssd_conv_w=ssd_conv_w, ssd_conv_b=ssd_conv_b, ssd_dt_bias=ssd_dt_bias,
             ssd_a_log=ssd_a_log, ssd_d=ssd_d, ssd_norm_g=ssd_norm_g, ssd_w_out=ssd_w_out,
             nsa_w_in=nsa_w_in, nsa_cmpk_w1=nsa_cmpk_w1, nsa_cmpk_w2=nsa_cmpk_w2, nsa_cmpk_pe=nsa_cmpk_pe,
             nsa_cmpv_w1=nsa_cmpv_w1, nsa_cmpv_w2=nsa_cmpv_w2, nsa_cmpv_pe=nsa_cmpv_pe, nsa_w_out=nsa_w_out,
             pool_w=pool_w, pool_scale=pool_scale,
             ffn_w_up=ffn_w_up, ffn_conv_w=ffn_conv_w, ffn_conv_b=ffn_conv_b, ffn_w_down=ffn_w_down)
    y_prompt, sp = run_trunk(x_prompt, c_prompt, 0, None, P)
    past_len = page_table.shape[1] * PAGE_SIZE
    nsa_past = [(gather_pages(cache_cmp_k[j], page_table), gather_pages(cache_cmp_v[j], page_table),
                 gather_pages(cache_slc_k[j], page_table), gather_pages(cache_slc_v[j], page_table),
                 cache_win_k[j], cache_win_v[j]) for j in range(N_NSA_LAYERS)]
    past = dict(ssd_state=state_ssd, ssd_conv=state_ssd_conv, nsa=nsa_past, pool=state_pool, ffn=state_ffn_conv)
    y_sample, ss = run_trunk(x_sample, c_sample, past_len, past, P)
    return (y_prompt, y_sample,
            sp['ssd_state'], ss['ssd_state'], sp['ssd_conv'], ss['ssd_conv'],
            sp['cmp_k'], ss['cmp_k'], sp['cmp_v'], ss['cmp_v'],
            sp['slc_k'], ss['slc_k'], sp['slc_v'], ss['slc_v'],
            sp['win_k'], ss['win_k'], sp['win_v'], ss['win_v'],
            sp['pool'], ss['pool'], sp['ffn'], ss['ffn'])
```

```python
import functools
import math

import jax
import jax.numpy as jnp
import numpy as np
from jax import lax
from jax.experimental import pallas as pl
from jax.experimental.pallas import tpu as pltpu

D_MODEL = 2048
DEPTH = 4
PAGE_SIZE = 128
N_MIXERS = 3

SSD_D_INNER = 2 * D_MODEL
SSD_HEAD_DIM = 64
SSD_HEADS = SSD_D_INNER // SSD_HEAD_DIM
SSD_GROUPS = 8
SSD_STATE = 128
SSD_CONV = 4
SSD_CHUNK = 128
SSD_CONV_DIM = SSD_D_INNER + 2 * SSD_GROUPS * SSD_STATE

NSA_HEADS = 16
NSA_KV_HEADS = 4
NSA_HEAD_DIM = D_MODEL // NSA_HEADS
NSA_REP = NSA_HEADS // NSA_KV_HEADS
NSA_Q_DIM = NSA_HEADS * NSA_HEAD_DIM
NSA_KV_DIM = NSA_KV_HEADS * NSA_HEAD_DIM
CMP_LEN = 32
CMP_STRIDE = 16
CMP_HIDDEN = 256
SEL_LEN = 64
SEL_TOPK = 16
WINDOW = 512
WIN_Q_BLOCK = 128
SEL_Q_BLOCK = 64
ROPE_THETA = 10000.0
FORCE_BONUS = 1000.0

POOL_WINDOWS = (2, 4, 8, 16)
POOL_GROUPS = len(POOL_WINDOWS)
POOL_GROUP_DIM = D_MODEL // POOL_GROUPS
POOL_BUF = max(POOL_WINDOWS) - 1

D_FF = 5632
FFN_CONV = 3

EPS = 1e-6
NEG_INF = -1e30
TINY = 1e-30

V7X_VMEM_LIMIT_BYTES = 52 * 1024 * 1024
LANE = 128
SUBLANE = 8

F32 = jnp.float32
BF16 = jnp.bfloat16


def _mm_kernel(a_ref, w_ref, o_ref, wb_ref):
    @pl.when(pl.program_id(1) == 0)
    def _():
        wb_ref[...] = w_ref[...].astype(BF16)

    o_ref[...] = jnp.dot(a_ref[...].astype(BF16), wb_ref[...], preferred_element_type=F32)


def _pick_tile(dim, pref, align):
    t = min(pref, dim)
    t -= t % align
    while t >= align:
        if dim % t == 0:
            return t
        t -= align
    return dim


def _mm(a, w, *, n_cols=None, tm=512, tn=1024):
    m, k = a.shape
    k2, n = w.shape
    assert k == k2
    n = n if n_cols is None else n_cols
    tm = _pick_tile(m, tm, SUBLANE)
    tn = _pick_tile(n, tn, LANE)
    while k * tn * 10 > 30 * 1024 * 1024 and tn % (2 * LANE) == 0:
        tn //= 2
    while k * tm * 8 > 12 * 1024 * 1024 and tm % (2 * SUBLANE) == 0:
        tm //= 2
    return pl.pallas_call(
        _mm_kernel,
        grid=(n // tn, m // tm),
        in_specs=[
            pl.BlockSpec((tm, k), lambda j, i: (i, 0)),
            pl.BlockSpec((k, tn), lambda j, i: (0, j)),
        ],
        out_specs=pl.BlockSpec((tm, tn), lambda j, i: (i, j)),
        out_shape=jax.ShapeDtypeStruct((m, n), F32),
        scratch_shapes=[pltpu.VMEM((k, tn), BF16)],
        compiler_params=pltpu.CompilerParams(
            dimension_semantics=("arbitrary", "arbitrary"),
            vmem_limit_bytes=V7X_VMEM_LIMIT_BYTES,
        ),
        name="mm",
    )(a, w)


def _mm3(h, w, **kw):
    b, t, k = h.shape
    out = _mm(h.reshape(b * t, k), w, **kw)
    return out.reshape(b, t, out.shape[1])


def _rmsnorm(x, g):
    xf = x.astype(F32)
    y = xf * lax.rsqrt(jnp.mean(xf * xf, axis=-1, keepdims=True) + EPS)
    return (y * g.astype(F32)).astype(x.dtype)


def _modulate(h, shift, scale):
    return h * (1 + scale[:, None, :]) + shift[:, None, :]


def _rope(x, pos):
    hd = x.shape[-1]
    half = hd // 2
    inv = jnp.exp(-math.log(ROPE_THETA) * jnp.arange(half, dtype=F32) * 2.0 / hd)
    ang = pos.astype(F32)[:, None] * inv[None, :]
    cos = jnp.cos(ang)[:, None, :].astype(x.dtype)
    sin = jnp.sin(ang)[:, None, :].astype(x.dtype)
    x1, x2 = x[..., :half], x[..., half:]
    return jnp.concatenate([x1 * cos - x2 * sin, x2 * cos + x1 * sin], axis=-1)


def _masked_softmax(s, mask):
    s = jnp.where(mask, s.astype(F32), NEG_INF)
    m = jnp.max(s, axis=-1, keepdims=True)
    p = jnp.where(mask, jnp.exp(s - m), 0.0)
    return p / jnp.maximum(jnp.sum(p, axis=-1, keepdims=True), TINY)


def _causal_dwconv(x, buf, w, bias):
    xe = jnp.concatenate([buf.astype(x.dtype), x], axis=1)
    y = lax.conv_general_dilated(xe, w[:, None, :].astype(x.dtype), window_strides=(1,), padding='VALID',
                                 dimension_numbers=('NWC', 'WIO', 'NWC'), feature_group_count=x.shape[-1])
    return y + bias, xe[:, -(w.shape[0] - 1):]


def _ssd_scan(x, dt, a_neg, bm, cm, s0):
    b, t = x.shape[:2]
    cl = min(SSD_CHUNK, t)
    pad = (-t) % cl
    if pad:
        padf = lambda a: jnp.pad(a, [(0, 0), (0, pad)] + [(0, 0)] * (a.ndim - 2))
        x, dt, bm, cm = padf(x), padf(dt), padf(bm), padf(cm)
    nc = (t + pad) // cl
    chunk = lambda a: a.reshape((b, nc, cl) + a.shape[2:])
    xdt = chunk(x * dt[..., None])
    a_cs = jnp.cumsum(chunk(dt * a_neg), axis=2)
    bc, cc = chunk(bm), chunk(cm)
    causal = jnp.tril(jnp.ones((cl, cl), bool))[:, :, None, None]
    seg = a_cs[:, :, :, None] - a_cs[:, :, None, :]
    lmat = jnp.exp(jnp.where(causal, seg, NEG_INF))
    cb = jnp.einsum('bclgn,bcsgn->bclsg', cc, bc)
    y_diag = jnp.einsum('bclsgr,bcsgrp->bclgrp', cb[..., None] * lmat, xdt)
    decay_to_end = jnp.exp(a_cs[:, :, -1:] - a_cs)
    chunk_states = jnp.einsum('bclgn,bclgr,bclgrp->bcgrpn', bc, decay_to_end, xdt)
    chunk_decay = jnp.exp(a_cs[:, :, -1])

    def step(s, inp):
        st, dc = inp
        return s * dc[..., None, None] + st, s

    s_final, s_in = lax.scan(step, s0, (jnp.moveaxis(chunk_states, 1, 0), jnp.moveaxis(chunk_decay, 1, 0)))
    s_in = jnp.moveaxis(s_in, 0, 1)
    y_off = jnp.einsum('bclgn,bcgrpn,bclgr->bclgrp', cc, s_in, jnp.exp(a_cs))
    y = (y_diag + y_off).reshape((b, nc * cl) + x.shape[2:])[:, :t]
    return y, s_final


def _ssd_mixer(h, P, j, conv_buf, s0):
    b, t, _ = h.shape
    rep = SSD_HEADS // SSD_GROUPS
    gn = SSD_GROUPS * SSD_STATE
    zxbcdt = _mm3(h, P['ssd_w_in'][j], n_cols=SSD_D_INNER + SSD_CONV_DIM)
    z = zxbcdt[..., :SSD_D_INNER]
    xbc = zxbcdt[..., SSD_D_INNER:]
    dt_raw = _mm3(h, P['ssd_w_in'][j][:, SSD_D_INNER + SSD_CONV_DIM:])
    xbc, conv_new = _causal_dwconv(xbc, conv_buf, P['ssd_conv_w'][j], P['ssd_conv_b'][j])
    xbc = jax.nn.silu(xbc).astype(F32)
    xs = xbc[..., :SSD_D_INNER].reshape(b, t, SSD_GROUPS, rep, SSD_HEAD_DIM)
    bm = xbc[..., SSD_D_INNER:SSD_D_INNER + gn].reshape(b, t, SSD_GROUPS, SSD_STATE)
    cm = xbc[..., SSD_D_INNER + gn:].reshape(b, t, SSD_GROUPS, SSD_STATE)
    dt = jax.nn.softplus(dt_raw.astype(F32) + P['ssd_dt_bias'][j].astype(F32)).reshape(b, t, SSD_GROUPS, rep)
    a_neg = -jnp.exp(P['ssd_a_log'][j].astype(F32)).reshape(SSD_GROUPS, rep)
    s0 = s0.astype(F32).reshape(b, SSD_GROUPS, rep, SSD_HEAD_DIM, SSD_STATE)
    y, s_new = _ssd_scan(xs, dt, a_neg, bm, cm, s0)
    y = y + P['ssd_d'][j].astype(F32).reshape(SSD_GROUPS, rep, 1) * xs
    y = y.reshape(b, t, SSD_D_INNER).astype(h.dtype)
    y = _rmsnorm(y * jax.nn.silu(z), P['ssd_norm_g'][j])
    s_new = s_new.reshape(b, SSD_HEADS, SSD_HEAD_DIM, SSD_STATE).astype(h.dtype)
    return _mm3(y, P['ssd_w_out'][j]), conv_new, s_new


def _compress_blocks(rows, w1, w2, pe):
    b, tk = rows.shape[:2]
    ratio = CMP_LEN // CMP_STRIDE
    n_cmp = (tk - CMP_LEN) // CMP_STRIDE + 1
    n_sub = n_cmp + ratio - 1
    sub = rows[:, :n_sub * CMP_STRIDE].reshape(b, n_sub, CMP_STRIDE, NSA_KV_HEADS, NSA_HEAD_DIM)
    w1r = w1.reshape(ratio, CMP_STRIDE, NSA_HEAD_DIM, CMP_HIDDEN)
    hidden = jnp.einsum('ld,ldh->h', pe, w1)
    for r in range(ratio):
        hidden = hidden + jnp.einsum('bnsgd,sdh->bngh', sub[:, r:r + n_cmp], w1r[r])
    return jnp.einsum('bngh,hd->bngd', jax.nn.gelu(hidden), w2)


def _block_coverage(n_cmp, n_sel):
    start = np.arange(n_cmp)[:, None] * CMP_STRIDE
    sel_start = np.arange(n_sel)[None, :] * SEL_LEN
    inter = np.minimum(start + CMP_LEN, sel_start + SEL_LEN) - np.maximum(start, sel_start)
    return (np.clip(inter, 0, None) / CMP_LEN).astype(np.float32)


def _window_attention(q, k, v, q_pos, k_pos):
    diff = q_pos[:, None] - k_pos[None, :]
    mask = (diff >= 0) & (diff < WINDOW) & (k_pos[None, :] >= 0)
    p = _masked_softmax(jnp.einsum('btgrd,bsgd->bgrts', q, k), mask)
    return jnp.einsum('bgrts,bsgd->btgrd', p.astype(v.dtype), v)


def _banded_window_attention(q, k_pad, v_pad, q_pos, k_pos):
    b, t = q.shape[:2]
    qb = WIN_Q_BLOCK if t % WIN_Q_BLOCK == 0 else t
    nb = t // qb
    span = qb + WINDOW
    q_blocks = jnp.moveaxis(q.reshape((b, nb, qb) + q.shape[2:]), 1, 0)

    def one(args):
        qblk, qp, i = args
        start = i * qb
        kb = lax.dynamic_slice_in_dim(k_pad, start, span, axis=1)
        vb = lax.dynamic_slice_in_dim(v_pad, start, span, axis=1)
        kp = lax.dynamic_slice_in_dim(k_pos, start, span)
        return _window_attention(qblk, kb, vb, qp, kp)

    out = lax.map(one, (q_blocks, q_pos.reshape(nb, qb), jnp.arange(nb)))
    return jnp.moveaxis(out, 0, 1).reshape(q.shape)


def _selected_attention(q, k_blocks, v_blocks, sel_idx, q_pos):
    b, t = q.shape[:2]
    qb = SEL_Q_BLOCK if t % SEL_Q_BLOCK == 0 else t
    nb = t // qb
    n_top = sel_idx.shape[-1]
    gather = jax.vmap(jax.vmap(lambda blocks, ix: blocks[ix]))
    q_blocks = jnp.moveaxis(q.reshape((b, nb, qb) + q.shape[2:]), 1, 0)
    idx_blocks = jnp.moveaxis(sel_idx.reshape(b, NSA_KV_HEADS, nb, qb, n_top), 2, 0)

    def one(args):
        qblk, ix, qp = args
        kg = gather(k_blocks, ix)
        vg = gather(v_blocks, ix)
        k_pos = ix[..., None] * SEL_LEN + jnp.arange(SEL_LEN)
        mask = (k_pos <= qp[:, None, None])[:, :, None]
        s = jnp.einsum('btgrd,bgtksd->bgrtks', qblk, kg)
        shp = s.shape
        flat = shp[:4] + (-1,)
        p = _masked_softmax(s.reshape(flat), jnp.broadcast_to(mask, shp).reshape(flat)).reshape(shp)
        return jnp.einsum('bgrtks,bgtksd->btgrd', p.astype(vg.dtype), vg)

    out = lax.map(one, (q_blocks, idx_blocks, q_pos.reshape(nb, qb)))
    return jnp.moveaxis(out, 0, 1).reshape(q.shape)


def _nsa_mixer(h, pos0, P, j, past):
    b, t, _ = h.shape
    proj = _mm3(h, P['nsa_w_in'][j], n_cols=NSA_Q_DIM + 6 * NSA_KV_DIM)
    gate_proj = _mm3(h, P['nsa_w_in'][j][:, NSA_Q_DIM + 6 * NSA_KV_DIM:])
    q = proj[..., :NSA_Q_DIM].reshape(b, t, NSA_HEADS, NSA_HEAD_DIM)
    kv = proj[..., NSA_Q_DIM:NSA_Q_DIM + 6 * NSA_KV_DIM].reshape(b, t, 6, NSA_KV_HEADS, NSA_HEAD_DIM)
    gates = jax.nn.sigmoid(gate_proj.astype(F32))
    gates = gates.reshape(b, t, 3, NSA_KV_HEADS, NSA_REP, 1).astype(h.dtype)
    q_pos = pos0 + jnp.arange(t)
    q = _rope(q, q_pos).reshape(b, t, NSA_KV_HEADS, NSA_REP, NSA_HEAD_DIM) * (NSA_HEAD_DIM ** -0.5)
    kc, vc = _rope(kv[:, :, 0], q_pos), kv[:, :, 1]
    ks, vs = _rope(kv[:, :, 2], q_pos), kv[:, :, 3]
    kw, vw = _rope(kv[:, :, 4], q_pos), kv[:, :, 5]
    if past is None:
        kc_all, vc_all, ks_all, vs_all = kc, vc, ks, vs
    else:
        kc_all = jnp.concatenate([past[0].astype(h.dtype), kc], axis=1)
        vc_all = jnp.concatenate([past[1].astype(h.dtype), vc], axis=1)
        ks_all = jnp.concatenate([past[2].astype(h.dtype), ks], axis=1)
        vs_all = jnp.concatenate([past[3].astype(h.dtype), vs], axis=1)
    tk = ks_all.shape[1]

    kcmp = _compress_blocks(kc_all, P['nsa_cmpk_w1'][j], P['nsa_cmpk_w2'][j], P['nsa_cmpk_pe'][j])
    vcmp = _compress_blocks(vc_all, P['nsa_cmpv_w1'][j], P['nsa_cmpv_w2'][j], P['nsa_cmpv_pe'][j])
    n_cmp = kcmp.shape[1]
    cmp_end = jnp.arange(n_cmp) * CMP_STRIDE + CMP_LEN - 1
    p_cmp = _masked_softmax(jnp.einsum('btgrd,bngd->bgrtn', q, kcmp), cmp_end[None, :] <= q_pos[:, None])
    o_cmp = jnp.einsum('bgrtn,bngd->btgrd', p_cmp.astype(h.dtype), vcmp)

    n_sel = -(-tk // SEL_LEN)
    imp = jnp.einsum('bgrtn,nj->bgtj', p_cmp, jnp.asarray(_block_coverage(n_cmp, n_sel)))
    blk = jnp.arange(n_sel)[None, :]
    cur = (q_pos // SEL_LEN)[:, None]
    forced = (blk == 0) | (blk == cur) | (blk == cur - 1)
    score = jnp.where(blk <= cur, imp + FORCE_BONUS * forced, -1.0)
    _, sel_idx = lax.top_k(score, min(SEL_TOPK, n_sel))
    pad = n_sel * SEL_LEN - tk

    def to_blocks(a):
        a = jnp.pad(a, ((0, 0), (0, pad), (0, 0), (0, 0)))
        return a.reshape(b, n_sel, SEL_LEN, NSA_KV_HEADS, NSA_HEAD_DIM).transpose(0, 3, 1, 2, 4)

    o_slc = _selected_attention(q, to_blocks(ks_all), to_blocks(vs_all), sel_idx, q_pos)

    if past is None:
        zeros = jnp.zeros((b, WINDOW, NSA_KV_HEADS, NSA_HEAD_DIM), h.dtype)
        k_pos = jnp.arange(t + WINDOW) - WINDOW
        o_win = _banded_window_attention(q, jnp.concatenate([zeros, kw], axis=1),
                                         jnp.concatenate([zeros, vw], axis=1), q_pos, k_pos)
        keep = min(WINDOW, t)
        kw_new, vw_new = kw[:, t - keep:], vw[:, t - keep:]
    else:
        nbuf = past[4].shape[1]
        kw_all = jnp.concatenate([past[4].astype(h.dtype), kw], axis=1)
        vw_all = jnp.concatenate([past[5].astype(h.dtype), vw], axis=1)
        k_pos = pos0 - nbuf + jnp.arange(nbuf + t)
        o_win = _window_attention(q, kw_all, vw_all, q_pos, k_pos)
        kw_new, vw_new = kw_all[:, t:], vw_all[:, t:]

    o = gates[:, :, 0] * o_cmp + gates[:, :, 1] * o_slc + gates[:, :, 2] * o_win
    out = _mm3(o.reshape(b, t, NSA_Q_DIM), P['nsa_w_out'][j])
    return out, kc, vc, ks, vs, kw_new, vw_new


def _pool_mixer(h, pos0, P, j, buf):
    b, t, _ = h.shape
    q_pos = pos0 + jnp.arange(t)
    he = jnp.concatenate([buf.astype(h.dtype), h], axis=1)
    hf = he.astype(F32)
    cs = jnp.concatenate([jnp.zeros((b, 1, D_MODEL), F32), jnp.cumsum(hf, axis=1)], axis=1)
    end = POOL_BUF + 1
    means = []
    for gi, w in enumerate(POOL_WINDOWS):
        lo, hi = gi * POOL_GROUP_DIM, (gi + 1) * POOL_GROUP_DIM
        win_sum = cs[:, end:end + t, lo:hi] - cs[:, end - w:end - w + t, lo:hi]
        count = jnp.minimum(w, q_pos + 1).astype(F32)[None, :, None]
        means.append(win_sum / count)
    mixed = (jnp.concatenate(means, axis=-1) - hf[:, POOL_BUF:]).astype(h.dtype)
    mixed = jnp.einsum('btgc,gcd->btgd', mixed.reshape(b, t, POOL_GROUPS, POOL_GROUP_DIM), P['pool_w'][j])
    return mixed.reshape(b, t, D_MODEL) * P['pool_scale'][j], he[:, -POOL_BUF:]


def _conv_ffn(h, P, i, buf):
    u = _mm3(h, P['ffn_w_up'][i])
    u, buf_new = _causal_dwconv(u, buf, P['ffn_conv_w'][i], P['ffn_conv_b'][i])
    a, g = jnp.split(u, 2, axis=-1)
    return _mm3(jax.nn.silu(g) * a, P['ffn_w_down'][i]), buf_new


def _run_trunk(x, c, pos0, past, P):
    b = x.shape[0]
    names = ('ssd_state', 'ssd_conv', 'cmp_k', 'cmp_v', 'slc_k', 'slc_v', 'win_k', 'win_v', 'pool', 'ffn')
    new = {n: [] for n in names}
    c_act = jax.nn.silu(c)
    for i in range(DEPTH):
        kind, j = i % N_MIXERS, i // N_MIXERS
        mod = c_act @ P['ada_w'][i] + P['ada_b'][i]
        sh1, sc1, g1, sh2, sc2, g2 = jnp.split(mod, 6, axis=-1)
        h = _modulate(_rmsnorm(x, P['norm1_g'][i]), sh1, sc1)
        if kind == 0:
            if past is None:
                conv_buf = jnp.zeros((b, SSD_CONV - 1, SSD_CONV_DIM), x.dtype)
                s0 = jnp.zeros((b, SSD_HEADS, SSD_HEAD_DIM, SSD_STATE), F32)
            else:
                conv_buf, s0 = past['ssd_conv'][j], past['ssd_state'][j]
            m, conv_new, s_new = _ssd_mixer(h, P, j, conv_buf, s0)
            new['ssd_conv'].append(conv_new)
            new['ssd_state'].append(s_new)
        elif kind == 1:
            m, kc, vc, ks, vs, kw, vw = _nsa_mixer(h, pos0, P, j, None if past is None else past['nsa'][j])
            new['cmp_k'].append(kc)
            new['cmp_v'].append(vc)
            new['slc_k'].append(ks)
            new['slc_v'].append(vs)
            new['win_k'].append(kw)
            new['win_v'].append(vw)
        else:
            buf = jnp.zeros((b, POOL_BUF, D_MODEL), x.dtype) if past is None else past['pool'][j]
            m, buf_new = _pool_mixer(h, pos0, P, j, buf)
            new['pool'].append(buf_new)
        x = x + g1[:, None, :] * m
        h2 = _modulate(_rmsnorm(x, P['norm2_g'][i]), sh2, sc2)
        fbuf = jnp.zeros((b, FFN_CONV - 1, 2 * D_FF), x.dtype) if past is None else past['ffn'][i]
        f, fbuf_new = _conv_ffn(h2, P, i, fbuf)
        new['ffn'].append(fbuf_new)
        x = x + g2[:, None, :] * f
    y = _rmsnorm(x, P['final_g'])
    return y, {n: jnp.stack(v) for n, v in new.items()}


def _gather_pages(pool, page_table):
    g = pool[page_table]
    return g.reshape((g.shape[0], g.shape[1] * g.shape[2]) + g.shape[3:])


def kernel(x_prompt, x_sample, state_ssd, state_ssd_conv, cache_cmp_k, cache_cmp_v, cache_slc_k, cache_slc_v,
           cache_win_k, cache_win_v, state_pool, state_ffn_conv, page_table, c_prompt, c_sample,
           ada_w, ada_b, norm1_g, norm2_g, final_g,
           ssd_w_in, ssd_conv_w, ssd_conv_b, ssd_dt_bias, ssd_a_log, ssd_d, ssd_norm_g, ssd_w_out,
           nsa_w_in, nsa_cmpk_w1, nsa_cmpk_w2, nsa_cmpk_pe, nsa_cmpv_w1, nsa_cmpv_w2, nsa_cmpv_pe, nsa_w_out,
           pool_w, pool_scale, ffn_w_up, ffn_conv_w, ffn_conv_b, ffn_w_down):
    P = dict(ada_w=ada_w, ada_b=ada_b, norm1_g=norm1_g, norm2_g=norm2_g, final_g=final_g,
             ssd_w_in=ssd_w_in, ssd_conv_w=ssd_conv_w, ssd_conv_b=ssd_conv_b, ssd_dt_bias=ssd_dt_bias,
             ssd_a_log=ssd_a_log, ssd_d=ssd_d, ssd_norm_g=ssd_norm_g, ssd_w_out=ssd_w_out,
             nsa_w_in=nsa_w_in, nsa_cmpk_w1=nsa_cmpk_w1, nsa_cmpk_w2=nsa_cmpk_w2, nsa_cmpk_pe=nsa_cmpk_pe,
             nsa_cmpv_w1=nsa_cmpv_w1, nsa_cmpv_w2=nsa_cmpv_w2, nsa_cmpv_pe=nsa_cmpv_pe, nsa_w_out=nsa_w_out,
             pool_w=pool_w, pool_scale=pool_scale,
             ffn_w_up=ffn_w_up, ffn_conv_w=ffn_conv_w, ffn_conv_b=ffn_conv_b, ffn_w_down=ffn_w_down)
    y_prompt, sp = _run_trunk(x_prompt, c_prompt, 0, None, P)
    past_len = page_table.shape[1] * PAGE_SIZE
    nsa_past = [(_gather_pages(cache_cmp_k[j], page_table), _gather_pages(cache_cmp_v[j], page_table),
                 _gather_pages(cache_slc_k[j], page_table), _gather_pages(cache_slc_v[j], page_table),
                 cache_win_k[j], cache_win_v[j]) for j in range(cache_cmp_k.shape[0])]
    past = dict(ssd_state=state_ssd, ssd_conv=state_ssd_conv, nsa=nsa_past, pool=state_pool, ffn=state_ffn_conv)
    y_sample, ss = _run_trunk(x_sample, c_sample, past_len, past, P)
    return (y_prompt, y_sample,
            sp['ssd_state'], ss['ssd_state'], sp['ssd_conv'], ss['ssd_conv'],
            sp['cmp_k'], ss['cmp_k'], sp['cmp_v'], ss['cmp_v'],
            sp['slc_k'], ss['slc_k'], sp['slc_v'], ss['slc_v'],
            sp['win_k'], ss['win_k'], sp['win_v'], ss['win_v'],
            sp['pool'], ss['pool'], sp['ffn'], ss['ffn'])
```

```python
import functools
import math

import jax
import jax.numpy as jnp
import numpy as np
from jax import lax
from jax.experimental import pallas as pl
from jax.experimental.pallas import tpu as pltpu

D_MODEL = 2048
DEPTH = 4
PAGE_SIZE = 128
N_MIXERS = 3

SSD_D_INNER = 2 * D_MODEL
SSD_HEAD_DIM = 64
SSD_HEADS = SSD_D_INNER // SSD_HEAD_DIM
SSD_GROUPS = 8
SSD_REP = SSD_HEADS // SSD_GROUPS
SSD_STATE = 128
SSD_CONV = 4
SSD_CHUNK = 128
SSD_GROUP_DIM = SSD_REP * SSD_HEAD_DIM
SSD_CONV_DIM = SSD_D_INNER + 2 * SSD_GROUPS * SSD_STATE

NSA_HEADS = 16
NSA_KV_HEADS = 4
NSA_HEAD_DIM = D_MODEL // NSA_HEADS
NSA_REP = NSA_HEADS // NSA_KV_HEADS
NSA_Q_DIM = NSA_HEADS * NSA_HEAD_DIM
NSA_KV_DIM = NSA_KV_HEADS * NSA_HEAD_DIM
CMP_LEN = 32
CMP_STRIDE = 16
CMP_HIDDEN = 256
SEL_LEN = 64
SEL_TOPK = 16
WINDOW = 512
WIN_Q_BLOCK = 128
SEL_Q_BLOCK = 64
ROPE_THETA = 10000.0
FORCE_BONUS = 1000.0

POOL_WINDOWS = (2, 4, 8, 16)
POOL_GROUPS = len(POOL_WINDOWS)
POOL_GROUP_DIM = D_MODEL // POOL_GROUPS
POOL_BUF = max(POOL_WINDOWS) - 1

D_FF = 5632
FFN_CONV = 3

EPS = 1e-6
NEG_INF = -1e30
TINY = 1e-30

V7X_VMEM_LIMIT_BYTES = 52 * 1024 * 1024
LANE = 128
SUBLANE = 8

F32 = jnp.float32
BF16 = jnp.bfloat16
HIGHEST = lax.Precision.HIGHEST

NSA_TQ = 128
NSA_SLC_KC = 256
NSA_WIN_SPAN = WINDOW + NSA_TQ


def _cparams(n_axes):
    return pltpu.CompilerParams(dimension_semantics=("arbitrary",) * n_axes,
                                vmem_limit_bytes=V7X_VMEM_LIMIT_BYTES)


def _pick_tile(dim, pref, align):
    t = min(pref, dim)
    t -= t % align
    while t >= align:
        if dim % t == 0:
            return t
        t -= align
    return dim


def _sigmoid(x):
    return 1.0 / (1.0 + jnp.exp(-x))


def _silu(x):
    return x * _sigmoid(x)


def _dot_nt(a, b):
    return lax.dot_general(a, b, (((1,), (1,)), ((), ())), preferred_element_type=F32)


def _dot_tn(a, b):
    return lax.dot_general(a, b, (((0,), (0,)), ((), ())), preferred_element_type=F32)


def _dot_exact(a, b):
    return jnp.dot(a, b, precision=HIGHEST, preferred_element_type=F32)


def _normmod_kernel(x_ref, g_ref, sh_ref, sc_ref, o_ref):
    x = x_ref[0].astype(F32)
    y = x * lax.rsqrt(jnp.mean(x * x, axis=-1, keepdims=True) + EPS) * g_ref[...]
    y = y * (1.0 + sc_ref[0]) + sh_ref[0]
    o_ref[0] = y.astype(o_ref.dtype)


def _normmod(x, g, shift, scale, out_dtype):
    b, t, d = x.shape
    r = shift.shape[1]
    tt = _pick_tile(t, 256, SUBLANE)
    if r == 1:
        mod_spec = pl.BlockSpec((1, 1, d), lambda i, j: (i, 0, 0))
    else:
        mod_spec = pl.BlockSpec((1, tt, d), lambda i, j: (i, j, 0))
    return pl.pallas_call(
        _normmod_kernel,
        grid=(b, t // tt),
        in_specs=[pl.BlockSpec((1, tt, d), lambda i, j: (i, j, 0)),
                  pl.BlockSpec((1, d), lambda i, j: (0, 0)),
                  mod_spec, mod_spec],
        out_specs=pl.BlockSpec((1, tt, d), lambda i, j: (i, j, 0)),
        out_shape=jax.ShapeDtypeStruct((b, t, d), out_dtype),
        compiler_params=_cparams(2),
        name="normmod",
    )(x, g.reshape(1, d), shift, scale)


def _mm_kernel(*refs, a_silu, has_bias, has_res):
    it = iter(refs)
    a_ref, w_ref = next(it), next(it)
    bias_ref = next(it) if has_bias else None
    res_ref = next(it) if has_res else None
    gate_ref = next(it) if has_res else None
    o_ref, wb_ref = next(it), next(it)

    @pl.when(pl.program_id(1) == 0)
    def _():
        wb_ref[...] = w_ref[...].astype(BF16)

    a = a_ref[...]
    if a_silu:
        a = _silu(a.astype(F32))
    acc = jnp.dot(a.astype(BF16), wb_ref[...], preferred_element_type=F32)
    if has_bias:
        acc = acc + bias_ref[...]
    if has_res:
        acc = res_ref[...] + gate_ref[0] * acc
    o_ref[...] = acc.astype(o_ref.dtype)


def _mm(a, w, *, n_cols=None, a_silu=False, bias=None, res=None, gate=None, rows_per_gate=None,
        out_dtype=F32, tm=512, tn=1024):
    m, k = a.shape
    assert w.shape[0] == k
    n = w.shape[1] if n_cols is None else n_cols
    tm = _pick_tile(m if rows_per_gate is None else rows_per_gate, tm, SUBLANE)
    tn = _pick_tile(n, tn, LANE)
    while k * tn * 10 > 30 * 1024 * 1024 and tn % (2 * LANE) == 0:
        tn //= 2
    assert m % tm == 0 and n % tn == 0
    in_specs = [pl.BlockSpec((tm, k), lambda j, i: (i, 0)),
                pl.BlockSpec((k, tn), lambda j, i: (0, j))]
    args = [a, w]
    if bias is not None:
        in_specs.append(pl.BlockSpec((1, tn), lambda j, i: (0, j)))
        args.append(bias.reshape(1, -1))
    if res is not None:
        in_specs.append(pl.BlockSpec((tm, tn), lambda j, i: (i, j)))
        args.append(res)
        if gate.shape[1] == 1:
            tiles_per_gate = rows_per_gate // tm
            in_specs.append(pl.BlockSpec((1, 1, tn), lambda j, i: (i // tiles_per_gate, 0, j)))
        else:
            in_specs.append(pl.BlockSpec((1, tm, tn), lambda j, i: (0, i, j)))
        args.append(gate)
    return pl.pallas_call(
        functools.partial(_mm_kernel, a_silu=a_silu, has_bias=bias is not None, has_res=res is not None),
        grid=(n // tn, m // tm),
        in_specs=in_specs,
        out_specs=pl.BlockSpec((tm, tn), lambda j, i: (i, j)),
        out_shape=jax.ShapeDtypeStruct((m, n), out_dtype),
        scratch_shapes=[pltpu.VMEM((k, tn), BF16)],
        compiler_params=_cparams(2),
        name="mm",
    )(*args)


_CONV_PAD = SUBLANE


def _conv_tile(x_ref, buf_ref, w_ref, b_ref, st_ref, xe_ref, width, tt):
    t = pl.program_id(2)
    lo = _CONV_PAD - (width - 1)

    @pl.when(t == 0)
    def _():
        xe_ref[lo:_CONV_PAD, :] = buf_ref[0]

    @pl.when(t > 0)
    def _():
        xe_ref[0:_CONV_PAD, :] = xe_ref[tt:tt + _CONV_PAD, :]

    xe_ref[_CONV_PAD:_CONV_PAD + tt, :] = x_ref[0]
    acc = b_ref[...] + w_ref[0:1, :] * xe_ref[lo:lo + tt, :]
    for k in range(1, width):
        acc = acc + w_ref[k:k + 1, :] * xe_ref[lo + k:lo + k + tt, :]
    st_ref[0] = xe_ref[tt + lo:tt + _CONV_PAD, :]
    return acc


def _conv_silu_kernel(x_ref, buf_ref, w_ref, b_ref, y_ref, st_ref, xe_ref, *, width, tt):
    y_ref[0] = _silu(_conv_tile(x_ref, buf_ref, w_ref, b_ref, st_ref, xe_ref, width, tt))


def _conv_glu_kernel(xa_ref, xg_ref, bufa_ref, bufg_ref, wa_ref, wg_ref, ba_ref, bg_ref,
                     act_ref, sta_ref, stg_ref, xea_ref, xeg_ref, *, width, tt):
    a = _conv_tile(xa_ref, bufa_ref, wa_ref, ba_ref, sta_ref, xea_ref, width, tt)
    g = _conv_tile(xg_ref, bufg_ref, wg_ref, bg_ref, stg_ref, xeg_ref, width, tt)
    act_ref[0] = (_silu(g) * a).astype(act_ref.dtype)


def _conv_silu(x, col0, n_ch, buf, w, bias, *, tc=512, tt=512):
    b, t, _ = x.shape
    width = w.shape[0]
    tt = _pick_tile(t, tt, SUBLANE)
    off = col0 // tc
    return pl.pallas_call(
        functools.partial(_conv_silu_kernel, width=width, tt=tt),
        grid=(b, n_ch // tc, t // tt),
        in_specs=[pl.BlockSpec((1, tt, tc), lambda i, j, s: (i, s, j + off)),
                  pl.BlockSpec((1, width - 1, tc), lambda i, j, s: (i, 0, j)),
                  pl.BlockSpec((width, tc), lambda i, j, s: (0, j)),
                  pl.BlockSpec((1, tc), lambda i, j, s: (0, j))],
        out_specs=[pl.BlockSpec((1, tt, tc), lambda i, j, s: (i, s, j)),
                   pl.BlockSpec((1, width - 1, tc), lambda i, j, s: (i, 0, j))],
        out_shape=[jax.ShapeDtypeStruct((b, t, n_ch), F32),
                   jax.ShapeDtypeStruct((b, width - 1, n_ch), F32)],
        scratch_shapes=[pltpu.VMEM((tt + _CONV_PAD, tc), F32)],
        compiler_params=_cparams(3),
        name="conv_silu",
    )(x, buf, w, bias.reshape(1, -1))


def _conv_glu(u, buf, w, bias, *, tc=512, tt=512):
    b, t, c2 = u.shape
    half = c2 // 2
    width = w.shape[0]
    tt = _pick_tile(t, tt, SUBLANE)
    hb = half // tc
    xa = pl.BlockSpec((1, tt, tc), lambda i, j, s: (i, s, j))
    xg = pl.BlockSpec((1, tt, tc), lambda i, j, s: (i, s, j + hb))
    ba = pl.BlockSpec((1, width - 1, tc), lambda i, j, s: (i, 0, j))
    bg = pl.BlockSpec((1, width - 1, tc), lambda i, j, s: (i, 0, j + hb))
    wa = pl.BlockSpec((width, tc), lambda i, j, s: (0, j))
    wg = pl.BlockSpec((width, tc), lambda i, j, s: (0, j + hb))
    ca = pl.BlockSpec((1, tc), lambda i, j, s: (0, j))
    cg = pl.BlockSpec((1, tc), lambda i, j, s: (0, j + hb))
    st = pl.BlockSpec((1, width - 1, tc), lambda i, j, s: (i, 0, j))
    bias2 = bias.reshape(1, -1)
    act, sta, stg = pl.pallas_call(
        functools.partial(_conv_glu_kernel, width=width, tt=tt),
        grid=(b, hb, t // tt),
        in_specs=[xa, xg, ba, bg, wa, wg, ca, cg],
        out_specs=[pl.BlockSpec((1, tt, tc), lambda i, j, s: (i, s, j)), st, st],
        out_shape=[jax.ShapeDtypeStruct((b, t, half), BF16),
                   jax.ShapeDtypeStruct((b, width - 1, half), F32),
                   jax.ShapeDtypeStruct((b, width - 1, half), F32)],
        scratch_shapes=[pltpu.VMEM((tt + _CONV_PAD, tc), F32), pltpu.VMEM((tt + _CONV_PAD, tc), F32)],
        compiler_params=_cparams(3),
        name="conv_glu",
    )(u, u, buf, buf, w, w, bias2, bias2)
    return act, jnp.concatenate([sta, stg], axis=-1)


def _softplus(x):
    return jnp.maximum(x, 0.0) + jnp.log1p(jnp.exp(-jnp.abs(x)))


def _ssd_kernel(x_ref, b_ref, c_ref, dtc_ref, dtr_ref, biasc_ref, biasr_ref, alogc_ref, alogr_ref,
                dch_ref, e512_ref, e1024_ref, et_ref, s0_ref, y_ref, sf_ref, s_ref, *, t_valid):
    L = SSD_CHUNK
    c = pl.program_id(2)

    @pl.when(c == 0)
    def _():
        s_ref[...] = s0_ref[0].reshape(SSD_GROUP_DIM, SSD_STATE)

    x = x_ref[0]
    bm = b_ref[0].astype(BF16)
    cm = c_ref[0].astype(BF16)
    row = lax.broadcasted_iota(jnp.int32, (L, L), 0)
    col = lax.broadcasted_iota(jnp.int32, (L, L), 1)
    causal = row >= col

    dt_c = _softplus(dtc_ref[0, 0] + biasc_ref[0])
    dt_r = _softplus(dtr_ref[0] + biasr_ref[0])
    if t_valid < L:
        dt_c = jnp.where(lax.broadcasted_iota(jnp.int32, dt_c.shape, 0) < t_valid, dt_c, 0.0)
        dt_r = jnp.where(lax.broadcasted_iota(jnp.int32, dt_r.shape, 1) < t_valid, dt_r, 0.0)
    dta_c = dt_c * (-jnp.exp(alogc_ref[0]))
    dta_r = dt_r * (-jnp.exp(alogr_ref[0]))

    tri = jnp.where(causal, 1.0, 0.0).astype(F32)
    upper = jnp.where(row <= col, 1.0, 0.0).astype(F32)
    acs_c = _dot_exact(tri, dta_c)
    acs_r = _dot_exact(dta_r, upper)
    e512 = e512_ref[...]
    acs_x = _dot_exact(acs_c, e512)
    dt_x = _dot_exact(dt_c, e512)
    acs_full = _dot_exact(acs_c, e1024_ref[...])
    alast_x = acs_x[L - 1:L, :]

    xdt = x * dt_x
    xdt_b = xdt.astype(BF16)
    cb = _dot_nt(cm, bm)
    lane = lax.broadcasted_iota(jnp.int32, (L, LANE), 1)
    pairs = []
    for pr in range(SSD_REP // 2):
        xp = xdt_b[:, pr * LANE:(pr + 1) * LANE]
        halves = []
        for r in (2 * pr, 2 * pr + 1):
            seg = acs_full[:, r * LANE:(r + 1) * LANE] - acs_r[r:r + 1, :]
            lm = jnp.exp(jnp.where(causal, seg, NEG_INF))
            halves.append(jnp.dot((cb * lm).astype(BF16), xp, preferred_element_type=F32))
        pairs.append(jnp.where(lane < SSD_HEAD_DIM, halves[0], halves[1]))
    y_diag = jnp.concatenate(pairs, axis=1)

    s_in = s_ref[...]
    y_off = _dot_nt(cm, s_in.astype(BF16)) * jnp.exp(acs_x)
    y_ref[0] = y_diag + y_off + x * dch_ref[...]

    xw = (xdt * jnp.exp(alast_x - acs_x)).astype(BF16)
    alast_r = jnp.sum(dta_r, axis=1, keepdims=True)
    dec = jnp.exp(_dot_exact(et_ref[...], jnp.broadcast_to(alast_r, (SSD_REP, SSD_STATE))))
    s_new = s_in * dec + _dot_tn(xw, bm)
    s_ref[...] = s_new
    sf_ref[0] = s_new.reshape(SSD_REP, SSD_HEAD_DIM, SSD_STATE)


def _ssd_expanders():
    e512 = np.repeat(np.eye(SSD_REP, dtype=np.float32), SSD_HEAD_DIM, axis=1)
    e1024 = np.repeat(np.eye(SSD_REP, dtype=np.float32), LANE, axis=1)
    return jnp.asarray(e512), jnp.asarray(e1024), jnp.asarray(e512.T.copy())


def _ssd_scan(xbc, dt_raw, dt_bias, a_log, d_skip, s0, t_valid):
    b, t, _ = xbc.shape
    L = SSD_CHUNK
    assert t % L == 0
    g, rep = SSD_GROUPS, SSD_REP
    dtc = dt_raw.reshape(b, t, g, rep).transpose(0, 2, 1, 3)
    dtr = dt_raw.transpose(0, 2, 1)
    bias_c = dt_bias.reshape(g, 1, rep)
    bias_r = jnp.broadcast_to(dt_bias.reshape(g, rep, 1), (g, rep, L))
    alog_c = a_log.reshape(g, 1, rep)
    alog_r = jnp.broadcast_to(a_log.reshape(g, rep, 1), (g, rep, L))
    dch = jnp.repeat(d_skip, SSD_HEAD_DIM).reshape(1, SSD_D_INNER)
    e512, e1024, et = _ssd_expanders()
    xb = SSD_D_INNER // SSD_STATE
    full = lambda shape: pl.BlockSpec(shape, lambda i, j, s: (0,) * len(shape))
    return pl.pallas_call(
        functools.partial(_ssd_kernel, t_valid=t_valid),
        grid=(b, g, t // L),
        in_specs=[pl.BlockSpec((1, L, SSD_GROUP_DIM), lambda i, j, s: (i, s, j)),
                  pl.BlockSpec((1, L, SSD_STATE), lambda i, j, s: (i, s, xb + j)),
                  pl.BlockSpec((1, L, SSD_STATE), lambda i, j, s: (i, s, xb + g + j)),
                  pl.BlockSpec((1, 1, L, rep), lambda i, j, s: (i, j, s, 0)),
                  pl.BlockSpec((1, rep, L), lambda i, j, s: (i, j, s)),
                  pl.BlockSpec((1, 1, rep), lambda i, j, s: (j, 0, 0)),
                  pl.BlockSpec((1, rep, L), lambda i, j, s: (j, 0, 0)),
                  pl.BlockSpec((1, 1, rep), lambda i, j, s: (j, 0, 0)),
                  pl.BlockSpec((1, rep, L), lambda i, j, s: (j, 0, 0)),
                  pl.BlockSpec((1, SSD_GROUP_DIM), lambda i, j, s: (0, j)),
                  full((rep, SSD_GROUP_DIM)), full((rep, rep * LANE)), full((SSD_GROUP_DIM, rep)),
                  pl.BlockSpec((1, rep, SSD_HEAD_DIM, SSD_STATE), lambda i, j, s: (i, j, 0, 0))],
        out_specs=[pl.BlockSpec((1, L, SSD_GROUP_DIM), lambda i, j, s: (i, s, j)),
                   pl.BlockSpec((1, rep, SSD_HEAD_DIM, SSD_STATE), lambda i, j, s: (i, j, 0, 0))],
        out_shape=[jax.ShapeDtypeStruct((b, t, SSD_D_INNER), F32),
                   jax.ShapeDtypeStruct((b, SSD_HEADS, SSD_HEAD_DIM, SSD_STATE), F32)],
        scratch_shapes=[pltpu.VMEM((SSD_GROUP_DIM, SSD_STATE), F32)],
        compiler_params=_cparams(3),
        name="ssd_scan",
    )(xbc, xbc, xbc, dtc, dtr, bias_c, bias_r, alog_c, alog_r, dch, e512, e1024, et, s0)


def _gated_norm_kernel(y_ref, z_ref, g_ref, o_ref):
    v = y_ref[0] * _silu(z_ref[0])
    o = v * lax.rsqrt(jnp.mean(v * v, axis=-1, keepdims=True) + EPS) * g_ref[...]
    o_ref[0] = o.astype(o_ref.dtype)


def _gated_norm(y, zx, g):
    b, t, d = y.shape
    tt = _pick_tile(t, 128, SUBLANE)
    return pl.pallas_call(
        _gated_norm_kernel,
        grid=(b, t // tt),
        in_specs=[pl.BlockSpec((1, tt, d), lambda i, j: (i, j, 0)),
                  pl.BlockSpec((1, tt, d), lambda i, j: (i, j, 0)),
                  pl.BlockSpec((1, d), lambda i, j: (0, 0))],
        out_specs=pl.BlockSpec((1, tt, d), lambda i, j: (i, j, 0)),
        out_shape=jax.ShapeDtypeStruct((b, t, d), BF16),
        compiler_params=_cparams(2),
        name="gated_norm",
    )(y, zx, g.reshape(1, d))


def _rope_tables(pos):
    half = NSA_HEAD_DIM // 2
    inv = jnp.exp(-math.log(ROPE_THETA) * jnp.arange(half, dtype=F32) * 2.0 / NSA_HEAD_DIM)
    ang = pos.astype(F32)[:, None] * inv[None, :]
    cos, sin = jnp.cos(ang), jnp.sin(ang)
    return jnp.concatenate([cos, cos], axis=1), jnp.concatenate([-sin, sin], axis=1)


def _rope_slab(x, cos2, sin2):
    return x * cos2 + pltpu.roll(x, NSA_HEAD_DIM // 2, 1) * sin2


def _rope_kernel(q_ref, kc_ref, ks_ref, vs_ref, kw_ref, vw_ref, cos_ref, sin_ref,
                 qo_ref, kco_ref, kso_ref, kwo_ref, ksb_ref, vsb_ref, kwb_ref, vwb_ref):
    cos2, sin2 = cos_ref[...], sin_ref[...]
    hd = NSA_HEAD_DIM
    scale = hd ** -0.5
    for h in range(NSA_HEADS):
        sl = slice(h * hd, (h + 1) * hd)
        qo_ref[0, :, sl] = (_rope_slab(q_ref[0, :, sl], cos2, sin2) * scale).astype(qo_ref.dtype)
    for h in range(NSA_KV_HEADS):
        sl = slice(h * hd, (h + 1) * hd)
        kco_ref[0, :, sl] = _rope_slab(kc_ref[0, :, sl], cos2, sin2)
        ks = _rope_slab(ks_ref[0, :, sl], cos2, sin2)
        kso_ref[0, :, sl] = ks
        ksb_ref[0, :, sl] = ks.astype(BF16)
        kw = _rope_slab(kw_ref[0, :, sl], cos2, sin2)
        kwo_ref[0, :, sl] = kw
        kwb_ref[0, :, sl] = kw.astype(BF16)
    vsb_ref[0] = vs_ref[0].astype(BF16)
    vwb_ref[0] = vw_ref[0].astype(BF16)


def _rope_split(proj, pos):
    b, t, _ = proj.shape
    tt = _pick_tile(t, 256, SUBLANE)
    cos2, sin2 = _rope_tables(pos)
    kvd = NSA_KV_DIM
    qb = NSA_Q_DIM // kvd
    kv_in = lambda k: pl.BlockSpec((1, tt, kvd), lambda i, j: (i, j, qb + k))
    kv_out = pl.BlockSpec((1, tt, kvd), lambda i, j: (i, j, 0))
    tab = pl.BlockSpec((tt, NSA_HEAD_DIM), lambda i, j: (j, 0))
    sds = lambda n, dt: jax.ShapeDtypeStruct((b, t, n), dt)
    return pl.pallas_call(
        _rope_kernel,
        grid=(b, t // tt),
        in_specs=[pl.BlockSpec((1, tt, NSA_Q_DIM), lambda i, j: (i, j, 0))]
                 + [kv_in(k) for k in (0, 2, 3, 4, 5)] + [tab, tab],
        out_specs=[pl.BlockSpec((1, tt, NSA_Q_DIM), lambda i, j: (i, j, 0))] + [kv_out] * 7,
        out_shape=[sds(NSA_Q_DIM, BF16), sds(kvd, F32), sds(kvd, F32), sds(kvd, F32),
                   sds(kvd, BF16), sds(kvd, BF16), sds(kvd, BF16), sds(kvd, BF16)],
        compiler_params=_cparams(2),
        name="rope_split",
    )(proj, proj, proj, proj, proj, proj, cos2, sin2)


def _cmp_finish_kernel(ab_ref, pe_ref, w2_ref, o_ref):
    ab = ab_ref[0]
    n_sub = ab.shape[0]
    pe_term = pe_ref[0:1, :CMP_HIDDEN] + pe_ref[1:2, CMP_HIDDEN:]
    nxt = pltpu.roll(ab[:, CMP_HIDDEN:], n_sub - 1, 0)
    hid = ab[:, :CMP_HIDDEN] + nxt + pe_term
    act = 0.5 * hid * (1.0 + jnp.tanh(math.sqrt(2.0 / math.pi) * (hid + 0.044715 * hid * hid * hid)))
    o_ref[0] = jnp.dot(act.astype(BF16), w2_ref[...].astype(BF16),
                       preferred_element_type=F32).astype(o_ref.dtype)


def _compress(rows, w1, w2, pe):
    b, t, _ = rows.shape
    n_sub = t // CMP_STRIDE
    ratio = CMP_LEN // CMP_STRIDE
    kdim = CMP_STRIDE * NSA_HEAD_DIM
    x = rows.reshape(b, n_sub, CMP_STRIDE, NSA_KV_HEADS, NSA_HEAD_DIM).transpose(0, 3, 1, 2, 4)
    x = x.reshape(b * NSA_KV_HEADS * n_sub, kdim)
    w1cat = w1.reshape(ratio, kdim, CMP_HIDDEN).transpose(1, 0, 2).reshape(kdim, ratio * CMP_HIDDEN)
    pe_rows = jnp.zeros((SUBLANE, kdim), F32).at[:ratio].set(pe.reshape(ratio, kdim))
    ab = _mm(x, w1cat).reshape(b * NSA_KV_HEADS, n_sub, ratio * CMP_HIDDEN)
    pe_ab = _mm(pe_rows, w1cat)
    return pl.pallas_call(
        _cmp_finish_kernel,
        grid=(b * NSA_KV_HEADS,),
        in_specs=[pl.BlockSpec((1, n_sub, ratio * CMP_HIDDEN), lambda i: (i, 0, 0)),
                  pl.BlockSpec((SUBLANE, ratio * CMP_HIDDEN), lambda i: (0, 0)),
                  pl.BlockSpec((CMP_HIDDEN, NSA_HEAD_DIM), lambda i: (0, 0))],
        out_specs=pl.BlockSpec((1, n_sub, NSA_HEAD_DIM), lambda i: (i, 0, 0)),
        out_shape=jax.ShapeDtypeStruct((b * NSA_KV_HEADS, n_sub, NSA_HEAD_DIM), BF16),
        compiler_params=_cparams(1),
        name="cmp_finish",
    )(ab, pe_ab, w2)


def _masked_softmax_rows(s, allowed):
    s = jnp.where(allowed, s, NEG_INF)
    m = jnp.max(s, axis=1, keepdims=True)
    p = jnp.where(allowed, jnp.exp(s - m), 0.0)
    return p / jnp.maximum(jnp.sum(p, axis=1, keepdims=True), TINY)


def _nsa_attn_kernel(q_ref, kcmp_ref, vcmp_ref, ks_ref, vs_ref, kw_ref, vw_ref, gp_ref, covt_ref, eg_ref,
                     o_ref, *, n_sel):
    tq, hd, rep = NSA_TQ, NSA_HEAD_DIM, NSA_REP
    rows = rep * tq
    i = pl.program_id(2)
    q0 = i * tq
    q = q_ref[0]
    qs = jnp.concatenate([q[:, r * hd:(r + 1) * hd] for r in range(rep)], axis=0)

    def qpos(shape):
        return q0 + (lax.broadcasted_iota(jnp.int32, shape, 0) & (tq - 1))

    n_cmp_pad = kcmp_ref.shape[1]
    s_c = _dot_nt(qs, kcmp_ref[0])
    blk_end = lax.broadcasted_iota(jnp.int32, (rows, n_cmp_pad), 1) * CMP_STRIDE + (CMP_LEN - 1)
    p_c = _masked_softmax_rows(s_c, blk_end <= qpos((rows, n_cmp_pad)))
    o_cmp = jnp.dot(p_c.astype(BF16), vcmp_ref[0], preferred_element_type=F32)

    p_sum = p_c[0:tq]
    for r in range(1, rep):
        p_sum = p_sum + p_c[r * tq:(r + 1) * tq]
    imp_t = _dot_nt(covt_ref[...], p_sum.astype(BF16))
    nb = 32
    assert n_sel <= nb
    imp_t = imp_t[0:nb]
    jb = lax.broadcasted_iota(jnp.int32, (nb, tq), 0)
    sel_shift = SEL_LEN.bit_length() - 1
    cur = (q0 + lax.broadcasted_iota(jnp.int32, (nb, tq), 1)) >> sel_shift
    forced = (jb == 0) | (jb == cur) | (jb == cur - 1)
    score = jnp.where(jb <= cur, imp_t + jnp.where(forced, FORCE_BONUS, 0.0), -1.0)
    score = jnp.where(jb < n_sel, score, -2.0)
    cnt = jnp.zeros((nb, tq), F32)
    for jp in range(n_sel):
        rowv = score[jp:jp + 1, :]
        before = (rowv > score) | ((rowv == score) & (jb > jp))
        cnt = cnt + jnp.where(before, 1.0, 0.0)
    sel_t = jnp.where(cnt < float(min(SEL_TOPK, n_sel)), 1.0, 0.0)
    sel_t = jnp.concatenate([sel_t, jnp.zeros((LANE - nb, tq), F32)], axis=0)
    sel = sel_t.T.astype(BF16)

    kc = NSA_SLC_KC

    def slc_body(c, carry):
        m, l, acc = carry
        k0 = pl.multiple_of(c * kc, kc)
        kblk = ks_ref[0, pl.ds(k0, kc), :]
        vblk = vs_ref[0, pl.ds(k0, kc), :]
        s = _dot_nt(qs, kblk)
        kidx = k0 + lax.broadcasted_iota(jnp.int32, (LANE, kc), 1)
        expand = jnp.where(lax.broadcasted_iota(jnp.int32, (LANE, kc), 0) == (kidx >> sel_shift), 1.0, 0.0)
        mk = jnp.dot(sel, expand.astype(BF16), preferred_element_type=F32)
        mk = jnp.concatenate([mk] * rep, axis=0)
        kpos = k0 + lax.broadcasted_iota(jnp.int32, (rows, kc), 1)
        allowed = (mk > 0.5) & (kpos <= qpos((rows, kc)))
        s = jnp.where(allowed, s, NEG_INF)
        m_new = jnp.maximum(m, jnp.max(s, axis=1, keepdims=True))
        alpha = jnp.exp(m - m_new)
        p = jnp.where(allowed, jnp.exp(s - m_new), 0.0)
        l = alpha * l + jnp.sum(p, axis=1, keepdims=True)
        acc = alpha * acc + jnp.dot(p.astype(BF16), vblk, preferred_element_type=F32)
        return m_new, l, acc

    n_chunks = (q0 + tq + kc - 1) // kc
    init = (jnp.full((rows, 1), NEG_INF, F32), jnp.zeros((rows, 1), F32), jnp.zeros((rows, hd), F32))
    _, l_s, acc_s = lax.fori_loop(0, n_chunks, slc_body, init)
    o_slc = acc_s / jnp.maximum(l_s, TINY)

    span = NSA_WIN_SPAN
    w0 = pl.multiple_of(jnp.maximum(q0 + tq - span, 0), tq)
    s_w = _dot_nt(qs, kw_ref[0, pl.ds(w0, span), :])
    dist = qpos((rows, span)) - (w0 + lax.broadcasted_iota(jnp.int32, (rows, span), 1))
    p_w = _masked_softmax_rows(s_w, (dist >= 0) & (dist < WINDOW))
    o_win = jnp.dot(p_w.astype(BF16), vw_ref[0, pl.ds(w0, span), :], preferred_element_type=F32)

    gexp = _dot_exact(_sigmoid(gp_ref[0]), eg_ref[0])
    unstack = lambda o: jnp.concatenate([o[r * tq:(r + 1) * tq] for r in range(rep)], axis=1)
    w = rep * hd
    out = gexp[:, 0:w] * unstack(o_cmp) + gexp[:, w:2 * w] * unstack(o_slc) + gexp[:, 2 * w:3 * w] * unstack(o_win)
    o_ref[0] = out.astype(o_ref.dtype)


def _block_coverage(n_cmp, n_sel):
    start = np.arange(n_cmp)[:, None] * CMP_STRIDE
    sel_start = np.arange(n_sel)[None, :] * SEL_LEN
    inter = np.minimum(start + CMP_LEN, sel_start + SEL_LEN) - np.maximum(start, sel_start)
    return (np.clip(inter, 0, None) / CMP_LEN).astype(np.float32)


def _nsa_attention(q, kcmp, vcmp, ks, vs, kw, vw, gate_proj):
    b, t, _ = q.shape
    n_cmp_pad = t // CMP_STRIDE
    n_cmp = (t - CMP_LEN) // CMP_STRIDE + 1
    n_sel = -(-t // SEL_LEN)
    assert n_cmp_pad == LANE and t % NSA_SLC_KC == 0 and t >= NSA_WIN_SPAN
    cov_t = np.zeros((LANE, n_cmp_pad), np.float32)
    cov_t[:n_sel, :n_cmp] = _block_coverage(n_cmp, n_sel).T
    n_gate = gate_proj.shape[-1]
    eg = np.zeros((NSA_KV_HEADS, n_gate, 3 * NSA_REP * NSA_HEAD_DIM), np.float32)
    for g in range(NSA_KV_HEADS):
        for br in range(3):
            for r in range(NSA_REP):
                c0 = (br * NSA_REP + r) * NSA_HEAD_DIM
                eg[g, br * NSA_HEADS + g * NSA_REP + r, c0:c0 + NSA_HEAD_DIM] = 1.0
    gw = NSA_REP * NSA_HEAD_DIM
    seq = pl.BlockSpec((1, t, NSA_HEAD_DIM), lambda i, g, s: (i, 0, g))
    cmp_spec = pl.BlockSpec((1, n_cmp_pad, NSA_HEAD_DIM), lambda i, g, s: (i * NSA_KV_HEADS + g, 0, 0))
    return pl.pallas_call(
        functools.partial(_nsa_attn_kernel, n_sel=n_sel),
        grid=(b, NSA_KV_HEADS, t // NSA_TQ),
        in_specs=[pl.BlockSpec((1, NSA_TQ, gw), lambda i, g, s: (i, s, g)),
                  cmp_spec, cmp_spec, seq, seq, seq, seq,
                  pl.BlockSpec((1, NSA_TQ, n_gate), lambda i, g, s: (i, s, 0)),
                  pl.BlockSpec((LANE, n_cmp_pad), lambda i, g, s: (0, 0)),
                  pl.BlockSpec((1, n_gate, 3 * gw), lambda i, g, s: (g, 0, 0))],
        out_specs=pl.BlockSpec((1, NSA_TQ, gw), lambda i, g, s: (i, s, g)),
        out_shape=jax.ShapeDtypeStruct((b, t, NSA_Q_DIM), BF16),
        compiler_params=_cparams(3),
        name="nsa_attention",
    )(q, kcmp, vcmp, ks, vs, kw, vw, gate_proj, jnp.asarray(cov_t, BF16), jnp.asarray(eg))


_POOL_PAD = 2 * SUBLANE


def _pool_kernel(h_ref, buf_ref, w_ref, x_ref, gate_ref, o_ref, he_ref, wb_ref, *, tt, pos0):
    t = pl.program_id(1)

    @pl.when((pl.program_id(0) == 0) & (t == 0))
    def _():
        wb_ref[...] = w_ref[...].astype(BF16)

    @pl.when(t == 0)
    def _():
        he_ref[0:_POOL_PAD, :] = buf_ref[0]

    @pl.when(t > 0)
    def _():
        he_ref[0:_POOL_PAD, :] = he_ref[tt:tt + _POOL_PAD, :]

    he_ref[_POOL_PAD:_POOL_PAD + tt, :] = h_ref[0]
    gd = POOL_GROUP_DIM
    q_pos = pos0 + t * tt + lax.broadcasted_iota(jnp.int32, (tt, gd), 0)
    for gi, w in enumerate(POOL_WINDOWS):
        sl = slice(gi * gd, (gi + 1) * gd)
        cur = he_ref[_POOL_PAD:_POOL_PAD + tt, sl]
        win = cur
        for k in range(1, w):
            win = win + he_ref[_POOL_PAD - k:_POOL_PAD - k + tt, sl]
        count = jnp.minimum(w, q_pos + 1).astype(F32)
        mixed = (win / count - cur).astype(BF16)
        acc = jnp.dot(mixed, wb_ref[gi], preferred_element_type=F32)
        o_ref[0, :, sl] = x_ref[0, :, sl] + gate_ref[0, :, sl] * acc


def _pool_mixer(h, buf, pool_w, x, gate, pos0):
    b, t, d = h.shape
    tt = _pick_tile(t, 256, SUBLANE)
    buf16 = jnp.concatenate([jnp.zeros((b, _POOL_PAD - POOL_BUF, d), F32), buf], axis=1)
    if gate.shape[1] == 1:
        gate_spec = pl.BlockSpec((1, 1, d), lambda i, j: (i, 0, 0))
    else:
        gate_spec = pl.BlockSpec((1, tt, d), lambda i, j: (i, j, 0))
    tile = pl.BlockSpec((1, tt, d), lambda i, j: (i, j, 0))
    return pl.pallas_call(
        functools.partial(_pool_kernel, tt=tt, pos0=pos0),
        grid=(b, t // tt),
        in_specs=[tile, pl.BlockSpec((1, _POOL_PAD, d), lambda i, j: (i, 0, 0)),
                  pl.BlockSpec(pool_w.shape, lambda i, j: (0, 0, 0)), tile, gate_spec],
        out_specs=tile,
        out_shape=jax.ShapeDtypeStruct((b, t, d), F32),
        scratch_shapes=[pltpu.VMEM((tt + _POOL_PAD, d), F32), pltpu.VMEM(pool_w.shape, BF16)],
        compiler_params=_cparams(2),
        name="pool_mixer",
    )(h, buf16, pool_w, x, gate)


def _rope(x, pos):
    hd = x.shape[-1]
    half = hd // 2
    inv = jnp.exp(-math.log(ROPE_THETA) * jnp.arange(half, dtype=F32) * 2.0 / hd)
    ang = pos.astype(F32)[:, None] * inv[None, :]
    cos = jnp.cos(ang)[:, None, :].astype(x.dtype)
    sin = jnp.sin(ang)[:, None, :].astype(x.dtype)
    x1, x2 = x[..., :half], x[..., half:]
    return jnp.concatenate([x1 * cos - x2 * sin, x2 * cos + x1 * sin], axis=-1)


def _masked_softmax(s, mask):
    s = jnp.where(mask, s.astype(F32), NEG_INF)
    m = jnp.max(s, axis=-1, keepdims=True)
    p = jnp.where(mask, jnp.exp(s - m), 0.0)
    return p / jnp.maximum(jnp.sum(p, axis=-1, keepdims=True), TINY)


def _compress_blocks(rows, w1, w2, pe):
    b, tk = rows.shape[:2]
    ratio = CMP_LEN // CMP_STRIDE
    n_cmp = (tk - CMP_LEN) // CMP_STRIDE + 1
    n_sub = n_cmp + ratio - 1
    sub = rows[:, :n_sub * CMP_STRIDE].reshape(b, n_sub, CMP_STRIDE, NSA_KV_HEADS, NSA_HEAD_DIM)
    w1r = w1.reshape(ratio, CMP_STRIDE, NSA_HEAD_DIM, CMP_HIDDEN)
    hidden = jnp.einsum('ld,ldh->h', pe, w1)
    for r in range(ratio):
        hidden = hidden + jnp.einsum('bnsgd,sdh->bngh', sub[:, r:r + n_cmp], w1r[r])
    return jnp.einsum('bngh,hd->bngd', jax.nn.gelu(hidden), w2)


def _window_attention(q, k, v, q_pos, k_pos):
    diff = q_pos[:, None] - k_pos[None, :]
    mask = (diff >= 0) & (diff < WINDOW) & (k_pos[None, :] >= 0)
    p = _masked_softmax(jnp.einsum('btgrd,bsgd->bgrts', q, k), mask)
    return jnp.einsum('bgrts,bsgd->btgrd', p.astype(v.dtype), v)


def _selected_attention(q, k_blocks, v_blocks, sel_idx, q_pos):
    gather = jax.vmap(jax.vmap(lambda blocks, ix: blocks[ix]))
    kg = gather(k_blocks, sel_idx)
    vg = gather(v_blocks, sel_idx)
    k_pos = sel_idx[..., None] * SEL_LEN + jnp.arange(SEL_LEN)
    mask = (k_pos <= q_pos[:, None, None])[:, :, None]
    s = jnp.einsum('btgrd,bgtksd->bgrtks', q, kg)
    shp = s.shape
    flat = shp[:4] + (-1,)
    p = _masked_softmax(s.reshape(flat), jnp.broadcast_to(mask, shp).reshape(flat)).reshape(shp)
    return jnp.einsum('bgrtks,bgtksd->btgrd', p.astype(vg.dtype), vg)


def _nsa_decode_mixer(proj, gate_proj, pos0, P, j, past):
    b, t, _ = proj.shape
    q = proj[..., :NSA_Q_DIM].reshape(b, t, NSA_HEADS, NSA_HEAD_DIM)
    kv = proj[..., NSA_Q_DIM:].reshape(b, t, 6, NSA_KV_HEADS, NSA_HEAD_DIM)
    gates = jax.nn.sigmoid(gate_proj.astype(F32)).reshape(b, t, 3, NSA_KV_HEADS, NSA_REP, 1)
    q_pos = pos0 + jnp.arange(t)
    q = _rope(q, q_pos).reshape(b, t, NSA_KV_HEADS, NSA_REP, NSA_HEAD_DIM) * (NSA_HEAD_DIM ** -0.5)
    kc, vc = _rope(kv[:, :, 0], q_pos), kv[:, :, 1]
    ks, vs = _rope(kv[:, :, 2], q_pos), kv[:, :, 3]
    kw, vw = _rope(kv[:, :, 4], q_pos), kv[:, :, 5]
    kc_all = jnp.concatenate([past[0], kc], axis=1)
    vc_all = jnp.concatenate([past[1], vc], axis=1)
    ks_all = jnp.concatenate([past[2], ks], axis=1)
    vs_all = jnp.concatenate([past[3], vs], axis=1)
    tk = ks_all.shape[1]
    kcmp = _compress_blocks(kc_all, P['nsa_cmpk_w1'][j], P['nsa_cmpk_w2'][j], P['nsa_cmpk_pe'][j])
    vcmp = _compress_blocks(vc_all, P['nsa_cmpv_w1'][j], P['nsa_cmpv_w2'][j], P['nsa_cmpv_pe'][j])
    n_cmp = kcmp.shape[1]
    cmp_end = jnp.arange(n_cmp) * CMP_STRIDE + CMP_LEN - 1
    p_cmp = _masked_softmax(jnp.einsum('btgrd,bngd->bgrtn', q, kcmp), cmp_end[None, :] <= q_pos[:, None])
    o_cmp = jnp.einsum('bgrtn,bngd->btgrd', p_cmp, vcmp)
    n_sel = -(-tk // SEL_LEN)
    imp = jnp.einsum('bgrtn,nj->bgtj', p_cmp, jnp.asarray(_block_coverage(n_cmp, n_sel)))
    blk = jnp.arange(n_sel)[None, :]
    cur = (q_pos // SEL_LEN)[:, None]
    forced = (blk == 0) | (blk == cur) | (blk == cur - 1)
    score = jnp.where(blk <= cur, imp + FORCE_BONUS * forced, -1.0)
    _, sel_idx = lax.top_k(score, min(SEL_TOPK, n_sel))
    pad = n_sel * SEL_LEN - tk

    def to_blocks(a):
        a = jnp.pad(a, ((0, 0), (0, pad), (0, 0), (0, 0)))
        return a.reshape(b, n_sel, SEL_LEN, NSA_KV_HEADS, NSA_HEAD_DIM).transpose(0, 3, 1, 2, 4)

    o_slc = _selected_attention(q, to_blocks(ks_all), to_blocks(vs_all), sel_idx, q_pos)
    nbuf = past[4].shape[1]
    kw_all = jnp.concatenate([past[4], kw], axis=1)
    vw_all = jnp.concatenate([past[5], vw], axis=1)
    k_pos = pos0 - nbuf + jnp.arange(nbuf + t)
    o_win = _window_attention(q, kw_all, vw_all, q_pos, k_pos)
    o = gates[:, :, 0] * o_cmp + gates[:, :, 1] * o_slc + gates[:, :, 2] * o_win
    return o.reshape(b, t, NSA_Q_DIM), kc, vc, ks, vs, kw_all[:, t:], vw_all[:, t:]


def _gather_pages(pool, page_table):
    g = pool[page_table]
    return g.reshape((g.shape[0], g.shape[1] * g.shape[2]) + g.shape[3:])


def _run_trunk(x, mods, pos0, past, P):
    b, t, d = x.shape
    fresh = past is None
    per_batch = t % SUBLANE == 0 and t >= LANE
    names = ('ssd_state', 'ssd_conv', 'cmp_k', 'cmp_v', 'slc_k', 'slc_v', 'win_k', 'win_v', 'pool', 'ffn')
    new = {n: [] for n in names}

    bm, tm = (b, t) if per_batch else (1, b * t)

    def mod_rows(v):
        return v[:, None, :] if per_batch else jnp.repeat(v, t, axis=0)[None]

    def mm_rows(a, w, **kw):
        return _mm(a.reshape(b * t, a.shape[-1]), w, **kw)

    def pad_cols(w):
        return jnp.pad(w, ((0, 0), (0, LANE - w.shape[1])))

    def mm_res(a, w, xres, gate):
        out = _mm(a.reshape(b * t, a.shape[-1]), w, res=xres.reshape(b * t, d), gate=gate,
                  rows_per_gate=t if per_batch else None)
        return out.reshape(b, t, d)

    for i in range(DEPTH):
        kind, j = i % N_MIXERS, i // N_MIXERS
        sh1, sc1, g1, sh2, sc2, g2 = [mod_rows(v) for v in jnp.split(mods[i], 6, axis=-1)]
        h_dtype = F32 if kind == 2 else BF16
        h = _normmod(x.reshape(bm, tm, d), P['norm1_g'][i], sh1, sc1, h_dtype).reshape(b, t, d)
        if kind == 0:
            if fresh:
                conv_buf = jnp.zeros((b, SSD_CONV - 1, SSD_CONV_DIM), F32)
                s0 = jnp.zeros((b, SSD_HEADS, SSD_HEAD_DIM, SSD_STATE), F32)
            else:
                conv_buf, s0 = past['ssd_conv'][j], past['ssd_state'][j]
            w_in = P['ssd_w_in'][j]
            nzx = SSD_D_INNER + SSD_CONV_DIM
            zx = mm_rows(h, w_in, n_cols=nzx).reshape(b, t, nzx)
            dt_raw = mm_rows(h, pad_cols(w_in[:, nzx:])).reshape(b, t, LANE)[..., :SSD_HEADS]
            xbc, conv_new = _conv_silu(zx, SSD_D_INNER, SSD_CONV_DIM, conv_buf,
                                       P['ssd_conv_w'][j], P['ssd_conv_b'][j])
            tp = -(-t // SSD_CHUNK) * SSD_CHUNK
            if tp != t:
                xbc_p = jnp.pad(xbc, ((0, 0), (0, tp - t), (0, 0)))
                dt_p = jnp.pad(dt_raw, ((0, 0), (0, tp - t), (0, 0)))
            else:
                xbc_p, dt_p = xbc, dt_raw
            y, s_new = _ssd_scan(xbc_p, dt_p, P['ssd_dt_bias'][j], P['ssd_a_log'][j], P['ssd_d'][j], s0,
                                 t_valid=min(t, SSD_CHUNK))
            yn = _gated_norm(y[:, :t], zx, P['ssd_norm_g'][j])
            x = mm_res(yn, P['ssd_w_out'][j], x, g1)
            new['ssd_conv'].append(conv_new)
            new['ssd_state'].append(s_new)
        elif kind == 1:
            w_in = P['nsa_w_in'][j]
            npj = NSA_Q_DIM + 6 * NSA_KV_DIM
            proj = mm_rows(h, w_in, n_cols=npj).reshape(b, t, npj)
            gate_proj = mm_rows(h, pad_cols(w_in[:, npj:])).reshape(b, t, LANE)
            if fresh:
                q_pos = pos0 + jnp.arange(t)
                q, kc, ks, kw, ks_b, vs_b, kw_b, vw_b = _rope_split(proj, q_pos)
                vc, vs, vw = [proj[..., NSA_Q_DIM + k * NSA_KV_DIM:NSA_Q_DIM + (k + 1) * NSA_KV_DIM]
                              for k in (1, 3, 5)]
                kcmp = _compress(kc, P['nsa_cmpk_w1'][j], P['nsa_cmpk_w2'][j], P['nsa_cmpk_pe'][j])
                vcmp = _compress(vc, P['nsa_cmpv_w1'][j], P['nsa_cmpv_w2'][j], P['nsa_cmpv_pe'][j])
                o = _nsa_attention(q, kcmp, vcmp, ks_b, vs_b, kw_b, vw_b, gate_proj)
                keep = min(WINDOW, t)
                kw_new, vw_new = kw[:, t - keep:], vw[:, t - keep:]
            else:
                o, kc, vc, ks, vs, kw_new, vw_new = _nsa_decode_mixer(
                    proj, gate_proj[..., :3 * NSA_HEADS], pos0, P, j, past['nsa'][j])
            x = mm_res(o, P['nsa_w_out'][j], x, g1)
            shp = (b, t, NSA_KV_HEADS, NSA_HEAD_DIM)
            for n, v in (('cmp_k', kc), ('cmp_v', vc), ('slc_k', ks), ('slc_v', vs)):
                new[n].append(v.reshape(shp))
            new['win_k'].append(kw_new.reshape(b, -1, NSA_KV_HEADS, NSA_HEAD_DIM))
            new['win_v'].append(vw_new.reshape(b, -1, NSA_KV_HEADS, NSA_HEAD_DIM))
        else:
            buf = jnp.zeros((b, POOL_BUF, d), F32) if fresh else past['pool'][j]
            gate = (g1 * P['pool_scale'][j]).reshape(b, -1, d)
            x = _pool_mixer(h, buf, P['pool_w'][j], x, gate, pos0)
            new['pool'].append(jnp.concatenate([buf, h], axis=1)[:, -POOL_BUF:])
        h2 = _normmod(x.reshape(bm, tm, d), P['norm2_g'][i], sh2, sc2, BF16)
        fbuf = jnp.zeros((b, FFN_CONV - 1, 2 * D_FF), F32) if fresh else past['ffn'][i]
        u = mm_rows(h2, P['ffn_w_up'][i]).reshape(b, t, 2 * D_FF)
        act, fbuf_new = _conv_glu(u, fbuf, P['ffn_conv_w'][i], P['ffn_conv_b'][i])
        x = mm_res(act, P['ffn_w_down'][i], x, g2)
        new['ffn'].append(fbuf_new)
    zero = jnp.zeros((bm, 1, d), F32)
    y = _normmod(x.reshape(bm, tm, d), P['final_g'], zero, zero, F32).reshape(b, t, d)
    return y, {n: jnp.stack(v) for n, v in new.items()}


def kernel(x_prompt, x_sample, state_ssd, state_ssd_conv, cache_cmp_k, cache_cmp_v, cache_slc_k, cache_slc_v,
           cache_win_k, cache_win_v, state_pool, state_ffn_conv, page_table, c_prompt, c_sample,
           ada_w, ada_b, norm1_g, norm2_g, final_g,
           ssd_w_in, ssd_conv_w, ssd_conv_b, ssd_dt_bias, ssd_a_log, ssd_d, ssd_norm_g, ssd_w_out,
           nsa_w_in, nsa_cmpk_w1, nsa_cmpk_w2, nsa_cmpk_pe, nsa_cmpv_w1, nsa_cmpv_w2, nsa_cmpv_pe, nsa_w_out,
           pool_w, pool_scale, ffn_w_up, ffn_conv_w, ffn_conv_b, ffn_w_down):
    P = dict(norm1_g=norm1_g, norm2_g=norm2_g, final_g=final_g,
             ssd_w_in=ssd_w_in, ssd_conv_w=ssd_conv_w, ssd_conv_b=ssd_conv_b, ssd_dt_bias=ssd_dt_bias,
             ssd_a_log=ssd_a_log, ssd_d=ssd_d, ssd_norm_g=ssd_norm_g, ssd_w_out=ssd_w_out,
             nsa_w_in=nsa_w_in, nsa_cmpk_w1=nsa_cmpk_w1, nsa_cmpk_w2=nsa_cmpk_w2, nsa_cmpk_pe=nsa_cmpk_pe,
             nsa_cmpv_w1=nsa_cmpv_w1, nsa_cmpv_w2=nsa_cmpv_w2, nsa_cmpv_pe=nsa_cmpv_pe, nsa_w_out=nsa_w_out,
             pool_w=pool_w, pool_scale=pool_scale,
             ffn_w_up=ffn_w_up, ffn_conv_w=ffn_conv_w, ffn_conv_b=ffn_conv_b, ffn_w_down=ffn_w_down)
    nbp, nbs = c_prompt.shape[0], c_sample.shape[0]
    c_rows = -(-(nbp + nbs) // (2 * SUBLANE)) * (2 * SUBLANE)
    c_all = jnp.concatenate([c_prompt, c_sample, jnp.zeros((c_rows - nbp - nbs, D_MODEL), F32)], axis=0)
    mods = [_mm(c_all, ada_w[i], a_silu=True, bias=ada_b[i]) for i in range(DEPTH)]
    mods_p = [m[:nbp] for m in mods]
    mods_s = [m[nbp:nbp + nbs] for m in mods]

    y_prompt, sp = _run_trunk(x_prompt, mods_p, 0, None, P)
    past_len = page_table.shape[1] * PAGE_SIZE
    nsa_past = [(_gather_pages(cache_cmp_k[j], page_table), _gather_pages(cache_cmp_v[j], page_table),
                 _gather_pages(cache_slc_k[j], page_table), _gather_pages(cache_slc_v[j], page_table),
                 cache_win_k[j], cache_win_v[j]) for j in range(cache_cmp_k.shape[0])]
    past = dict(ssd_state=state_ssd, ssd_conv=state_ssd_conv, nsa=nsa_past, pool=state_pool, ffn=state_ffn_conv)
    y_sample, ss = _run_trunk(x_sample, mods_s, past_len, past, P)
    return (y_prompt, y_sample,
            sp['ssd_state'], ss['ssd_state'], sp['ssd_conv'], ss['ssd_conv'],
            sp['cmp_k'], ss['cmp_k'], sp['cmp_v'], ss['cmp_v'],
            sp['slc_k'], ss['slc_k'], sp['slc_v'], ss['slc_v'],
            sp['win_k'], ss['win_k'], sp['win_v'], ss['win_v'],
            sp['pool'], ss['pool'], sp['ffn'], ss['ffn'])
```

```python
import functools
import math

import jax
import jax.numpy as jnp
import numpy as np
from jax import lax
from jax.experimental import pallas as pl
from jax.experimental.pallas import tpu as pltpu

D_MODEL = 2048
DEPTH = 4
PAGE_SIZE = 128
N_MIXERS = 3

SSD_D_INNER = 2 * D_MODEL
SSD_HEAD_DIM = 64
SSD_HEADS = SSD_D_INNER // SSD_HEAD_DIM
SSD_GROUPS = 8
SSD_REP = SSD_HEADS // SSD_GROUPS
SSD_STATE = 128
SSD_CONV = 4
SSD_CHUNK = 128
SSD_GROUP_DIM = SSD_REP * SSD_HEAD_DIM
SSD_CONV_DIM = SSD_D_INNER + 2 * SSD_GROUPS * SSD_STATE

NSA_HEADS = 16
NSA_KV_HEADS = 4
NSA_HEAD_DIM = D_MODEL // NSA_HEADS
NSA_REP = NSA_HEADS // NSA_KV_HEADS
NSA_Q_DIM = NSA_HEADS * NSA_HEAD_DIM
NSA_KV_DIM = NSA_KV_HEADS * NSA_HEAD_DIM
CMP_LEN = 32
CMP_STRIDE = 16
CMP_HIDDEN = 256
SEL_LEN = 64
SEL_TOPK = 16
WINDOW = 512
WIN_Q_BLOCK = 128
SEL_Q_BLOCK = 64
ROPE_THETA = 10000.0
FORCE_BONUS = 1000.0

POOL_WINDOWS = (2, 4, 8, 16)
POOL_GROUPS = len(POOL_WINDOWS)
POOL_GROUP_DIM = D_MODEL // POOL_GROUPS
POOL_BUF = max(POOL_WINDOWS) - 1

D_FF = 5632
FFN_CONV = 3

EPS = 1e-6
NEG_INF = -1e30
TINY = 1e-30

V7X_VMEM_LIMIT_BYTES = 52 * 1024 * 1024
LANE = 128
SUBLANE = 8

F32 = jnp.float32
BF16 = jnp.bfloat16
HIGHEST = lax.Precision.HIGHEST

NSA_TQ = 128
NSA_SLC_KC = 256
NSA_WIN_SPAN = WINDOW + NSA_TQ


def _cparams(n_axes):
    return pltpu.CompilerParams(dimension_semantics=("arbitrary",) * n_axes,
                                vmem_limit_bytes=V7X_VMEM_LIMIT_BYTES)


def _pick_tile(dim, pref, align):
    t = min(pref, dim)
    t -= t % align
    while t >= align:
        if dim % t == 0:
            return t
        t -= align
    return dim


def _sigmoid(x):
    return 1.0 / (1.0 + jnp.exp(-x))


def _silu(x):
    return x * _sigmoid(x)


def _dot_nt(a, b):
    return lax.dot_general(a, b, (((1,), (1,)), ((), ())), preferred_element_type=F32)


def _dot_tn(a, b):
    return lax.dot_general(a, b, (((0,), (0,)), ((), ())), preferred_element_type=F32)


def _dot_exact(a, b):
    return jnp.dot(a, b, precision=HIGHEST, preferred_element_type=F32)


def _split3(x):
    hi = x.astype(BF16)
    r = x - hi.astype(F32)
    mid = r.astype(BF16)
    lo = (r - mid.astype(F32)).astype(BF16)
    return hi, mid, lo


def _spread(x, e):
    hi, mid, lo = _split3(x)
    dot = lambda p: jnp.dot(p, e, preferred_element_type=F32)
    return dot(hi) + dot(mid) + dot(lo)


def _spread_rows(e, x):
    hi, mid, lo = _split3(x)
    dot = lambda p: jnp.dot(e, p, preferred_element_type=F32)
    return dot(hi) + dot(mid) + dot(lo)


def _normmod_kernel(x_ref, g_ref, sh_ref, sc_ref, o_ref):
    x = x_ref[0].astype(F32)
    y = x * lax.rsqrt(jnp.mean(x * x, axis=-1, keepdims=True) + EPS) * g_ref[...]
    y = y * (1.0 + sc_ref[0]) + sh_ref[0]
    o_ref[0] = y.astype(o_ref.dtype)


def _normmod(x, g, shift, scale, out_dtype):
    b, t, d = x.shape
    r = shift.shape[1]
    tt = _pick_tile(t, 256, SUBLANE)
    if r == 1:
        mod_spec = pl.BlockSpec((1, 1, d), lambda i, j: (i, 0, 0))
    else:
        mod_spec = pl.BlockSpec((1, tt, d), lambda i, j: (i, j, 0))
    return pl.pallas_call(
        _normmod_kernel,
        grid=(b, t // tt),
        in_specs=[pl.BlockSpec((1, tt, d), lambda i, j: (i, j, 0)),
                  pl.BlockSpec((1, d), lambda i, j: (0, 0)),
                  mod_spec, mod_spec],
        out_specs=pl.BlockSpec((1, tt, d), lambda i, j: (i, j, 0)),
        out_shape=jax.ShapeDtypeStruct((b, t, d), out_dtype),
        compiler_params=_cparams(2),
        name="normmod",
    )(x, g.reshape(1, d), shift, scale)


def _mm_kernel(*refs, a_silu, has_bias, has_res):
    it = iter(refs)
    a_ref, w_ref = next(it), next(it)
    bias_ref = next(it) if has_bias else None
    res_ref = next(it) if has_res else None
    gate_ref = next(it) if has_res else None
    o_ref, wb_ref = next(it), next(it)

    @pl.when(pl.program_id(1) == 0)
    def _():
        wb_ref[...] = w_ref[...].astype(BF16)

    a = a_ref[...]
    if a_silu:
        a = _silu(a.astype(F32))
    acc = jnp.dot(a.astype(BF16), wb_ref[...], preferred_element_type=F32)
    if has_bias:
        acc = acc + bias_ref[...]
    if has_res:
        acc = res_ref[...] + gate_ref[0] * acc
    o_ref[...] = acc.astype(o_ref.dtype)


def _mm(a, w, *, n_cols=None, a_silu=False, bias=None, res=None, gate=None, rows_per_gate=None,
        out_dtype=F32, tm=512, tn=1024):
    m, k = a.shape
    assert w.shape[0] == k
    n = w.shape[1] if n_cols is None else n_cols
    tm = _pick_tile(m if rows_per_gate is None else rows_per_gate, tm, SUBLANE)
    tn = _pick_tile(n, tn, LANE)
    while k * tn * 10 > 30 * 1024 * 1024 and tn % (2 * LANE) == 0:
        tn //= 2
    assert m % tm == 0 and n % tn == 0
    in_specs = [pl.BlockSpec((tm, k), lambda j, i: (i, 0)),
                pl.BlockSpec((k, tn), lambda j, i: (0, j))]
    args = [a, w]
    if bias is not None:
        in_specs.append(pl.BlockSpec((1, tn), lambda j, i: (0, j)))
        args.append(bias.reshape(1, -1))
    if res is not None:
        in_specs.append(pl.BlockSpec((tm, tn), lambda j, i: (i, j)))
        args.append(res)
        if gate.shape[1] == 1:
            tiles_per_gate = rows_per_gate // tm
            in_specs.append(pl.BlockSpec((1, 1, tn), lambda j, i: (i // tiles_per_gate, 0, j)))
        else:
            in_specs.append(pl.BlockSpec((1, tm, tn), lambda j, i: (0, i, j)))
        args.append(gate)
    return pl.pallas_call(
        functools.partial(_mm_kernel, a_silu=a_silu, has_bias=bias is not None, has_res=res is not None),
        grid=(n // tn, m // tm),
        in_specs=in_specs,
        out_specs=pl.BlockSpec((tm, tn), lambda j, i: (i, j)),
        out_shape=jax.ShapeDtypeStruct((m, n), out_dtype),
        scratch_shapes=[pltpu.VMEM((k, tn), BF16)],
        compiler_params=_cparams(2),
        name="mm",
    )(*args)


_CONV_PAD = SUBLANE


def _conv_tile(x, first, buf_ref, w_ref, b_ref, st_ref, xe_ref, width, tt):
    lo = _CONV_PAD - (width - 1)

    @pl.when(first)
    def _():
        xe_ref[lo:_CONV_PAD, :] = buf_ref[0]

    @pl.when(jnp.logical_not(first))
    def _():
        xe_ref[0:_CONV_PAD, :] = xe_ref[tt:tt + _CONV_PAD, :]

    xe_ref[_CONV_PAD:_CONV_PAD + tt, :] = x
    acc = b_ref[...] + w_ref[0:1, :] * xe_ref[lo:lo + tt, :]
    for k in range(1, width):
        acc = acc + w_ref[k:k + 1, :] * xe_ref[lo + k:lo + k + tt, :]
    st_ref[0] = xe_ref[tt + lo:tt + _CONV_PAD, :]
    return acc


def _conv_silu_kernel(x_ref, buf_ref, w_ref, b_ref, y_ref, st_ref, xe_ref, *, width, tt):
    first = pl.program_id(2) == 0
    y_ref[0] = _silu(_conv_tile(x_ref[0], first, buf_ref, w_ref, b_ref, st_ref, xe_ref, width, tt))


def _conv_glu_kernel(xa_ref, xg_ref, bufa_ref, bufg_ref, wa_ref, wg_ref, ba_ref, bg_ref,
                     act_ref, sta_ref, stg_ref, xea_ref, xeg_ref, *, width, tt):
    first = pl.program_id(2) == 0
    a = _conv_tile(xa_ref[0], first, bufa_ref, wa_ref, ba_ref, sta_ref, xea_ref, width, tt)
    g = _conv_tile(xg_ref[0], first, bufg_ref, wg_ref, bg_ref, stg_ref, xeg_ref, width, tt)
    act_ref[0] = (_silu(g) * a).astype(act_ref.dtype)


def _mm_conv_silu_kernel(a_ref, w_ref, buf_ref, cw_ref, cb_ref, y_ref, st_ref, wb_ref, xe_ref,
                         *, width, tm, tiles_per_seq):
    i = pl.program_id(1)

    @pl.when(i == 0)
    def _():
        wb_ref[...] = w_ref[...].astype(BF16)

    u = jnp.dot(a_ref[...], wb_ref[...], preferred_element_type=F32)
    first = i % tiles_per_seq == 0
    y_ref[...] = _silu(_conv_tile(u, first, buf_ref, cw_ref, cb_ref, st_ref, xe_ref, width, tm))


def _mm_conv_glu_kernel(a_ref, wa_ref, wg_ref, bufa_ref, bufg_ref, cwa_ref, cwg_ref, cba_ref, cbg_ref,
                        act_ref, sta_ref, stg_ref, wba_ref, wbg_ref, xea_ref, xeg_ref,
                        *, width, tm, tiles_per_seq):
    i = pl.program_id(1)

    @pl.when(i == 0)
    def _():
        wba_ref[...] = wa_ref[...].astype(BF16)
        wbg_ref[...] = wg_ref[...].astype(BF16)

    a = a_ref[...]
    first = i % tiles_per_seq == 0
    ua = jnp.dot(a, wba_ref[...], preferred_element_type=F32)
    ca = _conv_tile(ua, first, bufa_ref, cwa_ref, cba_ref, sta_ref, xea_ref, width, tm)
    ug = jnp.dot(a, wbg_ref[...], preferred_element_type=F32)
    cg = _conv_tile(ug, first, bufg_ref, cwg_ref, cbg_ref, stg_ref, xeg_ref, width, tm)
    act_ref[...] = (_silu(cg) * ca).astype(act_ref.dtype)


def _mm_conv_silu(a, w, col0, n_ch, seq_len, buf, cw, cbias, *, tm=512, tn=1024):
    m, k = a.shape
    width = cw.shape[0]
    nb = m // seq_len
    tm = _pick_tile(seq_len, tm, 2 * SUBLANE)
    tn = _pick_tile(n_ch, tn, LANE)
    assert col0 % tn == 0
    off = col0 // tn
    tps = seq_len // tm
    return pl.pallas_call(
        functools.partial(_mm_conv_silu_kernel, width=width, tm=tm, tiles_per_seq=tps),
        grid=(n_ch // tn, m // tm),
        in_specs=[pl.BlockSpec((tm, k), lambda j, i: (i, 0)),
                  pl.BlockSpec((k, tn), lambda j, i: (0, j + off)),
                  pl.BlockSpec((1, width - 1, tn), lambda j, i: (i // tps, 0, j)),
                  pl.BlockSpec((width, tn), lambda j, i: (0, j)),
                  pl.BlockSpec((1, tn), lambda j, i: (0, j))],
        out_specs=[pl.BlockSpec((tm, tn), lambda j, i: (i, j)),
                   pl.BlockSpec((1, width - 1, tn), lambda j, i: (i // tps, 0, j))],
        out_shape=[jax.ShapeDtypeStruct((m, n_ch), F32), jax.ShapeDtypeStruct((nb, width - 1, n_ch), F32)],
        scratch_shapes=[pltpu.VMEM((k, tn), BF16), pltpu.VMEM((tm + _CONV_PAD, tn), F32)],
        compiler_params=_cparams(2),
        name="mm_conv_silu",
    )(a, w, buf, cw, cbias.reshape(1, -1))


def _mm_conv_glu(a, w, seq_len, buf, cw, cbias, *, tm=512, tn=512):
    m, k = a.shape
    half = w.shape[1] // 2
    width = cw.shape[0]
    nb = m // seq_len
    tm = _pick_tile(seq_len, tm, 2 * SUBLANE)
    tn = _pick_tile(half, tn, LANE)
    hb = half // tn
    tps = seq_len // tm
    cb2 = cbias.reshape(1, -1)
    lo_hi = lambda shape, index: [pl.BlockSpec(shape, lambda j, i: index(j, i)),
                                  pl.BlockSpec(shape, lambda j, i: index(j + hb, i))]
    st_spec = pl.BlockSpec((1, width - 1, tn), lambda j, i: (i // tps, 0, j))
    act, sta, stg = pl.pallas_call(
        functools.partial(_mm_conv_glu_kernel, width=width, tm=tm, tiles_per_seq=tps),
        grid=(hb, m // tm),
        in_specs=[pl.BlockSpec((tm, k), lambda j, i: (i, 0))]
                 + lo_hi((k, tn), lambda j, i: (0, j))
                 + lo_hi((1, width - 1, tn), lambda j, i: (i // tps, 0, j))
                 + lo_hi((width, tn), lambda j, i: (0, j))
                 + lo_hi((1, tn), lambda j, i: (0, j)),
        out_specs=[pl.BlockSpec((tm, tn), lambda j, i: (i, j)), st_spec, st_spec],
        out_shape=[jax.ShapeDtypeStruct((m, half), BF16),
                   jax.ShapeDtypeStruct((nb, width - 1, half), F32),
                   jax.ShapeDtypeStruct((nb, width - 1, half), F32)],
        scratch_shapes=[pltpu.VMEM((k, tn), BF16), pltpu.VMEM((k, tn), BF16),
                        pltpu.VMEM((tm + _CONV_PAD, tn), F32), pltpu.VMEM((tm + _CONV_PAD, tn), F32)],
        compiler_params=_cparams(2),
        name="mm_conv_glu",
    )(a, w, w, buf, buf, cw, cw, cb2, cb2)
    return act, jnp.concatenate([sta, stg], axis=-1)


def _conv_silu(x, col0, n_ch, buf, w, bias, *, tc=512, tt=512):
    b, t, _ = x.shape
    width = w.shape[0]
    tt = _pick_tile(t, tt, SUBLANE)
    off = col0 // tc
    return pl.pallas_call(
        functools.partial(_conv_silu_kernel, width=width, tt=tt),
        grid=(b, n_ch // tc, t // tt),
        in_specs=[pl.BlockSpec((1, tt, tc), lambda i, j, s: (i, s, j + off)),
                  pl.BlockSpec((1, width - 1, tc), lambda i, j, s: (i, 0, j)),
                  pl.BlockSpec((width, tc), lambda i, j, s: (0, j)),
                  pl.BlockSpec((1, tc), lambda i, j, s: (0, j))],
        out_specs=[pl.BlockSpec((1, tt, tc), lambda i, j, s: (i, s, j)),
                   pl.BlockSpec((1, width - 1, tc), lambda i, j, s: (i, 0, j))],
        out_shape=[jax.ShapeDtypeStruct((b, t, n_ch), F32),
                   jax.ShapeDtypeStruct((b, width - 1, n_ch), F32)],
        scratch_shapes=[pltpu.VMEM((tt + _CONV_PAD, tc), F32)],
        compiler_params=_cparams(3),
        name="conv_silu",
    )(x, buf, w, bias.reshape(1, -1))


def _conv_glu(u, buf, w, bias, *, tc=512, tt=512):
    b, t, c2 = u.shape
    half = c2 // 2
    width = w.shape[0]
    tt = _pick_tile(t, tt, SUBLANE)
    hb = half // tc
    xa = pl.BlockSpec((1, tt, tc), lambda i, j, s: (i, s, j))
    xg = pl.BlockSpec((1, tt, tc), lambda i, j, s: (i, s, j + hb))
    ba = pl.BlockSpec((1, width - 1, tc), lambda i, j, s: (i, 0, j))
    bg = pl.BlockSpec((1, width - 1, tc), lambda i, j, s: (i, 0, j + hb))
    wa = pl.BlockSpec((width, tc), lambda i, j, s: (0, j))
    wg = pl.BlockSpec((width, tc), lambda i, j, s: (0, j + hb))
    ca = pl.BlockSpec((1, tc), lambda i, j, s: (0, j))
    cg = pl.BlockSpec((1, tc), lambda i, j, s: (0, j + hb))
    st = pl.BlockSpec((1, width - 1, tc), lambda i, j, s: (i, 0, j))
    bias2 = bias.reshape(1, -1)
    act, sta, stg = pl.pallas_call(
        functools.partial(_conv_glu_kernel, width=width, tt=tt),
        grid=(b, hb, t // tt),
        in_specs=[xa, xg, ba, bg, wa, wg, ca, cg],
        out_specs=[pl.BlockSpec((1, tt, tc), lambda i, j, s: (i, s, j)), st, st],
        out_shape=[jax.ShapeDtypeStruct((b, t, half), BF16),
                   jax.ShapeDtypeStruct((b, width - 1, half), F32),
                   jax.ShapeDtypeStruct((b, width - 1, half), F32)],
        scratch_shapes=[pltpu.VMEM((tt + _CONV_PAD, tc), F32), pltpu.VMEM((tt + _CONV_PAD, tc), F32)],
        compiler_params=_cparams(3),
        name="conv_glu",
    )(u, u, buf, buf, w, w, bias2, bias2)
    return act, jnp.concatenate([sta, stg], axis=-1)


def _softplus(x):
    return jnp.maximum(x, 0.0) + jnp.log1p(jnp.exp(-jnp.abs(x)))


def _ssd_kernel(x_ref, b_ref, c_ref, dtc_ref, dtr_ref, biasc_ref, biasr_ref, alogc_ref, alogr_ref,
                dch_ref, e512_ref, e1024_ref, et_ref, s0_ref, y_ref, sf_ref, s_ref, *, t_valid):
    L = SSD_CHUNK
    c = pl.program_id(2)

    @pl.when(c == 0)
    def _():
        s_ref[...] = s0_ref[0].reshape(SSD_GROUP_DIM, SSD_STATE)

    x = x_ref[0]
    bm = b_ref[0].astype(BF16)
    cm = c_ref[0].astype(BF16)
    row = lax.broadcasted_iota(jnp.int32, (L, L), 0)
    col = lax.broadcasted_iota(jnp.int32, (L, L), 1)
    causal = row >= col

    dt_c = _softplus(dtc_ref[0, 0] + biasc_ref[0])
    dt_r = _softplus(dtr_ref[0] + biasr_ref[0])
    if t_valid < L:
        dt_c = jnp.where(lax.broadcasted_iota(jnp.int32, dt_c.shape, 0) < t_valid, dt_c, 0.0)
        dt_r = jnp.where(lax.broadcasted_iota(jnp.int32, dt_r.shape, 1) < t_valid, dt_r, 0.0)
    dta_c = dt_c * (-jnp.exp(alogc_ref[0]))
    dta_r = dt_r * (-jnp.exp(alogr_ref[0]))

    tri = jnp.where(causal, 1.0, 0.0).astype(F32)
    upper = jnp.where(row <= col, 1.0, 0.0).astype(F32)
    acs_c = _dot_exact(tri, dta_c)
    acs_r = _dot_exact(dta_r, upper)
    e512 = e512_ref[...]
    acs_x = _spread(acs_c, e512)
    dt_x = _spread(dt_c, e512)
    acs_full = _spread(acs_c, e1024_ref[...])
    alast_x = acs_x[L - 1:L, :]

    xdt = x * dt_x
    xdt_b = xdt.astype(BF16)
    cb = _dot_nt(cm, bm)
    lane = lax.broadcasted_iota(jnp.int32, (L, LANE), 1)
    pairs = []
    for pr in range(SSD_REP // 2):
        xp = xdt_b[:, pr * LANE:(pr + 1) * LANE]
        halves = []
        for r in (2 * pr, 2 * pr + 1):
            seg = acs_full[:, r * LANE:(r + 1) * LANE] - acs_r[r:r + 1, :]
            lm = jnp.exp(jnp.where(causal, seg, NEG_INF))
            halves.append(jnp.dot((cb * lm).astype(BF16), xp, preferred_element_type=F32))
        pairs.append(jnp.where(lane < SSD_HEAD_DIM, halves[0], halves[1]))
    y_diag = jnp.concatenate(pairs, axis=1)

    s_in = s_ref[...]
    y_off = _dot_nt(cm, s_in.astype(BF16)) * jnp.exp(acs_x)
    y_ref[0] = y_diag + y_off + x * dch_ref[...]

    xw = (xdt * jnp.exp(alast_x - acs_x)).astype(BF16)
    alast_r = jnp.sum(dta_r, axis=1, keepdims=True)
    dec = jnp.exp(_spread_rows(et_ref[...], jnp.broadcast_to(alast_r, (SSD_REP, SSD_STATE))))
    s_new = s_in * dec + _dot_tn(xw, bm)
    s_ref[...] = s_new
    sf_ref[0] = s_new.reshape(SSD_REP, SSD_HEAD_DIM, SSD_STATE)


def _ssd_expanders():
    e512 = np.repeat(np.eye(SSD_REP, dtype=np.float32), SSD_HEAD_DIM, axis=1)
    e1024 = np.repeat(np.eye(SSD_REP, dtype=np.float32), LANE, axis=1)
    return jnp.asarray(e512, BF16), jnp.asarray(e1024, BF16), jnp.asarray(e512.T.copy(), BF16)


def _ssd_scan(xbc, dt_raw, dt_bias, a_log, d_skip, s0, t_valid):
    b, t, _ = xbc.shape
    L = SSD_CHUNK
    assert t % L == 0
    g, rep = SSD_GROUPS, SSD_REP
    dtc = dt_raw.reshape(b, t, g, rep).transpose(0, 2, 1, 3)
    dtr = dt_raw.transpose(0, 2, 1)
    bias_c = dt_bias.reshape(g, 1, rep)
    bias_r = jnp.broadcast_to(dt_bias.reshape(g, rep, 1), (g, rep, L))
    alog_c = a_log.reshape(g, 1, rep)
    alog_r = jnp.broadcast_to(a_log.reshape(g, rep, 1), (g, rep, L))
    dch = jnp.repeat(d_skip, SSD_HEAD_DIM).reshape(1, SSD_D_INNER)
    e512, e1024, et = _ssd_expanders()
    xb = SSD_D_INNER // SSD_STATE
    full = lambda shape: pl.BlockSpec(shape, lambda i, j, s: (0,) * len(shape))
    return pl.pallas_call(
        functools.partial(_ssd_kernel, t_valid=t_valid),
        grid=(b, g, t // L),
        in_specs=[pl.BlockSpec((1, L, SSD_GROUP_DIM), lambda i, j, s: (i, s, j)),
                  pl.BlockSpec((1, L, SSD_STATE), lambda i, j, s: (i, s, xb + j)),
                  pl.BlockSpec((1, L, SSD_STATE), lambda i, j, s: (i, s, xb + g + j)),
                  pl.BlockSpec((1, 1, L, rep), lambda i, j, s: (i, j, s, 0)),
                  pl.BlockSpec((1, rep, L), lambda i, j, s: (i, j, s)),
                  pl.BlockSpec((1, 1, rep), lambda i, j, s: (j, 0, 0)),
                  pl.BlockSpec((1, rep, L), lambda i, j, s: (j, 0, 0)),
                  pl.BlockSpec((1, 1, rep), lambda i, j, s: (j, 0, 0)),
                  pl.BlockSpec((1, rep, L), lambda i, j, s: (j, 0, 0)),
                  pl.BlockSpec((1, SSD_GROUP_DIM), lambda i, j, s: (0, j)),
                  full((rep, SSD_GROUP_DIM)), full((rep, rep * LANE)), full((SSD_GROUP_DIM, rep)),
                  pl.BlockSpec((1, rep, SSD_HEAD_DIM, SSD_STATE), lambda i, j, s: (i, j, 0, 0))],
        out_specs=[pl.BlockSpec((1, L, SSD_GROUP_DIM), lambda i, j, s: (i, s, j)),
                   pl.BlockSpec((1, rep, SSD_HEAD_DIM, SSD_STATE), lambda i, j, s: (i, j, 0, 0))],
        out_shape=[jax.ShapeDtypeStruct((b, t, SSD_D_INNER), F32),
                   jax.ShapeDtypeStruct((b, SSD_HEADS, SSD_HEAD_DIM, SSD_STATE), F32)],
        scratch_shapes=[pltpu.VMEM((SSD_GROUP_DIM, SSD_STATE), F32)],
        compiler_params=_cparams(3),
        name="ssd_scan",
    )(xbc, xbc, xbc, dtc, dtr, bias_c, bias_r, alog_c, alog_r, dch, e512, e1024, et, s0)


def _gated_norm_kernel(y_ref, z_ref, g_ref, o_ref):
    v = y_ref[0] * _silu(z_ref[0])
    o = v * lax.rsqrt(jnp.mean(v * v, axis=-1, keepdims=True) + EPS) * g_ref[...]
    o_ref[0] = o.astype(o_ref.dtype)


def _gated_norm(y, zx, g):
    b, t, d = y.shape
    tt = _pick_tile(t, 128, SUBLANE)
    return pl.pallas_call(
        _gated_norm_kernel,
        grid=(b, t // tt),
        in_specs=[pl.BlockSpec((1, tt, d), lambda i, j: (i, j, 0)),
                  pl.BlockSpec((1, tt, d), lambda i, j: (i, j, 0)),
                  pl.BlockSpec((1, d), lambda i, j: (0, 0))],
        out_specs=pl.BlockSpec((1, tt, d), lambda i, j: (i, j, 0)),
        out_shape=jax.ShapeDtypeStruct((b, t, d), BF16),
        compiler_params=_cparams(2),
        name="gated_norm",
    )(y, zx, g.reshape(1, d))


def _rope_tables(pos):
    half = NSA_HEAD_DIM // 2
    inv = jnp.exp(-math.log(ROPE_THETA) * jnp.arange(half, dtype=F32) * 2.0 / NSA_HEAD_DIM)
    ang = pos.astype(F32)[:, None] * inv[None, :]
    cos, sin = jnp.cos(ang), jnp.sin(ang)
    return jnp.concatenate([cos, cos], axis=1), jnp.concatenate([-sin, sin], axis=1)


def _rope_slab(x, cos2, sin2):
    return x * cos2 + pltpu.roll(x, NSA_HEAD_DIM // 2, 1) * sin2


def _rope_kernel(q_ref, kc_ref, ks_ref, vs_ref, kw_ref, vw_ref, cos_ref, sin_ref,
                 qo_ref, kco_ref, kso_ref, kwo_ref, ksb_ref, vsb_ref, kwb_ref, vwb_ref):
    cos2, sin2 = cos_ref[...], sin_ref[...]
    hd = NSA_HEAD_DIM
    scale = hd ** -0.5
    for h in range(NSA_HEADS):
        sl = slice(h * hd, (h + 1) * hd)
        qo_ref[0, :, sl] = (_rope_slab(q_ref[0, :, sl], cos2, sin2) * scale).astype(qo_ref.dtype)
    for h in range(NSA_KV_HEADS):
        sl = slice(h * hd, (h + 1) * hd)
        kco_ref[0, :, sl] = _rope_slab(kc_ref[0, :, sl], cos2, sin2)
        ks = _rope_slab(ks_ref[0, :, sl], cos2, sin2)
        kso_ref[0, :, sl] = ks
        ksb_ref[0, :, sl] = ks.astype(ksb_ref.dtype)
        kw = _rope_slab(kw_ref[0, :, sl], cos2, sin2)
        kwo_ref[0, :, sl] = kw
        kwb_ref[0, :, sl] = kw.astype(kwb_ref.dtype)
    vsb_ref[0] = vs_ref[0].astype(vsb_ref.dtype)
    vwb_ref[0] = vw_ref[0].astype(vwb_ref.dtype)


def _rope_split(proj, pos, lowp=BF16):
    b, t, _ = proj.shape
    tt = _pick_tile(t, 256, SUBLANE)
    cos2, sin2 = _rope_tables(pos)
    kvd = NSA_KV_DIM
    qb = NSA_Q_DIM // kvd
    kv_in = lambda k: pl.BlockSpec((1, tt, kvd), lambda i, j: (i, j, qb + k))
    kv_out = pl.BlockSpec((1, tt, kvd), lambda i, j: (i, j, 0))
    tab = pl.BlockSpec((tt, NSA_HEAD_DIM), lambda i, j: (j, 0))
    sds = lambda n, dt: jax.ShapeDtypeStruct((b, t, n), dt)
    return pl.pallas_call(
        _rope_kernel,
        grid=(b, t // tt),
        in_specs=[pl.BlockSpec((1, tt, NSA_Q_DIM), lambda i, j: (i, j, 0))]
                 + [kv_in(k) for k in (0, 2, 3, 4, 5)] + [tab, tab],
        out_specs=[pl.BlockSpec((1, tt, NSA_Q_DIM), lambda i, j: (i, j, 0))] + [kv_out] * 7,
        out_shape=[sds(NSA_Q_DIM, lowp), sds(kvd, F32), sds(kvd, F32), sds(kvd, F32),
                   sds(kvd, lowp), sds(kvd, lowp), sds(kvd, lowp), sds(kvd, lowp)],
        compiler_params=_cparams(2),
        name="rope_split",
    )(proj, proj, proj, proj, proj, proj, cos2, sin2)


def _cmp_finish_kernel(ab_ref, pe_ref, w2_ref, o_ref):
    ab = ab_ref[0]
    n_sub = ab.shape[0]
    pe_term = pe_ref[0:1, :CMP_HIDDEN] + pe_ref[1:2, CMP_HIDDEN:]
    nxt = pltpu.roll(ab[:, CMP_HIDDEN:], n_sub - 1, 0)
    hid = ab[:, :CMP_HIDDEN] + nxt + pe_term
    act = 0.5 * hid * (1.0 + jnp.tanh(math.sqrt(2.0 / math.pi) * (hid + 0.044715 * hid * hid * hid)))
    o_ref[0] = jnp.dot(act.astype(BF16), w2_ref[...].astype(BF16),
                       preferred_element_type=F32).astype(o_ref.dtype)


def _compress(rows, w1, w2, pe):
    b, t, _ = rows.shape
    n_sub = t // CMP_STRIDE
    ratio = CMP_LEN // CMP_STRIDE
    kdim = CMP_STRIDE * NSA_HEAD_DIM
    x = rows.reshape(b, n_sub, CMP_STRIDE, NSA_KV_HEADS, NSA_HEAD_DIM).transpose(0, 3, 1, 2, 4)
    x = x.reshape(b * NSA_KV_HEADS * n_sub, kdim)
    w1cat = w1.reshape(ratio, kdim, CMP_HIDDEN).transpose(1, 0, 2).reshape(kdim, ratio * CMP_HIDDEN)
    ab = _mm(x, w1cat).reshape(b * NSA_KV_HEADS, n_sub, ratio * CMP_HIDDEN)
    return _cmp_finish(ab, w1cat, w2, pe)


def _cmp_finish(ab, w1cat, w2, pe):
    ratio = CMP_LEN // CMP_STRIDE
    kdim = CMP_STRIDE * NSA_HEAD_DIM
    n_sub = ab.shape[1]
    pe_rows = jnp.zeros((SUBLANE, kdim), F32).at[:ratio].set(pe.reshape(ratio, kdim))
    pe_ab = _mm(pe_rows, w1cat)
    return pl.pallas_call(
        _cmp_finish_kernel,
        grid=(ab.shape[0],),
        in_specs=[pl.BlockSpec((1, n_sub, ratio * CMP_HIDDEN), lambda i: (i, 0, 0)),
                  pl.BlockSpec((SUBLANE, ratio * CMP_HIDDEN), lambda i: (0, 0)),
                  pl.BlockSpec((CMP_HIDDEN, NSA_HEAD_DIM), lambda i: (0, 0))],
        out_specs=pl.BlockSpec((1, n_sub, NSA_HEAD_DIM), lambda i: (i, 0, 0)),
        out_shape=jax.ShapeDtypeStruct((ab.shape[0], n_sub, NSA_HEAD_DIM), BF16),
        compiler_params=_cparams(1),
        name="cmp_finish",
    )(ab, pe_ab, w2)


def _masked_softmax_rows(s, allowed):
    s = jnp.where(allowed, s, NEG_INF)
    m = jnp.max(s, axis=1, keepdims=True)
    p = jnp.where(allowed, jnp.exp(s - m), 0.0)
    return p / jnp.maximum(jnp.sum(p, axis=1, keepdims=True), TINY)


def _nsa_attn_kernel(q_ref, kcmp_ref, vcmp_ref, ks_ref, vs_ref, kw_ref, vw_ref, gp_ref, covt_ref, eg_ref,
                     o_ref, *, n_sel):
    tq, hd, rep = NSA_TQ, NSA_HEAD_DIM, NSA_REP
    rows = rep * tq
    i = pl.program_id(2)
    q0 = i * tq
    q = q_ref[0]
    qs = jnp.concatenate([q[:, r * hd:(r + 1) * hd] for r in range(rep)], axis=0)

    def qpos(shape):
        return q0 + (lax.broadcasted_iota(jnp.int32, shape, 0) & (tq - 1))

    n_cmp_pad = kcmp_ref.shape[1]
    s_c = _dot_nt(qs, kcmp_ref[0])
    blk_end = lax.broadcasted_iota(jnp.int32, (rows, n_cmp_pad), 1) * CMP_STRIDE + (CMP_LEN - 1)
    p_c = _masked_softmax_rows(s_c, blk_end <= qpos((rows, n_cmp_pad)))
    o_cmp = jnp.dot(p_c.astype(BF16), vcmp_ref[0], preferred_element_type=F32)

    p_sum = p_c[0:tq]
    for r in range(1, rep):
        p_sum = p_sum + p_c[r * tq:(r + 1) * tq]
    imp_t = _dot_nt(covt_ref[...], p_sum.astype(BF16))
    nb = 32
    assert n_sel <= nb
    imp_t = imp_t[0:nb]
    jb = lax.broadcasted_iota(jnp.int32, (nb, tq), 0)
    sel_shift = SEL_LEN.bit_length() - 1
    cur = (q0 + lax.broadcasted_iota(jnp.int32, (nb, tq), 1)) >> sel_shift
    forced = (jb == 0) | (jb == cur) | (jb == cur - 1)
    score = jnp.where(jb <= cur, imp_t + jnp.where(forced, FORCE_BONUS, 0.0), -1.0)
    score = jnp.where(jb < n_sel, score, -2.0)
    cnt = jnp.zeros((nb, tq), F32)
    for jp in range(n_sel):
        rowv = score[jp:jp + 1, :]
        before = (rowv > score) | ((rowv == score) & (jb > jp))
        cnt = cnt + jnp.where(before, 1.0, 0.0)
    sel_t = jnp.where(cnt < float(min(SEL_TOPK, n_sel)), 1.0, 0.0)
    sel_t = jnp.concatenate([sel_t, jnp.zeros((LANE - nb, tq), F32)], axis=0)
    sel = sel_t.T.astype(BF16)

    kc = NSA_SLC_KC

    def slc_body(c, carry):
        m, l, acc = carry
        k0 = pl.multiple_of(c * kc, kc)
        kblk = ks_ref[0, pl.ds(k0, kc), :]
        vblk = vs_ref[0, pl.ds(k0, kc), :]
        s = _dot_nt(qs, kblk)
        kidx = k0 + lax.broadcasted_iota(jnp.int32, (LANE, kc), 1)
        expand = jnp.where(lax.broadcasted_iota(jnp.int32, (LANE, kc), 0) == (kidx >> sel_shift), 1.0, 0.0)
        mk = jnp.dot(sel, expand.astype(BF16), preferred_element_type=F32)
        mk = jnp.concatenate([mk] * rep, axis=0)
        kpos = k0 + lax.broadcasted_iota(jnp.int32, (rows, kc), 1)
        allowed = (mk > 0.5) & (kpos <= qpos((rows, kc)))
        s = jnp.where(allowed, s, NEG_INF)
        m_new = jnp.maximum(m, jnp.max(s, axis=1, keepdims=True))
        alpha = jnp.exp(m - m_new)
        p = jnp.where(allowed, jnp.exp(s - m_new), 0.0)
        l = alpha * l + jnp.sum(p, axis=1, keepdims=True)
        acc = alpha * acc + jnp.dot(p.astype(BF16), vblk, preferred_element_type=F32)
        return m_new, l, acc

    n_chunks = (q0 + tq + kc - 1) // kc
    init = (jnp.full((rows, 1), NEG_INF, F32), jnp.zeros((rows, 1), F32), jnp.zeros((rows, hd), F32))
    _, l_s, acc_s = lax.fori_loop(0, n_chunks, slc_body, init)
    o_slc = acc_s / jnp.maximum(l_s, TINY)

    span = NSA_WIN_SPAN
    w0 = pl.multiple_of(jnp.maximum(q0 + tq - span, 0), tq)
    s_w = _dot_nt(qs, kw_ref[0, pl.ds(w0, span), :])
    dist = qpos((rows, span)) - (w0 + lax.broadcasted_iota(jnp.int32, (rows, span), 1))
    p_w = _masked_softmax_rows(s_w, (dist >= 0) & (dist < WINDOW))
    o_win = jnp.dot(p_w.astype(BF16), vw_ref[0, pl.ds(w0, span), :], preferred_element_type=F32)

    gexp = _spread(_sigmoid(gp_ref[0]), eg_ref[0])
    unstack = lambda o: jnp.concatenate([o[r * tq:(r + 1) * tq] for r in range(rep)], axis=1)
    w = rep * hd
    out = gexp[:, 0:w] * unstack(o_cmp) + gexp[:, w:2 * w] * unstack(o_slc) + gexp[:, 2 * w:3 * w] * unstack(o_win)
    o_ref[0] = out.astype(o_ref.dtype)


def _block_coverage(n_cmp, n_sel):
    start = np.arange(n_cmp)[:, None] * CMP_STRIDE
    sel_start = np.arange(n_sel)[None, :] * SEL_LEN
    inter = np.minimum(start + CMP_LEN, sel_start + SEL_LEN) - np.maximum(start, sel_start)
    return (np.clip(inter, 0, None) / CMP_LEN).astype(np.float32)


def _nsa_attention(q, kcmp, vcmp, ks, vs, kw, vw, gate_proj):
    b, t, _ = q.shape
    n_cmp_pad = t // CMP_STRIDE
    n_cmp = (t - CMP_LEN) // CMP_STRIDE + 1
    n_sel = -(-t // SEL_LEN)
    assert n_cmp_pad == LANE and t % NSA_SLC_KC == 0 and t >= NSA_WIN_SPAN
    cov_t = np.zeros((LANE, n_cmp_pad), np.float32)
    cov_t[:n_sel, :n_cmp] = _block_coverage(n_cmp, n_sel).T
    n_gate = gate_proj.shape[-1]
    eg = np.zeros((NSA_KV_HEADS, n_gate, 3 * NSA_REP * NSA_HEAD_DIM), np.float32)
    for g in range(NSA_KV_HEADS):
        for br in range(3):
            for r in range(NSA_REP):
                c0 = (br * NSA_REP + r) * NSA_HEAD_DIM
                eg[g, br * NSA_HEADS + g * NSA_REP + r, c0:c0 + NSA_HEAD_DIM] = 1.0
    gw = NSA_REP * NSA_HEAD_DIM
    seq = pl.BlockSpec((1, t, NSA_HEAD_DIM), lambda i, g, s: (i, 0, g))
    cmp_spec = pl.BlockSpec((1, n_cmp_pad, NSA_HEAD_DIM), lambda i, g, s: (i * NSA_KV_HEADS + g, 0, 0))
    return pl.pallas_call(
        functools.partial(_nsa_attn_kernel, n_sel=n_sel),
        grid=(b, NSA_KV_HEADS, t // NSA_TQ),
        in_specs=[pl.BlockSpec((1, NSA_TQ, gw), lambda i, g, s: (i, s, g)),
                  cmp_spec, cmp_spec, seq, seq, seq, seq,
                  pl.BlockSpec((1, NSA_TQ, n_gate), lambda i, g, s: (i, s, 0)),
                  pl.BlockSpec((LANE, n_cmp_pad), lambda i, g, s: (0, 0)),
                  pl.BlockSpec((1, n_gate, 3 * gw), lambda i, g, s: (g, 0, 0))],
        out_specs=pl.BlockSpec((1, NSA_TQ, gw), lambda i, g, s: (i, s, g)),
        out_shape=jax.ShapeDtypeStruct((b, t, NSA_Q_DIM), BF16),
        compiler_params=_cparams(3),
        name="nsa_attention",
    )(q, kcmp, vcmp, ks, vs, kw, vw, gate_proj, jnp.asarray(cov_t, BF16), jnp.asarray(eg, BF16))


_POOL_PAD = 2 * SUBLANE


def _pool_kernel(h_ref, buf_ref, w_ref, x_ref, gate_ref, o_ref, he_ref, wb_ref, *, tt, pos0):
    t = pl.program_id(1)

    @pl.when((pl.program_id(0) == 0) & (t == 0))
    def _():
        wb_ref[...] = w_ref[...].astype(BF16)

    @pl.when(t == 0)
    def _():
        he_ref[0:_POOL_PAD, :] = buf_ref[0]

    @pl.when(t > 0)
    def _():
        he_ref[0:_POOL_PAD, :] = he_ref[tt:tt + _POOL_PAD, :]

    he_ref[_POOL_PAD:_POOL_PAD + tt, :] = h_ref[0]
    gd = POOL_GROUP_DIM
    q_pos = pos0 + t * tt + lax.broadcasted_iota(jnp.int32, (tt, gd), 0)
    for gi, w in enumerate(POOL_WINDOWS):
        sl = slice(gi * gd, (gi + 1) * gd)
        cur = he_ref[_POOL_PAD:_POOL_PAD + tt, sl]
        win = cur
        for k in range(1, w):
            win = win + he_ref[_POOL_PAD - k:_POOL_PAD - k + tt, sl]
        count = jnp.minimum(w, q_pos + 1).astype(F32)
        mixed = (win / count - cur).astype(BF16)
        acc = jnp.dot(mixed, wb_ref[gi], preferred_element_type=F32)
        o_ref[0, :, sl] = x_ref[0, :, sl] + gate_ref[0, :, sl] * acc


def _pool_mixer(h, buf, pool_w, x, gate, pos0):
    b, t, d = h.shape
    tt = _pick_tile(t, 256, SUBLANE)
    buf16 = jnp.concatenate([jnp.zeros((b, _POOL_PAD - POOL_BUF, d), F32), buf], axis=1)
    if gate.shape[1] == 1:
        gate_spec = pl.BlockSpec((1, 1, d), lambda i, j: (i, 0, 0))
    else:
        gate_spec = pl.BlockSpec((1, tt, d), lambda i, j: (i, j, 0))
    tile = pl.BlockSpec((1, tt, d), lambda i, j: (i, j, 0))
    return pl.pallas_call(
        functools.partial(_pool_kernel, tt=tt, pos0=pos0),
        grid=(b, t // tt),
        in_specs=[tile, pl.BlockSpec((1, _POOL_PAD, d), lambda i, j: (i, 0, 0)),
                  pl.BlockSpec(pool_w.shape, lambda i, j: (0, 0, 0)), tile, gate_spec],
        out_specs=tile,
        out_shape=jax.ShapeDtypeStruct((b, t, d), F32),
        scratch_shapes=[pltpu.VMEM((tt + _POOL_PAD, d), F32), pltpu.VMEM(pool_w.shape, BF16)],
        compiler_params=_cparams(2),
        name="pool_mixer",
    )(h, buf16, pool_w, x, gate)


_CMP_PAGES_PER_STEP = 8


def _cmp_pages_kernel(tbl_ref, *refs):
    del tbl_ref
    npg = _CMP_PAGES_PER_STEP
    n_in = npg * NSA_KV_HEADS
    pages = refs[:n_in]
    w_ref, o_ref, x_ref, wb_ref = refs[n_in:]
    sub_per_page = PAGE_SIZE // CMP_STRIDE

    @pl.when((pl.program_id(0) == 0) & (pl.program_id(1) == 0))
    def _():
        wb_ref[...] = w_ref[...].astype(BF16)

    for k in range(npg):
        for g in range(NSA_KV_HEADS):
            pg = pages[k * NSA_KV_HEADS + g]
            r0 = (g * npg + k) * sub_per_page
            for s in range(CMP_STRIDE):
                x_ref[r0:r0 + sub_per_page, s * NSA_HEAD_DIM:(s + 1) * NSA_HEAD_DIM] = (
                    pg[0, pl.ds(s, sub_per_page, stride=CMP_STRIDE), :])
    ab = jnp.dot(x_ref[...].astype(BF16), wb_ref[...], preferred_element_type=F32)
    o_ref[0] = ab.reshape(NSA_KV_HEADS, npg * sub_per_page, ab.shape[-1])


def _cmp_pages(cache, page_table, w1):
    b, n_pages = page_table.shape
    npg = _CMP_PAGES_PER_STEP
    assert n_pages % npg == 0
    ratio = CMP_LEN // CMP_STRIDE
    kdim = CMP_STRIDE * NSA_HEAD_DIM
    w1cat = w1.reshape(ratio, kdim, CMP_HIDDEN).transpose(1, 0, 2).reshape(kdim, ratio * CMP_HIDDEN)
    sub_per_page = PAGE_SIZE // CMP_STRIDE
    rows = NSA_KV_HEADS * npg * sub_per_page
    page_spec = lambda k, g: pl.BlockSpec((1, PAGE_SIZE, NSA_HEAD_DIM),
                                          lambda i, p, tbl: (tbl[i, p * npg + k], 0, g))
    out = pl.pallas_call(
        _cmp_pages_kernel,
        grid_spec=pltpu.PrefetchScalarGridSpec(
            num_scalar_prefetch=1,
            grid=(b, n_pages // npg),
            in_specs=[page_spec(k, g) for k in range(npg) for g in range(NSA_KV_HEADS)]
                     + [pl.BlockSpec(w1cat.shape, lambda i, p, tbl: (0, 0))],
            out_specs=pl.BlockSpec((1, NSA_KV_HEADS, npg * sub_per_page, ratio * CMP_HIDDEN),
                                   lambda i, p, tbl: (i, 0, p, 0)),
            scratch_shapes=[pltpu.VMEM((rows, kdim), F32), pltpu.VMEM(w1cat.shape, BF16)]),
        out_shape=jax.ShapeDtypeStruct((b, NSA_KV_HEADS, n_pages * sub_per_page, ratio * CMP_HIDDEN), F32),
        compiler_params=_cparams(2),
        name="cmp_pages",
    )(page_table, *([cache] * (npg * NSA_KV_HEADS)), w1cat)
    return out.reshape(b * NSA_KV_HEADS, n_pages * sub_per_page, ratio * CMP_HIDDEN), w1cat


def _joint_softmax(s1, ok1, s2, ok2):
    s1 = jnp.where(ok1, s1, NEG_INF)
    s2 = jnp.where(ok2, s2, NEG_INF)
    m = jnp.maximum(jnp.max(s1, axis=1, keepdims=True), jnp.max(s2, axis=1, keepdims=True))
    p1 = jnp.where(ok1, jnp.exp(s1 - m), 0.0)
    p2 = jnp.where(ok2, jnp.exp(s2 - m), 0.0)
    den = jnp.maximum(jnp.sum(p1, axis=1, keepdims=True) + jnp.sum(p2, axis=1, keepdims=True), TINY)
    return p1 / den, p2 / den


def _dec_cmp_win_kernel(q_ref, kcmp_ref, vcmp_ref, kwin_ref, vwin_ref, kwn_ref, vwn_ref, cov_ref, rsum_ref,
                        ocmp_ref, owin_ref, imp_ref, *, t, n_cmp, pos0, nbuf):
    q = q_ref[0].astype(BF16)
    rows = q.shape[0]
    tok = lax.broadcasted_iota(jnp.int32, (rows, 1), 0) & (t - 1)
    q_pos = pos0 + tok

    n_pad = kcmp_ref.shape[1]
    s_c = _dot_nt(q, kcmp_ref[0])
    nidx = lax.broadcasted_iota(jnp.int32, (rows, n_pad), 1)
    ok = (nidx < n_cmp) & (nidx * CMP_STRIDE + (CMP_LEN - 1) <= q_pos)
    p_c = _masked_softmax_rows(s_c, ok)
    ocmp_ref[0] = jnp.dot(p_c.astype(BF16), vcmp_ref[0], preferred_element_type=F32)
    p_sum = _dot_exact(rsum_ref[...], p_c)
    imp_ref[0] = jnp.dot(p_sum.astype(BF16), cov_ref[...], preferred_element_type=F32)

    s1 = _dot_nt(q, kwin_ref[0].astype(BF16))
    kp1 = (pos0 - nbuf) + lax.broadcasted_iota(jnp.int32, (rows, nbuf), 1)
    d1 = q_pos - kp1
    kn = kwn_ref[0].astype(BF16)
    s2 = _dot_nt(q, kn)
    i2 = lax.broadcasted_iota(jnp.int32, (rows, kn.shape[0]), 1)
    d2 = tok - i2
    p1, p2 = _joint_softmax(s1, (d1 >= 0) & (d1 < WINDOW), s2, (d2 >= 0) & (i2 < t))
    owin_ref[0] = (jnp.dot(p1.astype(BF16), vwin_ref[0].astype(BF16), preferred_element_type=F32)
                   + jnp.dot(p2.astype(BF16), vwn_ref[0].astype(BF16), preferred_element_type=F32))


def _dec_cmp_win(q_stk, kcmp, vcmp, win_k, win_v, kw_new, vw_new, n_cmp, n_sel, pos0, t):
    bg, rows, hd = q_stk.shape
    n_pad = kcmp.shape[1]
    nbuf = win_k.shape[1]
    sel_pad = -(-n_sel // LANE) * LANE
    cov = np.zeros((n_pad, sel_pad), np.float32)
    cov[:n_cmp, :n_sel] = _block_coverage(n_cmp, n_sel)
    rsum = np.zeros((SUBLANE, rows), np.float32)
    for r in range(NSA_REP):
        for tt in range(t):
            rsum[tt, r * t + tt] = 1.0
    per_bg = lambda n: pl.BlockSpec((1, n, hd), lambda i: (i, 0, 0))
    per_b = lambda n: pl.BlockSpec((1, n, hd), lambda i: (i // NSA_KV_HEADS, 0, i % NSA_KV_HEADS))
    return pl.pallas_call(
        functools.partial(_dec_cmp_win_kernel, t=t, n_cmp=n_cmp, pos0=pos0, nbuf=nbuf),
        grid=(bg,),
        in_specs=[per_bg(rows), per_bg(n_pad), per_bg(n_pad), per_b(nbuf), per_b(nbuf),
                  per_b(SUBLANE), per_b(SUBLANE),
                  pl.BlockSpec(cov.shape, lambda i: (0, 0)), pl.BlockSpec(rsum.shape, lambda i: (0, 0))],
        out_specs=[per_bg(rows), per_bg(rows), pl.BlockSpec((1, SUBLANE, sel_pad), lambda i: (i, 0, 0))],
        out_shape=[jax.ShapeDtypeStruct((bg, rows, hd), F32), jax.ShapeDtypeStruct((bg, rows, hd), F32),
                   jax.ShapeDtypeStruct((bg, SUBLANE, sel_pad), F32)],
        compiler_params=_cparams(1),
        name="dec_cmp_win",
    )(q_stk, kcmp, vcmp, win_k, win_v, kw_new, vw_new, jnp.asarray(cov, BF16), jnp.asarray(rsum))


def _dec_topk_kernel(imp_ref, o_ref, *, t, n_sel, pos0, top):
    imp_t = imp_ref[...].T
    shape = imp_t.shape
    jb = lax.broadcasted_iota(jnp.int32, shape, 0)
    q_pos = pos0 + (lax.broadcasted_iota(jnp.int32, shape, 1) & (t - 1))
    cur = q_pos >> (SEL_LEN.bit_length() - 1)
    forced = (jb == 0) | (jb == cur) | (jb == cur - 1)
    score = jnp.where(jb <= cur, imp_t + jnp.where(forced, FORCE_BONUS, 0.0), -1.0)
    score = jnp.where(jb < n_sel, score, -2.0)
    jbf = jb.astype(F32)
    for k in range(top):
        m = jnp.max(score, axis=0, keepdims=True)
        idx = jnp.min(jnp.where(score == m, jbf, float(shape[0])), axis=0, keepdims=True)
        o_ref[k:k + 1, :] = idx.astype(jnp.int32)
        score = jnp.where(jbf == idx, -3.0, score)


def _dec_topk(imp, n_sel, pos0, t):
    slots, sel_pad = imp.shape
    top = min(SEL_TOPK, n_sel)
    return pl.pallas_call(
        functools.partial(_dec_topk_kernel, t=t, n_sel=n_sel, pos0=pos0, top=top),
        grid=(1,),
        in_specs=[pl.BlockSpec((slots, sel_pad), lambda i: (0, 0))],
        out_specs=pl.BlockSpec((top, slots), lambda i: (0, 0)),
        out_shape=jax.ShapeDtypeStruct((top, slots), jnp.int32),
        compiler_params=_cparams(1),
        name="dec_topk",
    )(imp)


def _dec_slc_kernel(sel_ref, tbl_ref, *refs, t, n_past_blk, pos0, top):
    del tbl_ref
    kblk, vblk = refs[:top], refs[top:2 * top]
    q_ref, kn_ref, vn_ref, o_ref, kcat_ref, vcat_ref = refs[2 * top:]
    slot = pl.program_id(0)
    tok = slot & (t - 1)
    q_pos = pos0 + tok
    for k in range(top):
        kcat_ref[k * SEL_LEN:(k + 1) * SEL_LEN, :] = kblk[k][0].astype(BF16)
        vcat_ref[k * SEL_LEN:(k + 1) * SEL_LEN, :] = vblk[k][0].astype(BF16)
    q = q_ref[0].astype(BF16)
    rows = q.shape[0]
    n_keys = top * SEL_LEN
    s1 = _dot_nt(q, kcat_ref[...])
    col = lax.broadcasted_iota(jnp.int32, (rows, n_keys), 1)
    cblk = col >> (SEL_LEN.bit_length() - 1)
    kpos = col & (SEL_LEN - 1)
    in_past = cblk < 0
    new_lim = jnp.int32(-1)
    for k in range(top):
        blk = sel_ref[slot, k]
        kpos = kpos + jnp.where(cblk == k, blk * SEL_LEN, 0)
        in_past = in_past | (cblk == jnp.where(blk < n_past_blk, k, -1))
        new_lim = jnp.where(blk == n_past_blk, tok, new_lim)
    kn = kn_ref[0].astype(BF16)
    s2 = _dot_nt(q, kn)
    i2 = lax.broadcasted_iota(jnp.int32, (rows, kn.shape[0]), 1)
    ok2 = (i2 <= new_lim) & (i2 < t)
    p1, p2 = _joint_softmax(s1, in_past & (kpos <= q_pos), s2, ok2)
    o_ref[0] = (jnp.dot(p1.astype(BF16), vcat_ref[...], preferred_element_type=F32)
                + jnp.dot(p2.astype(BF16), vn_ref[0].astype(BF16), preferred_element_type=F32))


def _dec_slc(sel, page_table, cache_k, cache_v, q_slot, ks_new, vs_new, pos0, t):
    slots, top = sel.shape
    hd = NSA_HEAD_DIM
    n_past_blk = page_table.shape[1] * PAGE_SIZE // SEL_LEN
    halves = PAGE_SIZE // SEL_LEN
    per_b = NSA_KV_HEADS * t

    def blk_spec(k):
        def index(s, sel_ref, tbl_ref):
            blk = jnp.minimum(sel_ref[s, k], n_past_blk - 1)
            page = tbl_ref[s // per_b, blk // halves]
            return page * halves + blk % halves, 0, (s // t) % NSA_KV_HEADS
        return pl.BlockSpec((1, SEL_LEN, hd), index)

    new_spec = pl.BlockSpec((1, SUBLANE, hd), lambda s, a, b_: (s // per_b, 0, (s // t) % NSA_KV_HEADS))
    slot_spec = pl.BlockSpec((1, SUBLANE, hd), lambda s, a, b_: (s, 0, 0))
    return pl.pallas_call(
        functools.partial(_dec_slc_kernel, t=t, n_past_blk=n_past_blk, pos0=pos0, top=top),
        grid_spec=pltpu.PrefetchScalarGridSpec(
            num_scalar_prefetch=2,
            grid=(slots,),
            in_specs=[blk_spec(k) for k in range(top)] + [blk_spec(k) for k in range(top)]
                     + [slot_spec, new_spec, new_spec],
            out_specs=slot_spec,
            scratch_shapes=[pltpu.VMEM((top * SEL_LEN, hd), BF16), pltpu.VMEM((top * SEL_LEN, hd), BF16)]),
        out_shape=jax.ShapeDtypeStruct((slots, SUBLANE, hd), F32),
        compiler_params=_cparams(1),
        name="dec_slc",
    )(sel, page_table, *([cache_k] * top), *([cache_v] * top), q_slot, ks_new, vs_new)


def _dec_combine_kernel(gp_ref, eg_ref, ocmp_ref, oslc_ref, owin_ref, o_ref):
    gexp = _spread(_sigmoid(gp_ref[...]), eg_ref[...])
    w = NSA_Q_DIM
    out = gexp[:, 0:w] * ocmp_ref[...] + gexp[:, w:2 * w] * oslc_ref[...] + gexp[:, 2 * w:3 * w] * owin_ref[...]
    o_ref[...] = out.astype(o_ref.dtype)


def _dec_combine(gate_proj, o_cmp, o_slc, o_win):
    rows, n_gate = gate_proj.shape
    eg = np.zeros((n_gate, 3 * NSA_Q_DIM), np.float32)
    for br in range(3):
        for h in range(NSA_HEADS):
            c0 = br * NSA_Q_DIM + h * NSA_HEAD_DIM
            eg[br * NSA_HEADS + h, c0:c0 + NSA_HEAD_DIM] = 1.0
    full = lambda a: pl.BlockSpec(a.shape, lambda i: (0, 0))
    args = (gate_proj, jnp.asarray(eg, BF16), o_cmp, o_slc, o_win)
    return pl.pallas_call(
        _dec_combine_kernel,
        grid=(1,),
        in_specs=[full(a) for a in args],
        out_specs=pl.BlockSpec((rows, NSA_Q_DIM), lambda i: (0, 0)),
        out_shape=jax.ShapeDtypeStruct((rows, NSA_Q_DIM), BF16),
        compiler_params=_cparams(1),
        name="dec_combine",
    )(*args)


def _nsa_decode(proj, gate_proj, pos0, P, j, caches, page_table):
    b, t, _ = proj.shape
    cache_ck, cache_cv, cache_sk, cache_sv, win_k, win_v = caches
    n_phys = cache_ck.shape[0]
    assert t & (t - 1) == 0 and t <= SUBLANE
    hd, g, rep = NSA_HEAD_DIM, NSA_KV_HEADS, NSA_REP
    pad_t = lambda a: jnp.pad(a, ((0, 0), (0, SUBLANE - t), (0, 0)))
    q, kc, ks, kw, _, _, _, _ = _rope_split(pad_t(proj), pos0 + jnp.arange(SUBLANE), lowp=F32)
    q, kc, ks, kw = q[:, :t], kc[:, :t], ks[:, :t], kw[:, :t]
    vc, vs, vw = [proj[..., NSA_Q_DIM + k * NSA_KV_DIM:NSA_Q_DIM + (k + 1) * NSA_KV_DIM] for k in (1, 3, 5)]

    tk = page_table.shape[1] * PAGE_SIZE + t
    n_cmp = (tk - CMP_LEN) // CMP_STRIDE + 1
    n_sel = -(-tk // SEL_LEN)
    assert (n_cmp + 1) * CMP_STRIDE <= page_table.shape[1] * PAGE_SIZE
    flat = lambda c: c.reshape(n_phys, PAGE_SIZE, NSA_KV_DIM)
    ab_k, w1k = _cmp_pages(flat(cache_ck), page_table, P['nsa_cmpk_w1'][j])
    ab_v, w1v = _cmp_pages(flat(cache_cv), page_table, P['nsa_cmpv_w1'][j])
    kcmp = _cmp_finish(ab_k, w1k, P['nsa_cmpk_w2'][j], P['nsa_cmpk_pe'][j])
    vcmp = _cmp_finish(ab_v, w1v, P['nsa_cmpv_w2'][j], P['nsa_cmpv_pe'][j])

    q5 = q.reshape(b, t, g, rep, hd)
    q_stk = q5.transpose(0, 2, 3, 1, 4).reshape(b * g, rep * t, hd)
    nbuf = win_k.shape[1]
    o_cmp, o_win, imp = _dec_cmp_win(q_stk, kcmp, vcmp, win_k.reshape(b, nbuf, NSA_KV_DIM),
                                     win_v.reshape(b, nbuf, NSA_KV_DIM), pad_t(kw), pad_t(vw),
                                     n_cmp, n_sel, pos0, t)
    sel = _dec_topk(imp[:, :t].reshape(b * g * t, -1), n_sel, pos0, t).T
    q_slot = q5.transpose(0, 2, 1, 3, 4).reshape(b * g * t, rep, hd)
    q_slot = jnp.pad(q_slot, ((0, 0), (0, SUBLANE - rep), (0, 0)))
    half = lambda c: c.reshape(n_phys * (PAGE_SIZE // SEL_LEN), SEL_LEN, NSA_KV_DIM)
    o_slc = _dec_slc(sel, page_table, half(cache_sk), half(cache_sv), q_slot, pad_t(ks), pad_t(vs), pos0, t)

    unstk = lambda o: o.reshape(b, g, rep, t, hd).transpose(0, 3, 1, 2, 4).reshape(b * t, NSA_Q_DIM)
    o_slc = o_slc[:, :rep].reshape(b, g, t, rep, hd).transpose(0, 2, 1, 3, 4).reshape(b * t, NSA_Q_DIM)
    o = _dec_combine(gate_proj.reshape(b * t, -1), unstk(o_cmp), o_slc, unstk(o_win))
    kw_new = jnp.concatenate([win_k.reshape(b, nbuf, NSA_KV_DIM), kw], axis=1)[:, t:]
    vw_new = jnp.concatenate([win_v.reshape(b, nbuf, NSA_KV_DIM), vw], axis=1)[:, t:]
    return o.reshape(b, t, NSA_Q_DIM), kc, vc, ks, vs, kw_new, vw_new


def _run_trunk(x, mods, pos0, past, P):
    b, t, d = x.shape
    fresh = past is None
    per_batch = t % SUBLANE == 0 and t >= LANE
    names = ('ssd_state', 'ssd_conv', 'cmp_k', 'cmp_v', 'slc_k', 'slc_v', 'win_k', 'win_v', 'pool', 'ffn')
    new = {n: [] for n in names}

    bm, tm = (b, t) if per_batch else (1, b * t)

    def mod_rows(v):
        return v[:, None, :] if per_batch else jnp.repeat(v, t, axis=0)[None]

    def mm_rows(a, w, **kw):
        return _mm(a.reshape(b * t, a.shape[-1]), w, **kw)

    def pad_cols(w):
        return jnp.pad(w, ((0, 0), (0, LANE - w.shape[1])))

    def mm_res(a, w, xres, gate):
        out = _mm(a.reshape(b * t, a.shape[-1]), w, res=xres.reshape(b * t, d), gate=gate,
                  rows_per_gate=t if per_batch else None)
        return out.reshape(b, t, d)

    for i in range(DEPTH):
        kind, j = i % N_MIXERS, i // N_MIXERS
        sh1, sc1, g1, sh2, sc2, g2 = [mod_rows(v) for v in jnp.split(mods[i], 6, axis=-1)]
        h_dtype = F32 if kind == 2 else BF16
        h = _normmod(x.reshape(bm, tm, d), P['norm1_g'][i], sh1, sc1, h_dtype).reshape(b, t, d)
        if kind == 0:
            if fresh:
                conv_buf = jnp.zeros((b, SSD_CONV - 1, SSD_CONV_DIM), F32)
                s0 = jnp.zeros((b, SSD_HEADS, SSD_HEAD_DIM, SSD_STATE), F32)
            else:
                conv_buf, s0 = past['ssd_conv'][j], past['ssd_state'][j]
            w_in = P['ssd_w_in'][j]
            nzx = SSD_D_INNER + SSD_CONV_DIM
            dt_raw = mm_rows(h, pad_cols(w_in[:, nzx:])).reshape(b, t, LANE)[..., :SSD_HEADS]
            if per_batch:
                zx = mm_rows(h, w_in, n_cols=SSD_D_INNER).reshape(b, t, SSD_D_INNER)
                xbc, conv_new = _mm_conv_silu(h.reshape(b * t, d), w_in, SSD_D_INNER, SSD_CONV_DIM, t, conv_buf,
                                              P['ssd_conv_w'][j], P['ssd_conv_b'][j])
                xbc = xbc.reshape(b, t, SSD_CONV_DIM)
            else:
                zx = mm_rows(h, w_in, n_cols=nzx).reshape(b, t, nzx)
                xbc, conv_new = _conv_silu(zx, SSD_D_INNER, SSD_CONV_DIM, conv_buf,
                                           P['ssd_conv_w'][j], P['ssd_conv_b'][j])
            tp = -(-t // SSD_CHUNK) * SSD_CHUNK
            if tp != t:
                xbc_p = jnp.pad(xbc, ((0, 0), (0, tp - t), (0, 0)))
                dt_p = jnp.pad(dt_raw, ((0, 0), (0, tp - t), (0, 0)))
            else:
                xbc_p, dt_p = xbc, dt_raw
            y, s_new = _ssd_scan(xbc_p, dt_p, P['ssd_dt_bias'][j], P['ssd_a_log'][j], P['ssd_d'][j], s0,
                                 t_valid=min(t, SSD_CHUNK))
            yn = _gated_norm(y[:, :t], zx, P['ssd_norm_g'][j])
            x = mm_res(yn, P['ssd_w_out'][j], x, g1)
            new['ssd_conv'].append(conv_new)
            new['ssd_state'].append(s_new)
        elif kind == 1:
            w_in = P['nsa_w_in'][j]
            npj = NSA_Q_DIM + 6 * NSA_KV_DIM
            proj = mm_rows(h, w_in, n_cols=npj).reshape(b, t, npj)
            gate_proj = mm_rows(h, pad_cols(w_in[:, npj:])).reshape(b, t, LANE)
            if fresh:
                q_pos = pos0 + jnp.arange(t)
                q, kc, ks, kw, ks_b, vs_b, kw_b, vw_b = _rope_split(proj, q_pos)
                vc, vs, vw = [proj[..., NSA_Q_DIM + k * NSA_KV_DIM:NSA_Q_DIM + (k + 1) * NSA_KV_DIM]
                              for k in (1, 3, 5)]
                kcmp = _compress(kc, P['nsa_cmpk_w1'][j], P['nsa_cmpk_w2'][j], P['nsa_cmpk_pe'][j])
                vcmp = _compress(vc, P['nsa_cmpv_w1'][j], P['nsa_cmpv_w2'][j], P['nsa_cmpv_pe'][j])
                o = _nsa_attention(q, kcmp, vcmp, ks_b, vs_b, kw_b, vw_b, gate_proj)
                keep = min(WINDOW, t)
                kw_new, vw_new = kw[:, t - keep:], vw[:, t - keep:]
            else:
                o, kc, vc, ks, vs, kw_new, vw_new = _nsa_decode(
                    proj, gate_proj, pos0, P, j, past['nsa'][j], past['page_table'])
            x = mm_res(o, P['nsa_w_out'][j], x, g1)
            shp = (b, t, NSA_KV_HEADS, NSA_HEAD_DIM)
            for n, v in (('cmp_k', kc), ('cmp_v', vc), ('slc_k', ks), ('slc_v', vs)):
                new[n].append(v.reshape(shp))
            new['win_k'].append(kw_new.reshape(b, -1, NSA_KV_HEADS, NSA_HEAD_DIM))
            new['win_v'].append(vw_new.reshape(b, -1, NSA_KV_HEADS, NSA_HEAD_DIM))
        else:
            buf = jnp.zeros((b, POOL_BUF, d), F32) if fresh else past['pool'][j]
            gate = (g1 * P['pool_scale'][j]).reshape(b, -1, d)
            x = _pool_mixer(h, buf, P['pool_w'][j], x, gate, pos0)
            new['pool'].append(jnp.concatenate([buf, h], axis=1)[:, -POOL_BUF:])
        h2 = _normmod(x.reshape(bm, tm, d), P['norm2_g'][i], sh2, sc2, BF16)
        fbuf = jnp.zeros((b, FFN_CONV - 1, 2 * D_FF), F32) if fresh else past['ffn'][i]
        if per_batch:
            act, fbuf_new = _mm_conv_glu(h2.reshape(b * t, d), P['ffn_w_up'][i], t, fbuf,
                                         P['ffn_conv_w'][i], P['ffn_conv_b'][i])
        else:
            u = mm_rows(h2, P['ffn_w_up'][i]).reshape(b, t, 2 * D_FF)
            act, fbuf_new = _conv_glu(u, fbuf, P['ffn_conv_w'][i], P['ffn_conv_b'][i])
        x = mm_res(act, P['ffn_w_down'][i], x, g2)
        new['ffn'].append(fbuf_new)
    zero = jnp.zeros((bm, 1, d), F32)
    y = _normmod(x.reshape(bm, tm, d), P['final_g'], zero, zero, F32).reshape(b, t, d)
    return y, {n: jnp.stack(v) for n, v in new.items()}


def kernel(x_prompt, x_sample, state_ssd, state_ssd_conv, cache_cmp_k, cache_cmp_v, cache_slc_k, cache_slc_v,
           cache_win_k, cache_win_v, state_pool, state_ffn_conv, page_table, c_prompt, c_sample,
           ada_w, ada_b, norm1_g, norm2_g, final_g,
           ssd_w_in, ssd_conv_w, ssd_conv_b, ssd_dt_bias, ssd_a_log, ssd_d, ssd_norm_g, ssd_w_out,
           nsa_w_in, nsa_cmpk_w1, nsa_cmpk_w2, nsa_cmpk_pe, nsa_cmpv_w1, nsa_cmpv_w2, nsa_cmpv_pe, nsa_w_out,
           pool_w, pool_scale, ffn_w_up, ffn_conv_w, ffn_conv_b, ffn_w_down):
    P = dict(norm1_g=norm1_g, norm2_g=norm2_g, final_g=final_g,
             ssd_w_in=ssd_w_in, ssd_conv_w=ssd_conv_w, ssd_conv_b=ssd_conv_b, ssd_dt_bias=ssd_dt_bias,
             ssd_a_log=ssd_a_log, ssd_d=ssd_d, ssd_norm_g=ssd_norm_g, ssd_w_out=ssd_w_out,
             nsa_w_in=nsa_w_in, nsa_cmpk_w1=nsa_cmpk_w1, nsa_cmpk_w2=nsa_cmpk_w2, nsa_cmpk_pe=nsa_cmpk_pe,
             nsa_cmpv_w1=nsa_cmpv_w1, nsa_cmpv_w2=nsa_cmpv_w2, nsa_cmpv_pe=nsa_cmpv_pe, nsa_w_out=nsa_w_out,
             pool_w=pool_w, pool_scale=pool_scale,
             ffn_w_up=ffn_w_up, ffn_conv_w=ffn_conv_w, ffn_conv_b=ffn_conv_b, ffn_w_down=ffn_w_down)
    nbp, nbs = c_prompt.shape[0], c_sample.shape[0]
    c_rows = -(-(nbp + nbs) // (2 * SUBLANE)) * (2 * SUBLANE)
    c_all = jnp.concatenate([c_prompt, c_sample, jnp.zeros((c_rows - nbp - nbs, D_MODEL), F32)], axis=0)
    mods = [_mm(c_all, ada_w[i], a_silu=True, bias=ada_b[i]) for i in range(DEPTH)]
    mods_p = [m[:nbp] for m in mods]
    mods_s = [m[nbp:nbp + nbs] for m in mods]

    y_prompt, sp = _run_trunk(x_prompt, mods_p, 0, None, P)
    past_len = page_table.shape[1] * PAGE_SIZE
    nsa_past = [(cache_cmp_k[j], cache_cmp_v[j], cache_slc_k[j], cache_slc_v[j], cache_win_k[j], cache_win_v[j])
                for j in range(cache_cmp_k.shape[0])]
    past = dict(ssd_state=state_ssd, ssd_conv=state_ssd_conv, nsa=nsa_past, page_table=page_table,
                pool=state_pool, ffn=state_ffn_conv)
    y_sample, ss = _run_trunk(x_sample, mods_s, past_len, past, P)
    return (y_prompt, y_sample,
            sp['ssd_state'], ss['ssd_state'], sp['ssd_conv'], ss['ssd_conv'],
            sp['cmp_k'], ss['cmp_k'], sp['cmp_v'], ss['cmp_v'],
            sp['slc_k'], ss['slc_k'], sp['slc_v'], ss['slc_v'],
            sp['win_k'], ss['win_k'], sp['win_v'], ss['win_v'],
            sp['pool'], ss['pool'], sp['ffn'], ss['ffn'])
```

```python
import functools
import math

import jax
import jax.numpy as jnp
import numpy as np
from jax import lax
from jax.experimental import pallas as pl
from jax.experimental.pallas import tpu as pltpu

D_MODEL = 2048
DEPTH = 4
PAGE_SIZE = 128
N_MIXERS = 3

SSD_D_INNER = 2 * D_MODEL
SSD_HEAD_DIM = 64
SSD_HEADS = SSD_D_INNER // SSD_HEAD_DIM
SSD_GROUPS = 8
SSD_REP = SSD_HEADS // SSD_GROUPS
SSD_STATE = 128
SSD_CONV = 4
SSD_CHUNK = 128
SSD_GROUP_DIM = SSD_REP * SSD_HEAD_DIM
SSD_CONV_DIM = SSD_D_INNER + 2 * SSD_GROUPS * SSD_STATE

NSA_HEADS = 16
NSA_KV_HEADS = 4
NSA_HEAD_DIM = D_MODEL // NSA_HEADS
NSA_REP = NSA_HEADS // NSA_KV_HEADS
NSA_Q_DIM = NSA_HEADS * NSA_HEAD_DIM
NSA_KV_DIM = NSA_KV_HEADS * NSA_HEAD_DIM
CMP_LEN = 32
CMP_STRIDE = 16
CMP_HIDDEN = 256
SEL_LEN = 64
SEL_TOPK = 16
WINDOW = 512
WIN_Q_BLOCK = 128
SEL_Q_BLOCK = 64
ROPE_THETA = 10000.0
FORCE_BONUS = 1000.0

POOL_WINDOWS = (2, 4, 8, 16)
POOL_GROUPS = len(POOL_WINDOWS)
POOL_GROUP_DIM = D_MODEL // POOL_GROUPS
POOL_BUF = max(POOL_WINDOWS) - 1

D_FF = 5632
FFN_CONV = 3

EPS = 1e-6
NEG_INF = -1e30
TINY = 1e-30

V7X_VMEM_LIMIT_BYTES = 52 * 1024 * 1024
LANE = 128
SUBLANE = 8

F32 = jnp.float32
BF16 = jnp.bfloat16
HIGHEST = lax.Precision.HIGHEST

NSA_TQ = 128
NSA_SLC_KC = 256
NSA_WIN_SPAN = WINDOW + NSA_TQ


def _cparams(n_axes):
    return pltpu.CompilerParams(dimension_semantics=("arbitrary",) * n_axes,
                                vmem_limit_bytes=V7X_VMEM_LIMIT_BYTES)


def _pick_tile(dim, pref, align):
    t = min(pref, dim)
    t -= t % align
    while t >= align:
        if dim % t == 0:
            return t
        t -= align
    return dim


def _sigmoid(x):
    return 1.0 / (1.0 + jnp.exp(-x))


def _silu(x):
    return x * _sigmoid(x)


def _dot_nt(a, b):
    return lax.dot_general(a, b, (((1,), (1,)), ((), ())), preferred_element_type=F32)


def _dot_tn(a, b):
    return lax.dot_general(a, b, (((0,), (0,)), ((), ())), preferred_element_type=F32)


def _dot_exact(a, b):
    return jnp.dot(a, b, precision=HIGHEST, preferred_element_type=F32)


def _split3(x):
    hi = x.astype(BF16)
    r = x - hi.astype(F32)
    mid = r.astype(BF16)
    lo = (r - mid.astype(F32)).astype(BF16)
    return hi, mid, lo


def _spread(x, e):
    hi, mid, lo = _split3(x)
    dot = lambda p: jnp.dot(p, e, preferred_element_type=F32)
    return dot(hi) + dot(mid) + dot(lo)


def _spread_rows(e, x):
    hi, mid, lo = _split3(x)
    dot = lambda p: jnp.dot(e, p, preferred_element_type=F32)
    return dot(hi) + dot(mid) + dot(lo)


def _normmod_kernel(x_ref, g_ref, sh_ref, sc_ref, o_ref):
    x = x_ref[0].astype(F32)
    y = x * lax.rsqrt(jnp.mean(x * x, axis=-1, keepdims=True) + EPS) * g_ref[...]
    y = y * (1.0 + sc_ref[0]) + sh_ref[0]
    o_ref[0] = y.astype(o_ref.dtype)


def _normmod(x, g, shift, scale, out_dtype):
    b, t, d = x.shape
    r = shift.shape[1]
    tt = _pick_tile(t, 256, SUBLANE)
    if r == 1:
        mod_spec = pl.BlockSpec((1, 1, d), lambda i, j: (i, 0, 0))
    else:
        mod_spec = pl.BlockSpec((1, tt, d), lambda i, j: (i, j, 0))
    return pl.pallas_call(
        _normmod_kernel,
        grid=(b, t // tt),
        in_specs=[pl.BlockSpec((1, tt, d), lambda i, j: (i, j, 0)),
                  pl.BlockSpec((1, d), lambda i, j: (0, 0)),
                  mod_spec, mod_spec],
        out_specs=pl.BlockSpec((1, tt, d), lambda i, j: (i, j, 0)),
        out_shape=jax.ShapeDtypeStruct((b, t, d), out_dtype),
        compiler_params=_cparams(2),
        name="normmod",
    )(x, g.reshape(1, d), shift, scale)


def _mm_kernel(*refs, a_silu, has_bias, has_res):
    it = iter(refs)
    a_ref, w_ref = next(it), next(it)
    bias_ref = next(it) if has_bias else None
    res_ref = next(it) if has_res else None
    gate_ref = next(it) if has_res else None
    o_ref, wb_ref = next(it), next(it)

    @pl.when(pl.program_id(1) == 0)
    def _():
        wb_ref[...] = w_ref[...].astype(BF16)

    a = a_ref[...]
    if a_silu:
        a = _silu(a.astype(F32))
    acc = jnp.dot(a.astype(BF16), wb_ref[...], preferred_element_type=F32)
    if has_bias:
        acc = acc + bias_ref[...]
    if has_res:
        acc = res_ref[...] + gate_ref[0] * acc
    o_ref[...] = acc.astype(o_ref.dtype)


def _w_spec(w, layer, k, tn, col_block):
    if w.ndim == 2:
        return pl.BlockSpec((k, tn), lambda j, i: (0, col_block(j)))
    return pl.BlockSpec((None, k, tn), lambda j, i: (layer, 0, col_block(j)))


def _mm(a, w, *, layer=None, n_cols=None, a_silu=False, bias=None, res=None, gate=None, rows_per_gate=None,
        out_dtype=F32, tm=512, tn=1024):
    m, k = a.shape
    assert w.shape[-2] == k
    n = w.shape[-1] if n_cols is None else n_cols
    tm = _pick_tile(m if rows_per_gate is None else rows_per_gate, tm, SUBLANE)
    tn = _pick_tile(n, tn, LANE)
    while k * tn * 10 > 30 * 1024 * 1024 and tn % (2 * LANE) == 0:
        tn //= 2
    assert m % tm == 0 and n % tn == 0
    in_specs = [pl.BlockSpec((tm, k), lambda j, i: (i, 0)),
                _w_spec(w, layer, k, tn, lambda j: j)]
    args = [a, w]
    if bias is not None:
        in_specs.append(pl.BlockSpec((1, tn), lambda j, i: (0, j)))
        args.append(bias.reshape(1, -1))
    if res is not None:
        in_specs.append(pl.BlockSpec((tm, tn), lambda j, i: (i, j)))
        args.append(res)
        if gate.shape[1] == 1:
            tiles_per_gate = rows_per_gate // tm
            in_specs.append(pl.BlockSpec((1, 1, tn), lambda j, i: (i // tiles_per_gate, 0, j)))
        else:
            in_specs.append(pl.BlockSpec((1, tm, tn), lambda j, i: (0, i, j)))
        args.append(gate)
    return pl.pallas_call(
        functools.partial(_mm_kernel, a_silu=a_silu, has_bias=bias is not None, has_res=res is not None),
        grid=(n // tn, m // tm),
        in_specs=in_specs,
        out_specs=pl.BlockSpec((tm, tn), lambda j, i: (i, j)),
        out_shape=jax.ShapeDtypeStruct((m, n), out_dtype),
        scratch_shapes=[pltpu.VMEM((k, tn), BF16)],
        compiler_params=_cparams(2),
        name="mm",
    )(*args)


_CONV_PAD = SUBLANE


def _conv_tile(x, first, buf_ref, w_ref, b_ref, st_ref, xe_ref, width, tt):
    lo = _CONV_PAD - (width - 1)

    @pl.when(first)
    def _():
        xe_ref[lo:_CONV_PAD, :] = buf_ref[0]

    @pl.when(jnp.logical_not(first))
    def _():
        xe_ref[0:_CONV_PAD, :] = xe_ref[tt:tt + _CONV_PAD, :]

    xe_ref[_CONV_PAD:_CONV_PAD + tt, :] = x
    acc = b_ref[...] + w_ref[0:1, :] * xe_ref[lo:lo + tt, :]
    for k in range(1, width):
        acc = acc + w_ref[k:k + 1, :] * xe_ref[lo + k:lo + k + tt, :]
    st_ref[0] = xe_ref[tt + lo:tt + _CONV_PAD, :]
    return acc


def _conv_silu_kernel(x_ref, buf_ref, w_ref, b_ref, y_ref, st_ref, xe_ref, *, width, tt):
    first = pl.program_id(2) == 0
    y_ref[0] = _silu(_conv_tile(x_ref[0], first, buf_ref, w_ref, b_ref, st_ref, xe_ref, width, tt))


def _conv_glu_kernel(xa_ref, xg_ref, bufa_ref, bufg_ref, wa_ref, wg_ref, ba_ref, bg_ref,
                     act_ref, sta_ref, stg_ref, xea_ref, xeg_ref, *, width, tt):
    first = pl.program_id(2) == 0
    a = _conv_tile(xa_ref[0], first, bufa_ref, wa_ref, ba_ref, sta_ref, xea_ref, width, tt)
    g = _conv_tile(xg_ref[0], first, bufg_ref, wg_ref, bg_ref, stg_ref, xeg_ref, width, tt)
    act_ref[0] = (_silu(g) * a).astype(act_ref.dtype)


def _mm_conv_silu_kernel(a_ref, w_ref, buf_ref, cw_ref, cb_ref, y_ref, st_ref, wb_ref, xe_ref,
                         *, width, tm, tiles_per_seq):
    i = pl.program_id(1)

    @pl.when(i == 0)
    def _():
        wb_ref[...] = w_ref[...].astype(BF16)

    u = jnp.dot(a_ref[...], wb_ref[...], preferred_element_type=F32)
    first = i % tiles_per_seq == 0
    y_ref[...] = _silu(_conv_tile(u, first, buf_ref, cw_ref, cb_ref, st_ref, xe_ref, width, tm))


def _mm_conv_glu_kernel(a_ref, wa_ref, wg_ref, bufa_ref, bufg_ref, cwa_ref, cwg_ref, cba_ref, cbg_ref,
                        act_ref, sta_ref, stg_ref, wba_ref, wbg_ref, xea_ref, xeg_ref,
                        *, width, tm, tiles_per_seq):
    i = pl.program_id(1)

    @pl.when(i == 0)
    def _():
        wba_ref[...] = wa_ref[...].astype(BF16)
        wbg_ref[...] = wg_ref[...].astype(BF16)

    a = a_ref[...]
    first = i % tiles_per_seq == 0
    ua = jnp.dot(a, wba_ref[...], preferred_element_type=F32)
    ca = _conv_tile(ua, first, bufa_ref, cwa_ref, cba_ref, sta_ref, xea_ref, width, tm)
    ug = jnp.dot(a, wbg_ref[...], preferred_element_type=F32)
    cg = _conv_tile(ug, first, bufg_ref, cwg_ref, cbg_ref, stg_ref, xeg_ref, width, tm)
    act_ref[...] = (_silu(cg) * ca).astype(act_ref.dtype)


def _mm_conv_silu(a, w, layer, col0, n_ch, seq_len, buf, cw, cbias, *, tm=512, tn=1024):
    m, k = a.shape
    width = cw.shape[0]
    nb = m // seq_len
    tm = _pick_tile(seq_len, tm, 2 * SUBLANE)
    tn = _pick_tile(n_ch, tn, LANE)
    assert col0 % tn == 0
    off = col0 // tn
    tps = seq_len // tm
    return pl.pallas_call(
        functools.partial(_mm_conv_silu_kernel, width=width, tm=tm, tiles_per_seq=tps),
        grid=(n_ch // tn, m // tm),
        in_specs=[pl.BlockSpec((tm, k), lambda j, i: (i, 0)),
                  _w_spec(w, layer, k, tn, lambda j: j + off),
                  pl.BlockSpec((1, width - 1, tn), lambda j, i: (i // tps, 0, j)),
                  pl.BlockSpec((width, tn), lambda j, i: (0, j)),
                  pl.BlockSpec((1, tn), lambda j, i: (0, j))],
        out_specs=[pl.BlockSpec((tm, tn), lambda j, i: (i, j)),
                   pl.BlockSpec((1, width - 1, tn), lambda j, i: (i // tps, 0, j))],
        out_shape=[jax.ShapeDtypeStruct((m, n_ch), F32), jax.ShapeDtypeStruct((nb, width - 1, n_ch), F32)],
        scratch_shapes=[pltpu.VMEM((k, tn), BF16), pltpu.VMEM((tm + _CONV_PAD, tn), F32)],
        compiler_params=_cparams(2),
        name="mm_conv_silu",
    )(a, w, buf, cw, cbias.reshape(1, -1))


def _mm_conv_glu(a, w, layer, seq_len, buf, cw, cbias, *, tm=512, tn=512):
    m, k = a.shape
    half = w.shape[-1] // 2
    width = cw.shape[0]
    nb = m // seq_len
    tm = _pick_tile(seq_len, tm, 2 * SUBLANE)
    tn = _pick_tile(half, tn, LANE)
    hb = half // tn
    tps = seq_len // tm
    cb2 = cbias.reshape(1, -1)
    lo_hi = lambda shape, index: [pl.BlockSpec(shape, lambda j, i: index(j, i)),
                                  pl.BlockSpec(shape, lambda j, i: index(j + hb, i))]
    st_spec = pl.BlockSpec((1, width - 1, tn), lambda j, i: (i // tps, 0, j))
    act, sta, stg = pl.pallas_call(
        functools.partial(_mm_conv_glu_kernel, width=width, tm=tm, tiles_per_seq=tps),
        grid=(hb, m // tm),
        in_specs=[pl.BlockSpec((tm, k), lambda j, i: (i, 0)),
                  _w_spec(w, layer, k, tn, lambda j: j), _w_spec(w, layer, k, tn, lambda j: j + hb)]
                 + lo_hi((1, width - 1, tn), lambda j, i: (i // tps, 0, j))
                 + lo_hi((width, tn), lambda j, i: (0, j))
                 + lo_hi((1, tn), lambda j, i: (0, j)),
        out_specs=[pl.BlockSpec((tm, tn), lambda j, i: (i, j)), st_spec, st_spec],
        out_shape=[jax.ShapeDtypeStruct((m, half), BF16),
                   jax.ShapeDtypeStruct((nb, width - 1, half), F32),
                   jax.ShapeDtypeStruct((nb, width - 1, half), F32)],
        scratch_shapes=[pltpu.VMEM((k, tn), BF16), pltpu.VMEM((k, tn), BF16),
                        pltpu.VMEM((tm + _CONV_PAD, tn), F32), pltpu.VMEM((tm + _CONV_PAD, tn), F32)],
        compiler_params=_cparams(2),
        name="mm_conv_glu",
    )(a, w, w, buf, buf, cw, cw, cb2, cb2)
    return act, jnp.concatenate([sta, stg], axis=-1)


def _conv_silu(x, col0, n_ch, buf, w, bias, *, tc=512, tt=512):
    b, t, _ = x.shape
    width = w.shape[0]
    tt = _pick_tile(t, tt, SUBLANE)
    off = col0 // tc
    return pl.pallas_call(
        functools.partial(_conv_silu_kernel, width=width, tt=tt),
        grid=(b, n_ch // tc, t // tt),
        in_specs=[pl.BlockSpec((1, tt, tc), lambda i, j, s: (i, s, j + off)),
                  pl.BlockSpec((1, width - 1, tc), lambda i, j, s: (i, 0, j)),
                  pl.BlockSpec((width, tc), lambda i, j, s: (0, j)),
                  pl.BlockSpec((1, tc), lambda i, j, s: (0, j))],
        out_specs=[pl.BlockSpec((1, tt, tc), lambda i, j, s: (i, s, j)),
                   pl.BlockSpec((1, width - 1, tc), lambda i, j, s: (i, 0, j))],
        out_shape=[jax.ShapeDtypeStruct((b, t, n_ch), F32),
                   jax.ShapeDtypeStruct((b, width - 1, n_ch), F32)],
        scratch_shapes=[pltpu.VMEM((tt + _CONV_PAD, tc), F32)],
        compiler_params=_cparams(3),
        name="conv_silu",
    )(x, buf, w, bias.reshape(1, -1))


def _conv_glu(u, buf, w, bias, *, tc=512, tt=512):
    b, t, c2 = u.shape
    half = c2 // 2
    width = w.shape[0]
    tt = _pick_tile(t, tt, SUBLANE)
    hb = half // tc
    xa = pl.BlockSpec((1, tt, tc), lambda i, j, s: (i, s, j))
    xg = pl.BlockSpec((1, tt, tc), lambda i, j, s: (i, s, j + hb))
    ba = pl.BlockSpec((1, width - 1, tc), lambda i, j, s: (i, 0, j))
    bg = pl.BlockSpec((1, width - 1, tc), lambda i, j, s: (i, 0, j + hb))
    wa = pl.BlockSpec((width, tc), lambda i, j, s: (0, j))
    wg = pl.BlockSpec((width, tc), lambda i, j, s: (0, j + hb))
    ca = pl.BlockSpec((1, tc), lambda i, j, s: (0, j))
    cg = pl.BlockSpec((1, tc), lambda i, j, s: (0, j + hb))
    st = pl.BlockSpec((1, width - 1, tc), lambda i, j, s: (i, 0, j))
    bias2 = bias.reshape(1, -1)
    act, sta, stg = pl.pallas_call(
        functools.partial(_conv_glu_kernel, width=width, tt=tt),
        grid=(b, hb, t // tt),
        in_specs=[xa, xg, ba, bg, wa, wg, ca, cg],
        out_specs=[pl.BlockSpec((1, tt, tc), lambda i, j, s: (i, s, j)), st, st],
        out_shape=[jax.ShapeDtypeStruct((b, t, half), BF16),
                   jax.ShapeDtypeStruct((b, width - 1, half), F32),
                   jax.ShapeDtypeStruct((b, width - 1, half), F32)],
        scratch_shapes=[pltpu.VMEM((tt + _CONV_PAD, tc), F32), pltpu.VMEM((tt + _CONV_PAD, tc), F32)],
        compiler_params=_cparams(3),
        name="conv_glu",
    )(u, u, buf, buf, w, w, bias2, bias2)
    return act, jnp.concatenate([sta, stg], axis=-1)


def _softplus(x):
    return jnp.maximum(x, 0.0) + jnp.log1p(jnp.exp(-jnp.abs(x)))


def _ssd_kernel(x_ref, b_ref, c_ref, dtc_ref, dtr_ref, biasc_ref, biasr_ref, alogc_ref, alogr_ref,
                dch_ref, e512_ref, e1024_ref, et_ref, s0_ref, y_ref, sf_ref, s_ref, *, t_valid):
    L = SSD_CHUNK
    c = pl.program_id(2)

    @pl.when(c == 0)
    def _():
        s_ref[...] = s0_ref[0].reshape(SSD_GROUP_DIM, SSD_STATE)

    x = x_ref[0]
    bm = b_ref[0].astype(BF16)
    cm = c_ref[0].astype(BF16)
    row = lax.broadcasted_iota(jnp.int32, (L, L), 0)
    col = lax.broadcasted_iota(jnp.int32, (L, L), 1)
    causal = row >= col

    dt_c = _softplus(dtc_ref[0, 0] + biasc_ref[0])
    dt_r = _softplus(dtr_ref[0] + biasr_ref[0])
    if t_valid < L:
        dt_c = jnp.where(lax.broadcasted_iota(jnp.int32, dt_c.shape, 0) < t_valid, dt_c, 0.0)
        dt_r = jnp.where(lax.broadcasted_iota(jnp.int32, dt_r.shape, 1) < t_valid, dt_r, 0.0)
    dta_c = dt_c * (-jnp.exp(alogc_ref[0]))
    dta_r = dt_r * (-jnp.exp(alogr_ref[0]))

    tri = jnp.where(causal, 1.0, 0.0).astype(F32)
    upper = jnp.where(row <= col, 1.0, 0.0).astype(F32)
    acs_c = _dot_exact(tri, dta_c)
    acs_r = _dot_exact(dta_r, upper)
    e512 = e512_ref[...]
    acs_x = _spread(acs_c, e512)
    dt_x = _spread(dt_c, e512)
    acs_full = _spread(acs_c, e1024_ref[...])
    alast_x = acs_x[L - 1:L, :]

    xdt = x * dt_x
    xdt_b = xdt.astype(BF16)
    cb = _dot_nt(cm, bm)
    lane = lax.broadcasted_iota(jnp.int32, (L, LANE), 1)
    pairs = []
    for pr in range(SSD_REP // 2):
        xp = xdt_b[:, pr * LANE:(pr + 1) * LANE]
        halves = []
        for r in (2 * pr, 2 * pr + 1):
            seg = acs_full[:, r * LANE:(r + 1) * LANE] - acs_r[r:r + 1, :]
            lm = jnp.exp(jnp.where(causal, seg, NEG_INF))
            halves.append(jnp.dot((cb * lm).astype(BF16), xp, preferred_element_type=F32))
        pairs.append(jnp.where(lane < SSD_HEAD_DIM, halves[0], halves[1]))
    y_diag = jnp.concatenate(pairs, axis=1)

    s_in = s_ref[...]
    y_off = _dot_nt(cm, s_in.astype(BF16)) * jnp.exp(acs_x)
    y_ref[0] = y_diag + y_off + x * dch_ref[...]

    xw = (xdt * jnp.exp(alast_x - acs_x)).astype(BF16)
    alast_r = jnp.sum(dta_r, axis=1, keepdims=True)
    dec = jnp.exp(_spread_rows(et_ref[...], jnp.broadcast_to(alast_r, (SSD_REP, SSD_STATE))))
    s_new = s_in * dec + _dot_tn(xw, bm)
    s_ref[...] = s_new
    sf_ref[0] = s_new.reshape(SSD_REP, SSD_HEAD_DIM, SSD_STATE)


def _ssd_expanders():
    e512 = np.repeat(np.eye(SSD_REP, dtype=np.float32), SSD_HEAD_DIM, axis=1)
    e1024 = np.repeat(np.eye(SSD_REP, dtype=np.float32), LANE, axis=1)
    return jnp.asarray(e512, BF16), jnp.asarray(e1024, BF16), jnp.asarray(e512.T.copy(), BF16)


def _ssd_scan(xbc, dt_raw, dt_bias, a_log, d_skip, s0, t_valid):
    b, t, _ = xbc.shape
    L = SSD_CHUNK
    assert t % L == 0
    g, rep = SSD_GROUPS, SSD_REP
    dtc = dt_raw.reshape(b, t, g, rep).transpose(0, 2, 1, 3)
    dtr = dt_raw.transpose(0, 2, 1)
    bias_c = dt_bias.reshape(g, 1, rep)
    bias_r = jnp.broadcast_to(dt_bias.reshape(g, rep, 1), (g, rep, L))
    alog_c = a_log.reshape(g, 1, rep)
    alog_r = jnp.broadcast_to(a_log.reshape(g, rep, 1), (g, rep, L))
    dch = jnp.repeat(d_skip, SSD_HEAD_DIM).reshape(1, SSD_D_INNER)
    e512, e1024, et = _ssd_expanders()
    xb = SSD_D_INNER // SSD_STATE
    full = lambda shape: pl.BlockSpec(shape, lambda i, j, s: (0,) * len(shape))
    return pl.pallas_call(
        functools.partial(_ssd_kernel, t_valid=t_valid),
        grid=(b, g, t // L),
        in_specs=[pl.BlockSpec((1, L, SSD_GROUP_DIM), lambda i, j, s: (i, s, j)),
                  pl.BlockSpec((1, L, SSD_STATE), lambda i, j, s: (i, s, xb + j)),
                  pl.BlockSpec((1, L, SSD_STATE), lambda i, j, s: (i, s, xb + g + j)),
                  pl.BlockSpec((1, 1, L, rep), lambda i, j, s: (i, j, s, 0)),
                  pl.BlockSpec((1, rep, L), lambda i, j, s: (i, j, s)),
                  pl.BlockSpec((1, 1, rep), lambda i, j, s: (j, 0, 0)),
                  pl.BlockSpec((1, rep, L), lambda i, j, s: (j, 0, 0)),
                  pl.BlockSpec((1, 1, rep), lambda i, j, s: (j, 0, 0)),
                  pl.BlockSpec((1, rep, L), lambda i, j, s: (j, 0, 0)),
                  pl.BlockSpec((1, SSD_GROUP_DIM), lambda i, j, s: (0, j)),
                  full((rep, SSD_GROUP_DIM)), full((rep, rep * LANE)), full((SSD_GROUP_DIM, rep)),
                  pl.BlockSpec((1, rep, SSD_HEAD_DIM, SSD_STATE), lambda i, j, s: (i, j, 0, 0))],
        out_specs=[pl.BlockSpec((1, L, SSD_GROUP_DIM), lambda i, j, s: (i, s, j)),
                   pl.BlockSpec((1, rep, SSD_HEAD_DIM, SSD_STATE), lambda i, j, s: (i, j, 0, 0))],
        out_shape=[jax.ShapeDtypeStruct((b, t, SSD_D_INNER), F32),
                   jax.ShapeDtypeStruct((b, SSD_HEADS, SSD_HEAD_DIM, SSD_STATE), F32)],
        scratch_shapes=[pltpu.VMEM((SSD_GROUP_DIM, SSD_STATE), F32)],
        compiler_params=_cparams(3),
        name="ssd_scan",
    )(xbc, xbc, xbc, dtc, dtr, bias_c, bias_r, alog_c, alog_r, dch, e512, e1024, et, s0)


def _gated_norm_kernel(y_ref, z_ref, g_ref, o_ref):
    v = y_ref[0] * _silu(z_ref[0])
    o = v * lax.rsqrt(jnp.mean(v * v, axis=-1, keepdims=True) + EPS) * g_ref[...]
    o_ref[0] = o.astype(o_ref.dtype)


def _gated_norm(y, zx, g):
    b, t, d = y.shape
    tt = _pick_tile(t, 128, SUBLANE)
    return pl.pallas_call(
        _gated_norm_kernel,
        grid=(b, t // tt),
        in_specs=[pl.BlockSpec((1, tt, d), lambda i, j: (i, j, 0)),
                  pl.BlockSpec((1, tt, d), lambda i, j: (i, j, 0)),
                  pl.BlockSpec((1, d), lambda i, j: (0, 0))],
        out_specs=pl.BlockSpec((1, tt, d), lambda i, j: (i, j, 0)),
        out_shape=jax.ShapeDtypeStruct((b, t, d), BF16),
        compiler_params=_cparams(2),
        name="gated_norm",
    )(y, zx, g.reshape(1, d))


def _rope_tables(pos):
    half = NSA_HEAD_DIM // 2
    inv = jnp.exp(-math.log(ROPE_THETA) * jnp.arange(half, dtype=F32) * 2.0 / NSA_HEAD_DIM)
    ang = pos.astype(F32)[:, None] * inv[None, :]
    cos, sin = jnp.cos(ang), jnp.sin(ang)
    return jnp.concatenate([cos, cos], axis=1), jnp.concatenate([-sin, sin], axis=1)


def _rope_slab(x, cos2, sin2):
    return x * cos2 + pltpu.roll(x, NSA_HEAD_DIM // 2, 1) * sin2


def _rope_kernel(q_ref, kc_ref, ks_ref, vs_ref, kw_ref, vw_ref, cos_ref, sin_ref,
                 qo_ref, kco_ref, kso_ref, kwo_ref, ksb_ref, vsb_ref, kwb_ref, vwb_ref):
    cos2, sin2 = cos_ref[...], sin_ref[...]
    hd = NSA_HEAD_DIM
    scale = hd ** -0.5
    for h in range(NSA_HEADS):
        sl = slice(h * hd, (h + 1) * hd)
        qo_ref[0, :, sl] = (_rope_slab(q_ref[0, :, sl], cos2, sin2) * scale).astype(qo_ref.dtype)
    for h in range(NSA_KV_HEADS):
        sl = slice(h * hd, (h + 1) * hd)
        kco_ref[0, :, sl] = _rope_slab(kc_ref[0, :, sl], cos2, sin2)
        ks = _rope_slab(ks_ref[0, :, sl], cos2, sin2)
        kso_ref[0, :, sl] = ks
        ksb_ref[0, :, sl] = ks.astype(ksb_ref.dtype)
        kw = _rope_slab(kw_ref[0, :, sl], cos2, sin2)
        kwo_ref[0, :, sl] = kw
        kwb_ref[0, :, sl] = kw.astype(kwb_ref.dtype)
    vsb_ref[0] = vs_ref[0].astype(vsb_ref.dtype)
    vwb_ref[0] = vw_ref[0].astype(vwb_ref.dtype)


def _rope_split(proj, pos, lowp=BF16):
    b, t, _ = proj.shape
    tt = _pick_tile(t, 256, SUBLANE)
    cos2, sin2 = _rope_tables(pos)
    kvd = NSA_KV_DIM
    qb = NSA_Q_DIM // kvd
    kv_in = lambda k: pl.BlockSpec((1, tt, kvd), lambda i, j: (i, j, qb + k))
    kv_out = pl.BlockSpec((1, tt, kvd), lambda i, j: (i, j, 0))
    tab = pl.BlockSpec((tt, NSA_HEAD_DIM), lambda i, j: (j, 0))
    sds = lambda n, dt: jax.ShapeDtypeStruct((b, t, n), dt)
    return pl.pallas_call(
        _rope_kernel,
        grid=(b, t // tt),
        in_specs=[pl.BlockSpec((1, tt, NSA_Q_DIM), lambda i, j: (i, j, 0))]
                 + [kv_in(k) for k in (0, 2, 3, 4, 5)] + [tab, tab],
        out_specs=[pl.BlockSpec((1, tt, NSA_Q_DIM), lambda i, j: (i, j, 0))] + [kv_out] * 7,
        out_shape=[sds(NSA_Q_DIM, lowp), sds(kvd, F32), sds(kvd, F32), sds(kvd, F32),
                   sds(kvd, lowp), sds(kvd, lowp), sds(kvd, lowp), sds(kvd, lowp)],
        compiler_params=_cparams(2),
        name="rope_split",
    )(proj, proj, proj, proj, proj, proj, cos2, sin2)


def _cmp_finish_kernel(ab_ref, pe_ref, w2_ref, o_ref):
    ab = ab_ref[0]
    n_sub = ab.shape[0]
    pe_term = pe_ref[0:1, :CMP_HIDDEN] + pe_ref[1:2, CMP_HIDDEN:]
    nxt = pltpu.roll(ab[:, CMP_HIDDEN:], n_sub - 1, 0)
    hid = ab[:, :CMP_HIDDEN] + nxt + pe_term
    act = 0.5 * hid * (1.0 + jnp.tanh(math.sqrt(2.0 / math.pi) * (hid + 0.044715 * hid * hid * hid)))
    o_ref[0] = jnp.dot(act.astype(BF16), w2_ref[...].astype(BF16),
                       preferred_element_type=F32).astype(o_ref.dtype)


def _compress(rows, w1, w2, pe):
    b, t, _ = rows.shape
    n_sub = t // CMP_STRIDE
    ratio = CMP_LEN // CMP_STRIDE
    kdim = CMP_STRIDE * NSA_HEAD_DIM
    x = rows.reshape(b, n_sub, CMP_STRIDE, NSA_KV_HEADS, NSA_HEAD_DIM).transpose(0, 3, 1, 2, 4)
    x = x.reshape(b * NSA_KV_HEADS * n_sub, kdim)
    w1cat = w1.reshape(ratio, kdim, CMP_HIDDEN).transpose(1, 0, 2).reshape(kdim, ratio * CMP_HIDDEN)
    ab = _mm(x, w1cat).reshape(b * NSA_KV_HEADS, n_sub, ratio * CMP_HIDDEN)
    return _cmp_finish(ab, w1cat, w2, pe)


def _cmp_finish(ab, w1cat, w2, pe):
    ratio = CMP_LEN // CMP_STRIDE
    kdim = CMP_STRIDE * NSA_HEAD_DIM
    n_sub = ab.shape[1]
    pe_rows = jnp.zeros((SUBLANE, kdim), F32).at[:ratio].set(pe.reshape(ratio, kdim))
    pe_ab = _mm(pe_rows, w1cat)
    return pl.pallas_call(
        _cmp_finish_kernel,
        grid=(ab.shape[0],),
        in_specs=[pl.BlockSpec((1, n_sub, ratio * CMP_HIDDEN), lambda i: (i, 0, 0)),
                  pl.BlockSpec((SUBLANE, ratio * CMP_HIDDEN), lambda i: (0, 0)),
                  pl.BlockSpec((CMP_HIDDEN, NSA_HEAD_DIM), lambda i: (0, 0))],
        out_specs=pl.BlockSpec((1, n_sub, NSA_HEAD_DIM), lambda i: (i, 0, 0)),
        out_shape=jax.ShapeDtypeStruct((ab.shape[0], n_sub, NSA_HEAD_DIM), BF16),
        compiler_params=_cparams(1),
        name="cmp_finish",
    )(ab, pe_ab, w2)


def _masked_softmax_rows(s, allowed):
    s = jnp.where(allowed, s, NEG_INF)
    m = jnp.max(s, axis=1, keepdims=True)
    p = jnp.where(allowed, jnp.exp(s - m), 0.0)
    return p / jnp.maximum(jnp.sum(p, axis=1, keepdims=True), TINY)


def _nsa_attn_kernel(q_ref, kcmp_ref, vcmp_ref, ks_ref, vs_ref, kw_ref, vw_ref, gp_ref, covt_ref, eg_ref,
                     o_ref, *, n_sel):
    tq, hd, rep = NSA_TQ, NSA_HEAD_DIM, NSA_REP
    rows = rep * tq
    i = pl.program_id(2)
    q0 = i * tq
    q = q_ref[0]
    qs = jnp.concatenate([q[:, r * hd:(r + 1) * hd] for r in range(rep)], axis=0)

    def qpos(shape):
        return q0 + (lax.broadcasted_iota(jnp.int32, shape, 0) & (tq - 1))

    n_cmp_pad = kcmp_ref.shape[1]
    s_c = _dot_nt(qs, kcmp_ref[0])
    blk_end = lax.broadcasted_iota(jnp.int32, (rows, n_cmp_pad), 1) * CMP_STRIDE + (CMP_LEN - 1)
    p_c = _masked_softmax_rows(s_c, blk_end <= qpos((rows, n_cmp_pad)))
    o_cmp = jnp.dot(p_c.astype(BF16), vcmp_ref[0], preferred_element_type=F32)

    p_sum = p_c[0:tq]
    for r in range(1, rep):
        p_sum = p_sum + p_c[r * tq:(r + 1) * tq]
    imp_t = _dot_nt(covt_ref[...], p_sum.astype(BF16))
    nb = 32
    assert n_sel <= nb
    imp_t = imp_t[0:nb]
    jb = lax.broadcasted_iota(jnp.int32, (nb, tq), 0)
    sel_shift = SEL_LEN.bit_length() - 1
    cur = (q0 + lax.broadcasted_iota(jnp.int32, (nb, tq), 1)) >> sel_shift
    forced = (jb == 0) | (jb == cur) | (jb == cur - 1)
    score = jnp.where(jb <= cur, imp_t + jnp.where(forced, FORCE_BONUS, 0.0), -1.0)
    score = jnp.where(jb < n_sel, score, -2.0)
    cnt = jnp.zeros((nb, tq), F32)
    for jp in range(n_sel):
        rowv = score[jp:jp + 1, :]
        before = (rowv > score) | ((rowv == score) & (jb > jp))
        cnt = cnt + jnp.where(before, 1.0, 0.0)
    sel_t = jnp.where(cnt < float(min(SEL_TOPK, n_sel)), 1.0, 0.0)
    sel_t = jnp.concatenate([sel_t, jnp.zeros((LANE - nb, tq), F32)], axis=0)
    sel = sel_t.T.astype(BF16)

    kc = NSA_SLC_KC

    def slc_body(c, carry):
        m, l, acc = carry
        k0 = pl.multiple_of(c * kc, kc)
        kblk = ks_ref[0, pl.ds(k0, kc), :]
        vblk = vs_ref[0, pl.ds(k0, kc), :]
        s = _dot_nt(qs, kblk)
        kidx = k0 + lax.broadcasted_iota(jnp.int32, (LANE, kc), 1)
        expand = jnp.where(lax.broadcasted_iota(jnp.int32, (LANE, kc), 0) == (kidx >> sel_shift), 1.0, 0.0)
        mk = jnp.dot(sel, expand.astype(BF16), preferred_element_type=F32)
        mk = jnp.concatenate([mk] * rep, axis=0)
        kpos = k0 + lax.broadcasted_iota(jnp.int32, (rows, kc), 1)
        allowed = (mk > 0.5) & (kpos <= qpos((rows, kc)))
        s = jnp.where(allowed, s, NEG_INF)
        m_new = jnp.maximum(m, jnp.max(s, axis=1, keepdims=True))
        alpha = jnp.exp(m - m_new)
        p = jnp.where(allowed, jnp.exp(s - m_new), 0.0)
        l = alpha * l + jnp.sum(p, axis=1, keepdims=True)
        acc = alpha * acc + jnp.dot(p.astype(BF16), vblk, preferred_element_type=F32)
        return m_new, l, acc

    n_chunks = (q0 + tq + kc - 1) // kc
    init = (jnp.full((rows, 1), NEG_INF, F32), jnp.zeros((rows, 1), F32), jnp.zeros((rows, hd), F32))
    _, l_s, acc_s = lax.fori_loop(0, n_chunks, slc_body, init)
    o_slc = acc_s / jnp.maximum(l_s, TINY)

    span = NSA_WIN_SPAN
    w0 = pl.multiple_of(jnp.maximum(q0 + tq - span, 0), tq)
    s_w = _dot_nt(qs, kw_ref[0, pl.ds(w0, span), :])
    dist = qpos((rows, span)) - (w0 + lax.broadcasted_iota(jnp.int32, (rows, span), 1))
    p_w = _masked_softmax_rows(s_w, (dist >= 0) & (dist < WINDOW))
    o_win = jnp.dot(p_w.astype(BF16), vw_ref[0, pl.ds(w0, span), :], preferred_element_type=F32)

    gexp = _spread(_sigmoid(gp_ref[0]), eg_ref[0])
    unstack = lambda o: jnp.concatenate([o[r * tq:(r + 1) * tq] for r in range(rep)], axis=1)
    w = rep * hd
    out = gexp[:, 0:w] * unstack(o_cmp) + gexp[:, w:2 * w] * unstack(o_slc) + gexp[:, 2 * w:3 * w] * unstack(o_win)
    o_ref[0] = out.astype(o_ref.dtype)


def _block_coverage(n_cmp, n_sel):
    start = np.arange(n_cmp)[:, None] * CMP_STRIDE
    sel_start = np.arange(n_sel)[None, :] * SEL_LEN
    inter = np.minimum(start + CMP_LEN, sel_start + SEL_LEN) - np.maximum(start, sel_start)
    return (np.clip(inter, 0, None) / CMP_LEN).astype(np.float32)


def _nsa_attention(q, kcmp, vcmp, ks, vs, kw, vw, gate_proj):
    b, t, _ = q.shape
    n_cmp_pad = t // CMP_STRIDE
    n_cmp = (t - CMP_LEN) // CMP_STRIDE + 1
    n_sel = -(-t // SEL_LEN)
    assert n_cmp_pad == LANE and t % NSA_SLC_KC == 0 and t >= NSA_WIN_SPAN
    cov_t = np.zeros((LANE, n_cmp_pad), np.float32)
    cov_t[:n_sel, :n_cmp] = _block_coverage(n_cmp, n_sel).T
    n_gate = gate_proj.shape[-1]
    eg = np.zeros((NSA_KV_HEADS, n_gate, 3 * NSA_REP * NSA_HEAD_DIM), np.float32)
    for g in range(NSA_KV_HEADS):
        for br in range(3):
            for r in range(NSA_REP):
                c0 = (br * NSA_REP + r) * NSA_HEAD_DIM
                eg[g, br * NSA_HEADS + g * NSA_REP + r, c0:c0 + NSA_HEAD_DIM] = 1.0
    gw = NSA_REP * NSA_HEAD_DIM
    seq = pl.BlockSpec((1, t, NSA_HEAD_DIM), lambda i, g, s: (i, 0, g))
    cmp_spec = pl.BlockSpec((1, n_cmp_pad, NSA_HEAD_DIM), lambda i, g, s: (i * NSA_KV_HEADS + g, 0, 0))
    return pl.pallas_call(
        functools.partial(_nsa_attn_kernel, n_sel=n_sel),
        grid=(b, NSA_KV_HEADS, t // NSA_TQ),
        in_specs=[pl.BlockSpec((1, NSA_TQ, gw), lambda i, g, s: (i, s, g)),
                  cmp_spec, cmp_spec, seq, seq, seq, seq,
                  pl.BlockSpec((1, NSA_TQ, n_gate), lambda i, g, s: (i, s, 0)),
                  pl.BlockSpec((LANE, n_cmp_pad), lambda i, g, s: (0, 0)),
                  pl.BlockSpec((1, n_gate, 3 * gw), lambda i, g, s: (g, 0, 0))],
        out_specs=pl.BlockSpec((1, NSA_TQ, gw), lambda i, g, s: (i, s, g)),
        out_shape=jax.ShapeDtypeStruct((b, t, NSA_Q_DIM), BF16),
        compiler_params=_cparams(3),
        name="nsa_attention",
    )(q, kcmp, vcmp, ks, vs, kw, vw, gate_proj, jnp.asarray(cov_t, BF16), jnp.asarray(eg, BF16))


_POOL_PAD = 2 * SUBLANE


def _pool_kernel(h_ref, buf_ref, w_ref, x_ref, gate_ref, o_ref, he_ref, wb_ref, *, tt, pos0):
    t = pl.program_id(1)

    @pl.when((pl.program_id(0) == 0) & (t == 0))
    def _():
        wb_ref[...] = w_ref[...].astype(BF16)

    @pl.when(t == 0)
    def _():
        he_ref[0:_POOL_PAD, :] = buf_ref[0]

    @pl.when(t > 0)
    def _():
        he_ref[0:_POOL_PAD, :] = he_ref[tt:tt + _POOL_PAD, :]

    he_ref[_POOL_PAD:_POOL_PAD + tt, :] = h_ref[0]
    gd = POOL_GROUP_DIM
    q_pos = pos0 + t * tt + lax.broadcasted_iota(jnp.int32, (tt, gd), 0)
    for gi, w in enumerate(POOL_WINDOWS):
        sl = slice(gi * gd, (gi + 1) * gd)
        cur = he_ref[_POOL_PAD:_POOL_PAD + tt, sl]
        win = cur
        for k in range(1, w):
            win = win + he_ref[_POOL_PAD - k:_POOL_PAD - k + tt, sl]
        count = jnp.minimum(w, q_pos + 1).astype(F32)
        mixed = (win / count - cur).astype(BF16)
        acc = jnp.dot(mixed, wb_ref[gi], preferred_element_type=F32)
        o_ref[0, :, sl] = x_ref[0, :, sl] + gate_ref[0, :, sl] * acc


def _pool_mixer(h, buf, pool_w, x, gate, pos0):
    b, t, d = h.shape
    tt = _pick_tile(t, 256, SUBLANE)
    buf16 = jnp.concatenate([jnp.zeros((b, _POOL_PAD - POOL_BUF, d), F32), buf], axis=1)
    if gate.shape[1] == 1:
        gate_spec = pl.BlockSpec((1, 1, d), lambda i, j: (i, 0, 0))
    else:
        gate_spec = pl.BlockSpec((1, tt, d), lambda i, j: (i, j, 0))
    tile = pl.BlockSpec((1, tt, d), lambda i, j: (i, j, 0))
    return pl.pallas_call(
        functools.partial(_pool_kernel, tt=tt, pos0=pos0),
        grid=(b, t // tt),
        in_specs=[tile, pl.BlockSpec((1, _POOL_PAD, d), lambda i, j: (i, 0, 0)),
                  pl.BlockSpec(pool_w.shape, lambda i, j: (0, 0, 0)), tile, gate_spec],
        out_specs=tile,
        out_shape=jax.ShapeDtypeStruct((b, t, d), F32),
        scratch_shapes=[pltpu.VMEM((tt + _POOL_PAD, d), F32), pltpu.VMEM(pool_w.shape, BF16)],
        compiler_params=_cparams(2),
        name="pool_mixer",
    )(h, buf16, pool_w, x, gate)


_CMP_PAGES_PER_STEP = 8


def _cmp_pages_kernel(tbl_ref, *refs):
    del tbl_ref
    npg = _CMP_PAGES_PER_STEP
    pages = refs[:npg]
    w_ref, o_ref, x_ref, wb_ref = refs[npg:]
    sub_per_page = PAGE_SIZE // CMP_STRIDE

    @pl.when((pl.program_id(0) == 0) & (pl.program_id(1) == 0))
    def _():
        wb_ref[...] = w_ref[...].astype(BF16)

    for k in range(npg):
        for g in range(NSA_KV_HEADS):
            r0 = (g * npg + k) * sub_per_page
            for s in range(CMP_STRIDE):
                x_ref[r0:r0 + sub_per_page, s * NSA_HEAD_DIM:(s + 1) * NSA_HEAD_DIM] = (
                    pages[k][0, pl.ds(s, sub_per_page, stride=CMP_STRIDE), g, :])
    ab = jnp.dot(x_ref[...].astype(BF16), wb_ref[...], preferred_element_type=F32)
    o_ref[0] = ab.reshape(NSA_KV_HEADS, npg * sub_per_page, ab.shape[-1])


def _cmp_pages(cache, layer, page_table, w1):
    b, n_pages = page_table.shape
    npg = _CMP_PAGES_PER_STEP
    assert n_pages % npg == 0
    ratio = CMP_LEN // CMP_STRIDE
    kdim = CMP_STRIDE * NSA_HEAD_DIM
    w1cat = w1.reshape(ratio, kdim, CMP_HIDDEN).transpose(1, 0, 2).reshape(kdim, ratio * CMP_HIDDEN)
    sub_per_page = PAGE_SIZE // CMP_STRIDE
    rows = NSA_KV_HEADS * npg * sub_per_page
    page_spec = lambda k: pl.BlockSpec((None, 1, PAGE_SIZE, NSA_KV_HEADS, NSA_HEAD_DIM),
                                       lambda i, p, tbl: (layer, tbl[i, p * npg + k], 0, 0, 0))
    out = pl.pallas_call(
        _cmp_pages_kernel,
        grid_spec=pltpu.PrefetchScalarGridSpec(
            num_scalar_prefetch=1,
            grid=(b, n_pages // npg),
            in_specs=[page_spec(k) for k in range(npg)]
                     + [pl.BlockSpec(w1cat.shape, lambda i, p, tbl: (0, 0))],
            out_specs=pl.BlockSpec((1, NSA_KV_HEADS, npg * sub_per_page, ratio * CMP_HIDDEN),
                                   lambda i, p, tbl: (i, 0, p, 0)),
            scratch_shapes=[pltpu.VMEM((rows, kdim), F32), pltpu.VMEM(w1cat.shape, BF16)]),
        out_shape=jax.ShapeDtypeStruct((b, NSA_KV_HEADS, n_pages * sub_per_page, ratio * CMP_HIDDEN), F32),
        compiler_params=_cparams(2),
        name="cmp_pages",
    )(page_table, *([cache] * npg), w1cat)
    return out.reshape(b * NSA_KV_HEADS, n_pages * sub_per_page, ratio * CMP_HIDDEN), w1cat


def _joint_softmax(s1, ok1, s2, ok2):
    s1 = jnp.where(ok1, s1, NEG_INF)
    s2 = jnp.where(ok2, s2, NEG_INF)
    m = jnp.maximum(jnp.max(s1, axis=1, keepdims=True), jnp.max(s2, axis=1, keepdims=True))
    p1 = jnp.where(ok1, jnp.exp(s1 - m), 0.0)
    p2 = jnp.where(ok2, jnp.exp(s2 - m), 0.0)
    den = jnp.maximum(jnp.sum(p1, axis=1, keepdims=True) + jnp.sum(p2, axis=1, keepdims=True), TINY)
    return p1 / den, p2 / den


def _dec_cmp_win_kernel(q_ref, kcmp_ref, vcmp_ref, kwin_ref, vwin_ref, kwn_ref, vwn_ref, cov_ref, rsum_ref,
                        ocmp_ref, owin_ref, imp_ref, *, t, n_cmp, pos0, nbuf):
    q = q_ref[0].astype(BF16)
    rows = q.shape[0]
    tok = lax.broadcasted_iota(jnp.int32, (rows, 1), 0) & (t - 1)
    q_pos = pos0 + tok

    n_pad = kcmp_ref.shape[1]
    s_c = _dot_nt(q, kcmp_ref[0])
    nidx = lax.broadcasted_iota(jnp.int32, (rows, n_pad), 1)
    ok = (nidx < n_cmp) & (nidx * CMP_STRIDE + (CMP_LEN - 1) <= q_pos)
    p_c = _masked_softmax_rows(s_c, ok)
    ocmp_ref[0] = jnp.dot(p_c.astype(BF16), vcmp_ref[0], preferred_element_type=F32)
    p_sum = _dot_exact(rsum_ref[...], p_c)
    imp_ref[0] = jnp.dot(p_sum.astype(BF16), cov_ref[...], preferred_element_type=F32)

    s1 = _dot_nt(q, kwin_ref[0].astype(BF16))
    kp1 = (pos0 - nbuf) + lax.broadcasted_iota(jnp.int32, (rows, nbuf), 1)
    d1 = q_pos - kp1
    kn = kwn_ref[0].astype(BF16)
    s2 = _dot_nt(q, kn)
    i2 = lax.broadcasted_iota(jnp.int32, (rows, kn.shape[0]), 1)
    d2 = tok - i2
    p1, p2 = _joint_softmax(s1, (d1 >= 0) & (d1 < WINDOW), s2, (d2 >= 0) & (i2 < t))
    owin_ref[0] = (jnp.dot(p1.astype(BF16), vwin_ref[0].astype(BF16), preferred_element_type=F32)
                   + jnp.dot(p2.astype(BF16), vwn_ref[0].astype(BF16), preferred_element_type=F32))


def _dec_cmp_win(q_stk, kcmp, vcmp, win_k, win_v, layer, kw_new, vw_new, n_cmp, n_sel, pos0, t):
    bg, rows, hd = q_stk.shape
    n_pad = kcmp.shape[1]
    nbuf = win_k.shape[2]
    sel_pad = -(-n_sel // LANE) * LANE
    cov = np.zeros((n_pad, sel_pad), np.float32)
    cov[:n_cmp, :n_sel] = _block_coverage(n_cmp, n_sel)
    rsum = np.zeros((SUBLANE, rows), np.float32)
    for r in range(NSA_REP):
        for tt in range(t):
            rsum[tt, r * t + tt] = 1.0
    per_bg = lambda n: pl.BlockSpec((1, n, hd), lambda i: (i, 0, 0))
    per_b = lambda n: pl.BlockSpec((1, n, hd), lambda i: (i // NSA_KV_HEADS, 0, i % NSA_KV_HEADS))
    win_spec = per_b(nbuf)
    win_k, win_v = [w[layer].reshape(w.shape[1], nbuf, NSA_KV_DIM) for w in (win_k, win_v)]
    return pl.pallas_call(
        functools.partial(_dec_cmp_win_kernel, t=t, n_cmp=n_cmp, pos0=pos0, nbuf=nbuf),
        grid=(bg,),
        in_specs=[per_bg(rows), per_bg(n_pad), per_bg(n_pad), win_spec, win_spec,
                  per_b(SUBLANE), per_b(SUBLANE),
                  pl.BlockSpec(cov.shape, lambda i: (0, 0)), pl.BlockSpec(rsum.shape, lambda i: (0, 0))],
        out_specs=[per_bg(rows), per_bg(rows), pl.BlockSpec((1, SUBLANE, sel_pad), lambda i: (i, 0, 0))],
        out_shape=[jax.ShapeDtypeStruct((bg, rows, hd), F32), jax.ShapeDtypeStruct((bg, rows, hd), F32),
                   jax.ShapeDtypeStruct((bg, SUBLANE, sel_pad), F32)],
        compiler_params=_cparams(1),
        name="dec_cmp_win",
    )(q_stk, kcmp, vcmp, win_k, win_v, kw_new, vw_new, jnp.asarray(cov, BF16), jnp.asarray(rsum))


def _dec_topk_kernel(imp_ref, o_ref, *, t, n_sel, pos0, top):
    imp_t = imp_ref[...].T
    shape = imp_t.shape
    jb = lax.broadcasted_iota(jnp.int32, shape, 0)
    q_pos = pos0 + (lax.broadcasted_iota(jnp.int32, shape, 1) & (t - 1))
    cur = q_pos >> (SEL_LEN.bit_length() - 1)
    forced = (jb == 0) | (jb == cur) | (jb == cur - 1)
    score = jnp.where(jb <= cur, imp_t + jnp.where(forced, FORCE_BONUS, 0.0), -1.0)
    score = jnp.where(jb < n_sel, score, -2.0)
    jbf = jb.astype(F32)
    for k in range(top):
        m = jnp.max(score, axis=0, keepdims=True)
        idx = jnp.min(jnp.where(score == m, jbf, float(shape[0])), axis=0, keepdims=True)
        o_ref[k:k + 1, :] = idx.astype(jnp.int32)
        score = jnp.where(jbf == idx, -3.0, score)


def _dec_topk(imp, n_sel, pos0, t):
    slots, sel_pad = imp.shape
    top = min(SEL_TOPK, n_sel)
    return pl.pallas_call(
        functools.partial(_dec_topk_kernel, t=t, n_sel=n_sel, pos0=pos0, top=top),
        grid=(1,),
        in_specs=[pl.BlockSpec((slots, sel_pad), lambda i: (0, 0))],
        out_specs=pl.BlockSpec((top, slots), lambda i: (0, 0)),
        out_shape=jax.ShapeDtypeStruct((top, slots), jnp.int32),
        compiler_params=_cparams(1),
        name="dec_topk",
    )(imp)


def _dec_slc_kernel(sel_ref, tbl_ref, *refs, t, n_past_blk, pos0, top):
    del tbl_ref
    kblk, vblk = refs[:top], refs[top:2 * top]
    q_ref, kn_ref, vn_ref, o_ref, kcat_ref, vcat_ref = refs[2 * top:]
    slot = pl.program_id(0)
    tok = slot & (t - 1)
    q_pos = pos0 + tok
    head = (slot // t) % NSA_KV_HEADS
    for g in range(NSA_KV_HEADS):
        @pl.when(head == g)
        def _(g=g):
            for k in range(top):
                kcat_ref[k * SEL_LEN:(k + 1) * SEL_LEN, :] = kblk[k][0, :, g, :].astype(BF16)
                vcat_ref[k * SEL_LEN:(k + 1) * SEL_LEN, :] = vblk[k][0, :, g, :].astype(BF16)
    q = q_ref[0].astype(BF16)
    rows = q.shape[0]
    n_keys = top * SEL_LEN
    s1 = _dot_nt(q, kcat_ref[...])
    col = lax.broadcasted_iota(jnp.int32, (rows, n_keys), 1)
    cblk = col >> (SEL_LEN.bit_length() - 1)
    kpos = col & (SEL_LEN - 1)
    in_past = cblk < 0
    new_lim = jnp.int32(-1)
    for k in range(top):
        blk = sel_ref[slot, k]
        kpos = kpos + jnp.where(cblk == k, blk * SEL_LEN, 0)
        in_past = in_past | (cblk == jnp.where(blk < n_past_blk, k, -1))
        new_lim = jnp.where(blk == n_past_blk, tok, new_lim)
    kn = kn_ref[0].astype(BF16)
    s2 = _dot_nt(q, kn)
    i2 = lax.broadcasted_iota(jnp.int32, (rows, kn.shape[0]), 1)
    ok2 = (i2 <= new_lim) & (i2 < t)
    p1, p2 = _joint_softmax(s1, in_past & (kpos <= q_pos), s2, ok2)
    o_ref[0] = (jnp.dot(p1.astype(BF16), vcat_ref[...], preferred_element_type=F32)
                + jnp.dot(p2.astype(BF16), vn_ref[0].astype(BF16), preferred_element_type=F32))


def _dec_slc(sel, page_table, cache_k, cache_v, layer, q_slot, ks_new, vs_new, pos0, t):
    slots, top = sel.shape
    hd = NSA_HEAD_DIM
    n_past_blk = page_table.shape[1] * PAGE_SIZE // SEL_LEN
    halves = PAGE_SIZE // SEL_LEN
    per_b = NSA_KV_HEADS * t

    def blk_spec(k):
        def index(s, sel_ref, tbl_ref):
            blk = jnp.minimum(sel_ref[s, k], n_past_blk - 1)
            page = tbl_ref[s // per_b, blk // halves]
            return layer, page, blk % halves, 0, 0
        return pl.BlockSpec((None, 1, SEL_LEN, NSA_KV_HEADS, hd), index)

    new_spec = pl.BlockSpec((1, SUBLANE, hd), lambda s, a, b_: (s // per_b, 0, (s // t) % NSA_KV_HEADS))
    slot_spec = pl.BlockSpec((1, SUBLANE, hd), lambda s, a, b_: (s, 0, 0))
    return pl.pallas_call(
        functools.partial(_dec_slc_kernel, t=t, n_past_blk=n_past_blk, pos0=pos0, top=top),
        grid_spec=pltpu.PrefetchScalarGridSpec(
            num_scalar_prefetch=2,
            grid=(slots,),
            in_specs=[blk_spec(k) for k in range(top)] + [blk_spec(k) for k in range(top)]
                     + [slot_spec, new_spec, new_spec],
            out_specs=slot_spec,
            scratch_shapes=[pltpu.VMEM((top * SEL_LEN, hd), BF16), pltpu.VMEM((top * SEL_LEN, hd), BF16)]),
        out_shape=jax.ShapeDtypeStruct((slots, SUBLANE, hd), F32),
        compiler_params=_cparams(1),
        name="dec_slc",
    )(sel, page_table, *([cache_k] * top), *([cache_v] * top), q_slot, ks_new, vs_new)


def _dec_combine_kernel(gp_ref, eg_ref, ocmp_ref, oslc_ref, owin_ref, o_ref):
    gexp = _spread(_sigmoid(gp_ref[...]), eg_ref[...])
    w = NSA_Q_DIM
    out = gexp[:, 0:w] * ocmp_ref[...] + gexp[:, w:2 * w] * oslc_ref[...] + gexp[:, 2 * w:3 * w] * owin_ref[...]
    o_ref[...] = out.astype(o_ref.dtype)


def _dec_combine(gate_proj, o_cmp, o_slc, o_win):
    rows, n_gate = gate_proj.shape
    eg = np.zeros((n_gate, 3 * NSA_Q_DIM), np.float32)
    for br in range(3):
        for h in range(NSA_HEADS):
            c0 = br * NSA_Q_DIM + h * NSA_HEAD_DIM
            eg[br * NSA_HEADS + h, c0:c0 + NSA_HEAD_DIM] = 1.0
    full = lambda a: pl.BlockSpec(a.shape, lambda i: (0, 0))
    args = (gate_proj, jnp.asarray(eg, BF16), o_cmp, o_slc, o_win)
    return pl.pallas_call(
        _dec_combine_kernel,
        grid=(1,),
        in_specs=[full(a) for a in args],
        out_specs=pl.BlockSpec((rows, NSA_Q_DIM), lambda i: (0, 0)),
        out_shape=jax.ShapeDtypeStruct((rows, NSA_Q_DIM), BF16),
        compiler_params=_cparams(1),
        name="dec_combine",
    )(*args)


def _nsa_decode(proj, gate_proj, pos0, P, j, caches, page_table):
    b, t, _ = proj.shape
    cache_ck, cache_cv, cache_sk, cache_sv, win_k, win_v = caches
    assert t & (t - 1) == 0 and t <= SUBLANE
    hd, g, rep = NSA_HEAD_DIM, NSA_KV_HEADS, NSA_REP
    pad_t = lambda a: jnp.pad(a, ((0, 0), (0, SUBLANE - t), (0, 0)))
    q, kc, ks, kw, _, _, _, _ = _rope_split(pad_t(proj), pos0 + jnp.arange(SUBLANE), lowp=F32)
    q, kc, ks, kw = q[:, :t], kc[:, :t], ks[:, :t], kw[:, :t]
    vc, vs, vw = [proj[..., NSA_Q_DIM + k * NSA_KV_DIM:NSA_Q_DIM + (k + 1) * NSA_KV_DIM] for k in (1, 3, 5)]

    tk = page_table.shape[1] * PAGE_SIZE + t
    n_cmp = (tk - CMP_LEN) // CMP_STRIDE + 1
    n_sel = -(-tk // SEL_LEN)
    assert (n_cmp + 1) * CMP_STRIDE <= page_table.shape[1] * PAGE_SIZE
    ab_k, w1k = _cmp_pages(cache_ck, j, page_table, P['nsa_cmpk_w1'][j])
    ab_v, w1v = _cmp_pages(cache_cv, j, page_table, P['nsa_cmpv_w1'][j])
    kcmp = _cmp_finish(ab_k, w1k, P['nsa_cmpk_w2'][j], P['nsa_cmpk_pe'][j])
    vcmp = _cmp_finish(ab_v, w1v, P['nsa_cmpv_w2'][j], P['nsa_cmpv_pe'][j])

    q5 = q.reshape(b, t, g, rep, hd)
    q_stk = q5.transpose(0, 2, 3, 1, 4).reshape(b * g, rep * t, hd)
    o_cmp, o_win, imp = _dec_cmp_win(q_stk, kcmp, vcmp, win_k, win_v, j, pad_t(kw), pad_t(vw),
                                     n_cmp, n_sel, pos0, t)
    sel = _dec_topk(imp[:, :t].reshape(b * g * t, -1), n_sel, pos0, t).T
    q_slot = q5.transpose(0, 2, 1, 3, 4).reshape(b * g * t, rep, hd)
    q_slot = jnp.pad(q_slot, ((0, 0), (0, SUBLANE - rep), (0, 0)))
    o_slc = _dec_slc(sel, page_table, cache_sk, cache_sv, j, q_slot, pad_t(ks), pad_t(vs), pos0, t)

    unstk = lambda o: o.reshape(b, g, rep, t, hd).transpose(0, 3, 1, 2, 4).reshape(b * t, NSA_Q_DIM)
    o_slc = o_slc[:, :rep].reshape(b, g, t, rep, hd).transpose(0, 2, 1, 3, 4).reshape(b * t, NSA_Q_DIM)
    o = _dec_combine(gate_proj.reshape(b * t, -1), unstk(o_cmp), o_slc, unstk(o_win))
    heads = lambda a: a.reshape(b, t, g, hd)
    kw_new = jnp.concatenate([win_k[j][:, t:], heads(kw)], axis=1)
    vw_new = jnp.concatenate([win_v[j][:, t:], heads(vw)], axis=1)
    return o.reshape(b, t, NSA_Q_DIM), kc, vc, ks, vs, kw_new, vw_new


def _run_trunk(x, mods, pos0, past, P):
    b, t, d = x.shape
    fresh = past is None
    per_batch = t % SUBLANE == 0 and t >= LANE
    names = ('ssd_state', 'ssd_conv', 'cmp_k', 'cmp_v', 'slc_k', 'slc_v', 'win_k', 'win_v', 'pool', 'ffn')
    new = {n: [] for n in names}

    bm, tm = (b, t) if per_batch else (1, b * t)

    def mod_rows(v):
        return v[:, None, :] if per_batch else jnp.repeat(v, t, axis=0)[None]

    def mm_rows(a, w, **kw):
        return _mm(a.reshape(b * t, a.shape[-1]), w, **kw)

    def pad_cols(w):
        return jnp.pad(w, ((0, 0), (0, LANE - w.shape[1])))

    def mm_res(a, w, layer, xres, gate):
        out = _mm(a.reshape(b * t, a.shape[-1]), w, layer=layer, res=xres.reshape(b * t, d), gate=gate,
                  rows_per_gate=t if per_batch else None)
        return out.reshape(b, t, d)

    for i in range(DEPTH):
        kind, j = i % N_MIXERS, i // N_MIXERS
        sh1, sc1, g1, sh2, sc2, g2 = [mod_rows(v) for v in jnp.split(mods[i], 6, axis=-1)]
        h_dtype = F32 if kind == 2 else BF16
        h = _normmod(x.reshape(bm, tm, d), P['norm1_g'][i], sh1, sc1, h_dtype).reshape(b, t, d)
        if kind == 0:
            if fresh:
                conv_buf = jnp.zeros((b, SSD_CONV - 1, SSD_CONV_DIM), F32)
                s0 = jnp.zeros((b, SSD_HEADS, SSD_HEAD_DIM, SSD_STATE), F32)
            else:
                conv_buf, s0 = past['ssd_conv'][j], past['ssd_state'][j]
            w_in = P['ssd_w_in']
            nzx = SSD_D_INNER + SSD_CONV_DIM
            dt_raw = mm_rows(h, pad_cols(w_in[j, :, nzx:])).reshape(b, t, LANE)[..., :SSD_HEADS]
            if per_batch:
                zx = mm_rows(h, w_in, layer=j, n_cols=SSD_D_INNER).reshape(b, t, SSD_D_INNER)
                xbc, conv_new = _mm_conv_silu(h.reshape(b * t, d), w_in, j, SSD_D_INNER, SSD_CONV_DIM, t,
                                              conv_buf, P['ssd_conv_w'][j], P['ssd_conv_b'][j])
                xbc = xbc.reshape(b, t, SSD_CONV_DIM)
            else:
                zx = mm_rows(h, w_in, layer=j, n_cols=nzx).reshape(b, t, nzx)
                xbc, conv_new = _conv_silu(zx, SSD_D_INNER, SSD_CONV_DIM, conv_buf,
                                           P['ssd_conv_w'][j], P['ssd_conv_b'][j])
            tp = -(-t // SSD_CHUNK) * SSD_CHUNK
            if tp != t:
                xbc_p = jnp.pad(xbc, ((0, 0), (0, tp - t), (0, 0)))
                dt_p = jnp.pad(dt_raw, ((0, 0), (0, tp - t), (0, 0)))
            else:
                xbc_p, dt_p = xbc, dt_raw
            y, s_new = _ssd_scan(xbc_p, dt_p, P['ssd_dt_bias'][j], P['ssd_a_log'][j], P['ssd_d'][j], s0,
                                 t_valid=min(t, SSD_CHUNK))
            yn = _gated_norm(y[:, :t], zx, P['ssd_norm_g'][j])
            x = mm_res(yn, P['ssd_w_out'], j, x, g1)
            new['ssd_conv'].append(conv_new)
            new['ssd_state'].append(s_new)
        elif kind == 1:
            w_in = P['nsa_w_in']
            npj = NSA_Q_DIM + 6 * NSA_KV_DIM
            proj = mm_rows(h, w_in, layer=j, n_cols=npj).reshape(b, t, npj)
            gate_proj = mm_rows(h, pad_cols(w_in[j, :, npj:])).reshape(b, t, LANE)
            if fresh:
                q_pos = pos0 + jnp.arange(t)
                q, kc, ks, kw, ks_b, vs_b, kw_b, vw_b = _rope_split(proj, q_pos)
                vc, vs, vw = [proj[..., NSA_Q_DIM + k * NSA_KV_DIM:NSA_Q_DIM + (k + 1) * NSA_KV_DIM]
                              for k in (1, 3, 5)]
                kcmp = _compress(kc, P['nsa_cmpk_w1'][j], P['nsa_cmpk_w2'][j], P['nsa_cmpk_pe'][j])
                vcmp = _compress(vc, P['nsa_cmpv_w1'][j], P['nsa_cmpv_w2'][j], P['nsa_cmpv_pe'][j])
                o = _nsa_attention(q, kcmp, vcmp, ks_b, vs_b, kw_b, vw_b, gate_proj)
                keep = min(WINDOW, t)
                kw_new, vw_new = kw[:, t - keep:], vw[:, t - keep:]
            else:
                o, kc, vc, ks, vs, kw_new, vw_new = _nsa_decode(
                    proj, gate_proj, pos0, P, j, past['nsa'][j], past['page_table'])
            x = mm_res(o, P['nsa_w_out'], j, x, g1)
            shp = (b, t, NSA_KV_HEADS, NSA_HEAD_DIM)
            for n, v in (('cmp_k', kc), ('cmp_v', vc), ('slc_k', ks), ('slc_v', vs)):
                new[n].append(v.reshape(shp))
            new['win_k'].append(kw_new.reshape(b, -1, NSA_KV_HEADS, NSA_HEAD_DIM))
            new['win_v'].append(vw_new.reshape(b, -1, NSA_KV_HEADS, NSA_HEAD_DIM))
        else:
            buf = jnp.zeros((b, POOL_BUF, d), F32) if fresh else past['pool'][j]
            gate = (g1 * P['pool_scale'][j]).reshape(b, -1, d)
            x = _pool_mixer(h, buf, P['pool_w'][j], x, gate, pos0)
            new['pool'].append(jnp.concatenate([buf, h], axis=1)[:, -POOL_BUF:])
        h2 = _normmod(x.reshape(bm, tm, d), P['norm2_g'][i], sh2, sc2, BF16)
        fbuf = jnp.zeros((b, FFN_CONV - 1, 2 * D_FF), F32) if fresh else past['ffn'][i]
        if per_batch:
            act, fbuf_new = _mm_conv_glu(h2.reshape(b * t, d), P['ffn_w_up'], i, t, fbuf,
                                         P['ffn_conv_w'][i], P['ffn_conv_b'][i])
        else:
            u = mm_rows(h2, P['ffn_w_up'], layer=i).reshape(b, t, 2 * D_FF)
            act, fbuf_new = _conv_glu(u, fbuf, P['ffn_conv_w'][i], P['ffn_conv_b'][i])
        x = mm_res(act, P['ffn_w_down'], i, x, g2)
        new['ffn'].append(fbuf_new)
    zero = jnp.zeros((bm, 1, d), F32)
    y = _normmod(x.reshape(bm, tm, d), P['final_g'], zero, zero, F32).reshape(b, t, d)
    return y, {n: jnp.stack(v) for n, v in new.items()}


def kernel(x_prompt, x_sample, state_ssd, state_ssd_conv, cache_cmp_k, cache_cmp_v, cache_slc_k, cache_slc_v,
           cache_win_k, cache_win_v, state_pool, state_ffn_conv, page_table, c_prompt, c_sample,
           ada_w, ada_b, norm1_g, norm2_g, final_g,
           ssd_w_in, ssd_conv_w, ssd_conv_b, ssd_dt_bias, ssd_a_log, ssd_d, ssd_norm_g, ssd_w_out,
           nsa_w_in, nsa_cmpk_w1, nsa_cmpk_w2, nsa_cmpk_pe, nsa_cmpv_w1, nsa_cmpv_w2, nsa_cmpv_pe, nsa_w_out,
           pool_w, pool_scale, ffn_w_up, ffn_conv_w, ffn_conv_b, ffn_w_down):
    P = dict(norm1_g=norm1_g, norm2_g=norm2_g, final_g=final_g,
             ssd_w_in=ssd_w_in, ssd_conv_w=ssd_conv_w, ssd_conv_b=ssd_conv_b, ssd_dt_bias=ssd_dt_bias,
             ssd_a_log=ssd_a_log, ssd_d=ssd_d, ssd_norm_g=ssd_norm_g, ssd_w_out=ssd_w_out,
             nsa_w_in=nsa_w_in, nsa_cmpk_w1=nsa_cmpk_w1, nsa_cmpk_w2=nsa_cmpk_w2, nsa_cmpk_pe=nsa_cmpk_pe,
             nsa_cmpv_w1=nsa_cmpv_w1, nsa_cmpv_w2=nsa_cmpv_w2, nsa_cmpv_pe=nsa_cmpv_pe, nsa_w_out=nsa_w_out,
             pool_w=pool_w, pool_scale=pool_scale,
             ffn_w_up=ffn_w_up, ffn_conv_w=ffn_conv_w, ffn_conv_b=ffn_conv_b, ffn_w_down=ffn_w_down)
    nbp, nbs = c_prompt.shape[0], c_sample.shape[0]
    c_rows = -(-(nbp + nbs) // (2 * SUBLANE)) * (2 * SUBLANE)
    c_all = jnp.concatenate([c_prompt, c_sample, jnp.zeros((c_rows - nbp - nbs, D_MODEL), F32)], axis=0)
    mods = [_mm(c_all, ada_w, layer=i, a_silu=True, bias=ada_b[i]) for i in range(DEPTH)]
    mods_p = [m[:nbp] for m in mods]
    mods_s = [m[nbp:nbp + nbs] for m in mods]

    y_prompt, sp = _run_trunk(x_prompt, mods_p, 0, None, P)
    past_len = page_table.shape[1] * PAGE_SIZE
    nsa_past = [(cache_cmp_k, cache_cmp_v, cache_slc_k, cache_slc_v, cache_win_k, cache_win_v)
                for _ in range(cache_cmp_k.shape[0])]
    past = dict(ssd_state=state_ssd, ssd_conv=state_ssd_conv, nsa=nsa_past, page_table=page_table,
                pool=state_pool, ffn=state_ffn_conv)
    y_sample, ss = _run_trunk(x_sample, mods_s, past_len, past, P)
    return (y_prompt, y_sample,
            sp['ssd_state'], ss['ssd_state'], sp['ssd_conv'], ss['ssd_conv'],
            sp['cmp_k'], ss['cmp_k'], sp['cmp_v'], ss['cmp_v'],
            sp['slc_k'], ss['slc_k'], sp['slc_v'], ss['slc_v'],
            sp['win_k'], ss['win_k'], sp['win_v'], ss['win_v'],
            sp['pool'], ss['pool'], sp['ffn'], ss['ffn'])
```

```python
import functools
import math

import jax
import jax.numpy as jnp
import numpy as np
from jax import lax
from jax.experimental import pallas as pl
from jax.experimental.pallas import tpu as pltpu

D_MODEL = 2048
DEPTH = 4
PAGE_SIZE = 128
N_MIXERS = 3

SSD_D_INNER = 2 * D_MODEL
SSD_HEAD_DIM = 64
SSD_HEADS = SSD_D_INNER // SSD_HEAD_DIM
SSD_GROUPS = 8
SSD_REP = SSD_HEADS // SSD_GROUPS
SSD_STATE = 128
SSD_CONV = 4
SSD_CHUNK = 128
SSD_GROUP_DIM = SSD_REP * SSD_HEAD_DIM
SSD_CONV_DIM = SSD_D_INNER + 2 * SSD_GROUPS * SSD_STATE

NSA_HEADS = 16
NSA_KV_HEADS = 4
NSA_HEAD_DIM = D_MODEL // NSA_HEADS
NSA_REP = NSA_HEADS // NSA_KV_HEADS
NSA_Q_DIM = NSA_HEADS * NSA_HEAD_DIM
NSA_KV_DIM = NSA_KV_HEADS * NSA_HEAD_DIM
CMP_LEN = 32
CMP_STRIDE = 16
CMP_HIDDEN = 256
SEL_LEN = 64
SEL_TOPK = 16
WINDOW = 512
WIN_Q_BLOCK = 128
SEL_Q_BLOCK = 64
ROPE_THETA = 10000.0
FORCE_BONUS = 1000.0

POOL_WINDOWS = (2, 4, 8, 16)
POOL_GROUPS = len(POOL_WINDOWS)
POOL_GROUP_DIM = D_MODEL // POOL_GROUPS
POOL_BUF = max(POOL_WINDOWS) - 1

D_FF = 5632
FFN_CONV = 3

EPS = 1e-6
NEG_INF = -1e30
TINY = 1e-30

V7X_VMEM_LIMIT_BYTES = 52 * 1024 * 1024
LANE = 128
SUBLANE = 8

F32 = jnp.float32
BF16 = jnp.bfloat16
HIGHEST = lax.Precision.HIGHEST

NSA_TQ = 128
NSA_SLC_KC = 256
NSA_WIN_SPAN = WINDOW + NSA_TQ


def _cparams(n_axes):
    return pltpu.CompilerParams(dimension_semantics=("arbitrary",) * n_axes,
                                vmem_limit_bytes=V7X_VMEM_LIMIT_BYTES)


def _pick_tile(dim, pref, align):
    t = min(pref, dim)
    t -= t % align
    while t >= align:
        if dim % t == 0:
            return t
        t -= align
    return dim


def _sigmoid(x):
    return 1.0 / (1.0 + jnp.exp(-x))


def _silu(x):
    return x * _sigmoid(x)


def _dot_nt(a, b):
    return lax.dot_general(a, b, (((1,), (1,)), ((), ())), preferred_element_type=F32)


def _dot_tn(a, b):
    return lax.dot_general(a, b, (((0,), (0,)), ((), ())), preferred_element_type=F32)


def _dot_exact(a, b):
    return jnp.dot(a, b, precision=HIGHEST, preferred_element_type=F32)


def _split3(x):
    hi = x.astype(BF16)
    r = x - hi.astype(F32)
    mid = r.astype(BF16)
    lo = (r - mid.astype(F32)).astype(BF16)
    return hi, mid, lo


def _spread(x, e):
    hi, mid, lo = _split3(x)
    dot = lambda p: jnp.dot(p, e, preferred_element_type=F32)
    return dot(hi) + dot(mid) + dot(lo)


def _spread_rows(e, x):
    hi, mid, lo = _split3(x)
    dot = lambda p: jnp.dot(e, p, preferred_element_type=F32)
    return dot(hi) + dot(mid) + dot(lo)


def _normmod_kernel(x_ref, g_ref, sh_ref, sc_ref, o_ref):
    x = x_ref[0].astype(F32)
    y = x * lax.rsqrt(jnp.mean(x * x, axis=-1, keepdims=True) + EPS) * g_ref[...]
    y = y * (1.0 + sc_ref[0]) + sh_ref[0]
    o_ref[0] = y.astype(o_ref.dtype)


def _normmod(x, g, shift, scale, out_dtype):
    b, t, d = x.shape
    r = shift.shape[1]
    tt = _pick_tile(t, 256, SUBLANE)
    if r == 1:
        mod_spec = pl.BlockSpec((1, 1, d), lambda i, j: (i, 0, 0))
    else:
        mod_spec = pl.BlockSpec((1, tt, d), lambda i, j: (i, j, 0))
    return pl.pallas_call(
        _normmod_kernel,
        grid=(b, t // tt),
        in_specs=[pl.BlockSpec((1, tt, d), lambda i, j: (i, j, 0)),
                  pl.BlockSpec((1, d), lambda i, j: (0, 0)),
                  mod_spec, mod_spec],
        out_specs=pl.BlockSpec((1, tt, d), lambda i, j: (i, j, 0)),
        out_shape=jax.ShapeDtypeStruct((b, t, d), out_dtype),
        compiler_params=_cparams(2),
        name="normmod",
    )(x, g.reshape(1, d), shift, scale)


_MM_K_CHUNK = 512


def _dot_kchunks(a_ref, wb_ref, acc_ref, prep=None, rows=None):
    k = a_ref.shape[1]
    kc = _MM_K_CHUNK if k % _MM_K_CHUNK == 0 else k
    rows = slice(None) if rows is None else rows
    n_chunks = k // kc
    part = None
    for c in range(n_chunks):
        a = a_ref[:, c * kc:(c + 1) * kc]
        part = jnp.dot(a if prep is None else prep(a), wb_ref[c * kc:(c + 1) * kc, :],
                       preferred_element_type=F32)
        if c == 0 and n_chunks > 1:
            acc_ref[rows, :] = part
        elif c < n_chunks - 1:
            acc_ref[rows, :] += part
    return part if n_chunks == 1 else acc_ref[rows, :] + part


def _mm_kernel(*refs, a_silu, has_bias, has_res):
    it = iter(refs)
    a_ref, w_ref = next(it), next(it)
    bias_ref = next(it) if has_bias else None
    res_ref = next(it) if has_res else None
    gate_ref = next(it) if has_res else None
    o_ref, wb_ref, acc_ref = next(it), next(it), next(it)

    @pl.when(pl.program_id(1) == 0)
    def _():
        wb_ref[...] = w_ref[...].astype(BF16)

    prep = (lambda a: _silu(a.astype(F32)).astype(BF16)) if a_silu else (lambda a: a.astype(BF16))
    acc = _dot_kchunks(a_ref, wb_ref, acc_ref, prep)
    if has_bias:
        acc = acc + bias_ref[...]
    if has_res:
        acc = res_ref[...] + gate_ref[0] * acc
    o_ref[...] = acc.astype(o_ref.dtype)


def _w_spec(w, layer, k, tn, col_block):
    if w.ndim == 2:
        return pl.BlockSpec((k, tn), lambda j, i: (0, col_block(j)))
    return pl.BlockSpec((None, k, tn), lambda j, i: (layer, 0, col_block(j)))


def _mm(a, w, *, layer=None, n_cols=None, a_silu=False, bias=None, res=None, gate=None, rows_per_gate=None,
        out_dtype=F32, tm=512, tn=1024):
    m, k = a.shape
    assert w.shape[-2] == k
    n = w.shape[-1] if n_cols is None else n_cols
    tm = _pick_tile(m if rows_per_gate is None else rows_per_gate, tm, SUBLANE)
    tn = _pick_tile(n, tn, LANE)
    while k * tn * 10 > 30 * 1024 * 1024 and tn % (2 * LANE) == 0:
        tn //= 2
    assert m % tm == 0 and n % tn == 0
    in_specs = [pl.BlockSpec((tm, k), lambda j, i: (i, 0)),
                _w_spec(w, layer, k, tn, lambda j: j)]
    args = [a, w]
    if bias is not None:
        in_specs.append(pl.BlockSpec((1, tn), lambda j, i: (0, j)))
        args.append(bias.reshape(1, -1))
    if res is not None:
        in_specs.append(pl.BlockSpec((tm, tn), lambda j, i: (i, j)))
        args.append(res)
        if gate.shape[1] == 1:
            tiles_per_gate = rows_per_gate // tm
            in_specs.append(pl.BlockSpec((1, 1, tn), lambda j, i: (i // tiles_per_gate, 0, j)))
        else:
            in_specs.append(pl.BlockSpec((1, tm, tn), lambda j, i: (0, i, j)))
        args.append(gate)
    return pl.pallas_call(
        functools.partial(_mm_kernel, a_silu=a_silu, has_bias=bias is not None, has_res=res is not None),
        grid=(n // tn, m // tm),
        in_specs=in_specs,
        out_specs=pl.BlockSpec((tm, tn), lambda j, i: (i, j)),
        out_shape=jax.ShapeDtypeStruct((m, n), out_dtype),
        scratch_shapes=[pltpu.VMEM((k, tn), BF16), pltpu.VMEM((tm, tn), F32)],
        compiler_params=_cparams(2),
        name="mm",
    )(*args)


_CONV_PAD = SUBLANE


def _conv_tile(x, first, buf_ref, w_ref, b_ref, st_ref, xe_ref, width, tt):
    lo = _CONV_PAD - (width - 1)

    @pl.when(first)
    def _():
        xe_ref[lo:_CONV_PAD, :] = buf_ref[0]

    @pl.when(jnp.logical_not(first))
    def _():
        xe_ref[0:_CONV_PAD, :] = xe_ref[tt:tt + _CONV_PAD, :]

    xe_ref[_CONV_PAD:_CONV_PAD + tt, :] = x
    acc = b_ref[...] + w_ref[0:1, :] * xe_ref[lo:lo + tt, :]
    for k in range(1, width):
        acc = acc + w_ref[k:k + 1, :] * xe_ref[lo + k:lo + k + tt, :]
    st_ref[0] = xe_ref[tt + lo:tt + _CONV_PAD, :]
    return acc


def _conv_silu_kernel(x_ref, buf_ref, w_ref, b_ref, y_ref, st_ref, xe_ref, *, width, tt):
    first = pl.program_id(2) == 0
    y_ref[0] = _silu(_conv_tile(x_ref[0], first, buf_ref, w_ref, b_ref, st_ref, xe_ref, width, tt))


def _conv_glu_kernel(xa_ref, xg_ref, bufa_ref, bufg_ref, wa_ref, wg_ref, ba_ref, bg_ref,
                     act_ref, sta_ref, stg_ref, xea_ref, xeg_ref, *, width, tt):
    first = pl.program_id(2) == 0
    a = _conv_tile(xa_ref[0], first, bufa_ref, wa_ref, ba_ref, sta_ref, xea_ref, width, tt)
    g = _conv_tile(xg_ref[0], first, bufg_ref, wg_ref, bg_ref, stg_ref, xeg_ref, width, tt)
    act_ref[0] = (_silu(g) * a).astype(act_ref.dtype)


_CONV_ROWS = 64


def _conv_head(prev_ref, first, buf_ref, width):
    lo = _CONV_PAD - (width - 1)
    head = prev_ref[lo:_CONV_PAD, :]
    take_buf = jnp.full(head.shape, first.astype(jnp.int32)) > 0
    prev_ref[lo:_CONV_PAD, :] = jnp.where(take_buf, buf_ref[0], head)


def _conv_rows(prev_ref, cw_ref, cb_ref, r0, width):
    lo = _CONV_PAD - (width - 1) + r0
    acc = cb_ref[...] + cw_ref[0:1, :] * prev_ref[lo:lo + _CONV_ROWS, :]
    for k in range(1, width):
        acc = acc + cw_ref[k:k + 1, :] * prev_ref[lo + k:lo + k + _CONV_ROWS, :]
    return acc


def _conv_tail(prev_ref, cur_ref, st_ref, width, tm):
    st_ref[0] = prev_ref[tm + _CONV_PAD - (width - 1):tm + _CONV_PAD, :]
    cur_ref[0:_CONV_PAD, :] = prev_ref[tm:tm + _CONV_PAD, :]


def _zero_on_first_step(*refs):
    @pl.when((pl.program_id(0) == 0) & (pl.program_id(1) == 0))
    def _():
        for r in refs:
            r[...] = jnp.zeros(r.shape, r.dtype)


def _mm_conv_silu_kernel(a_ref, w_ref, buf_ref, cw_ref, cb_ref, y_ref, st_ref, wb_ref, u0_ref, u1_ref,
                         *, width, tm, tiles_per_seq):
    i = pl.program_id(1)

    @pl.when(i == 0)
    def _():
        wb_ref[...] = w_ref[...].astype(BF16)

    _zero_on_first_step(u0_ref, u1_ref)
    prev_first = (i + tiles_per_seq - 1) % tiles_per_seq == 0

    rows = slice(_CONV_PAD, _CONV_PAD + tm)

    def step(cur, prev):
        cur[rows, :] = _dot_kchunks(a_ref, wb_ref, cur, rows=rows)
        _conv_head(prev, prev_first, buf_ref, width)
        for r0 in range(0, tm, _CONV_ROWS):
            y_ref[r0:r0 + _CONV_ROWS, :] = _silu(_conv_rows(prev, cw_ref, cb_ref, r0, width))
        _conv_tail(prev, cur, st_ref, width, tm)

    pl.when(i % 2 == 0)(lambda: step(u0_ref, u1_ref))
    pl.when(i % 2 == 1)(lambda: step(u1_ref, u0_ref))


def _mm_conv_glu_kernel(a_ref, wa_ref, wg_ref, bufa_ref, bufg_ref, cwa_ref, cwg_ref, cba_ref, cbg_ref,
                        act_ref, sta_ref, stg_ref, wba_ref, wbg_ref, ua0_ref, ua1_ref, ug0_ref, ug1_ref,
                        *, width, tm, tiles_per_seq):
    i = pl.program_id(1)

    @pl.when(i == 0)
    def _():
        wba_ref[...] = wa_ref[...].astype(BF16)
        wbg_ref[...] = wg_ref[...].astype(BF16)

    _zero_on_first_step(ua0_ref, ua1_ref, ug0_ref, ug1_ref)
    prev_first = (i + tiles_per_seq - 1) % tiles_per_seq == 0

    rows = slice(_CONV_PAD, _CONV_PAD + tm)

    def step(cur_a, cur_g, prev_a, prev_g):
        cur_a[rows, :] = _dot_kchunks(a_ref, wba_ref, cur_a, rows=rows)
        cur_g[rows, :] = _dot_kchunks(a_ref, wbg_ref, cur_g, rows=rows)
        _conv_head(prev_a, prev_first, bufa_ref, width)
        _conv_head(prev_g, prev_first, bufg_ref, width)
        for r0 in range(0, tm, _CONV_ROWS):
            ca = _conv_rows(prev_a, cwa_ref, cba_ref, r0, width)
            cg = _conv_rows(prev_g, cwg_ref, cbg_ref, r0, width)
            act_ref[r0:r0 + _CONV_ROWS, :] = (_silu(cg) * ca).astype(act_ref.dtype)
        _conv_tail(prev_a, cur_a, sta_ref, width, tm)
        _conv_tail(prev_g, cur_g, stg_ref, width, tm)

    pl.when(i % 2 == 0)(lambda: step(ua0_ref, ug0_ref, ua1_ref, ug1_ref))
    pl.when(i % 2 == 1)(lambda: step(ua1_ref, ug1_ref, ua0_ref, ug0_ref))


def _mm_conv_silu(a, w, layer, col0, n_ch, seq_len, buf, cw, cbias, *, tm=512, tn=512):
    m, k = a.shape
    width = cw.shape[0]
    nb = m // seq_len
    tm = _pick_tile(seq_len, tm, 2 * SUBLANE)
    tn = _pick_tile(n_ch, tn, LANE)
    assert col0 % tn == 0
    off = col0 // tn
    tps = seq_len // tm
    n_i = m // tm
    mul_tile = lambda i: jnp.minimum(i, n_i - 1)
    conv_tile = lambda i: jnp.maximum(i - 1, 0)
    u_buf = pltpu.VMEM((tm + _CONV_PAD, tn), F32)
    return pl.pallas_call(
        functools.partial(_mm_conv_silu_kernel, width=width, tm=tm, tiles_per_seq=tps),
        grid=(n_ch // tn, n_i + 1),
        in_specs=[pl.BlockSpec((tm, k), lambda j, i: (mul_tile(i), 0)),
                  _w_spec(w, layer, k, tn, lambda j: j + off),
                  pl.BlockSpec((1, width - 1, tn), lambda j, i: (conv_tile(i) // tps, 0, j)),
                  pl.BlockSpec((width, tn), lambda j, i: (0, j)),
                  pl.BlockSpec((1, tn), lambda j, i: (0, j))],
        out_specs=[pl.BlockSpec((tm, tn), lambda j, i: (conv_tile(i), j)),
                   pl.BlockSpec((1, width - 1, tn), lambda j, i: (conv_tile(i) // tps, 0, j))],
        out_shape=[jax.ShapeDtypeStruct((m, n_ch), F32), jax.ShapeDtypeStruct((nb, width - 1, n_ch), F32)],
        scratch_shapes=[pltpu.VMEM((k, tn), BF16), u_buf, u_buf],
        compiler_params=_cparams(2),
        name="mm_conv_silu",
    )(a, w, buf, cw, cbias.reshape(1, -1))


def _mm_conv_glu(a, w, layer, seq_len, buf, cw, cbias, *, tm=512, tn=512):
    m, k = a.shape
    half = w.shape[-1] // 2
    width = cw.shape[0]
    nb = m // seq_len
    tm = _pick_tile(seq_len, tm, 2 * SUBLANE)
    tn = _pick_tile(half, tn, LANE)
    hb = half // tn
    tps = seq_len // tm
    cb2 = cbias.reshape(1, -1)
    lo_hi = lambda shape, index: [pl.BlockSpec(shape, lambda j, i: index(j, i)),
                                  pl.BlockSpec(shape, lambda j, i: index(j + hb, i))]
    n_i = m // tm
    mul_tile = lambda i: jnp.minimum(i, n_i - 1)
    conv_tile = lambda i: jnp.maximum(i - 1, 0)
    st_spec = pl.BlockSpec((1, width - 1, tn), lambda j, i: (conv_tile(i) // tps, 0, j))
    u_buf = pltpu.VMEM((tm + _CONV_PAD, tn), F32)
    act, sta, stg = pl.pallas_call(
        functools.partial(_mm_conv_glu_kernel, width=width, tm=tm, tiles_per_seq=tps),
        grid=(hb, n_i + 1),
        in_specs=[pl.BlockSpec((tm, k), lambda j, i: (mul_tile(i), 0)),
                  _w_spec(w, layer, k, tn, lambda j: j), _w_spec(w, layer, k, tn, lambda j: j + hb)]
                 + lo_hi((1, width - 1, tn), lambda j, i: (conv_tile(i) // tps, 0, j))
                 + lo_hi((width, tn), lambda j, i: (0, j))
                 + lo_hi((1, tn), lambda j, i: (0, j)),
        out_specs=[pl.BlockSpec((tm, tn), lambda j, i: (conv_tile(i), j)), st_spec, st_spec],
        out_shape=[jax.ShapeDtypeStruct((m, half), BF16),
                   jax.ShapeDtypeStruct((nb, width - 1, half), F32),
                   jax.ShapeDtypeStruct((nb, width - 1, half), F32)],
        scratch_shapes=[pltpu.VMEM((k, tn), BF16), pltpu.VMEM((k, tn), BF16), u_buf, u_buf, u_buf, u_buf],
        compiler_params=_cparams(2),
        name="mm_conv_glu",
    )(a, w, w, buf, buf, cw, cw, cb2, cb2)
    return act, jnp.concatenate([sta, stg], axis=-1)


def _conv_silu(x, col0, n_ch, buf, w, bias, *, tc=512, tt=512):
    b, t, _ = x.shape
    width = w.shape[0]
    tt = _pick_tile(t, tt, SUBLANE)
    off = col0 // tc
    return pl.pallas_call(
        functools.partial(_conv_silu_kernel, width=width, tt=tt),
        grid=(b, n_ch // tc, t // tt),
        in_specs=[pl.BlockSpec((1, tt, tc), lambda i, j, s: (i, s, j + off)),
                  pl.BlockSpec((1, width - 1, tc), lambda i, j, s: (i, 0, j)),
                  pl.BlockSpec((width, tc), lambda i, j, s: (0, j)),
                  pl.BlockSpec((1, tc), lambda i, j, s: (0, j))],
        out_specs=[pl.BlockSpec((1, tt, tc), lambda i, j, s: (i, s, j)),
                   pl.BlockSpec((1, width - 1, tc), lambda i, j, s: (i, 0, j))],
        out_shape=[jax.ShapeDtypeStruct((b, t, n_ch), F32),
                   jax.ShapeDtypeStruct((b, width - 1, n_ch), F32)],
        scratch_shapes=[pltpu.VMEM((tt + _CONV_PAD, tc), F32)],
        compiler_params=_cparams(3),
        name="conv_silu",
    )(x, buf, w, bias.reshape(1, -1))


def _conv_glu(u, buf, w, bias, *, tc=512, tt=512):
    b, t, c2 = u.shape
    half = c2 // 2
    width = w.shape[0]
    tt = _pick_tile(t, tt, SUBLANE)
    hb = half // tc
    xa = pl.BlockSpec((1, tt, tc), lambda i, j, s: (i, s, j))
    xg = pl.BlockSpec((1, tt, tc), lambda i, j, s: (i, s, j + hb))
    ba = pl.BlockSpec((1, width - 1, tc), lambda i, j, s: (i, 0, j))
    bg = pl.BlockSpec((1, width - 1, tc), lambda i, j, s: (i, 0, j + hb))
    wa = pl.BlockSpec((width, tc), lambda i, j, s: (0, j))
    wg = pl.BlockSpec((width, tc), lambda i, j, s: (0, j + hb))
    ca = pl.BlockSpec((1, tc), lambda i, j, s: (0, j))
    cg = pl.BlockSpec((1, tc), lambda i, j, s: (0, j + hb))
    st = pl.BlockSpec((1, width - 1, tc), lambda i, j, s: (i, 0, j))
    bias2 = bias.reshape(1, -1)
    act, sta, stg = pl.pallas_call(
        functools.partial(_conv_glu_kernel, width=width, tt=tt),
        grid=(b, hb, t // tt),
        in_specs=[xa, xg, ba, bg, wa, wg, ca, cg],
        out_specs=[pl.BlockSpec((1, tt, tc), lambda i, j, s: (i, s, j)), st, st],
        out_shape=[jax.ShapeDtypeStruct((b, t, half), BF16),
                   jax.ShapeDtypeStruct((b, width - 1, half), F32),
                   jax.ShapeDtypeStruct((b, width - 1, half), F32)],
        scratch_shapes=[pltpu.VMEM((tt + _CONV_PAD, tc), F32), pltpu.VMEM((tt + _CONV_PAD, tc), F32)],
        compiler_params=_cparams(3),
        name="conv_glu",
    )(u, u, buf, buf, w, w, bias2, bias2)
    return act, jnp.concatenate([sta, stg], axis=-1)


def _softplus(x):
    return jnp.maximum(x, 0.0) + jnp.log1p(jnp.exp(-jnp.abs(x)))


def _ssd_kernel(x_ref, b_ref, c_ref, dtc_ref, dtr_ref, biasc_ref, biasr_ref, alogc_ref, alogr_ref,
                dch_ref, e512_ref, e1024_ref, et_ref, s0_ref, y_ref, sf_ref, s_ref, *, t_valid):
    L = SSD_CHUNK
    c = pl.program_id(2)

    @pl.when(c == 0)
    def _():
        s_ref[...] = s0_ref[0].reshape(SSD_GROUP_DIM, SSD_STATE)

    x = x_ref[0]
    bm = b_ref[0].astype(BF16)
    cm = c_ref[0].astype(BF16)
    row = lax.broadcasted_iota(jnp.int32, (L, L), 0)
    col = lax.broadcasted_iota(jnp.int32, (L, L), 1)
    causal = row >= col

    dt_c = _softplus(dtc_ref[0, 0] + biasc_ref[0])
    dt_r = _softplus(dtr_ref[0] + biasr_ref[0])
    if t_valid < L:
        dt_c = jnp.where(lax.broadcasted_iota(jnp.int32, dt_c.shape, 0) < t_valid, dt_c, 0.0)
        dt_r = jnp.where(lax.broadcasted_iota(jnp.int32, dt_r.shape, 1) < t_valid, dt_r, 0.0)
    dta_c = dt_c * (-jnp.exp(alogc_ref[0]))
    dta_r = dt_r * (-jnp.exp(alogr_ref[0]))

    tri = jnp.where(causal, 1.0, 0.0).astype(F32)
    upper = jnp.where(row <= col, 1.0, 0.0).astype(F32)
    acs_c = _dot_exact(tri, dta_c)
    acs_r = _dot_exact(dta_r, upper)
    e512 = e512_ref[...]
    acs_x = _spread(acs_c, e512)
    dt_x = _spread(dt_c, e512)
    acs_full = _spread(acs_c, e1024_ref[...])
    alast_x = acs_x[L - 1:L, :]

    xdt = x * dt_x
    xdt_b = xdt.astype(BF16)
    cb = _dot_nt(cm, bm)
    lane = lax.broadcasted_iota(jnp.int32, (L, LANE), 1)
    pairs = []
    for pr in range(SSD_REP // 2):
        xp = xdt_b[:, pr * LANE:(pr + 1) * LANE]
        halves = []
        for r in (2 * pr, 2 * pr + 1):
            seg = acs_full[:, r * LANE:(r + 1) * LANE] - acs_r[r:r + 1, :]
            lm = jnp.exp(jnp.where(causal, seg, NEG_INF))
            halves.append(jnp.dot((cb * lm).astype(BF16), xp, preferred_element_type=F32))
        pairs.append(jnp.where(lane < SSD_HEAD_DIM, halves[0], halves[1]))
    y_diag = jnp.concatenate(pairs, axis=1)

    s_in = s_ref[...]
    y_off = _dot_nt(cm, s_in.astype(BF16)) * jnp.exp(acs_x)
    y_ref[0] = y_diag + y_off + x * dch_ref[...]

    xw = (xdt * jnp.exp(alast_x - acs_x)).astype(BF16)
    alast_r = jnp.sum(dta_r, axis=1, keepdims=True)
    dec = jnp.exp(_spread_rows(et_ref[...], jnp.broadcast_to(alast_r, (SSD_REP, SSD_STATE))))
    s_new = s_in * dec + _dot_tn(xw, bm)
    s_ref[...] = s_new
    sf_ref[0] = s_new.reshape(SSD_REP, SSD_HEAD_DIM, SSD_STATE)


def _ssd_expanders():
    e512 = np.repeat(np.eye(SSD_REP, dtype=np.float32), SSD_HEAD_DIM, axis=1)
    e1024 = np.repeat(np.eye(SSD_REP, dtype=np.float32), LANE, axis=1)
    return jnp.asarray(e512, BF16), jnp.asarray(e1024, BF16), jnp.asarray(e512.T.copy(), BF16)


def _ssd_scan(xbc, dt_raw, dt_bias, a_log, d_skip, s0, t_valid):
    b, t, _ = xbc.shape
    L = SSD_CHUNK
    assert t % L == 0
    g, rep = SSD_GROUPS, SSD_REP
    dtc = dt_raw.reshape(b, t, g, rep).transpose(0, 2, 1, 3)
    dtr = dt_raw.transpose(0, 2, 1)
    bias_c = dt_bias.reshape(g, 1, rep)
    bias_r = jnp.broadcast_to(dt_bias.reshape(g, rep, 1), (g, rep, L))
    alog_c = a_log.reshape(g, 1, rep)
    alog_r = jnp.broadcast_to(a_log.reshape(g, rep, 1), (g, rep, L))
    dch = jnp.repeat(d_skip, SSD_HEAD_DIM).reshape(1, SSD_D_INNER)
    e512, e1024, et = _ssd_expanders()
    xb = SSD_D_INNER // SSD_STATE
    full = lambda shape: pl.BlockSpec(shape, lambda i, j, s: (0,) * len(shape))
    return pl.pallas_call(
        functools.partial(_ssd_kernel, t_valid=t_valid),
        grid=(b, g, t // L),
        in_specs=[pl.BlockSpec((1, L, SSD_GROUP_DIM), lambda i, j, s: (i, s, j)),
                  pl.BlockSpec((1, L, SSD_STATE), lambda i, j, s: (i, s, xb + j)),
                  pl.BlockSpec((1, L, SSD_STATE), lambda i, j, s: (i, s, xb + g + j)),
                  pl.BlockSpec((1, 1, L, rep), lambda i, j, s: (i, j, s, 0)),
                  pl.BlockSpec((1, rep, L), lambda i, j, s: (i, j, s)),
                  pl.BlockSpec((1, 1, rep), lambda i, j, s: (j, 0, 0)),
                  pl.BlockSpec((1, rep, L), lambda i, j, s: (j, 0, 0)),
                  pl.BlockSpec((1, 1, rep), lambda i, j, s: (j, 0, 0)),
                  pl.BlockSpec((1, rep, L), lambda i, j, s: (j, 0, 0)),
                  pl.BlockSpec((1, SSD_GROUP_DIM), lambda i, j, s: (0, j)),
                  full((rep, SSD_GROUP_DIM)), full((rep, rep * LANE)), full((SSD_GROUP_DIM, rep)),
                  pl.BlockSpec((1, rep, SSD_HEAD_DIM, SSD_STATE), lambda i, j, s: (i, j, 0, 0))],
        out_specs=[pl.BlockSpec((1, L, SSD_GROUP_DIM), lambda i, j, s: (i, s, j)),
                   pl.BlockSpec((1, rep, SSD_HEAD_DIM, SSD_STATE), lambda i, j, s: (i, j, 0, 0))],
        out_shape=[jax.ShapeDtypeStruct((b, t, SSD_D_INNER), F32),
                   jax.ShapeDtypeStruct((b, SSD_HEADS, SSD_HEAD_DIM, SSD_STATE), F32)],
        scratch_shapes=[pltpu.VMEM((SSD_GROUP_DIM, SSD_STATE), F32)],
        compiler_params=_cparams(3),
        name="ssd_scan",
    )(xbc, xbc, xbc, dtc, dtr, bias_c, bias_r, alog_c, alog_r, dch, e512, e1024, et, s0)


def _gated_norm_kernel(y_ref, z_ref, g_ref, o_ref):
    v = y_ref[0] * _silu(z_ref[0])
    o = v * lax.rsqrt(jnp.mean(v * v, axis=-1, keepdims=True) + EPS) * g_ref[...]
    o_ref[0] = o.astype(o_ref.dtype)


def _gated_norm(y, zx, g):
    b, t, d = y.shape
    tt = _pick_tile(t, 128, SUBLANE)
    return pl.pallas_call(
        _gated_norm_kernel,
        grid=(b, t // tt),
        in_specs=[pl.BlockSpec((1, tt, d), lambda i, j: (i, j, 0)),
                  pl.BlockSpec((1, tt, d), lambda i, j: (i, j, 0)),
                  pl.BlockSpec((1, d), lambda i, j: (0, 0))],
        out_specs=pl.BlockSpec((1, tt, d), lambda i, j: (i, j, 0)),
        out_shape=jax.ShapeDtypeStruct((b, t, d), BF16),
        compiler_params=_cparams(2),
        name="gated_norm",
    )(y, zx, g.reshape(1, d))


def _rope_tables(pos):
    half = NSA_HEAD_DIM // 2
    inv = jnp.exp(-math.log(ROPE_THETA) * jnp.arange(half, dtype=F32) * 2.0 / NSA_HEAD_DIM)
    ang = pos.astype(F32)[:, None] * inv[None, :]
    cos, sin = jnp.cos(ang), jnp.sin(ang)
    return jnp.concatenate([cos, cos], axis=1), jnp.concatenate([-sin, sin], axis=1)


def _rope_slab(x, cos2, sin2):
    return x * cos2 + pltpu.roll(x, NSA_HEAD_DIM // 2, 1) * sin2


def _rope_kernel(q_ref, kc_ref, ks_ref, vs_ref, kw_ref, vw_ref, cos_ref, sin_ref,
                 qo_ref, kco_ref, kso_ref, kwo_ref, ksb_ref, vsb_ref, kwb_ref, vwb_ref):
    cos2, sin2 = cos_ref[...], sin_ref[...]
    hd = NSA_HEAD_DIM
    scale = hd ** -0.5
    for h in range(NSA_HEADS):
        sl = slice(h * hd, (h + 1) * hd)
        qo_ref[0, :, sl] = (_rope_slab(q_ref[0, :, sl], cos2, sin2) * scale).astype(qo_ref.dtype)
    for h in range(NSA_KV_HEADS):
        sl = slice(h * hd, (h + 1) * hd)
        kco_ref[0, :, sl] = _rope_slab(kc_ref[0, :, sl], cos2, sin2)
        ks = _rope_slab(ks_ref[0, :, sl], cos2, sin2)
        kso_ref[0, :, sl] = ks
        ksb_ref[0, :, sl] = ks.astype(ksb_ref.dtype)
        kw = _rope_slab(kw_ref[0, :, sl], cos2, sin2)
        kwo_ref[0, :, sl] = kw
        kwb_ref[0, :, sl] = kw.astype(kwb_ref.dtype)
    vsb_ref[0] = vs_ref[0].astype(vsb_ref.dtype)
    vwb_ref[0] = vw_ref[0].astype(vwb_ref.dtype)


def _rope_split(proj, pos, lowp=BF16):
    b, t, _ = proj.shape
    tt = _pick_tile(t, 256, SUBLANE)
    cos2, sin2 = _rope_tables(pos)
    kvd = NSA_KV_DIM
    qb = NSA_Q_DIM // kvd
    kv_in = lambda k: pl.BlockSpec((1, tt, kvd), lambda i, j: (i, j, qb + k))
    kv_out = pl.BlockSpec((1, tt, kvd), lambda i, j: (i, j, 0))
    tab = pl.BlockSpec((tt, NSA_HEAD_DIM), lambda i, j: (j, 0))
    sds = lambda n, dt: jax.ShapeDtypeStruct((b, t, n), dt)
    return pl.pallas_call(
        _rope_kernel,
        grid=(b, t // tt),
        in_specs=[pl.BlockSpec((1, tt, NSA_Q_DIM), lambda i, j: (i, j, 0))]
                 + [kv_in(k) for k in (0, 2, 3, 4, 5)] + [tab, tab],
        out_specs=[pl.BlockSpec((1, tt, NSA_Q_DIM), lambda i, j: (i, j, 0))] + [kv_out] * 7,
        out_shape=[sds(NSA_Q_DIM, lowp), sds(kvd, F32), sds(kvd, F32), sds(kvd, F32),
                   sds(kvd, lowp), sds(kvd, lowp), sds(kvd, lowp), sds(kvd, lowp)],
        compiler_params=_cparams(2),
        name="rope_split",
    )(proj, proj, proj, proj, proj, proj, cos2, sin2)


def _cmp_finish_kernel(ab_ref, pe_ref, w2_ref, o_ref):
    ab = ab_ref[0]
    n_sub = ab.shape[0]
    pe_term = pe_ref[0:1, :CMP_HIDDEN] + pe_ref[1:2, CMP_HIDDEN:]
    nxt = pltpu.roll(ab[:, CMP_HIDDEN:], n_sub - 1, 0)
    hid = ab[:, :CMP_HIDDEN] + nxt + pe_term
    act = 0.5 * hid * (1.0 + jnp.tanh(math.sqrt(2.0 / math.pi) * (hid + 0.044715 * hid * hid * hid)))
    o_ref[0] = jnp.dot(act.astype(BF16), w2_ref[...].astype(BF16),
                       preferred_element_type=F32).astype(o_ref.dtype)


def _compress(rows, w1, w2, pe):
    b, t, _ = rows.shape
    n_sub = t // CMP_STRIDE
    ratio = CMP_LEN // CMP_STRIDE
    kdim = CMP_STRIDE * NSA_HEAD_DIM
    x = rows.reshape(b, n_sub, CMP_STRIDE, NSA_KV_HEADS, NSA_HEAD_DIM).transpose(0, 3, 1, 2, 4)
    x = x.reshape(b * NSA_KV_HEADS * n_sub, kdim)
    w1cat = w1.reshape(ratio, kdim, CMP_HIDDEN).transpose(1, 0, 2).reshape(kdim, ratio * CMP_HIDDEN)
    ab = _mm(x, w1cat).reshape(b * NSA_KV_HEADS, n_sub, ratio * CMP_HIDDEN)
    return _cmp_finish(ab, w1cat, w2, pe)


def _cmp_finish(ab, w1cat, w2, pe):
    ratio = CMP_LEN // CMP_STRIDE
    kdim = CMP_STRIDE * NSA_HEAD_DIM
    n_sub = ab.shape[1]
    pe_rows = jnp.zeros((SUBLANE, kdim), F32).at[:ratio].set(pe.reshape(ratio, kdim))
    pe_ab = _mm(pe_rows, w1cat)
    return pl.pallas_call(
        _cmp_finish_kernel,
        grid=(ab.shape[0],),
        in_specs=[pl.BlockSpec((1, n_sub, ratio * CMP_HIDDEN), lambda i: (i, 0, 0)),
                  pl.BlockSpec((SUBLANE, ratio * CMP_HIDDEN), lambda i: (0, 0)),
                  pl.BlockSpec((CMP_HIDDEN, NSA_HEAD_DIM), lambda i: (0, 0))],
        out_specs=pl.BlockSpec((1, n_sub, NSA_HEAD_DIM), lambda i: (i, 0, 0)),
        out_shape=jax.ShapeDtypeStruct((ab.shape[0], n_sub, NSA_HEAD_DIM), BF16),
        compiler_params=_cparams(1),
        name="cmp_finish",
    )(ab, pe_ab, w2)


def _masked_softmax_rows(s, allowed):
    s = jnp.where(allowed, s, NEG_INF)
    m = jnp.max(s, axis=1, keepdims=True)
    p = jnp.where(allowed, jnp.exp(s - m), 0.0)
    return p / jnp.maximum(jnp.sum(p, axis=1, keepdims=True), TINY)


def _nsa_attn_kernel(q_ref, kcmp_ref, vcmp_ref, ks_ref, vs_ref, kw_ref, vw_ref, gp_ref, covt_ref, eg_ref,
                     o_ref, *, n_sel):
    tq, hd, rep = NSA_TQ, NSA_HEAD_DIM, NSA_REP
    rows = rep * tq
    i = pl.program_id(2)
    q0 = i * tq
    q = q_ref[0]
    qs = jnp.concatenate([q[:, r * hd:(r + 1) * hd] for r in range(rep)], axis=0)

    def qpos(shape):
        return q0 + (lax.broadcasted_iota(jnp.int32, shape, 0) & (tq - 1))

    n_cmp_pad = kcmp_ref.shape[1]
    s_c = _dot_nt(qs, kcmp_ref[0])
    blk_end = lax.broadcasted_iota(jnp.int32, (rows, n_cmp_pad), 1) * CMP_STRIDE + (CMP_LEN - 1)
    p_c = _masked_softmax_rows(s_c, blk_end <= qpos((rows, n_cmp_pad)))
    o_cmp = jnp.dot(p_c.astype(BF16), vcmp_ref[0], preferred_element_type=F32)

    p_sum = p_c[0:tq]
    for r in range(1, rep):
        p_sum = p_sum + p_c[r * tq:(r + 1) * tq]
    imp_t = _dot_nt(covt_ref[...], p_sum.astype(BF16))
    nb = 32
    assert n_sel <= nb
    imp_t = imp_t[0:nb]
    jb = lax.broadcasted_iota(jnp.int32, (nb, tq), 0)
    sel_shift = SEL_LEN.bit_length() - 1
    cur = (q0 + lax.broadcasted_iota(jnp.int32, (nb, tq), 1)) >> sel_shift
    forced = (jb == 0) | (jb == cur) | (jb == cur - 1)
    score = jnp.where(jb <= cur, imp_t + jnp.where(forced, FORCE_BONUS, 0.0), -1.0)
    score = jnp.where(jb < n_sel, score, -2.0)
    cnt = jnp.zeros((nb, tq), F32)
    for jp in range(n_sel):
        rowv = score[jp:jp + 1, :]
        before = (rowv > score) | ((rowv == score) & (jb > jp))
        cnt = cnt + jnp.where(before, 1.0, 0.0)
    sel_t = jnp.where(cnt < float(min(SEL_TOPK, n_sel)), 1.0, 0.0)
    sel_t = jnp.concatenate([sel_t, jnp.zeros((LANE - nb, tq), F32)], axis=0)
    sel = sel_t.T.astype(BF16)

    kc = NSA_SLC_KC

    def slc_body(c, carry):
        m, l, acc = carry
        k0 = pl.multiple_of(c * kc, kc)
        kblk = ks_ref[0, pl.ds(k0, kc), :]
        vblk = vs_ref[0, pl.ds(k0, kc), :]
        s = _dot_nt(qs, kblk)
        kidx = k0 + lax.broadcasted_iota(jnp.int32, (LANE, kc), 1)
        expand = jnp.where(lax.broadcasted_iota(jnp.int32, (LANE, kc), 0) == (kidx >> sel_shift), 1.0, 0.0)
        mk = jnp.dot(sel, expand.astype(BF16), preferred_element_type=F32)
        mk = jnp.concatenate([mk] * rep, axis=0)
        kpos = k0 + lax.broadcasted_iota(jnp.int32, (rows, kc), 1)
        allowed = (mk > 0.5) & (kpos <= qpos((rows, kc)))
        s = jnp.where(allowed, s, NEG_INF)
        m_new = jnp.maximum(m, jnp.max(s, axis=1, keepdims=True))
        alpha = jnp.exp(m - m_new)
        p = jnp.where(allowed, jnp.exp(s - m_new), 0.0)
        l = alpha * l + jnp.sum(p, axis=1, keepdims=True)
        acc = alpha * acc + jnp.dot(p.astype(BF16), vblk, preferred_element_type=F32)
        return m_new, l, acc

    n_chunks = (q0 + tq + kc - 1) // kc
    init = (jnp.full((rows, 1), NEG_INF, F32), jnp.zeros((rows, 1), F32), jnp.zeros((rows, hd), F32))
    _, l_s, acc_s = lax.fori_loop(0, n_chunks, slc_body, init)
    o_slc = acc_s / jnp.maximum(l_s, TINY)

    span = NSA_WIN_SPAN
    w0 = pl.multiple_of(jnp.maximum(q0 + tq - span, 0), tq)
    s_w = _dot_nt(qs, kw_ref[0, pl.ds(w0, span), :])
    dist = qpos((rows, span)) - (w0 + lax.broadcasted_iota(jnp.int32, (rows, span), 1))
    p_w = _masked_softmax_rows(s_w, (dist >= 0) & (dist < WINDOW))
    o_win = jnp.dot(p_w.astype(BF16), vw_ref[0, pl.ds(w0, span), :], preferred_element_type=F32)

    gexp = _spread(_sigmoid(gp_ref[0]), eg_ref[0])
    unstack = lambda o: jnp.concatenate([o[r * tq:(r + 1) * tq] for r in range(rep)], axis=1)
    w = rep * hd
    out = gexp[:, 0:w] * unstack(o_cmp) + gexp[:, w:2 * w] * unstack(o_slc) + gexp[:, 2 * w:3 * w] * unstack(o_win)
    o_ref[0] = out.astype(o_ref.dtype)


def _block_coverage(n_cmp, n_sel):
    start = np.arange(n_cmp)[:, None] * CMP_STRIDE
    sel_start = np.arange(n_sel)[None, :] * SEL_LEN
    inter = np.minimum(start + CMP_LEN, sel_start + SEL_LEN) - np.maximum(start, sel_start)
    return (np.clip(inter, 0, None) / CMP_LEN).astype(np.float32)


def _nsa_attention(q, kcmp, vcmp, ks, vs, kw, vw, gate_proj):
    b, t, _ = q.shape
    n_cmp_pad = t // CMP_STRIDE
    n_cmp = (t - CMP_LEN) // CMP_STRIDE + 1
    n_sel = -(-t // SEL_LEN)
    assert n_cmp_pad == LANE and t % NSA_SLC_KC == 0 and t >= NSA_WIN_SPAN
    cov_t = np.zeros((LANE, n_cmp_pad), np.float32)
    cov_t[:n_sel, :n_cmp] = _block_coverage(n_cmp, n_sel).T
    n_gate = gate_proj.shape[-1]
    eg = np.zeros((NSA_KV_HEADS, n_gate, 3 * NSA_REP * NSA_HEAD_DIM), np.float32)
    for g in range(NSA_KV_HEADS):
        for br in range(3):
            for r in range(NSA_REP):
                c0 = (br * NSA_REP + r) * NSA_HEAD_DIM
                eg[g, br * NSA_HEADS + g * NSA_REP + r, c0:c0 + NSA_HEAD_DIM] = 1.0
    gw = NSA_REP * NSA_HEAD_DIM
    seq = pl.BlockSpec((1, t, NSA_HEAD_DIM), lambda i, g, s: (i, 0, g))
    cmp_spec = pl.BlockSpec((1, n_cmp_pad, NSA_HEAD_DIM), lambda i, g, s: (i * NSA_KV_HEADS + g, 0, 0))
    return pl.pallas_call(
        functools.partial(_nsa_attn_kernel, n_sel=n_sel),
        grid=(b, NSA_KV_HEADS, t // NSA_TQ),
        in_specs=[pl.BlockSpec((1, NSA_TQ, gw), lambda i, g, s: (i, s, g)),
                  cmp_spec, cmp_spec, seq, seq, seq, seq,
                  pl.BlockSpec((1, NSA_TQ, n_gate), lambda i, g, s: (i, s, 0)),
                  pl.BlockSpec((LANE, n_cmp_pad), lambda i, g, s: (0, 0)),
                  pl.BlockSpec((1, n_gate, 3 * gw), lambda i, g, s: (g, 0, 0))],
        out_specs=pl.BlockSpec((1, NSA_TQ, gw), lambda i, g, s: (i, s, g)),
        out_shape=jax.ShapeDtypeStruct((b, t, NSA_Q_DIM), BF16),
        compiler_params=_cparams(3),
        name="nsa_attention",
    )(q, kcmp, vcmp, ks, vs, kw, vw, gate_proj, jnp.asarray(cov_t, BF16), jnp.asarray(eg, BF16))


_POOL_PAD = 2 * SUBLANE


def _pool_kernel(h_ref, buf_ref, w_ref, x_ref, gate_ref, o_ref, he_ref, wb_ref, *, tt, pos0):
    t = pl.program_id(1)

    @pl.when((pl.program_id(0) == 0) & (t == 0))
    def _():
        wb_ref[...] = w_ref[...].astype(BF16)

    @pl.when(t == 0)
    def _():
        he_ref[0:_POOL_PAD, :] = buf_ref[0]

    @pl.when(t > 0)
    def _():
        he_ref[0:_POOL_PAD, :] = he_ref[tt:tt + _POOL_PAD, :]

    he_ref[_POOL_PAD:_POOL_PAD + tt, :] = h_ref[0]
    gd = POOL_GROUP_DIM
    q_pos = pos0 + t * tt + lax.broadcasted_iota(jnp.int32, (tt, gd), 0)
    for gi, w in enumerate(POOL_WINDOWS):
        sl = slice(gi * gd, (gi + 1) * gd)
        cur = he_ref[_POOL_PAD:_POOL_PAD + tt, sl]
        win = cur
        for k in range(1, w):
            win = win + he_ref[_POOL_PAD - k:_POOL_PAD - k + tt, sl]
        count = jnp.minimum(w, q_pos + 1).astype(F32)
        mixed = (win / count - cur).astype(BF16)
        acc = jnp.dot(mixed, wb_ref[gi], preferred_element_type=F32)
        o_ref[0, :, sl] = x_ref[0, :, sl] + gate_ref[0, :, sl] * acc


def _pool_mixer(h, buf, pool_w, x, gate, pos0):
    b, t, d = h.shape
    tt = _pick_tile(t, 256, SUBLANE)
    buf16 = jnp.concatenate([jnp.zeros((b, _POOL_PAD - POOL_BUF, d), F32), buf], axis=1)
    if gate.shape[1] == 1:
        gate_spec = pl.BlockSpec((1, 1, d), lambda i, j: (i, 0, 0))
    else:
        gate_spec = pl.BlockSpec((1, tt, d), lambda i, j: (i, j, 0))
    tile = pl.BlockSpec((1, tt, d), lambda i, j: (i, j, 0))
    return pl.pallas_call(
        functools.partial(_pool_kernel, tt=tt, pos0=pos0),
        grid=(b, t // tt),
        in_specs=[tile, pl.BlockSpec((1, _POOL_PAD, d), lambda i, j: (i, 0, 0)),
                  pl.BlockSpec(pool_w.shape, lambda i, j: (0, 0, 0)), tile, gate_spec],
        out_specs=tile,
        out_shape=jax.ShapeDtypeStruct((b, t, d), F32),
        scratch_shapes=[pltpu.VMEM((tt + _POOL_PAD, d), F32), pltpu.VMEM(pool_w.shape, BF16)],
        compiler_params=_cparams(2),
        name="pool_mixer",
    )(h, buf16, pool_w, x, gate)


_CMP_PAGES_PER_STEP = 8


def _cmp_pages_kernel(tbl_ref, *refs):
    del tbl_ref
    npg = _CMP_PAGES_PER_STEP
    pages = refs[:npg]
    w_ref, o_ref, x_ref, wb_ref = refs[npg:]
    sub_per_page = PAGE_SIZE // CMP_STRIDE

    @pl.when((pl.program_id(0) == 0) & (pl.program_id(1) == 0))
    def _():
        wb_ref[...] = w_ref[...].astype(BF16)

    for k in range(npg):
        for g in range(NSA_KV_HEADS):
            r0 = (g * npg + k) * sub_per_page
            for s in range(CMP_STRIDE):
                x_ref[r0:r0 + sub_per_page, s * NSA_HEAD_DIM:(s + 1) * NSA_HEAD_DIM] = (
                    pages[k][0, pl.ds(s, sub_per_page, stride=CMP_STRIDE), g, :])
    ab = jnp.dot(x_ref[...].astype(BF16), wb_ref[...], preferred_element_type=F32)
    o_ref[0] = ab.reshape(NSA_KV_HEADS, npg * sub_per_page, ab.shape[-1])


def _cmp_pages(cache, layer, page_table, w1):
    b, n_pages = page_table.shape
    npg = _CMP_PAGES_PER_STEP
    assert n_pages % npg == 0
    ratio = CMP_LEN // CMP_STRIDE
    kdim = CMP_STRIDE * NSA_HEAD_DIM
    w1cat = w1.reshape(ratio, kdim, CMP_HIDDEN).transpose(1, 0, 2).reshape(kdim, ratio * CMP_HIDDEN)
    sub_per_page = PAGE_SIZE // CMP_STRIDE
    rows = NSA_KV_HEADS * npg * sub_per_page
    page_spec = lambda k: pl.BlockSpec((None, 1, PAGE_SIZE, NSA_KV_HEADS, NSA_HEAD_DIM),
                                       lambda i, p, tbl: (layer, tbl[i, p * npg + k], 0, 0, 0))
    out = pl.pallas_call(
        _cmp_pages_kernel,
        grid_spec=pltpu.PrefetchScalarGridSpec(
            num_scalar_prefetch=1,
            grid=(b, n_pages // npg),
            in_specs=[page_spec(k) for k in range(npg)]
                     + [pl.BlockSpec(w1cat.shape, lambda i, p, tbl: (0, 0))],
            out_specs=pl.BlockSpec((1, NSA_KV_HEADS, npg * sub_per_page, ratio * CMP_HIDDEN),
                                   lambda i, p, tbl: (i, 0, p, 0)),
            scratch_shapes=[pltpu.VMEM((rows, kdim), F32), pltpu.VMEM(w1cat.shape, BF16)]),
        out_shape=jax.ShapeDtypeStruct((b, NSA_KV_HEADS, n_pages * sub_per_page, ratio * CMP_HIDDEN), F32),
        compiler_params=_cparams(2),
        name="cmp_pages",
    )(page_table, *([cache] * npg), w1cat)
    return out.reshape(b * NSA_KV_HEADS, n_pages * sub_per_page, ratio * CMP_HIDDEN), w1cat


def _joint_softmax(s1, ok1, s2, ok2):
    s1 = jnp.where(ok1, s1, NEG_INF)
    s2 = jnp.where(ok2, s2, NEG_INF)
    m = jnp.maximum(jnp.max(s1, axis=1, keepdims=True), jnp.max(s2, axis=1, keepdims=True))
    p1 = jnp.where(ok1, jnp.exp(s1 - m), 0.0)
    p2 = jnp.where(ok2, jnp.exp(s2 - m), 0.0)
    den = jnp.maximum(jnp.sum(p1, axis=1, keepdims=True) + jnp.sum(p2, axis=1, keepdims=True), TINY)
    return p1 / den, p2 / den


def _dec_cmp_win_kernel(q_ref, kcmp_ref, vcmp_ref, kwin_ref, vwin_ref, kwn_ref, vwn_ref, cov_ref, rsum_ref,
                        ocmp_ref, owin_ref, imp_ref, *, t, n_cmp, pos0, nbuf):
    q = q_ref[0].astype(BF16)
    rows = q.shape[0]
    tok = lax.broadcasted_iota(jnp.int32, (rows, 1), 0) & (t - 1)
    q_pos = pos0 + tok

    n_pad = kcmp_ref.shape[1]
    s_c = _dot_nt(q, kcmp_ref[0])
    nidx = lax.broadcasted_iota(jnp.int32, (rows, n_pad), 1)
    ok = (nidx < n_cmp) & (nidx * CMP_STRIDE + (CMP_LEN - 1) <= q_pos)
    p_c = _masked_softmax_rows(s_c, ok)
    ocmp_ref[0] = jnp.dot(p_c.astype(BF16), vcmp_ref[0], preferred_element_type=F32)
    p_sum = _dot_exact(rsum_ref[...], p_c)
    imp_ref[0] = jnp.dot(p_sum.astype(BF16), cov_ref[...], preferred_element_type=F32)

    s1 = _dot_nt(q, kwin_ref[0].astype(BF16))
    kp1 = (pos0 - nbuf) + lax.broadcasted_iota(jnp.int32, (rows, nbuf), 1)
    d1 = q_pos - kp1
    kn = kwn_ref[0].astype(BF16)
    s2 = _dot_nt(q, kn)
    i2 = lax.broadcasted_iota(jnp.int32, (rows, kn.shape[0]), 1)
    d2 = tok - i2
    p1, p2 = _joint_softmax(s1, (d1 >= 0) & (d1 < WINDOW), s2, (d2 >= 0) & (i2 < t))
    owin_ref[0] = (jnp.dot(p1.astype(BF16), vwin_ref[0].astype(BF16), preferred_element_type=F32)
                   + jnp.dot(p2.astype(BF16), vwn_ref[0].astype(BF16), preferred_element_type=F32))


def _dec_cmp_win(q_stk, kcmp, vcmp, win_k, win_v, layer, kw_new, vw_new, n_cmp, n_sel, pos0, t):
    bg, rows, hd = q_stk.shape
    n_pad = kcmp.shape[1]
    nbuf = win_k.shape[2]
    sel_pad = -(-n_sel // LANE) * LANE
    cov = np.zeros((n_pad, sel_pad), np.float32)
    cov[:n_cmp, :n_sel] = _block_coverage(n_cmp, n_sel)
    rsum = np.zeros((SUBLANE, rows), np.float32)
    for r in range(NSA_REP):
        for tt in range(t):
            rsum[tt, r * t + tt] = 1.0
    per_bg = lambda n: pl.BlockSpec((1, n, hd), lambda i: (i, 0, 0))
    per_b = lambda n: pl.BlockSpec((1, n, hd), lambda i: (i // NSA_KV_HEADS, 0, i % NSA_KV_HEADS))
    win_spec = per_b(nbuf)
    win_k, win_v = [w[layer].reshape(w.shape[1], nbuf, NSA_KV_DIM) for w in (win_k, win_v)]
    return pl.pallas_call(
        functools.partial(_dec_cmp_win_kernel, t=t, n_cmp=n_cmp, pos0=pos0, nbuf=nbuf),
        grid=(bg,),
        in_specs=[per_bg(rows), per_bg(n_pad), per_bg(n_pad), win_spec, win_spec,
                  per_b(SUBLANE), per_b(SUBLANE),
                  pl.BlockSpec(cov.shape, lambda i: (0, 0)), pl.BlockSpec(rsum.shape, lambda i: (0, 0))],
        out_specs=[per_bg(rows), per_bg(rows), pl.BlockSpec((1, SUBLANE, sel_pad), lambda i: (i, 0, 0))],
        out_shape=[jax.ShapeDtypeStruct((bg, rows, hd), F32), jax.ShapeDtypeStruct((bg, rows, hd), F32),
                   jax.ShapeDtypeStruct((bg, SUBLANE, sel_pad), F32)],
        compiler_params=_cparams(1),
        name="dec_cmp_win",
    )(q_stk, kcmp, vcmp, win_k, win_v, kw_new, vw_new, jnp.asarray(cov, BF16), jnp.asarray(rsum))


def _dec_topk_kernel(imp_ref, o_ref, *, t, n_sel, pos0, top):
    imp_t = imp_ref[...].T
    shape = imp_t.shape
    jb = lax.broadcasted_iota(jnp.int32, shape, 0)
    q_pos = pos0 + (lax.broadcasted_iota(jnp.int32, shape, 1) & (t - 1))
    cur = q_pos >> (SEL_LEN.bit_length() - 1)
    forced = (jb == 0) | (jb == cur) | (jb == cur - 1)
    score = jnp.where(jb <= cur, imp_t + jnp.where(forced, FORCE_BONUS, 0.0), -1.0)
    score = jnp.where(jb < n_sel, score, -2.0)
    jbf = jb.astype(F32)
    for k in range(top):
        m = jnp.max(score, axis=0, keepdims=True)
        idx = jnp.min(jnp.where(score == m, jbf, float(shape[0])), axis=0, keepdims=True)
        o_ref[k:k + 1, :] = idx.astype(jnp.int32)
        score = jnp.where(jbf == idx, -3.0, score)


def _dec_topk(imp, n_sel, pos0, t):
    slots, sel_pad = imp.shape
    top = min(SEL_TOPK, n_sel)
    return pl.pallas_call(
        functools.partial(_dec_topk_kernel, t=t, n_sel=n_sel, pos0=pos0, top=top),
        grid=(1,),
        in_specs=[pl.BlockSpec((slots, sel_pad), lambda i: (0, 0))],
        out_specs=pl.BlockSpec((top, slots), lambda i: (0, 0)),
        out_shape=jax.ShapeDtypeStruct((top, slots), jnp.int32),
        compiler_params=_cparams(1),
        name="dec_topk",
    )(imp)


def _dec_slc_kernel(sel_ref, tbl_ref, *refs, t, n_past_blk, pos0, top):
    del tbl_ref
    kblk, vblk = refs[:top], refs[top:2 * top]
    q_ref, kn_ref, vn_ref, o_ref, kcat_ref, vcat_ref = refs[2 * top:]
    slot = pl.program_id(0)
    tok = slot & (t - 1)
    q_pos = pos0 + tok
    head = (slot // t) % NSA_KV_HEADS
    for g in range(NSA_KV_HEADS):
        @pl.when(head == g)
        def _(g=g):
            for k in range(top):
                kcat_ref[k * SEL_LEN:(k + 1) * SEL_LEN, :] = kblk[k][0, :, g, :].astype(BF16)
                vcat_ref[k * SEL_LEN:(k + 1) * SEL_LEN, :] = vblk[k][0, :, g, :].astype(BF16)
    q = q_ref[0].astype(BF16)
    rows = q.shape[0]
    n_keys = top * SEL_LEN
    s1 = _dot_nt(q, kcat_ref[...])
    col = lax.broadcasted_iota(jnp.int32, (rows, n_keys), 1)
    cblk = col >> (SEL_LEN.bit_length() - 1)
    kpos = col & (SEL_LEN - 1)
    in_past = cblk < 0
    new_lim = jnp.int32(-1)
    for k in range(top):
        blk = sel_ref[slot, k]
        kpos = kpos + jnp.where(cblk == k, blk * SEL_LEN, 0)
        in_past = in_past | (cblk == jnp.where(blk < n_past_blk, k, -1))
        new_lim = jnp.where(blk == n_past_blk, tok, new_lim)
    kn = kn_ref[0].astype(BF16)
    s2 = _dot_nt(q, kn)
    i2 = lax.broadcasted_iota(jnp.int32, (rows, kn.shape[0]), 1)
    ok2 = (i2 <= new_lim) & (i2 < t)
    p1, p2 = _joint_softmax(s1, in_past & (kpos <= q_pos), s2, ok2)
    o_ref[0] = (jnp.dot(p1.astype(BF16), vcat_ref[...], preferred_element_type=F32)
                + jnp.dot(p2.astype(BF16), vn_ref[0].astype(BF16), preferred_element_type=F32))


def _dec_slc(sel, page_table, cache_k, cache_v, layer, q_slot, ks_new, vs_new, pos0, t):
    slots, top = sel.shape
    hd = NSA_HEAD_DIM
    n_past_blk = page_table.shape[1] * PAGE_SIZE // SEL_LEN
    halves = PAGE_SIZE // SEL_LEN
    per_b = NSA_KV_HEADS * t

    def blk_spec(k):
        def index(s, sel_ref, tbl_ref):
            blk = jnp.minimum(sel_ref[s, k], n_past_blk - 1)
            page = tbl_ref[s // per_b, blk // halves]
            return layer, page, blk % halves, 0, 0
        return pl.BlockSpec((None, 1, SEL_LEN, NSA_KV_HEADS, hd), index)

    new_spec = pl.BlockSpec((1, SUBLANE, hd), lambda s, a, b_: (s // per_b, 0, (s // t) % NSA_KV_HEADS))
    slot_spec = pl.BlockSpec((1, SUBLANE, hd), lambda s, a, b_: (s, 0, 0))
    return pl.pallas_call(
        functools.partial(_dec_slc_kernel, t=t, n_past_blk=n_past_blk, pos0=pos0, top=top),
        grid_spec=pltpu.PrefetchScalarGridSpec(
            num_scalar_prefetch=2,
            grid=(slots,),
            in_specs=[blk_spec(k) for k in range(top)] + [blk_spec(k) for k in range(top)]
                     + [slot_spec, new_spec, new_spec],
            out_specs=slot_spec,
            scratch_shapes=[pltpu.VMEM((top * SEL_LEN, hd), BF16), pltpu.VMEM((top * SEL_LEN, hd), BF16)]),
        out_shape=jax.ShapeDtypeStruct((slots, SUBLANE, hd), F32),
        compiler_params=_cparams(1),
        name="dec_slc",
    )(sel, page_table, *([cache_k] * top), *([cache_v] * top), q_slot, ks_new, vs_new)


def _dec_combine_kernel(gp_ref, eg_ref, ocmp_ref, oslc_ref, owin_ref, o_ref):
    gexp = _spread(_sigmoid(gp_ref[...]), eg_ref[...])
    w = NSA_Q_DIM
    out = gexp[:, 0:w] * ocmp_ref[...] + gexp[:, w:2 * w] * oslc_ref[...] + gexp[:, 2 * w:3 * w] * owin_ref[...]
    o_ref[...] = out.astype(o_ref.dtype)


def _dec_combine(gate_proj, o_cmp, o_slc, o_win):
    rows, n_gate = gate_proj.shape
    eg = np.zeros((n_gate, 3 * NSA_Q_DIM), np.float32)
    for br in range(3):
        for h in range(NSA_HEADS):
            c0 = br * NSA_Q_DIM + h * NSA_HEAD_DIM
            eg[br * NSA_HEADS + h, c0:c0 + NSA_HEAD_DIM] = 1.0
    full = lambda a: pl.BlockSpec(a.shape, lambda i: (0, 0))
    args = (gate_proj, jnp.asarray(eg, BF16), o_cmp, o_slc, o_win)
    return pl.pallas_call(
        _dec_combine_kernel,
        grid=(1,),
        in_specs=[full(a) for a in args],
        out_specs=pl.BlockSpec((rows, NSA_Q_DIM), lambda i: (0, 0)),
        out_shape=jax.ShapeDtypeStruct((rows, NSA_Q_DIM), BF16),
        compiler_params=_cparams(1),
        name="dec_combine",
    )(*args)


def _nsa_decode(proj, gate_proj, pos0, P, j, caches, page_table):
    b, t, _ = proj.shape
    cache_ck, cache_cv, cache_sk, cache_sv, win_k, win_v = caches
    assert t & (t - 1) == 0 and t <= SUBLANE
    hd, g, rep = NSA_HEAD_DIM, NSA_KV_HEADS, NSA_REP
    pad_t = lambda a: jnp.pad(a, ((0, 0), (0, SUBLANE - t), (0, 0)))
    q, kc, ks, kw, _, _, _, _ = _rope_split(pad_t(proj), pos0 + jnp.arange(SUBLANE), lowp=F32)
    q, kc, ks, kw = q[:, :t], kc[:, :t], ks[:, :t], kw[:, :t]
    vc, vs, vw = [proj[..., NSA_Q_DIM + k * NSA_KV_DIM:NSA_Q_DIM + (k + 1) * NSA_KV_DIM] for k in (1, 3, 5)]

    tk = page_table.shape[1] * PAGE_SIZE + t
    n_cmp = (tk - CMP_LEN) // CMP_STRIDE + 1
    n_sel = -(-tk // SEL_LEN)
    assert (n_cmp + 1) * CMP_STRIDE <= page_table.shape[1] * PAGE_SIZE
    ab_k, w1k = _cmp_pages(cache_ck, j, page_table, P['nsa_cmpk_w1'][j])
    ab_v, w1v = _cmp_pages(cache_cv, j, page_table, P['nsa_cmpv_w1'][j])
    kcmp = _cmp_finish(ab_k, w1k, P['nsa_cmpk_w2'][j], P['nsa_cmpk_pe'][j])
    vcmp = _cmp_finish(ab_v, w1v, P['nsa_cmpv_w2'][j], P['nsa_cmpv_pe'][j])

    q5 = q.reshape(b, t, g, rep, hd)
    q_stk = q5.transpose(0, 2, 3, 1, 4).reshape(b * g, rep * t, hd)
    o_cmp, o_win, imp = _dec_cmp_win(q_stk, kcmp, vcmp, win_k, win_v, j, pad_t(kw), pad_t(vw),
                                     n_cmp, n_sel, pos0, t)
    sel = _dec_topk(imp[:, :t].reshape(b * g * t, -1), n_sel, pos0, t).T
    q_slot = q5.transpose(0, 2, 1, 3, 4).reshape(b * g * t, rep, hd)
    q_slot = jnp.pad(q_slot, ((0, 0), (0, SUBLANE - rep), (0, 0)))
    o_slc = _dec_slc(sel, page_table, cache_sk, cache_sv, j, q_slot, pad_t(ks), pad_t(vs), pos0, t)

    unstk = lambda o: o.reshape(b, g, rep, t, hd).transpose(0, 3, 1, 2, 4).reshape(b * t, NSA_Q_DIM)
    o_slc = o_slc[:, :rep].reshape(b, g, t, rep, hd).transpose(0, 2, 1, 3, 4).reshape(b * t, NSA_Q_DIM)
    o = _dec_combine(gate_proj.reshape(b * t, -1), unstk(o_cmp), o_slc, unstk(o_win))
    heads = lambda a: a.reshape(b, t, g, hd)
    kw_new = jnp.concatenate([win_k[j][:, t:], heads(kw)], axis=1)
    vw_new = jnp.concatenate([win_v[j][:, t:], heads(vw)], axis=1)
    return o.reshape(b, t, NSA_Q_DIM), kc, vc, ks, vs, kw_new, vw_new


def _run_trunk(x, mods, pos0, past, P):
    b, t, d = x.shape
    fresh = past is None
    per_batch = t % SUBLANE == 0 and t >= LANE
    names = ('ssd_state', 'ssd_conv', 'cmp_k', 'cmp_v', 'slc_k', 'slc_v', 'win_k', 'win_v', 'pool', 'ffn')
    new = {n: [] for n in names}

    bm, tm = (b, t) if per_batch else (1, b * t)

    def mod_rows(v):
        return v[:, None, :] if per_batch else jnp.repeat(v, t, axis=0)[None]

    def mm_rows(a, w, **kw):
        return _mm(a.reshape(b * t, a.shape[-1]), w, **kw)

    def pad_cols(w):
        return jnp.pad(w, ((0, 0), (0, LANE - w.shape[1])))

    def mm_res(a, w, layer, xres, gate):
        out = _mm(a.reshape(b * t, a.shape[-1]), w, layer=layer, res=xres.reshape(b * t, d), gate=gate,
                  rows_per_gate=t if per_batch else None)
        return out.reshape(b, t, d)

    for i in range(DEPTH):
        kind, j = i % N_MIXERS, i // N_MIXERS
        sh1, sc1, g1, sh2, sc2, g2 = [mod_rows(v) for v in jnp.split(mods[i], 6, axis=-1)]
        h_dtype = F32 if kind == 2 else BF16
        h = _normmod(x.reshape(bm, tm, d), P['norm1_g'][i], sh1, sc1, h_dtype).reshape(b, t, d)
        if kind == 0:
            if fresh:
                conv_buf = jnp.zeros((b, SSD_CONV - 1, SSD_CONV_DIM), F32)
                s0 = jnp.zeros((b, SSD_HEADS, SSD_HEAD_DIM, SSD_STATE), F32)
            else:
                conv_buf, s0 = past['ssd_conv'][j], past['ssd_state'][j]
            w_in = P['ssd_w_in']
            nzx = SSD_D_INNER + SSD_CONV_DIM
            dt_raw = mm_rows(h, pad_cols(w_in[j, :, nzx:])).reshape(b, t, LANE)[..., :SSD_HEADS]
            if per_batch:
                zx = mm_rows(h, w_in, layer=j, n_cols=SSD_D_INNER).reshape(b, t, SSD_D_INNER)
                xbc, conv_new = _mm_conv_silu(h.reshape(b * t, d), w_in, j, SSD_D_INNER, SSD_CONV_DIM, t,
                                              conv_buf, P['ssd_conv_w'][j], P['ssd_conv_b'][j])
                xbc = xbc.reshape(b, t, SSD_CONV_DIM)
            else:
                zx = mm_rows(h, w_in, layer=j, n_cols=nzx).reshape(b, t, nzx)
                xbc, conv_new = _conv_silu(zx, SSD_D_INNER, SSD_CONV_DIM, conv_buf,
                                           P['ssd_conv_w'][j], P['ssd_conv_b'][j])
            tp = -(-t // SSD_CHUNK) * SSD_CHUNK
            if tp != t:
                xbc_p = jnp.pad(xbc, ((0, 0), (0, tp - t), (0, 0)))
                dt_p = jnp.pad(dt_raw, ((0, 0), (0, tp - t), (0, 0)))
            else:
                xbc_p, dt_p = xbc, dt_raw
            y, s_new = _ssd_scan(xbc_p, dt_p, P['ssd_dt_bias'][j], P['ssd_a_log'][j], P['ssd_d'][j], s0,
                                 t_valid=min(t, SSD_CHUNK))
            yn = _gated_norm(y[:, :t], zx, P['ssd_norm_g'][j])
            x = mm_res(yn, P['ssd_w_out'], j, x, g1)
            new['ssd_conv'].append(conv_new)
            new['ssd_state'].append(s_new)
        elif kind == 1:
            w_in = P['nsa_w_in']
            npj = NSA_Q_DIM + 6 * NSA_KV_DIM
            proj = mm_rows(h, w_in, layer=j, n_cols=npj).reshape(b, t, npj)
            gate_proj = mm_rows(h, pad_cols(w_in[j, :, npj:])).reshape(b, t, LANE)
            if fresh:
                q_pos = pos0 + jnp.arange(t)
                q, kc, ks, kw, ks_b, vs_b, kw_b, vw_b = _rope_split(proj, q_pos)
                vc, vs, vw = [proj[..., NSA_Q_DIM + k * NSA_KV_DIM:NSA_Q_DIM + (k + 1) * NSA_KV_DIM]
                              for k in (1, 3, 5)]
                kcmp = _compress(kc, P['nsa_cmpk_w1'][j], P['nsa_cmpk_w2'][j], P['nsa_cmpk_pe'][j])
                vcmp = _compress(vc, P['nsa_cmpv_w1'][j], P['nsa_cmpv_w2'][j], P['nsa_cmpv_pe'][j])
                o = _nsa_attention(q, kcmp, vcmp, ks_b, vs_b, kw_b, vw_b, gate_proj)
                keep = min(WINDOW, t)
                kw_new, vw_new = kw[:, t - keep:], vw[:, t - keep:]
            else:
                o, kc, vc, ks, vs, kw_new, vw_new = _nsa_decode(
                    proj, gate_proj, pos0, P, j, past['nsa'][j], past['page_table'])
            x = mm_res(o, P['nsa_w_out'], j, x, g1)
            shp = (b, t, NSA_KV_HEADS, NSA_HEAD_DIM)
            for n, v in (('cmp_k', kc), ('cmp_v', vc), ('slc_k', ks), ('slc_v', vs)):
                new[n].append(v.reshape(shp))
            new['win_k'].append(kw_new.reshape(b, -1, NSA_KV_HEADS, NSA_HEAD_DIM))
            new['win_v'].append(vw_new.reshape(b, -1, NSA_KV_HEADS, NSA_HEAD_DIM))
        else:
            buf = jnp.zeros((b, POOL_BUF, d), F32) if fresh else past['pool'][j]
            gate = (g1 * P['pool_scale'][j]).reshape(b, -1, d)
            x = _pool_mixer(h, buf, P['pool_w'][j], x, gate, pos0)
            new['pool'].append(jnp.concatenate([buf, h], axis=1)[:, -POOL_BUF:])
        h2 = _normmod(x.reshape(bm, tm, d), P['norm2_g'][i], sh2, sc2, BF16)
        fbuf = jnp.zeros((b, FFN_CONV - 1, 2 * D_FF), F32) if fresh else past['ffn'][i]
        if per_batch:
            act, fbuf_new = _mm_conv_glu(h2.reshape(b * t, d), P['ffn_w_up'], i, t, fbuf,
                                         P['ffn_conv_w'][i], P['ffn_conv_b'][i])
        else:
            u = mm_rows(h2, P['ffn_w_up'], layer=i).reshape(b, t, 2 * D_FF)
            act, fbuf_new = _conv_glu(u, fbuf, P['ffn_conv_w'][i], P['ffn_conv_b'][i])
        x = mm_res(act, P['ffn_w_down'], i, x, g2)
        new['ffn'].append(fbuf_new)
    zero = jnp.zeros((bm, 1, d), F32)
    y = _normmod(x.reshape(bm, tm, d), P['final_g'], zero, zero, F32).reshape(b, t, d)
    return y, {n: jnp.stack(v) for n, v in new.items()}


def kernel(x_prompt, x_sample, state_ssd, state_ssd_conv, cache_cmp_k, cache_cmp_v, cache_slc_k, cache_slc_v,
           cache_win_k, cache_win_v, state_pool, state_ffn_conv, page_table, c_prompt, c_sample,
           ada_w, ada_b, norm1_g, norm2_g, final_g,
           ssd_w_in, ssd_conv_w, ssd_conv_b, ssd_dt_bias, ssd_a_log, ssd_d, ssd_norm_g, ssd_w_out,
           nsa_w_in, nsa_cmpk_w1, nsa_cmpk_w2, nsa_cmpk_pe, nsa_cmpv_w1, nsa_cmpv_w2, nsa_cmpv_pe, nsa_w_out,
           pool_w, pool_scale, ffn_w_up, ffn_conv_w, ffn_conv_b, ffn_w_down):
    P = dict(norm1_g=norm1_g, norm2_g=norm2_g, final_g=final_g,
             ssd_w_in=ssd_w_in, ssd_conv_w=ssd_conv_w, ssd_conv_b=ssd_conv_b, ssd_dt_bias=ssd_dt_bias,
             ssd_a_log=ssd_a_log, ssd_d=ssd_d, ssd_norm_g=ssd_norm_g, ssd_w_out=ssd_w_out,
             nsa_w_in=nsa_w_in, nsa_cmpk_w1=nsa_cmpk_w1, nsa_cmpk_w2=nsa_cmpk_w2, nsa_cmpk_pe=nsa_cmpk_pe,
             nsa_cmpv_w1=nsa_cmpv_w1, nsa_cmpv_w2=nsa_cmpv_w2, nsa_cmpv_pe=nsa_cmpv_pe, nsa_w_out=nsa_w_out,
             pool_w=pool_w, pool_scale=pool_scale,
             ffn_w_up=ffn_w_up, ffn_conv_w=ffn_conv_w, ffn_conv_b=ffn_conv_b, ffn_w_down=ffn_w_down)
    nbp, nbs = c_prompt.shape[0], c_sample.shape[0]
    c_rows = -(-(nbp + nbs) // (2 * SUBLANE)) * (2 * SUBLANE)
    c_all = jnp.concatenate([c_prompt, c_sample, jnp.zeros((c_rows - nbp - nbs, D_MODEL), F32)], axis=0)
    mods = [_mm(c_all, ada_w, layer=i, a_silu=True, bias=ada_b[i]) for i in range(DEPTH)]
    mods_p = [m[:nbp] for m in mods]
    mods_s = [m[nbp:nbp + nbs] for m in mods]

    y_prompt, sp = _run_trunk(x_prompt, mods_p, 0, None, P)
    past_len = page_table.shape[1] * PAGE_SIZE
    nsa_past = [(cache_cmp_k, cache_cmp_v, cache_slc_k, cache_slc_v, cache_win_k, cache_win_v)
                for _ in range(cache_cmp_k.shape[0])]
    past = dict(ssd_state=state_ssd, ssd_conv=state_ssd_conv, nsa=nsa_past, page_table=page_table,
                pool=state_pool, ffn=state_ffn_conv)
    y_sample, ss = _run_trunk(x_sample, mods_s, past_len, past, P)
    return (y_prompt, y_sample,
            sp['ssd_state'], ss['ssd_state'], sp['ssd_conv'], ss['ssd_conv'],
            sp['cmp_k'], ss['cmp_k'], sp['cmp_v'], ss['cmp_v'],
            sp['slc_k'], ss['slc_k'], sp['slc_v'], ss['slc_v'],
            sp['win_k'], ss['win_k'], sp['win_v'], ss['win_v'],
            sp['pool'], ss['pool'], sp['ffn'], ss['ffn'])
```

```python
import functools
import math

import jax
import jax.numpy as jnp
import numpy as np
from jax import lax
from jax.experimental import pallas as pl
from jax.experimental.pallas import tpu as pltpu

D_MODEL = 2048
DEPTH = 4
PAGE_SIZE = 128
N_MIXERS = 3

SSD_D_INNER = 2 * D_MODEL
SSD_HEAD_DIM = 64
SSD_HEADS = SSD_D_INNER // SSD_HEAD_DIM
SSD_GROUPS = 8
SSD_REP = SSD_HEADS // SSD_GROUPS
SSD_STATE = 128
SSD_CONV = 4
SSD_CHUNK = 128
SSD_GROUP_DIM = SSD_REP * SSD_HEAD_DIM
SSD_CONV_DIM = SSD_D_INNER + 2 * SSD_GROUPS * SSD_STATE

NSA_HEADS = 16
NSA_KV_HEADS = 4
NSA_HEAD_DIM = D_MODEL // NSA_HEADS
NSA_REP = NSA_HEADS // NSA_KV_HEADS
NSA_Q_DIM = NSA_HEADS * NSA_HEAD_DIM
NSA_KV_DIM = NSA_KV_HEADS * NSA_HEAD_DIM
CMP_LEN = 32
CMP_STRIDE = 16
CMP_HIDDEN = 256
SEL_LEN = 64
SEL_TOPK = 16
WINDOW = 512
WIN_Q_BLOCK = 128
SEL_Q_BLOCK = 64
ROPE_THETA = 10000.0
FORCE_BONUS = 1000.0

POOL_WINDOWS = (2, 4, 8, 16)
POOL_GROUPS = len(POOL_WINDOWS)
POOL_GROUP_DIM = D_MODEL // POOL_GROUPS
POOL_BUF = max(POOL_WINDOWS) - 1

D_FF = 5632
FFN_CONV = 3

EPS = 1e-6
NEG_INF = -1e30
TINY = 1e-30

V7X_VMEM_LIMIT_BYTES = 52 * 1024 * 1024
LANE = 128
SUBLANE = 8

F32 = jnp.float32
BF16 = jnp.bfloat16
HIGHEST = lax.Precision.HIGHEST

NSA_TQ = 128
NSA_SLC_KC = 512
NSA_WIN_SPAN = WINDOW + NSA_TQ


def _cparams(n_axes):
    return pltpu.CompilerParams(dimension_semantics=("arbitrary",) * n_axes,
                                vmem_limit_bytes=V7X_VMEM_LIMIT_BYTES)


def _pick_tile(dim, pref, align):
    t = min(pref, dim)
    t -= t % align
    while t >= align:
        if dim % t == 0:
            return t
        t -= align
    return dim


def _sigmoid(x):
    return 1.0 / (1.0 + jnp.exp(-x))


def _silu(x):
    return x * _sigmoid(x)


def _dot_nt(a, b):
    return lax.dot_general(a, b, (((1,), (1,)), ((), ())), preferred_element_type=F32)


def _dot_tn(a, b):
    return lax.dot_general(a, b, (((0,), (0,)), ((), ())), preferred_element_type=F32)


def _dot_exact(a, b):
    return jnp.dot(a, b, precision=HIGHEST, preferred_element_type=F32)


def _split3(x):
    hi = x.astype(BF16)
    r = x - hi.astype(F32)
    mid = r.astype(BF16)
    lo = (r - mid.astype(F32)).astype(BF16)
    return hi, mid, lo


def _spread(x, e):
    hi, mid, lo = _split3(x)
    dot = lambda p: jnp.dot(p, e, preferred_element_type=F32)
    return dot(hi) + dot(mid) + dot(lo)


def _spread_rows(e, x):
    hi, mid, lo = _split3(x)
    dot = lambda p: jnp.dot(e, p, preferred_element_type=F32)
    return dot(hi) + dot(mid) + dot(lo)


def _normmod_kernel(x_ref, g_ref, sh_ref, sc_ref, o_ref):
    x = x_ref[0].astype(F32)
    y = x * lax.rsqrt(jnp.mean(x * x, axis=-1, keepdims=True) + EPS) * g_ref[...]
    y = y * (1.0 + sc_ref[0]) + sh_ref[0]
    o_ref[0] = y.astype(o_ref.dtype)


def _normmod(x, g, shift, scale, out_dtype):
    b, t, d = x.shape
    r = shift.shape[1]
    tt = _pick_tile(t, 256, SUBLANE)
    if r == 1:
        mod_spec = pl.BlockSpec((1, 1, d), lambda i, j: (i, 0, 0))
    else:
        mod_spec = pl.BlockSpec((1, tt, d), lambda i, j: (i, j, 0))
    return pl.pallas_call(
        _normmod_kernel,
        grid=(b, t // tt),
        in_specs=[pl.BlockSpec((1, tt, d), lambda i, j: (i, j, 0)),
                  pl.BlockSpec((1, d), lambda i, j: (0, 0)),
                  mod_spec, mod_spec],
        out_specs=pl.BlockSpec((1, tt, d), lambda i, j: (i, j, 0)),
        out_shape=jax.ShapeDtypeStruct((b, t, d), out_dtype),
        compiler_params=_cparams(2),
        name="normmod",
    )(x, g.reshape(1, d), shift, scale)


def _mm_kernel(*refs, a_silu, has_bias, has_res):
    it = iter(refs)
    a_ref, w_ref = next(it), next(it)
    bias_ref = next(it) if has_bias else None
    res_ref = next(it) if has_res else None
    gate_ref = next(it) if has_res else None
    o_ref, wb_ref = next(it), next(it)

    @pl.when(pl.program_id(1) == 0)
    def _():
        wb_ref[...] = w_ref[...].astype(BF16)

    a = a_ref[...]
    if a_silu:
        a = _silu(a.astype(F32))
    acc = jnp.dot(a.astype(BF16), wb_ref[...], preferred_element_type=F32)
    if has_bias:
        acc = acc + bias_ref[...]
    if has_res:
        acc = res_ref[...] + gate_ref[0] * acc
    o_ref[...] = acc.astype(o_ref.dtype)


def _w_spec(w, layer, k, tn, col_block):
    if w.ndim == 2:
        return pl.BlockSpec((k, tn), lambda j, i: (0, col_block(j)))
    return pl.BlockSpec((None, k, tn), lambda j, i: (layer, 0, col_block(j)))


def _mm(a, w, *, layer=None, n_cols=None, a_silu=False, bias=None, res=None, gate=None, rows_per_gate=None,
        out_dtype=F32, tm=1024, tn=1024):
    m, k = a.shape
    assert w.shape[-2] == k
    n = w.shape[-1] if n_cols is None else n_cols
    tm = _pick_tile(m if rows_per_gate is None else rows_per_gate, tm, SUBLANE)
    tn = _pick_tile(n, tn, LANE)
    while k * tn * 10 > 30 * 1024 * 1024 and tn % (2 * LANE) == 0:
        tn //= 2
    while 2 * k * tm * a.dtype.itemsize > 16 * 1024 * 1024 and tm % (2 * SUBLANE) == 0:
        tm //= 2
    assert m % tm == 0 and n % tn == 0
    in_specs = [pl.BlockSpec((tm, k), lambda j, i: (i, 0)),
                _w_spec(w, layer, k, tn, lambda j: j)]
    args = [a, w]
    if bias is not None:
        in_specs.append(pl.BlockSpec((1, tn), lambda j, i: (0, j)))
        args.append(bias.reshape(1, -1))
    if res is not None:
        in_specs.append(pl.BlockSpec((tm, tn), lambda j, i: (i, j)))
        args.append(res)
        if gate.shape[1] == 1:
            tiles_per_gate = rows_per_gate // tm
            in_specs.append(pl.BlockSpec((1, 1, tn), lambda j, i: (i // tiles_per_gate, 0, j)))
        else:
            in_specs.append(pl.BlockSpec((1, tm, tn), lambda j, i: (0, i, j)))
        args.append(gate)
    return pl.pallas_call(
        functools.partial(_mm_kernel, a_silu=a_silu, has_bias=bias is not None, has_res=res is not None),
        grid=(n // tn, m // tm),
        in_specs=in_specs,
        out_specs=pl.BlockSpec((tm, tn), lambda j, i: (i, j)),
        out_shape=jax.ShapeDtypeStruct((m, n), out_dtype),
        scratch_shapes=[pltpu.VMEM((k, tn), BF16)],
        compiler_params=_cparams(2),
        name="mm",
    )(*args)


_CONV_PAD = SUBLANE


def _conv_tile(x, first, buf_ref, w_ref, b_ref, st_ref, xe_ref, width, tt):
    lo = _CONV_PAD - (width - 1)

    @pl.when(first)
    def _():
        xe_ref[lo:_CONV_PAD, :] = buf_ref[0]

    @pl.when(jnp.logical_not(first))
    def _():
        xe_ref[0:_CONV_PAD, :] = xe_ref[tt:tt + _CONV_PAD, :]

    xe_ref[_CONV_PAD:_CONV_PAD + tt, :] = x
    acc = b_ref[...] + w_ref[0:1, :] * xe_ref[lo:lo + tt, :]
    for k in range(1, width):
        acc = acc + w_ref[k:k + 1, :] * xe_ref[lo + k:lo + k + tt, :]
    st_ref[0] = xe_ref[tt + lo:tt + _CONV_PAD, :]
    return acc


def _conv_silu_kernel(x_ref, buf_ref, w_ref, b_ref, y_ref, st_ref, xe_ref, *, width, tt):
    first = pl.program_id(2) == 0
    y_ref[0] = _silu(_conv_tile(x_ref[0], first, buf_ref, w_ref, b_ref, st_ref, xe_ref, width, tt))


def _conv_glu_kernel(xa_ref, xg_ref, bufa_ref, bufg_ref, wa_ref, wg_ref, ba_ref, bg_ref,
                     act_ref, sta_ref, stg_ref, xea_ref, xeg_ref, *, width, tt):
    first = pl.program_id(2) == 0
    a = _conv_tile(xa_ref[0], first, bufa_ref, wa_ref, ba_ref, sta_ref, xea_ref, width, tt)
    g = _conv_tile(xg_ref[0], first, bufg_ref, wg_ref, bg_ref, stg_ref, xeg_ref, width, tt)
    act_ref[0] = (_silu(g) * a).astype(act_ref.dtype)


def _mm_conv_silu_kernel(a_ref, w_ref, buf_ref, cw_ref, cb_ref, y_ref, st_ref, wb_ref, xe_ref,
                         *, width, tm, tiles_per_seq):
    i = pl.program_id(1)

    @pl.when(i == 0)
    def _():
        wb_ref[...] = w_ref[...].astype(BF16)

    u = jnp.dot(a_ref[...], wb_ref[...], preferred_element_type=F32)
    first = i % tiles_per_seq == 0
    y_ref[...] = _silu(_conv_tile(u, first, buf_ref, cw_ref, cb_ref, st_ref, xe_ref, width, tm))


def _mm_conv_glu_kernel(a_ref, wa_ref, wg_ref, bufa_ref, bufg_ref, cwa_ref, cwg_ref, cba_ref, cbg_ref,
                        act_ref, sta_ref, stg_ref, wba_ref, wbg_ref, xea_ref, xeg_ref,
                        *, width, tm, tiles_per_seq):
    i = pl.program_id(1)

    @pl.when(i == 0)
    def _():
        wba_ref[...] = wa_ref[...].astype(BF16)
        wbg_ref[...] = wg_ref[...].astype(BF16)

    a = a_ref[...]
    first = i % tiles_per_seq == 0
    ua = jnp.dot(a, wba_ref[...], preferred_element_type=F32)
    ca = _conv_tile(ua, first, bufa_ref, cwa_ref, cba_ref, sta_ref, xea_ref, width, tm)
    ug = jnp.dot(a, wbg_ref[...], preferred_element_type=F32)
    cg = _conv_tile(ug, first, bufg_ref, cwg_ref, cbg_ref, stg_ref, xeg_ref, width, tm)
    act_ref[...] = (_silu(cg) * ca).astype(act_ref.dtype)


def _mm_conv_silu(a, w, layer, col0, n_ch, seq_len, buf, cw, cbias, *, tm=1024, tn=1024):
    m, k = a.shape
    width = cw.shape[0]
    nb = m // seq_len
    tm = _pick_tile(seq_len, tm, 2 * SUBLANE)
    tn = _pick_tile(n_ch, tn, LANE)
    assert col0 % tn == 0
    off = col0 // tn
    tps = seq_len // tm
    return pl.pallas_call(
        functools.partial(_mm_conv_silu_kernel, width=width, tm=tm, tiles_per_seq=tps),
        grid=(n_ch // tn, m // tm),
        in_specs=[pl.BlockSpec((tm, k), lambda j, i: (i, 0)),
                  _w_spec(w, layer, k, tn, lambda j: j + off),
                  pl.BlockSpec((1, width - 1, tn), lambda j, i: (i // tps, 0, j)),
                  pl.BlockSpec((width, tn), lambda j, i: (0, j)),
                  pl.BlockSpec((1, tn), lambda j, i: (0, j))],
        out_specs=[pl.BlockSpec((tm, tn), lambda j, i: (i, j)),
                   pl.BlockSpec((1, width - 1, tn), lambda j, i: (i // tps, 0, j))],
        out_shape=[jax.ShapeDtypeStruct((m, n_ch), F32), jax.ShapeDtypeStruct((nb, width - 1, n_ch), F32)],
        scratch_shapes=[pltpu.VMEM((k, tn), BF16), pltpu.VMEM((tm + _CONV_PAD, tn), F32)],
        compiler_params=_cparams(2),
        name="mm_conv_silu",
    )(a, w, buf, cw, cbias.reshape(1, -1))


def _mm_conv_glu(a, w, layer, seq_len, buf, cw, cbias, *, tm=1024, tn=512):
    m, k = a.shape
    half = w.shape[-1] // 2
    width = cw.shape[0]
    nb = m // seq_len
    tm = _pick_tile(seq_len, tm, 2 * SUBLANE)
    tn = _pick_tile(half, tn, LANE)
    hb = half // tn
    tps = seq_len // tm
    cb2 = cbias.reshape(1, -1)
    lo_hi = lambda shape, index: [pl.BlockSpec(shape, lambda j, i: index(j, i)),
                                  pl.BlockSpec(shape, lambda j, i: index(j + hb, i))]
    st_spec = pl.BlockSpec((1, width - 1, tn), lambda j, i: (i // tps, 0, j))
    act, sta, stg = pl.pallas_call(
        functools.partial(_mm_conv_glu_kernel, width=width, tm=tm, tiles_per_seq=tps),
        grid=(hb, m // tm),
        in_specs=[pl.BlockSpec((tm, k), lambda j, i: (i, 0)),
                  _w_spec(w, layer, k, tn, lambda j: j), _w_spec(w, layer, k, tn, lambda j: j + hb)]
                 + lo_hi((1, width - 1, tn), lambda j, i: (i // tps, 0, j))
                 + lo_hi((width, tn), lambda j, i: (0, j))
                 + lo_hi((1, tn), lambda j, i: (0, j)),
        out_specs=[pl.BlockSpec((tm, tn), lambda j, i: (i, j)), st_spec, st_spec],
        out_shape=[jax.ShapeDtypeStruct((m, half), BF16),
                   jax.ShapeDtypeStruct((nb, width - 1, half), F32),
                   jax.ShapeDtypeStruct((nb, width - 1, half), F32)],
        scratch_shapes=[pltpu.VMEM((k, tn), BF16), pltpu.VMEM((k, tn), BF16),
                        pltpu.VMEM((tm + _CONV_PAD, tn), F32), pltpu.VMEM((tm + _CONV_PAD, tn), F32)],
        compiler_params=_cparams(2),
        name="mm_conv_glu",
    )(a, w, w, buf, buf, cw, cw, cb2, cb2)
    return act, jnp.concatenate([sta, stg], axis=-1)


def _conv_silu(x, col0, n_ch, buf, w, bias, *, tc=512, tt=512):
    b, t, _ = x.shape
    width = w.shape[0]
    tt = _pick_tile(t, tt, SUBLANE)
    off = col0 // tc
    return pl.pallas_call(
        functools.partial(_conv_silu_kernel, width=width, tt=tt),
        grid=(b, n_ch // tc, t // tt),
        in_specs=[pl.BlockSpec((1, tt, tc), lambda i, j, s: (i, s, j + off)),
                  pl.BlockSpec((1, width - 1, tc), lambda i, j, s: (i, 0, j)),
                  pl.BlockSpec((width, tc), lambda i, j, s: (0, j)),
                  pl.BlockSpec((1, tc), lambda i, j, s: (0, j))],
        out_specs=[pl.BlockSpec((1, tt, tc), lambda i, j, s: (i, s, j)),
                   pl.BlockSpec((1, width - 1, tc), lambda i, j, s: (i, 0, j))],
        out_shape=[jax.ShapeDtypeStruct((b, t, n_ch), F32),
                   jax.ShapeDtypeStruct((b, width - 1, n_ch), F32)],
        scratch_shapes=[pltpu.VMEM((tt + _CONV_PAD, tc), F32)],
        compiler_params=_cparams(3),
        name="conv_silu",
    )(x, buf, w, bias.reshape(1, -1))


def _conv_glu(u, buf, w, bias, *, tc=512, tt=512):
    b, t, c2 = u.shape
    half = c2 // 2
    width = w.shape[0]
    tt = _pick_tile(t, tt, SUBLANE)
    hb = half // tc
    xa = pl.BlockSpec((1, tt, tc), lambda i, j, s: (i, s, j))
    xg = pl.BlockSpec((1, tt, tc), lambda i, j, s: (i, s, j + hb))
    ba = pl.BlockSpec((1, width - 1, tc), lambda i, j, s: (i, 0, j))
    bg = pl.BlockSpec((1, width - 1, tc), lambda i, j, s: (i, 0, j + hb))
    wa = pl.BlockSpec((width, tc), lambda i, j, s: (0, j))
    wg = pl.BlockSpec((width, tc), lambda i, j, s: (0, j + hb))
    ca = pl.BlockSpec((1, tc), lambda i, j, s: (0, j))
    cg = pl.BlockSpec((1, tc), lambda i, j, s: (0, j + hb))
    st = pl.BlockSpec((1, width - 1, tc), lambda i, j, s: (i, 0, j))
    bias2 = bias.reshape(1, -1)
    act, sta, stg = pl.pallas_call(
        functools.partial(_conv_glu_kernel, width=width, tt=tt),
        grid=(b, hb, t // tt),
        in_specs=[xa, xg, ba, bg, wa, wg, ca, cg],
        out_specs=[pl.BlockSpec((1, tt, tc), lambda i, j, s: (i, s, j)), st, st],
        out_shape=[jax.ShapeDtypeStruct((b, t, half), BF16),
                   jax.ShapeDtypeStruct((b, width - 1, half), F32),
                   jax.ShapeDtypeStruct((b, width - 1, half), F32)],
        scratch_shapes=[pltpu.VMEM((tt + _CONV_PAD, tc), F32), pltpu.VMEM((tt + _CONV_PAD, tc), F32)],
        compiler_params=_cparams(3),
        name="conv_glu",
    )(u, u, buf, buf, w, w, bias2, bias2)
    return act, jnp.concatenate([sta, stg], axis=-1)


def _softplus(x):
    return jnp.maximum(x, 0.0) + jnp.log1p(jnp.exp(-jnp.abs(x)))


def _ssd_kernel(x_ref, b_ref, c_ref, dtc_ref, dtr_ref, biasc_ref, biasr_ref, alogc_ref, alogr_ref,
                dch_ref, e512_ref, e1024_ref, et_ref, s0_ref, y_ref, sf_ref, s_ref, *, t_valid):
    L = SSD_CHUNK
    c = pl.program_id(2)

    @pl.when(c == 0)
    def _():
        s_ref[...] = s0_ref[0].reshape(SSD_GROUP_DIM, SSD_STATE)

    x = x_ref[0]
    bm = b_ref[0].astype(BF16)
    cm = c_ref[0].astype(BF16)
    row = lax.broadcasted_iota(jnp.int32, (L, L), 0)
    col = lax.broadcasted_iota(jnp.int32, (L, L), 1)
    causal = row >= col

    dt_c = _softplus(dtc_ref[0, 0] + biasc_ref[0])
    dt_r = _softplus(dtr_ref[0] + biasr_ref[0])
    if t_valid < L:
        dt_c = jnp.where(lax.broadcasted_iota(jnp.int32, dt_c.shape, 0) < t_valid, dt_c, 0.0)
        dt_r = jnp.where(lax.broadcasted_iota(jnp.int32, dt_r.shape, 1) < t_valid, dt_r, 0.0)
    dta_c = dt_c * (-jnp.exp(alogc_ref[0]))
    dta_r = dt_r * (-jnp.exp(alogr_ref[0]))

    tri = jnp.where(causal, 1.0, 0.0).astype(F32)
    upper = jnp.where(row <= col, 1.0, 0.0).astype(F32)
    acs_c = _dot_exact(tri, dta_c)
    acs_r = _dot_exact(dta_r, upper)
    e512 = e512_ref[...]
    acs_x = _spread(acs_c, e512)
    dt_x = _spread(dt_c, e512)
    acs_full = _spread(acs_c, e1024_ref[...])
    alast_x = acs_x[L - 1:L, :]

    xdt = x * dt_x
    xdt_b = xdt.astype(BF16)
    cb = _dot_nt(cm, bm)
    lane = lax.broadcasted_iota(jnp.int32, (L, LANE), 1)
    pairs = []
    for pr in range(SSD_REP // 2):
        xp = xdt_b[:, pr * LANE:(pr + 1) * LANE]
        halves = []
        for r in (2 * pr, 2 * pr + 1):
            seg = acs_full[:, r * LANE:(r + 1) * LANE] - acs_r[r:r + 1, :]
            lm = jnp.exp(jnp.where(causal, seg, NEG_INF))
            halves.append(jnp.dot((cb * lm).astype(BF16), xp, preferred_element_type=F32))
        pairs.append(jnp.where(lane < SSD_HEAD_DIM, halves[0], halves[1]))
    y_diag = jnp.concatenate(pairs, axis=1)

    s_in = s_ref[...]
    y_off = _dot_nt(cm, s_in.astype(BF16)) * jnp.exp(acs_x)
    y_ref[0] = y_diag + y_off + x * dch_ref[...]

    xw = (xdt * jnp.exp(alast_x - acs_x)).astype(BF16)
    alast_r = jnp.sum(dta_r, axis=1, keepdims=True)
    dec = jnp.exp(_spread_rows(et_ref[...], jnp.broadcast_to(alast_r, (SSD_REP, SSD_STATE))))
    s_new = s_in * dec + _dot_tn(xw, bm)
    s_ref[...] = s_new
    sf_ref[0] = s_new.reshape(SSD_REP, SSD_HEAD_DIM, SSD_STATE)


def _ssd_expanders():
    e512 = np.repeat(np.eye(SSD_REP, dtype=np.float32), SSD_HEAD_DIM, axis=1)
    e1024 = np.repeat(np.eye(SSD_REP, dtype=np.float32), LANE, axis=1)
    return jnp.asarray(e512, BF16), jnp.asarray(e1024, BF16), jnp.asarray(e512.T.copy(), BF16)


def _ssd_scan(xbc, dt_raw, dt_bias, a_log, d_skip, s0, t_valid):
    b, t, _ = xbc.shape
    L = SSD_CHUNK
    assert t % L == 0
    g, rep = SSD_GROUPS, SSD_REP
    dtc = dt_raw.reshape(b, t, g, rep).transpose(0, 2, 1, 3)
    dtr = dt_raw.transpose(0, 2, 1)
    bias_c = dt_bias.reshape(g, 1, rep)
    bias_r = jnp.broadcast_to(dt_bias.reshape(g, rep, 1), (g, rep, L))
    alog_c = a_log.reshape(g, 1, rep)
    alog_r = jnp.broadcast_to(a_log.reshape(g, rep, 1), (g, rep, L))
    dch = jnp.repeat(d_skip, SSD_HEAD_DIM).reshape(1, SSD_D_INNER)
    e512, e1024, et = _ssd_expanders()
    xb = SSD_D_INNER // SSD_STATE
    full = lambda shape: pl.BlockSpec(shape, lambda i, j, s: (0,) * len(shape))
    return pl.pallas_call(
        functools.partial(_ssd_kernel, t_valid=t_valid),
        grid=(b, g, t // L),
        in_specs=[pl.BlockSpec((1, L, SSD_GROUP_DIM), lambda i, j, s: (i, s, j)),
                  pl.BlockSpec((1, L, SSD_STATE), lambda i, j, s: (i, s, xb + j)),
                  pl.BlockSpec((1, L, SSD_STATE), lambda i, j, s: (i, s, xb + g + j)),
                  pl.BlockSpec((1, 1, L, rep), lambda i, j, s: (i, j, s, 0)),
                  pl.BlockSpec((1, rep, L), lambda i, j, s: (i, j, s)),
                  pl.BlockSpec((1, 1, rep), lambda i, j, s: (j, 0, 0)),
                  pl.BlockSpec((1, rep, L), lambda i, j, s: (j, 0, 0)),
                  pl.BlockSpec((1, 1, rep), lambda i, j, s: (j, 0, 0)),
                  pl.BlockSpec((1, rep, L), lambda i, j, s: (j, 0, 0)),
                  pl.BlockSpec((1, SSD_GROUP_DIM), lambda i, j, s: (0, j)),
                  full((rep, SSD_GROUP_DIM)), full((rep, rep * LANE)), full((SSD_GROUP_DIM, rep)),
                  pl.BlockSpec((1, rep, SSD_HEAD_DIM, SSD_STATE), lambda i, j, s: (i, j, 0, 0))],
        out_specs=[pl.BlockSpec((1, L, SSD_GROUP_DIM), lambda i, j, s: (i, s, j)),
                   pl.BlockSpec((1, rep, SSD_HEAD_DIM, SSD_STATE), lambda i, j, s: (i, j, 0, 0))],
        out_shape=[jax.ShapeDtypeStruct((b, t, SSD_D_INNER), F32),
                   jax.ShapeDtypeStruct((b, SSD_HEADS, SSD_HEAD_DIM, SSD_STATE), F32)],
        scratch_shapes=[pltpu.VMEM((SSD_GROUP_DIM, SSD_STATE), F32)],
        compiler_params=_cparams(3),
        name="ssd_scan",
    )(xbc, xbc, xbc, dtc, dtr, bias_c, bias_r, alog_c, alog_r, dch, e512, e1024, et, s0)


def _gated_norm_kernel(y_ref, z_ref, g_ref, o_ref):
    v = y_ref[0] * _silu(z_ref[0])
    o = v * lax.rsqrt(jnp.mean(v * v, axis=-1, keepdims=True) + EPS) * g_ref[...]
    o_ref[0] = o.astype(o_ref.dtype)


def _gated_norm(y, zx, g):
    b, t, d = y.shape
    tt = _pick_tile(t, 128, SUBLANE)
    return pl.pallas_call(
        _gated_norm_kernel,
        grid=(b, t // tt),
        in_specs=[pl.BlockSpec((1, tt, d), lambda i, j: (i, j, 0)),
                  pl.BlockSpec((1, tt, d), lambda i, j: (i, j, 0)),
                  pl.BlockSpec((1, d), lambda i, j: (0, 0))],
        out_specs=pl.BlockSpec((1, tt, d), lambda i, j: (i, j, 0)),
        out_shape=jax.ShapeDtypeStruct((b, t, d), BF16),
        compiler_params=_cparams(2),
        name="gated_norm",
    )(y, zx, g.reshape(1, d))


def _rope_tables(pos):
    half = NSA_HEAD_DIM // 2
    inv = jnp.exp(-math.log(ROPE_THETA) * jnp.arange(half, dtype=F32) * 2.0 / NSA_HEAD_DIM)
    ang = pos.astype(F32)[:, None] * inv[None, :]
    cos, sin = jnp.cos(ang), jnp.sin(ang)
    return jnp.concatenate([cos, cos], axis=1), jnp.concatenate([-sin, sin], axis=1)


def _rope_slab(x, cos2, sin2):
    return x * cos2 + pltpu.roll(x, NSA_HEAD_DIM // 2, 1) * sin2


def _rope_kernel(q_ref, kc_ref, ks_ref, vs_ref, kw_ref, vw_ref, cos_ref, sin_ref,
                 qo_ref, kco_ref, kso_ref, kwo_ref, ksb_ref, vsb_ref, kwb_ref, vwb_ref):
    cos2, sin2 = cos_ref[...], sin_ref[...]
    hd = NSA_HEAD_DIM
    scale = hd ** -0.5
    for h in range(NSA_HEADS):
        sl = slice(h * hd, (h + 1) * hd)
        qo_ref[0, :, sl] = (_rope_slab(q_ref[0, :, sl], cos2, sin2) * scale).astype(qo_ref.dtype)
    for h in range(NSA_KV_HEADS):
        sl = slice(h * hd, (h + 1) * hd)
        kco_ref[0, :, sl] = _rope_slab(kc_ref[0, :, sl], cos2, sin2)
        ks = _rope_slab(ks_ref[0, :, sl], cos2, sin2)
        kso_ref[0, :, sl] = ks
        ksb_ref[0, :, sl] = ks.astype(ksb_ref.dtype)
        kw = _rope_slab(kw_ref[0, :, sl], cos2, sin2)
        kwo_ref[0, :, sl] = kw
        kwb_ref[0, :, sl] = kw.astype(kwb_ref.dtype)
    vsb_ref[0] = vs_ref[0].astype(vsb_ref.dtype)
    vwb_ref[0] = vw_ref[0].astype(vwb_ref.dtype)


def _rope_split(proj, pos, lowp=BF16):
    b, t, _ = proj.shape
    tt = _pick_tile(t, 256, SUBLANE)
    cos2, sin2 = _rope_tables(pos)
    kvd = NSA_KV_DIM
    qb = NSA_Q_DIM // kvd
    kv_in = lambda k: pl.BlockSpec((1, tt, kvd), lambda i, j: (i, j, qb + k))
    kv_out = pl.BlockSpec((1, tt, kvd), lambda i, j: (i, j, 0))
    tab = pl.BlockSpec((tt, NSA_HEAD_DIM), lambda i, j: (j, 0))
    sds = lambda n, dt: jax.ShapeDtypeStruct((b, t, n), dt)
    return pl.pallas_call(
        _rope_kernel,
        grid=(b, t // tt),
        in_specs=[pl.BlockSpec((1, tt, NSA_Q_DIM), lambda i, j: (i, j, 0))]
                 + [kv_in(k) for k in (0, 2, 3, 4, 5)] + [tab, tab],
        out_specs=[pl.BlockSpec((1, tt, NSA_Q_DIM), lambda i, j: (i, j, 0))] + [kv_out] * 7,
        out_shape=[sds(NSA_Q_DIM, lowp), sds(kvd, F32), sds(kvd, F32), sds(kvd, F32),
                   sds(kvd, lowp), sds(kvd, lowp), sds(kvd, lowp), sds(kvd, lowp)],
        compiler_params=_cparams(2),
        name="rope_split",
    )(proj, proj, proj, proj, proj, proj, cos2, sin2)


def _cmp_finish_kernel(ab_ref, pe_ref, w2_ref, o_ref):
    ab = ab_ref[0]
    n_sub = ab.shape[0]
    pe_term = pe_ref[0:1, :CMP_HIDDEN] + pe_ref[1:2, CMP_HIDDEN:]
    nxt = pltpu.roll(ab[:, CMP_HIDDEN:], n_sub - 1, 0)
    hid = ab[:, :CMP_HIDDEN] + nxt + pe_term
    act = 0.5 * hid * (1.0 + jnp.tanh(math.sqrt(2.0 / math.pi) * (hid + 0.044715 * hid * hid * hid)))
    o_ref[0] = jnp.dot(act.astype(BF16), w2_ref[...].astype(BF16),
                       preferred_element_type=F32).astype(o_ref.dtype)


def _compress(rows, w1, w2, pe):
    b, t, _ = rows.shape
    n_sub = t // CMP_STRIDE
    ratio = CMP_LEN // CMP_STRIDE
    kdim = CMP_STRIDE * NSA_HEAD_DIM
    x = rows.reshape(b, n_sub, CMP_STRIDE, NSA_KV_HEADS, NSA_HEAD_DIM).transpose(0, 3, 1, 2, 4)
    x = x.reshape(b * NSA_KV_HEADS * n_sub, kdim)
    w1cat = w1.reshape(ratio, kdim, CMP_HIDDEN).transpose(1, 0, 2).reshape(kdim, ratio * CMP_HIDDEN)
    ab = _mm(x, w1cat).reshape(b * NSA_KV_HEADS, n_sub, ratio * CMP_HIDDEN)
    return _cmp_finish(ab, w1cat, w2, pe)


def _cmp_finish(ab, w1cat, w2, pe):
    ratio = CMP_LEN // CMP_STRIDE
    kdim = CMP_STRIDE * NSA_HEAD_DIM
    n_sub = ab.shape[1]
    pe_rows = jnp.zeros((SUBLANE, kdim), F32).at[:ratio].set(pe.reshape(ratio, kdim))
    pe_ab = _mm(pe_rows, w1cat)
    return pl.pallas_call(
        _cmp_finish_kernel,
        grid=(ab.shape[0],),
        in_specs=[pl.BlockSpec((1, n_sub, ratio * CMP_HIDDEN), lambda i: (i, 0, 0)),
                  pl.BlockSpec((SUBLANE, ratio * CMP_HIDDEN), lambda i: (0, 0)),
                  pl.BlockSpec((CMP_HIDDEN, NSA_HEAD_DIM), lambda i: (0, 0))],
        out_specs=pl.BlockSpec((1, n_sub, NSA_HEAD_DIM), lambda i: (i, 0, 0)),
        out_shape=jax.ShapeDtypeStruct((ab.shape[0], n_sub, NSA_HEAD_DIM), BF16),
        compiler_params=_cparams(1),
        name="cmp_finish",
    )(ab, pe_ab, w2)


def _masked_softmax_rows(s, allowed):
    s = jnp.where(allowed, s, NEG_INF)
    m = jnp.max(s, axis=1, keepdims=True)
    p = jnp.where(allowed, jnp.exp(s - m), 0.0)
    return p / jnp.maximum(jnp.sum(p, axis=1, keepdims=True), TINY)


def _nsa_attn_kernel(q_ref, kcmp_ref, vcmp_ref, ks_ref, vs_ref, kw_ref, vw_ref, gp_ref, covt_ref, eg_ref,
                     o_ref, *, n_sel):
    tq, hd, rep = NSA_TQ, NSA_HEAD_DIM, NSA_REP
    rows = rep * tq
    i = pl.program_id(2)
    q0 = i * tq
    q = q_ref[0]
    qs = jnp.concatenate([q[:, r * hd:(r + 1) * hd] for r in range(rep)], axis=0)

    def qpos(shape):
        return q0 + (lax.broadcasted_iota(jnp.int32, shape, 0) & (tq - 1))

    n_cmp_pad = kcmp_ref.shape[1]
    s_c = _dot_nt(qs, kcmp_ref[0])
    blk_end = lax.broadcasted_iota(jnp.int32, (rows, n_cmp_pad), 1) * CMP_STRIDE + (CMP_LEN - 1)
    p_c = _masked_softmax_rows(s_c, blk_end <= qpos((rows, n_cmp_pad)))
    o_cmp = jnp.dot(p_c.astype(BF16), vcmp_ref[0], preferred_element_type=F32)

    p_sum = p_c[0:tq]
    for r in range(1, rep):
        p_sum = p_sum + p_c[r * tq:(r + 1) * tq]
    imp_t = _dot_nt(covt_ref[...], p_sum.astype(BF16))
    nb = 32
    assert n_sel <= nb
    imp_t = imp_t[0:nb]
    jb = lax.broadcasted_iota(jnp.int32, (nb, tq), 0)
    sel_shift = SEL_LEN.bit_length() - 1
    cur = (q0 + lax.broadcasted_iota(jnp.int32, (nb, tq), 1)) >> sel_shift
    forced = (jb == 0) | (jb == cur) | (jb == cur - 1)
    score = jnp.where(jb <= cur, imp_t + jnp.where(forced, FORCE_BONUS, 0.0), -1.0)
    score = jnp.where(jb < n_sel, score, -2.0)
    cnt = jnp.zeros((nb, tq), F32)
    for jp in range(n_sel):
        rowv = score[jp:jp + 1, :]
        before = (rowv > score) | ((rowv == score) & (jb > jp))
        cnt = cnt + jnp.where(before, 1.0, 0.0)
    sel_t = jnp.where(cnt < float(min(SEL_TOPK, n_sel)), 1.0, 0.0)
    sel_t = jnp.concatenate([sel_t, jnp.zeros((LANE - nb, tq), F32)], axis=0)
    sel = sel_t.T.astype(BF16)

    kc = NSA_SLC_KC

    def slc_body(c, carry):
        m, l, acc = carry
        k0 = pl.multiple_of(c * kc, kc)
        kblk = ks_ref[0, pl.ds(k0, kc), :]
        vblk = vs_ref[0, pl.ds(k0, kc), :]
        s = _dot_nt(qs, kblk)
        kidx = k0 + lax.broadcasted_iota(jnp.int32, (LANE, kc), 1)
        expand = jnp.where(lax.broadcasted_iota(jnp.int32, (LANE, kc), 0) == (kidx >> sel_shift), 1.0, 0.0)
        mk = jnp.dot(sel, expand.astype(BF16), preferred_element_type=F32)
        mk = jnp.concatenate([mk] * rep, axis=0)
        kpos = k0 + lax.broadcasted_iota(jnp.int32, (rows, kc), 1)
        allowed = (mk > 0.5) & (kpos <= qpos((rows, kc)))
        s = jnp.where(allowed, s, NEG_INF)
        m_new = jnp.maximum(m, jnp.max(s, axis=1, keepdims=True))
        alpha = jnp.exp(m - m_new)
        p = jnp.where(allowed, jnp.exp(s - m_new), 0.0)
        l = alpha * l + jnp.sum(p, axis=1, keepdims=True)
        acc = alpha * acc + jnp.dot(p.astype(BF16), vblk, preferred_element_type=F32)
        return m_new, l, acc

    n_chunks = (q0 + tq + kc - 1) // kc
    init = (jnp.full((rows, 1), NEG_INF, F32), jnp.zeros((rows, 1), F32), jnp.zeros((rows, hd), F32))
    _, l_s, acc_s = lax.fori_loop(0, n_chunks, slc_body, init)
    o_slc = acc_s / jnp.maximum(l_s, TINY)

    span = NSA_WIN_SPAN
    w0 = pl.multiple_of(jnp.maximum(q0 + tq - span, 0), tq)
    s_w = _dot_nt(qs, kw_ref[0, pl.ds(w0, span), :])
    dist = qpos((rows, span)) - (w0 + lax.broadcasted_iota(jnp.int32, (rows, span), 1))
    p_w = _masked_softmax_rows(s_w, (dist >= 0) & (dist < WINDOW))
    o_win = jnp.dot(p_w.astype(BF16), vw_ref[0, pl.ds(w0, span), :], preferred_element_type=F32)

    gexp = _spread(_sigmoid(gp_ref[0]), eg_ref[0])
    unstack = lambda o: jnp.concatenate([o[r * tq:(r + 1) * tq] for r in range(rep)], axis=1)
    w = rep * hd
    out = gexp[:, 0:w] * unstack(o_cmp) + gexp[:, w:2 * w] * unstack(o_slc) + gexp[:, 2 * w:3 * w] * unstack(o_win)
    o_ref[0] = out.astype(o_ref.dtype)


def _block_coverage(n_cmp, n_sel):
    start = np.arange(n_cmp)[:, None] * CMP_STRIDE
    sel_start = np.arange(n_sel)[None, :] * SEL_LEN
    inter = np.minimum(start + CMP_LEN, sel_start + SEL_LEN) - np.maximum(start, sel_start)
    return (np.clip(inter, 0, None) / CMP_LEN).astype(np.float32)


def _nsa_attention(q, kcmp, vcmp, ks, vs, kw, vw, gate_proj):
    b, t, _ = q.shape
    n_cmp_pad = t // CMP_STRIDE
    n_cmp = (t - CMP_LEN) // CMP_STRIDE + 1
    n_sel = -(-t // SEL_LEN)
    assert n_cmp_pad == LANE and t % NSA_SLC_KC == 0 and t >= NSA_WIN_SPAN
    cov_t = np.zeros((LANE, n_cmp_pad), np.float32)
    cov_t[:n_sel, :n_cmp] = _block_coverage(n_cmp, n_sel).T
    n_gate = gate_proj.shape[-1]
    eg = np.zeros((NSA_KV_HEADS, n_gate, 3 * NSA_REP * NSA_HEAD_DIM), np.float32)
    for g in range(NSA_KV_HEADS):
        for br in range(3):
            for r in range(NSA_REP):
                c0 = (br * NSA_REP + r) * NSA_HEAD_DIM
                eg[g, br * NSA_HEADS + g * NSA_REP + r, c0:c0 + NSA_HEAD_DIM] = 1.0
    gw = NSA_REP * NSA_HEAD_DIM
    seq = pl.BlockSpec((1, t, NSA_HEAD_DIM), lambda i, g, s: (i, 0, g))
    cmp_spec = pl.BlockSpec((1, n_cmp_pad, NSA_HEAD_DIM), lambda i, g, s: (i * NSA_KV_HEADS + g, 0, 0))
    return pl.pallas_call(
        functools.partial(_nsa_attn_kernel, n_sel=n_sel),
        grid=(b, NSA_KV_HEADS, t // NSA_TQ),
        in_specs=[pl.BlockSpec((1, NSA_TQ, gw), lambda i, g, s: (i, s, g)),
                  cmp_spec, cmp_spec, seq, seq, seq, seq,
                  pl.BlockSpec((1, NSA_TQ, n_gate), lambda i, g, s: (i, s, 0)),
                  pl.BlockSpec((LANE, n_cmp_pad), lambda i, g, s: (0, 0)),
                  pl.BlockSpec((1, n_gate, 3 * gw), lambda i, g, s: (g, 0, 0))],
        out_specs=pl.BlockSpec((1, NSA_TQ, gw), lambda i, g, s: (i, s, g)),
        out_shape=jax.ShapeDtypeStruct((b, t, NSA_Q_DIM), BF16),
        compiler_params=_cparams(3),
        name="nsa_attention",
    )(q, kcmp, vcmp, ks, vs, kw, vw, gate_proj, jnp.asarray(cov_t, BF16), jnp.asarray(eg, BF16))


_POOL_PAD = 2 * SUBLANE


def _pool_kernel(h_ref, buf_ref, w_ref, x_ref, gate_ref, o_ref, he_ref, wb_ref, *, tt, pos0):
    t = pl.program_id(1)

    @pl.when((pl.program_id(0) == 0) & (t == 0))
    def _():
        wb_ref[...] = w_ref[...].astype(BF16)

    @pl.when(t == 0)
    def _():
        he_ref[0:_POOL_PAD, :] = buf_ref[0]

    @pl.when(t > 0)
    def _():
        he_ref[0:_POOL_PAD, :] = he_ref[tt:tt + _POOL_PAD, :]

    he_ref[_POOL_PAD:_POOL_PAD + tt, :] = h_ref[0]
    gd = POOL_GROUP_DIM
    q_pos = pos0 + t * tt + lax.broadcasted_iota(jnp.int32, (tt, gd), 0)
    for gi, w in enumerate(POOL_WINDOWS):
        sl = slice(gi * gd, (gi + 1) * gd)
        cur = he_ref[_POOL_PAD:_POOL_PAD + tt, sl]
        win = cur
        for k in range(1, w):
            win = win + he_ref[_POOL_PAD - k:_POOL_PAD - k + tt, sl]
        count = jnp.minimum(w, q_pos + 1).astype(F32)
        mixed = (win / count - cur).astype(BF16)
        acc = jnp.dot(mixed, wb_ref[gi], preferred_element_type=F32)
        o_ref[0, :, sl] = x_ref[0, :, sl] + gate_ref[0, :, sl] * acc


def _pool_mixer(h, buf, pool_w, x, gate, pos0):
    b, t, d = h.shape
    tt = _pick_tile(t, 256, SUBLANE)
    buf16 = jnp.concatenate([jnp.zeros((b, _POOL_PAD - POOL_BUF, d), F32), buf], axis=1)
    if gate.shape[1] == 1:
        gate_spec = pl.BlockSpec((1, 1, d), lambda i, j: (i, 0, 0))
    else:
        gate_spec = pl.BlockSpec((1, tt, d), lambda i, j: (i, j, 0))
    tile = pl.BlockSpec((1, tt, d), lambda i, j: (i, j, 0))
    return pl.pallas_call(
        functools.partial(_pool_kernel, tt=tt, pos0=pos0),
        grid=(b, t // tt),
        in_specs=[tile, pl.BlockSpec((1, _POOL_PAD, d), lambda i, j: (i, 0, 0)),
                  pl.BlockSpec(pool_w.shape, lambda i, j: (0, 0, 0)), tile, gate_spec],
        out_specs=tile,
        out_shape=jax.ShapeDtypeStruct((b, t, d), F32),
        scratch_shapes=[pltpu.VMEM((tt + _POOL_PAD, d), F32), pltpu.VMEM(pool_w.shape, BF16)],
        compiler_params=_cparams(2),
        name="pool_mixer",
    )(h, buf16, pool_w, x, gate)


_CMP_PAGES_PER_STEP = 8


def _cmp_pages_kernel(tbl_ref, *refs):
    del tbl_ref
    npg = _CMP_PAGES_PER_STEP
    pages = refs[:npg]
    w_ref, o_ref, x_ref, wb_ref = refs[npg:]
    sub_per_page = PAGE_SIZE // CMP_STRIDE

    @pl.when((pl.program_id(0) == 0) & (pl.program_id(1) == 0))
    def _():
        wb_ref[...] = w_ref[...].astype(BF16)

    for k in range(npg):
        for g in range(NSA_KV_HEADS):
            r0 = (g * npg + k) * sub_per_page
            for s in range(CMP_STRIDE):
                x_ref[r0:r0 + sub_per_page, s * NSA_HEAD_DIM:(s + 1) * NSA_HEAD_DIM] = (
                    pages[k][0, pl.ds(s, sub_per_page, stride=CMP_STRIDE), g, :])
    ab = jnp.dot(x_ref[...].astype(BF16), wb_ref[...], preferred_element_type=F32)
    o_ref[0] = ab.reshape(NSA_KV_HEADS, npg * sub_per_page, ab.shape[-1])


def _cmp_pages(cache, layer, page_table, w1):
    b, n_pages = page_table.shape
    npg = _CMP_PAGES_PER_STEP
    assert n_pages % npg == 0
    ratio = CMP_LEN // CMP_STRIDE
    kdim = CMP_STRIDE * NSA_HEAD_DIM
    w1cat = w1.reshape(ratio, kdim, CMP_HIDDEN).transpose(1, 0, 2).reshape(kdim, ratio * CMP_HIDDEN)
    sub_per_page = PAGE_SIZE // CMP_STRIDE
    rows = NSA_KV_HEADS * npg * sub_per_page
    page_spec = lambda k: pl.BlockSpec((None, 1, PAGE_SIZE, NSA_KV_HEADS, NSA_HEAD_DIM),
                                       lambda i, p, tbl: (layer, tbl[i, p * npg + k], 0, 0, 0))
    out = pl.pallas_call(
        _cmp_pages_kernel,
        grid_spec=pltpu.PrefetchScalarGridSpec(
            num_scalar_prefetch=1,
            grid=(b, n_pages // npg),
            in_specs=[page_spec(k) for k in range(npg)]
                     + [pl.BlockSpec(w1cat.shape, lambda i, p, tbl: (0, 0))],
            out_specs=pl.BlockSpec((1, NSA_KV_HEADS, npg * sub_per_page, ratio * CMP_HIDDEN),
                                   lambda i, p, tbl: (i, 0, p, 0)),
            scratch_shapes=[pltpu.VMEM((rows, kdim), F32), pltpu.VMEM(w1cat.shape, BF16)]),
        out_shape=jax.ShapeDtypeStruct((b, NSA_KV_HEADS, n_pages * sub_per_page, ratio * CMP_HIDDEN), F32),
        compiler_params=_cparams(2),
        name="cmp_pages",
    )(page_table, *([cache] * npg), w1cat)
    return out.reshape(b * NSA_KV_HEADS, n_pages * sub_per_page, ratio * CMP_HIDDEN), w1cat


def _joint_softmax(s1, ok1, s2, ok2):
    s1 = jnp.where(ok1, s1, NEG_INF)
    s2 = jnp.where(ok2, s2, NEG_INF)
    m = jnp.maximum(jnp.max(s1, axis=1, keepdims=True), jnp.max(s2, axis=1, keepdims=True))
    p1 = jnp.where(ok1, jnp.exp(s1 - m), 0.0)
    p2 = jnp.where(ok2, jnp.exp(s2 - m), 0.0)
    den = jnp.maximum(jnp.sum(p1, axis=1, keepdims=True) + jnp.sum(p2, axis=1, keepdims=True), TINY)
    return p1 / den, p2 / den


def _dec_cmp_win_kernel(q_ref, kcmp_ref, vcmp_ref, kwin_ref, vwin_ref, kwn_ref, vwn_ref, cov_ref, rsum_ref,
                        ocmp_ref, owin_ref, imp_ref, *, t, n_cmp, pos0, nbuf):
    q = q_ref[0].astype(BF16)
    rows = q.shape[0]
    tok = lax.broadcasted_iota(jnp.int32, (rows, 1), 0) & (t - 1)
    q_pos = pos0 + tok

    n_pad = kcmp_ref.shape[1]
    s_c = _dot_nt(q, kcmp_ref[0])
    nidx = lax.broadcasted_iota(jnp.int32, (rows, n_pad), 1)
    ok = (nidx < n_cmp) & (nidx * CMP_STRIDE + (CMP_LEN - 1) <= q_pos)
    p_c = _masked_softmax_rows(s_c, ok)
    ocmp_ref[0] = jnp.dot(p_c.astype(BF16), vcmp_ref[0], preferred_element_type=F32)
    p_sum = _dot_exact(rsum_ref[...], p_c)
    imp_ref[0] = jnp.dot(p_sum.astype(BF16), cov_ref[...], preferred_element_type=F32)

    s1 = _dot_nt(q, kwin_ref[0].astype(BF16))
    kp1 = (pos0 - nbuf) + lax.broadcasted_iota(jnp.int32, (rows, nbuf), 1)
    d1 = q_pos - kp1
    kn = kwn_ref[0].astype(BF16)
    s2 = _dot_nt(q, kn)
    i2 = lax.broadcasted_iota(jnp.int32, (rows, kn.shape[0]), 1)
    d2 = tok - i2
    p1, p2 = _joint_softmax(s1, (d1 >= 0) & (d1 < WINDOW), s2, (d2 >= 0) & (i2 < t))
    owin_ref[0] = (jnp.dot(p1.astype(BF16), vwin_ref[0].astype(BF16), preferred_element_type=F32)
                   + jnp.dot(p2.astype(BF16), vwn_ref[0].astype(BF16), preferred_element_type=F32))


def _dec_cmp_win(q_stk, kcmp, vcmp, win_k, win_v, layer, kw_new, vw_new, n_cmp, n_sel, pos0, t):
    bg, rows, hd = q_stk.shape
    n_pad = kcmp.shape[1]
    nbuf = win_k.shape[2]
    sel_pad = -(-n_sel // LANE) * LANE
    cov = np.zeros((n_pad, sel_pad), np.float32)
    cov[:n_cmp, :n_sel] = _block_coverage(n_cmp, n_sel)
    rsum = np.zeros((SUBLANE, rows), np.float32)
    for r in range(NSA_REP):
        for tt in range(t):
            rsum[tt, r * t + tt] = 1.0
    per_bg = lambda n: pl.BlockSpec((1, n, hd), lambda i: (i, 0, 0))
    per_b = lambda n: pl.BlockSpec((1, n, hd), lambda i: (i // NSA_KV_HEADS, 0, i % NSA_KV_HEADS))
    win_spec = per_b(nbuf)
    win_k, win_v = [w[layer].reshape(w.shape[1], nbuf, NSA_KV_DIM) for w in (win_k, win_v)]
    return pl.pallas_call(
        functools.partial(_dec_cmp_win_kernel, t=t, n_cmp=n_cmp, pos0=pos0, nbuf=nbuf),
        grid=(bg,),
        in_specs=[per_bg(rows), per_bg(n_pad), per_bg(n_pad), win_spec, win_spec,
                  per_b(SUBLANE), per_b(SUBLANE),
                  pl.BlockSpec(cov.shape, lambda i: (0, 0)), pl.BlockSpec(rsum.shape, lambda i: (0, 0))],
        out_specs=[per_bg(rows), per_bg(rows), pl.BlockSpec((1, SUBLANE, sel_pad), lambda i: (i, 0, 0))],
        out_shape=[jax.ShapeDtypeStruct((bg, rows, hd), F32), jax.ShapeDtypeStruct((bg, rows, hd), F32),
                   jax.ShapeDtypeStruct((bg, SUBLANE, sel_pad), F32)],
        compiler_params=_cparams(1),
        name="dec_cmp_win",
    )(q_stk, kcmp, vcmp, win_k, win_v, kw_new, vw_new, jnp.asarray(cov, BF16), jnp.asarray(rsum))


def _dec_topk_kernel(imp_ref, o_ref, *, t, n_sel, pos0, top):
    imp_t = imp_ref[...].T
    shape = imp_t.shape
    jb = lax.broadcasted_iota(jnp.int32, shape, 0)
    q_pos = pos0 + (lax.broadcasted_iota(jnp.int32, shape, 1) & (t - 1))
    cur = q_pos >> (SEL_LEN.bit_length() - 1)
    forced = (jb == 0) | (jb == cur) | (jb == cur - 1)
    score = jnp.where(jb <= cur, imp_t + jnp.where(forced, FORCE_BONUS, 0.0), -1.0)
    score = jnp.where(jb < n_sel, score, -2.0)
    jbf = jb.astype(F32)
    for k in range(top):
        m = jnp.max(score, axis=0, keepdims=True)
        idx = jnp.min(jnp.where(score == m, jbf, float(shape[0])), axis=0, keepdims=True)
        o_ref[k:k + 1, :] = idx.astype(jnp.int32)
        score = jnp.where(jbf == idx, -3.0, score)


def _dec_topk(imp, n_sel, pos0, t):
    slots, sel_pad = imp.shape
    top = min(SEL_TOPK, n_sel)
    return pl.pallas_call(
        functools.partial(_dec_topk_kernel, t=t, n_sel=n_sel, pos0=pos0, top=top),
        grid=(1,),
        in_specs=[pl.BlockSpec((slots, sel_pad), lambda i: (0, 0))],
        out_specs=pl.BlockSpec((top, slots), lambda i: (0, 0)),
        out_shape=jax.ShapeDtypeStruct((top, slots), jnp.int32),
        compiler_params=_cparams(1),
        name="dec_topk",
    )(imp)


def _dec_slc_kernel(sel_ref, tbl_ref, *refs, t, n_past_blk, pos0, top):
    del tbl_ref
    kblk, vblk = refs[:top], refs[top:2 * top]
    q_ref, kn_ref, vn_ref, o_ref, kcat_ref, vcat_ref = refs[2 * top:]
    slot = pl.program_id(0)
    tok = slot & (t - 1)
    q_pos = pos0 + tok
    head = (slot // t) % NSA_KV_HEADS
    for g in range(NSA_KV_HEADS):
        @pl.when(head == g)
        def _(g=g):
            for k in range(top):
                kcat_ref[k * SEL_LEN:(k + 1) * SEL_LEN, :] = kblk[k][0, :, g, :].astype(BF16)
                vcat_ref[k * SEL_LEN:(k + 1) * SEL_LEN, :] = vblk[k][0, :, g, :].astype(BF16)
    q = q_ref[0].astype(BF16)
    rows = q.shape[0]
    n_keys = top * SEL_LEN
    s1 = _dot_nt(q, kcat_ref[...])
    col = lax.broadcasted_iota(jnp.int32, (rows, n_keys), 1)
    cblk = col >> (SEL_LEN.bit_length() - 1)
    kpos = col & (SEL_LEN - 1)
    in_past = cblk < 0
    new_lim = jnp.int32(-1)
    for k in range(top):
        blk = sel_ref[slot, k]
        kpos = kpos + jnp.where(cblk == k, blk * SEL_LEN, 0)
        in_past = in_past | (cblk == jnp.where(blk < n_past_blk, k, -1))
        new_lim = jnp.where(blk == n_past_blk, tok, new_lim)
    kn = kn_ref[0].astype(BF16)
    s2 = _dot_nt(q, kn)
    i2 = lax.broadcasted_iota(jnp.int32, (rows, kn.shape[0]), 1)
    ok2 = (i2 <= new_lim) & (i2 < t)
    p1, p2 = _joint_softmax(s1, in_past & (kpos <= q_pos), s2, ok2)
    o_ref[0] = (jnp.dot(p1.astype(BF16), vcat_ref[...], preferred_element_type=F32)
                + jnp.dot(p2.astype(BF16), vn_ref[0].astype(BF16), preferred_element_type=F32))


def _dec_slc(sel, page_table, cache_k, cache_v, layer, q_slot, ks_new, vs_new, pos0, t):
    slots, top = sel.shape
    hd = NSA_HEAD_DIM
    n_past_blk = page_table.shape[1] * PAGE_SIZE // SEL_LEN
    halves = PAGE_SIZE // SEL_LEN
    per_b = NSA_KV_HEADS * t

    def blk_spec(k):
        def index(s, sel_ref, tbl_ref):
            blk = jnp.minimum(sel_ref[s, k], n_past_blk - 1)
            page = tbl_ref[s // per_b, blk // halves]
            return layer, page, blk % halves, 0, 0
        return pl.BlockSpec((None, 1, SEL_LEN, NSA_KV_HEADS, hd), index)

    new_spec = pl.BlockSpec((1, SUBLANE, hd), lambda s, a, b_: (s // per_b, 0, (s // t) % NSA_KV_HEADS))
    slot_spec = pl.BlockSpec((1, SUBLANE, hd), lambda s, a, b_: (s, 0, 0))
    return pl.pallas_call(
        functools.partial(_dec_slc_kernel, t=t, n_past_blk=n_past_blk, pos0=pos0, top=top),
        grid_spec=pltpu.PrefetchScalarGridSpec(
            num_scalar_prefetch=2,
            grid=(slots,),
            in_specs=[blk_spec(k) for k in range(top)] + [blk_spec(k) for k in range(top)]
                     + [slot_spec, new_spec, new_spec],
            out_specs=slot_spec,
            scratch_shapes=[pltpu.VMEM((top * SEL_LEN, hd), BF16), pltpu.VMEM((top * SEL_LEN, hd), BF16)]),
        out_shape=jax.ShapeDtypeStruct((slots, SUBLANE, hd), F32),
        compiler_params=_cparams(1),
        name="dec_slc",
    )(sel, page_table, *([cache_k] * top), *([cache_v] * top), q_slot, ks_new, vs_new)


def _dec_combine_kernel(gp_ref, eg_ref, ocmp_ref, oslc_ref, owin_ref, o_ref):
    gexp = _spread(_sigmoid(gp_ref[...]), eg_ref[...])
    w = NSA_Q_DIM
    out = gexp[:, 0:w] * ocmp_ref[...] + gexp[:, w:2 * w] * oslc_ref[...] + gexp[:, 2 * w:3 * w] * owin_ref[...]
    o_ref[...] = out.astype(o_ref.dtype)


def _dec_combine(gate_proj, o_cmp, o_slc, o_win):
    rows, n_gate = gate_proj.shape
    eg = np.zeros((n_gate, 3 * NSA_Q_DIM), np.float32)
    for br in range(3):
        for h in range(NSA_HEADS):
            c0 = br * NSA_Q_DIM + h * NSA_HEAD_DIM
            eg[br * NSA_HEADS + h, c0:c0 + NSA_HEAD_DIM] = 1.0
    full = lambda a: pl.BlockSpec(a.shape, lambda i: (0, 0))
    args = (gate_proj, jnp.asarray(eg, BF16), o_cmp, o_slc, o_win)
    return pl.pallas_call(
        _dec_combine_kernel,
        grid=(1,),
        in_specs=[full(a) for a in args],
        out_specs=pl.BlockSpec((rows, NSA_Q_DIM), lambda i: (0, 0)),
        out_shape=jax.ShapeDtypeStruct((rows, NSA_Q_DIM), BF16),
        compiler_params=_cparams(1),
        name="dec_combine",
    )(*args)


def _nsa_decode(proj, gate_proj, pos0, P, j, caches, page_table):
    b, t, _ = proj.shape
    cache_ck, cache_cv, cache_sk, cache_sv, win_k, win_v = caches
    assert t & (t - 1) == 0 and t <= SUBLANE
    hd, g, rep = NSA_HEAD_DIM, NSA_KV_HEADS, NSA_REP
    pad_t = lambda a: jnp.pad(a, ((0, 0), (0, SUBLANE - t), (0, 0)))
    q, kc, ks, kw, _, _, _, _ = _rope_split(pad_t(proj), pos0 + jnp.arange(SUBLANE), lowp=F32)
    q, kc, ks, kw = q[:, :t], kc[:, :t], ks[:, :t], kw[:, :t]
    vc, vs, vw = [proj[..., NSA_Q_DIM + k * NSA_KV_DIM:NSA_Q_DIM + (k + 1) * NSA_KV_DIM] for k in (1, 3, 5)]

    tk = page_table.shape[1] * PAGE_SIZE + t
    n_cmp = (tk - CMP_LEN) // CMP_STRIDE + 1
    n_sel = -(-tk // SEL_LEN)
    assert (n_cmp + 1) * CMP_STRIDE <= page_table.shape[1] * PAGE_SIZE
    ab_k, w1k = _cmp_pages(cache_ck, j, page_table, P['nsa_cmpk_w1'][j])
    ab_v, w1v = _cmp_pages(cache_cv, j, page_table, P['nsa_cmpv_w1'][j])
    kcmp = _cmp_finish(ab_k, w1k, P['nsa_cmpk_w2'][j], P['nsa_cmpk_pe'][j])
    vcmp = _cmp_finish(ab_v, w1v, P['nsa_cmpv_w2'][j], P['nsa_cmpv_pe'][j])

    q5 = q.reshape(b, t, g, rep, hd)
    q_stk = q5.transpose(0, 2, 3, 1, 4).reshape(b * g, rep * t, hd)
    o_cmp, o_win, imp = _dec_cmp_win(q_stk, kcmp, vcmp, win_k, win_v, j, pad_t(kw), pad_t(vw),
                                     n_cmp, n_sel, pos0, t)
    sel = _dec_topk(imp[:, :t].reshape(b * g * t, -1), n_sel, pos0, t).T
    q_slot = q5.transpose(0, 2, 1, 3, 4).reshape(b * g * t, rep, hd)
    q_slot = jnp.pad(q_slot, ((0, 0), (0, SUBLANE - rep), (0, 0)))
    o_slc = _dec_slc(sel, page_table, cache_sk, cache_sv, j, q_slot, pad_t(ks), pad_t(vs), pos0, t)

    unstk = lambda o: o.reshape(b, g, rep, t, hd).transpose(0, 3, 1, 2, 4).reshape(b * t, NSA_Q_DIM)
    o_slc = o_slc[:, :rep].reshape(b, g, t, rep, hd).transpose(0, 2, 1, 3, 4).reshape(b * t, NSA_Q_DIM)
    o = _dec_combine(gate_proj.reshape(b * t, -1), unstk(o_cmp), o_slc, unstk(o_win))
    heads = lambda a: a.reshape(b, t, g, hd)
    kw_new = jnp.concatenate([win_k[j][:, t:], heads(kw)], axis=1)
    vw_new = jnp.concatenate([win_v[j][:, t:], heads(vw)], axis=1)
    return o.reshape(b, t, NSA_Q_DIM), kc, vc, ks, vs, kw_new, vw_new


def _run_trunk(x, mods, pos0, past, P):
    b, t, d = x.shape
    fresh = past is None
    per_batch = t % SUBLANE == 0 and t >= LANE
    names = ('ssd_state', 'ssd_conv', 'cmp_k', 'cmp_v', 'slc_k', 'slc_v', 'win_k', 'win_v', 'pool', 'ffn')
    new = {n: [] for n in names}

    bm, tm = (b, t) if per_batch else (1, b * t)

    def mod_rows(v):
        return v[:, None, :] if per_batch else jnp.repeat(v, t, axis=0)[None]

    def mm_rows(a, w, **kw):
        return _mm(a.reshape(b * t, a.shape[-1]), w, **kw)

    def pad_cols(w):
        return jnp.pad(w, ((0, 0), (0, LANE - w.shape[1])))

    def mm_res(a, w, layer, xres, gate):
        out = _mm(a.reshape(b * t, a.shape[-1]), w, layer=layer, res=xres.reshape(b * t, d), gate=gate,
                  rows_per_gate=t if per_batch else None)
        return out.reshape(b, t, d)

    for i in range(DEPTH):
        kind, j = i % N_MIXERS, i // N_MIXERS
        sh1, sc1, g1, sh2, sc2, g2 = [mod_rows(v) for v in jnp.split(mods[i], 6, axis=-1)]
        h_dtype = F32 if kind == 2 else BF16
        h = _normmod(x.reshape(bm, tm, d), P['norm1_g'][i], sh1, sc1, h_dtype).reshape(b, t, d)
        if kind == 0:
            if fresh:
                conv_buf = jnp.zeros((b, SSD_CONV - 1, SSD_CONV_DIM), F32)
                s0 = jnp.zeros((b, SSD_HEADS, SSD_HEAD_DIM, SSD_STATE), F32)
            else:
                conv_buf, s0 = past['ssd_conv'][j], past['ssd_state'][j]
            w_in = P['ssd_w_in']
            nzx = SSD_D_INNER + SSD_CONV_DIM
            dt_raw = mm_rows(h, pad_cols(w_in[j, :, nzx:])).reshape(b, t, LANE)[..., :SSD_HEADS]
            if per_batch:
                zx = mm_rows(h, w_in, layer=j, n_cols=SSD_D_INNER).reshape(b, t, SSD_D_INNER)
                xbc, conv_new = _mm_conv_silu(h.reshape(b * t, d), w_in, j, SSD_D_INNER, SSD_CONV_DIM, t,
                                              conv_buf, P['ssd_conv_w'][j], P['ssd_conv_b'][j])
                xbc = xbc.reshape(b, t, SSD_CONV_DIM)
            else:
                zx = mm_rows(h, w_in, layer=j, n_cols=nzx).reshape(b, t, nzx)
                xbc, conv_new = _conv_silu(zx, SSD_D_INNER, SSD_CONV_DIM, conv_buf,
                                           P['ssd_conv_w'][j], P['ssd_conv_b'][j])
            tp = -(-t // SSD_CHUNK) * SSD_CHUNK
            if tp != t:
                xbc_p = jnp.pad(xbc, ((0, 0), (0, tp - t), (0, 0)))
                dt_p = jnp.pad(dt_raw, ((0, 0), (0, tp - t), (0, 0)))
            else:
                xbc_p, dt_p = xbc, dt_raw
            y, s_new = _ssd_scan(xbc_p, dt_p, P['ssd_dt_bias'][j], P['ssd_a_log'][j], P['ssd_d'][j], s0,
                                 t_valid=min(t, SSD_CHUNK))
            yn = _gated_norm(y[:, :t], zx, P['ssd_norm_g'][j])
            x = mm_res(yn, P['ssd_w_out'], j, x, g1)
            new['ssd_conv'].append(conv_new)
            new['ssd_state'].append(s_new)
        elif kind == 1:
            w_in = P['nsa_w_in']
            npj = NSA_Q_DIM + 6 * NSA_KV_DIM
            proj = mm_rows(h, w_in, layer=j, n_cols=npj).reshape(b, t, npj)
            gate_proj = mm_rows(h, pad_cols(w_in[j, :, npj:])).reshape(b, t, LANE)
            if fresh:
                q_pos = pos0 + jnp.arange(t)
                q, kc, ks, kw, ks_b, vs_b, kw_b, vw_b = _rope_split(proj, q_pos)
                vc, vs, vw = [proj[..., NSA_Q_DIM + k * NSA_KV_DIM:NSA_Q_DIM + (k + 1) * NSA_KV_DIM]
                              for k in (1, 3, 5)]
                kcmp = _compress(kc, P['nsa_cmpk_w1'][j], P['nsa_cmpk_w2'][j], P['nsa_cmpk_pe'][j])
                vcmp = _compress(vc, P['nsa_cmpv_w1'][j], P['nsa_cmpv_w2'][j], P['nsa_cmpv_pe'][j])
                o = _nsa_attention(q, kcmp, vcmp, ks_b, vs_b, kw_b, vw_b, gate_proj)
                keep = min(WINDOW, t)
                kw_new, vw_new = kw[:, t - keep:], vw[:, t - keep:]
            else:
                o, kc, vc, ks, vs, kw_new, vw_new = _nsa_decode(
                    proj, gate_proj, pos0, P, j, past['nsa'][j], past['page_table'])
            x = mm_res(o, P['nsa_w_out'], j, x, g1)
            shp = (b, t, NSA_KV_HEADS, NSA_HEAD_DIM)
            for n, v in (('cmp_k', kc), ('cmp_v', vc), ('slc_k', ks), ('slc_v', vs)):
                new[n].append(v.reshape(shp))
            new['win_k'].append(kw_new.reshape(b, -1, NSA_KV_HEADS, NSA_HEAD_DIM))
            new['win_v'].append(vw_new.reshape(b, -1, NSA_KV_HEADS, NSA_HEAD_DIM))
        else:
            buf = jnp.zeros((b, POOL_BUF, d), F32) if fresh else past['pool'][j]
            gate = (g1 * P['pool_scale'][j]).reshape(b, -1, d)
            x = _pool_mixer(h, buf, P['pool_w'][j], x, gate, pos0)
            new['pool'].append(jnp.concatenate([buf, h], axis=1)[:, -POOL_BUF:])
        h2 = _normmod(x.reshape(bm, tm, d), P['norm2_g'][i], sh2, sc2, BF16)
        fbuf = jnp.zeros((b, FFN_CONV - 1, 2 * D_FF), F32) if fresh else past['ffn'][i]
        if per_batch:
            act, fbuf_new = _mm_conv_glu(h2.reshape(b * t, d), P['ffn_w_up'], i, t, fbuf,
                                         P['ffn_conv_w'][i], P['ffn_conv_b'][i])
        else:
            u = mm_rows(h2, P['ffn_w_up'], layer=i).reshape(b, t, 2 * D_FF)
            act, fbuf_new = _conv_glu(u, fbuf, P['ffn_conv_w'][i], P['ffn_conv_b'][i])
        x = mm_res(act, P['ffn_w_down'], i, x, g2)
        new['ffn'].append(fbuf_new)
    zero = jnp.zeros((bm, 1, d), F32)
    y = _normmod(x.reshape(bm, tm, d), P['final_g'], zero, zero, F32).reshape(b, t, d)
    return y, {n: jnp.stack(v) for n, v in new.items()}


def kernel(x_prompt, x_sample, state_ssd, state_ssd_conv, cache_cmp_k, cache_cmp_v, cache_slc_k, cache_slc_v,
           cache_win_k, cache_win_v, state_pool, state_ffn_conv, page_table, c_prompt, c_sample,
           ada_w, ada_b, norm1_g, norm2_g, final_g,
           ssd_w_in, ssd_conv_w, ssd_conv_b, ssd_dt_bias, ssd_a_log, ssd_d, ssd_norm_g, ssd_w_out,
           nsa_w_in, nsa_cmpk_w1, nsa_cmpk_w2, nsa_cmpk_pe, nsa_cmpv_w1, nsa_cmpv_w2, nsa_cmpv_pe, nsa_w_out,
           pool_w, pool_scale, ffn_w_up, ffn_conv_w, ffn_conv_b, ffn_w_down):
    P = dict(norm1_g=norm1_g, norm2_g=norm2_g, final_g=final_g,
             ssd_w_in=ssd_w_in, ssd_conv_w=ssd_conv_w, ssd_conv_b=ssd_conv_b, ssd_dt_bias=ssd_dt_bias,
             ssd_a_log=ssd_a_log, ssd_d=ssd_d, ssd_norm_g=ssd_norm_g, ssd_w_out=ssd_w_out,
             nsa_w_in=nsa_w_in, nsa_cmpk_w1=nsa_cmpk_w1, nsa_cmpk_w2=nsa_cmpk_w2, nsa_cmpk_pe=nsa_cmpk_pe,
             nsa_cmpv_w1=nsa_cmpv_w1, nsa_cmpv_w2=nsa_cmpv_w2, nsa_cmpv_pe=nsa_cmpv_pe, nsa_w_out=nsa_w_out,
             pool_w=pool_w, pool_scale=pool_scale,
             ffn_w_up=ffn_w_up, ffn_conv_w=ffn_conv_w, ffn_conv_b=ffn_conv_b, ffn_w_down=ffn_w_down)
    nbp, nbs = c_prompt.shape[0], c_sample.shape[0]
    c_rows = -(-(nbp + nbs) // (2 * SUBLANE)) * (2 * SUBLANE)
    c_all = jnp.concatenate([c_prompt, c_sample, jnp.zeros((c_rows - nbp - nbs, D_MODEL), F32)], axis=0)
    mods = [_mm(c_all, ada_w, layer=i, a_silu=True, bias=ada_b[i]) for i in range(DEPTH)]
    mods_p = [m[:nbp] for m in mods]
    mods_s = [m[nbp:nbp + nbs] for m in mods]

    y_prompt, sp = _run_trunk(x_prompt, mods_p, 0, None, P)
    past_len = page_table.shape[1] * PAGE_SIZE
    nsa_past = [(cache_cmp_k, cache_cmp_v, cache_slc_k, cache_slc_v, cache_win_k, cache_win_v)
                for _ in range(cache_cmp_k.shape[0])]
    past = dict(ssd_state=state_ssd, ssd_conv=state_ssd_conv, nsa=nsa_past, page_table=page_table,
                pool=state_pool, ffn=state_ffn_conv)
    y_sample, ss = _run_trunk(x_sample, mods_s, past_len, past, P)
    return (y_prompt, y_sample,
            sp['ssd_state'], ss['ssd_state'], sp['ssd_conv'], ss['ssd_conv'],
            sp['cmp_k'], ss['cmp_k'], sp['cmp_v'], ss['cmp_v'],
            sp['slc_k'], ss['slc_k'], sp['slc_v'], ss['slc_v'],
            sp['win_k'], ss['win_k'], sp['win_v'], ss['win_v'],
            sp['pool'], ss['pool'], sp['ffn'], ss['ffn'])
```

```python
import functools
import math

import jax
import jax.numpy as jnp
import numpy as np
from jax import lax
from jax.experimental import pallas as pl
from jax.experimental.pallas import tpu as pltpu

D_MODEL = 2048
DEPTH = 4
PAGE_SIZE = 128
N_MIXERS = 3

SSD_D_INNER = 2 * D_MODEL
SSD_HEAD_DIM = 64
SSD_HEADS = SSD_D_INNER // SSD_HEAD_DIM
SSD_GROUPS = 8
SSD_REP = SSD_HEADS // SSD_GROUPS
SSD_STATE = 128
SSD_CONV = 4
SSD_CHUNK = 128
SSD_GROUP_DIM = SSD_REP * SSD_HEAD_DIM
SSD_CONV_DIM = SSD_D_INNER + 2 * SSD_GROUPS * SSD_STATE

NSA_HEADS = 16
NSA_KV_HEADS = 4
NSA_HEAD_DIM = D_MODEL // NSA_HEADS
NSA_REP = NSA_HEADS // NSA_KV_HEADS
NSA_Q_DIM = NSA_HEADS * NSA_HEAD_DIM
NSA_KV_DIM = NSA_KV_HEADS * NSA_HEAD_DIM
CMP_LEN = 32
CMP_STRIDE = 16
CMP_HIDDEN = 256
SEL_LEN = 64
SEL_TOPK = 16
WINDOW = 512
WIN_Q_BLOCK = 128
SEL_Q_BLOCK = 64
ROPE_THETA = 10000.0
FORCE_BONUS = 1000.0

POOL_WINDOWS = (2, 4, 8, 16)
POOL_GROUPS = len(POOL_WINDOWS)
POOL_GROUP_DIM = D_MODEL // POOL_GROUPS
POOL_BUF = max(POOL_WINDOWS) - 1

D_FF = 5632
FFN_CONV = 3

EPS = 1e-6
NEG_INF = -1e30
TINY = 1e-30

V7X_VMEM_LIMIT_BYTES = 52 * 1024 * 1024
LANE = 128
SUBLANE = 8

F32 = jnp.float32
BF16 = jnp.bfloat16
HIGHEST = lax.Precision.HIGHEST

NSA_TQ = 128
NSA_SLC_KC = 512
NSA_WIN_SPAN = WINDOW + NSA_TQ


def _cparams(n_axes):
    return pltpu.CompilerParams(dimension_semantics=("arbitrary",) * n_axes,
                                vmem_limit_bytes=V7X_VMEM_LIMIT_BYTES)


def _pick_tile(dim, pref, align):
    t = min(pref, dim)
    t -= t % align
    while t >= align:
        if dim % t == 0:
            return t
        t -= align
    return dim


def _sigmoid(x):
    return 1.0 / (1.0 + jnp.exp(-x))


def _silu(x):
    return x * _sigmoid(x)


def _dot_nt(a, b):
    return lax.dot_general(a, b, (((1,), (1,)), ((), ())), preferred_element_type=F32)


def _dot_tn(a, b):
    return lax.dot_general(a, b, (((0,), (0,)), ((), ())), preferred_element_type=F32)


def _dot_exact(a, b):
    return jnp.dot(a, b, precision=HIGHEST, preferred_element_type=F32)


def _split3(x):
    hi = x.astype(BF16)
    r = x - hi.astype(F32)
    mid = r.astype(BF16)
    lo = (r - mid.astype(F32)).astype(BF16)
    return hi, mid, lo


def _spread(x, e):
    hi, mid, lo = _split3(x)
    dot = lambda p: jnp.dot(p, e, preferred_element_type=F32)
    return dot(hi) + dot(mid) + dot(lo)


def _spread_rows(e, x):
    hi, mid, lo = _split3(x)
    dot = lambda p: jnp.dot(e, p, preferred_element_type=F32)
    return dot(hi) + dot(mid) + dot(lo)


def _normmod_kernel(x_ref, g_ref, sh_ref, sc_ref, o_ref):
    x = x_ref[0].astype(F32)
    y = x * lax.rsqrt(jnp.mean(x * x, axis=-1, keepdims=True) + EPS) * g_ref[...]
    y = y * (1.0 + sc_ref[0]) + sh_ref[0]
    o_ref[0] = y.astype(o_ref.dtype)


def _normmod(x, g, shift, scale, out_dtype):
    b, t, d = x.shape
    r = shift.shape[1]
    tt = _pick_tile(t, 512, SUBLANE)
    if r == 1:
        mod_spec = pl.BlockSpec((1, 1, d), lambda i, j: (i, 0, 0))
    else:
        mod_spec = pl.BlockSpec((1, tt, d), lambda i, j: (i, j, 0))
    return pl.pallas_call(
        _normmod_kernel,
        grid=(b, t // tt),
        in_specs=[pl.BlockSpec((1, tt, d), lambda i, j: (i, j, 0)),
                  pl.BlockSpec((1, d), lambda i, j: (0, 0)),
                  mod_spec, mod_spec],
        out_specs=pl.BlockSpec((1, tt, d), lambda i, j: (i, j, 0)),
        out_shape=jax.ShapeDtypeStruct((b, t, d), out_dtype),
        compiler_params=_cparams(2),
        name="normmod",
    )(x, g.reshape(1, d), shift, scale)


def _mm_kernel(*refs, a_silu, has_bias, has_res):
    it = iter(refs)
    a_ref, w_ref = next(it), next(it)
    bias_ref = next(it) if has_bias else None
    res_ref = next(it) if has_res else None
    gate_ref = next(it) if has_res else None
    o_ref, wb_ref = next(it), next(it)

    @pl.when(pl.program_id(1) == 0)
    def _():
        wb_ref[...] = w_ref[...].astype(BF16)

    a = a_ref[...]
    if a_silu:
        a = _silu(a.astype(F32))
    acc = jnp.dot(a.astype(BF16), wb_ref[...], preferred_element_type=F32)
    if has_bias:
        acc = acc + bias_ref[...]
    if has_res:
        acc = res_ref[...] + gate_ref[0] * acc
    o_ref[...] = acc.astype(o_ref.dtype)


def _w_spec(w, layer, k, tn, col_block):
    if w.ndim == 2:
        return pl.BlockSpec((k, tn), lambda j, i: (0, col_block(j)))
    return pl.BlockSpec((None, k, tn), lambda j, i: (layer, 0, col_block(j)))


def _mm(a, w, *, layer=None, n_cols=None, a_silu=False, bias=None, res=None, gate=None, rows_per_gate=None,
        out_dtype=F32, tm=1024, tn=1024):
    m, k = a.shape
    assert w.shape[-2] == k
    n = w.shape[-1] if n_cols is None else n_cols
    tm = _pick_tile(m if rows_per_gate is None else rows_per_gate, tm, SUBLANE)
    tn = _pick_tile(n, tn, LANE)
    while k * tn * 10 > 30 * 1024 * 1024 and tn % (2 * LANE) == 0:
        tn //= 2
    while 2 * k * tm * a.dtype.itemsize > 16 * 1024 * 1024 and tm % (2 * SUBLANE) == 0:
        tm //= 2
    assert m % tm == 0 and n % tn == 0
    in_specs = [pl.BlockSpec((tm, k), lambda j, i: (i, 0)),
                _w_spec(w, layer, k, tn, lambda j: j)]
    args = [a, w]
    if bias is not None:
        in_specs.append(pl.BlockSpec((1, tn), lambda j, i: (0, j)))
        args.append(bias.reshape(1, -1))
    if res is not None:
        in_specs.append(pl.BlockSpec((tm, tn), lambda j, i: (i, j)))
        args.append(res)
        if gate.shape[1] == 1:
            tiles_per_gate = rows_per_gate // tm
            in_specs.append(pl.BlockSpec((1, 1, tn), lambda j, i: (i // tiles_per_gate, 0, j)))
        else:
            in_specs.append(pl.BlockSpec((1, tm, tn), lambda j, i: (0, i, j)))
        args.append(gate)
    return pl.pallas_call(
        functools.partial(_mm_kernel, a_silu=a_silu, has_bias=bias is not None, has_res=res is not None),
        grid=(n // tn, m // tm),
        in_specs=in_specs,
        out_specs=pl.BlockSpec((tm, tn), lambda j, i: (i, j)),
        out_shape=jax.ShapeDtypeStruct((m, n), out_dtype),
        scratch_shapes=[pltpu.VMEM((k, tn), BF16)],
        compiler_params=_cparams(2),
        name="mm",
    )(*args)


_CONV_PAD = SUBLANE


def _conv_tile(x, first, buf_ref, w_ref, b_ref, st_ref, xe_ref, width, tt):
    lo = _CONV_PAD - (width - 1)

    @pl.when(first)
    def _():
        xe_ref[lo:_CONV_PAD, :] = buf_ref[0]

    @pl.when(jnp.logical_not(first))
    def _():
        xe_ref[0:_CONV_PAD, :] = xe_ref[tt:tt + _CONV_PAD, :]

    xe_ref[_CONV_PAD:_CONV_PAD + tt, :] = x
    acc = b_ref[...] + w_ref[0:1, :] * xe_ref[lo:lo + tt, :]
    for k in range(1, width):
        acc = acc + w_ref[k:k + 1, :] * xe_ref[lo + k:lo + k + tt, :]
    st_ref[0] = xe_ref[tt + lo:tt + _CONV_PAD, :]
    return acc


def _conv_silu_kernel(x_ref, buf_ref, w_ref, b_ref, y_ref, st_ref, xe_ref, *, width, tt):
    first = pl.program_id(2) == 0
    y_ref[0] = _silu(_conv_tile(x_ref[0], first, buf_ref, w_ref, b_ref, st_ref, xe_ref, width, tt))


def _conv_glu_kernel(xa_ref, xg_ref, bufa_ref, bufg_ref, wa_ref, wg_ref, ba_ref, bg_ref,
                     act_ref, sta_ref, stg_ref, xea_ref, xeg_ref, *, width, tt):
    first = pl.program_id(2) == 0
    a = _conv_tile(xa_ref[0], first, bufa_ref, wa_ref, ba_ref, sta_ref, xea_ref, width, tt)
    g = _conv_tile(xg_ref[0], first, bufg_ref, wg_ref, bg_ref, stg_ref, xeg_ref, width, tt)
    act_ref[0] = (_silu(g) * a).astype(act_ref.dtype)


def _mm_conv_silu_kernel(a_ref, w_ref, buf_ref, cw_ref, cb_ref, y_ref, st_ref, wb_ref, xe_ref,
                         *, width, tm, tiles_per_seq):
    i = pl.program_id(1)

    @pl.when(i == 0)
    def _():
        wb_ref[...] = w_ref[...].astype(BF16)

    u = jnp.dot(a_ref[...], wb_ref[...], preferred_element_type=F32)
    first = i % tiles_per_seq == 0
    y_ref[...] = _silu(_conv_tile(u, first, buf_ref, cw_ref, cb_ref, st_ref, xe_ref, width, tm))


def _mm_conv_glu_kernel(a_ref, wa_ref, wg_ref, bufa_ref, bufg_ref, cwa_ref, cwg_ref, cba_ref, cbg_ref,
                        act_ref, sta_ref, stg_ref, wba_ref, wbg_ref, xea_ref, xeg_ref,
                        *, width, tm, tiles_per_seq):
    i = pl.program_id(1)

    @pl.when(i == 0)
    def _():
        wba_ref[...] = wa_ref[...].astype(BF16)
        wbg_ref[...] = wg_ref[...].astype(BF16)

    a = a_ref[...]
    first = i % tiles_per_seq == 0
    ua = jnp.dot(a, wba_ref[...], preferred_element_type=F32)
    ca = _conv_tile(ua, first, bufa_ref, cwa_ref, cba_ref, sta_ref, xea_ref, width, tm)
    ug = jnp.dot(a, wbg_ref[...], preferred_element_type=F32)
    cg = _conv_tile(ug, first, bufg_ref, cwg_ref, cbg_ref, stg_ref, xeg_ref, width, tm)
    act_ref[...] = (_silu(cg) * ca).astype(act_ref.dtype)


def _mm_conv_silu(a, w, layer, col0, n_ch, seq_len, buf, cw, cbias, *, tm=2048, tn=512):
    m, k = a.shape
    width = cw.shape[0]
    nb = m // seq_len
    tm = _pick_tile(seq_len, tm, 2 * SUBLANE)
    tn = _pick_tile(n_ch, tn, LANE)
    assert col0 % tn == 0
    off = col0 // tn
    tps = seq_len // tm
    return pl.pallas_call(
        functools.partial(_mm_conv_silu_kernel, width=width, tm=tm, tiles_per_seq=tps),
        grid=(n_ch // tn, m // tm),
        in_specs=[pl.BlockSpec((tm, k), lambda j, i: (i, 0)),
                  _w_spec(w, layer, k, tn, lambda j: j + off),
                  pl.BlockSpec((1, width - 1, tn), lambda j, i: (i // tps, 0, j)),
                  pl.BlockSpec((width, tn), lambda j, i: (0, j)),
                  pl.BlockSpec((1, tn), lambda j, i: (0, j))],
        out_specs=[pl.BlockSpec((tm, tn), lambda j, i: (i, j)),
                   pl.BlockSpec((1, width - 1, tn), lambda j, i: (i // tps, 0, j))],
        out_shape=[jax.ShapeDtypeStruct((m, n_ch), F32), jax.ShapeDtypeStruct((nb, width - 1, n_ch), F32)],
        scratch_shapes=[pltpu.VMEM((k, tn), BF16), pltpu.VMEM((tm + _CONV_PAD, tn), F32)],
        compiler_params=_cparams(2),
        name="mm_conv_silu",
    )(a, w, buf, cw, cbias.reshape(1, -1))


def _mm_conv_glu(a, w, layer, seq_len, buf, cw, cbias, *, tm=2048, tn=256):
    m, k = a.shape
    half = w.shape[-1] // 2
    width = cw.shape[0]
    nb = m // seq_len
    tm = _pick_tile(seq_len, tm, 2 * SUBLANE)
    tn = _pick_tile(half, tn, LANE)
    hb = half // tn
    tps = seq_len // tm
    cb2 = cbias.reshape(1, -1)
    lo_hi = lambda shape, index: [pl.BlockSpec(shape, lambda j, i: index(j, i)),
                                  pl.BlockSpec(shape, lambda j, i: index(j + hb, i))]
    st_spec = pl.BlockSpec((1, width - 1, tn), lambda j, i: (i // tps, 0, j))
    act, sta, stg = pl.pallas_call(
        functools.partial(_mm_conv_glu_kernel, width=width, tm=tm, tiles_per_seq=tps),
        grid=(hb, m // tm),
        in_specs=[pl.BlockSpec((tm, k), lambda j, i: (i, 0)),
                  _w_spec(w, layer, k, tn, lambda j: j), _w_spec(w, layer, k, tn, lambda j: j + hb)]
                 + lo_hi((1, width - 1, tn), lambda j, i: (i // tps, 0, j))
                 + lo_hi((width, tn), lambda j, i: (0, j))
                 + lo_hi((1, tn), lambda j, i: (0, j)),
        out_specs=[pl.BlockSpec((tm, tn), lambda j, i: (i, j)), st_spec, st_spec],
        out_shape=[jax.ShapeDtypeStruct((m, half), BF16),
                   jax.ShapeDtypeStruct((nb, width - 1, half), F32),
                   jax.ShapeDtypeStruct((nb, width - 1, half), F32)],
        scratch_shapes=[pltpu.VMEM((k, tn), BF16), pltpu.VMEM((k, tn), BF16),
                        pltpu.VMEM((tm + _CONV_PAD, tn), F32), pltpu.VMEM((tm + _CONV_PAD, tn), F32)],
        compiler_params=_cparams(2),
        name="mm_conv_glu",
    )(a, w, w, buf, buf, cw, cw, cb2, cb2)
    return act, jnp.concatenate([sta, stg], axis=-1)


def _conv_silu(x, col0, n_ch, buf, w, bias, *, tc=512, tt=512):
    b, t, _ = x.shape
    width = w.shape[0]
    tt = _pick_tile(t, tt, SUBLANE)
    off = col0 // tc
    return pl.pallas_call(
        functools.partial(_conv_silu_kernel, width=width, tt=tt),
        grid=(b, n_ch // tc, t // tt),
        in_specs=[pl.BlockSpec((1, tt, tc), lambda i, j, s: (i, s, j + off)),
                  pl.BlockSpec((1, width - 1, tc), lambda i, j, s: (i, 0, j)),
                  pl.BlockSpec((width, tc), lambda i, j, s: (0, j)),
                  pl.BlockSpec((1, tc), lambda i, j, s: (0, j))],
        out_specs=[pl.BlockSpec((1, tt, tc), lambda i, j, s: (i, s, j)),
                   pl.BlockSpec((1, width - 1, tc), lambda i, j, s: (i, 0, j))],
        out_shape=[jax.ShapeDtypeStruct((b, t, n_ch), F32),
                   jax.ShapeDtypeStruct((b, width - 1, n_ch), F32)],
        scratch_shapes=[pltpu.VMEM((tt + _CONV_PAD, tc), F32)],
        compiler_params=_cparams(3),
        name="conv_silu",
    )(x, buf, w, bias.reshape(1, -1))


def _conv_glu(u, buf, w, bias, *, tc=512, tt=512):
    b, t, c2 = u.shape
    half = c2 // 2
    width = w.shape[0]
    tt = _pick_tile(t, tt, SUBLANE)
    hb = half // tc
    xa = pl.BlockSpec((1, tt, tc), lambda i, j, s: (i, s, j))
    xg = pl.BlockSpec((1, tt, tc), lambda i, j, s: (i, s, j + hb))
    ba = pl.BlockSpec((1, width - 1, tc), lambda i, j, s: (i, 0, j))
    bg = pl.BlockSpec((1, width - 1, tc), lambda i, j, s: (i, 0, j + hb))
    wa = pl.BlockSpec((width, tc), lambda i, j, s: (0, j))
    wg = pl.BlockSpec((width, tc), lambda i, j, s: (0, j + hb))
    ca = pl.BlockSpec((1, tc), lambda i, j, s: (0, j))
    cg = pl.BlockSpec((1, tc), lambda i, j, s: (0, j + hb))
    st = pl.BlockSpec((1, width - 1, tc), lambda i, j, s: (i, 0, j))
    bias2 = bias.reshape(1, -1)
    act, sta, stg = pl.pallas_call(
        functools.partial(_conv_glu_kernel, width=width, tt=tt),
        grid=(b, hb, t // tt),
        in_specs=[xa, xg, ba, bg, wa, wg, ca, cg],
        out_specs=[pl.BlockSpec((1, tt, tc), lambda i, j, s: (i, s, j)), st, st],
        out_shape=[jax.ShapeDtypeStruct((b, t, half), BF16),
                   jax.ShapeDtypeStruct((b, width - 1, half), F32),
                   jax.ShapeDtypeStruct((b, width - 1, half), F32)],
        scratch_shapes=[pltpu.VMEM((tt + _CONV_PAD, tc), F32), pltpu.VMEM((tt + _CONV_PAD, tc), F32)],
        compiler_params=_cparams(3),
        name="conv_glu",
    )(u, u, buf, buf, w, w, bias2, bias2)
    return act, jnp.concatenate([sta, stg], axis=-1)


def _softplus(x):
    return jnp.maximum(x, 0.0) + jnp.log1p(jnp.exp(-jnp.abs(x)))


def _ssd_kernel(x_ref, b_ref, c_ref, dtc_ref, dtr_ref, biasc_ref, biasr_ref, alogc_ref, alogr_ref,
                dch_ref, e512_ref, e1024_ref, et_ref, s0_ref, y_ref, sf_ref, s_ref, *, t_valid):
    L = SSD_CHUNK
    c = pl.program_id(2)

    @pl.when(c == 0)
    def _():
        s_ref[...] = s0_ref[0].reshape(SSD_GROUP_DIM, SSD_STATE)

    x = x_ref[0]
    bm = b_ref[0].astype(BF16)
    cm = c_ref[0].astype(BF16)
    row = lax.broadcasted_iota(jnp.int32, (L, L), 0)
    col = lax.broadcasted_iota(jnp.int32, (L, L), 1)
    causal = row >= col

    dt_c = _softplus(dtc_ref[0, 0] + biasc_ref[0])
    dt_r = _softplus(dtr_ref[0] + biasr_ref[0])
    if t_valid < L:
        dt_c = jnp.where(lax.broadcasted_iota(jnp.int32, dt_c.shape, 0) < t_valid, dt_c, 0.0)
        dt_r = jnp.where(lax.broadcasted_iota(jnp.int32, dt_r.shape, 1) < t_valid, dt_r, 0.0)
    dta_c = dt_c * (-jnp.exp(alogc_ref[0]))
    dta_r = dt_r * (-jnp.exp(alogr_ref[0]))

    tri = jnp.where(causal, 1.0, 0.0).astype(F32)
    upper = jnp.where(row <= col, 1.0, 0.0).astype(F32)
    acs_c = _dot_exact(tri, dta_c)
    acs_r = _dot_exact(dta_r, upper)
    e512 = e512_ref[...]
    acs_x = _spread(acs_c, e512)
    dt_x = _spread(dt_c, e512)
    acs_full = _spread(acs_c, e1024_ref[...])
    alast_x = acs_x[L - 1:L, :]

    xdt = x * dt_x
    xdt_b = xdt.astype(BF16)
    cb = _dot_nt(cm, bm)
    lane = lax.broadcasted_iota(jnp.int32, (L, LANE), 1)
    pairs = []
    for pr in range(SSD_REP // 2):
        xp = xdt_b[:, pr * LANE:(pr + 1) * LANE]
        halves = []
        for r in (2 * pr, 2 * pr + 1):
            seg = acs_full[:, r * LANE:(r + 1) * LANE] - acs_r[r:r + 1, :]
            lm = jnp.exp(jnp.where(causal, seg, NEG_INF))
            halves.append(jnp.dot((cb * lm).astype(BF16), xp, preferred_element_type=F32))
        pairs.append(jnp.where(lane < SSD_HEAD_DIM, halves[0], halves[1]))
    y_diag = jnp.concatenate(pairs, axis=1)

    s_in = s_ref[...]
    y_off = _dot_nt(cm, s_in.astype(BF16)) * jnp.exp(acs_x)
    y_ref[0] = y_diag + y_off + x * dch_ref[...]

    xw = (xdt * jnp.exp(alast_x - acs_x)).astype(BF16)
    alast_r = jnp.sum(dta_r, axis=1, keepdims=True)
    dec = jnp.exp(_spread_rows(et_ref[...], jnp.broadcast_to(alast_r, (SSD_REP, SSD_STATE))))
    s_new = s_in * dec + _dot_tn(xw, bm)
    s_ref[...] = s_new
    sf_ref[0] = s_new.reshape(SSD_REP, SSD_HEAD_DIM, SSD_STATE)


def _ssd_expanders():
    e512 = np.repeat(np.eye(SSD_REP, dtype=np.float32), SSD_HEAD_DIM, axis=1)
    e1024 = np.repeat(np.eye(SSD_REP, dtype=np.float32), LANE, axis=1)
    return jnp.asarray(e512, BF16), jnp.asarray(e1024, BF16), jnp.asarray(e512.T.copy(), BF16)


def _ssd_scan(xbc, dt_raw, dt_bias, a_log, d_skip, s0, t_valid):
    b, t, _ = xbc.shape
    L = SSD_CHUNK
    assert t % L == 0
    g, rep = SSD_GROUPS, SSD_REP
    dtc = dt_raw.reshape(b, t, g, rep).transpose(0, 2, 1, 3)
    dtr = dt_raw.transpose(0, 2, 1)
    bias_c = dt_bias.reshape(g, 1, rep)
    bias_r = jnp.broadcast_to(dt_bias.reshape(g, rep, 1), (g, rep, L))
    alog_c = a_log.reshape(g, 1, rep)
    alog_r = jnp.broadcast_to(a_log.reshape(g, rep, 1), (g, rep, L))
    dch = jnp.repeat(d_skip, SSD_HEAD_DIM).reshape(1, SSD_D_INNER)
    e512, e1024, et = _ssd_expanders()
    xb = SSD_D_INNER // SSD_STATE
    full = lambda shape: pl.BlockSpec(shape, lambda i, j, s: (0,) * len(shape))
    return pl.pallas_call(
        functools.partial(_ssd_kernel, t_valid=t_valid),
        grid=(b, g, t // L),
        in_specs=[pl.BlockSpec((1, L, SSD_GROUP_DIM), lambda i, j, s: (i, s, j)),
                  pl.BlockSpec((1, L, SSD_STATE), lambda i, j, s: (i, s, xb + j)),
                  pl.BlockSpec((1, L, SSD_STATE), lambda i, j, s: (i, s, xb + g + j)),
                  pl.BlockSpec((1, 1, L, rep), lambda i, j, s: (i, j, s, 0)),
                  pl.BlockSpec((1, rep, L), lambda i, j, s: (i, j, s)),
                  pl.BlockSpec((1, 1, rep), lambda i, j, s: (j, 0, 0)),
                  pl.BlockSpec((1, rep, L), lambda i, j, s: (j, 0, 0)),
                  pl.BlockSpec((1, 1, rep), lambda i, j, s: (j, 0, 0)),
                  pl.BlockSpec((1, rep, L), lambda i, j, s: (j, 0, 0)),
                  pl.BlockSpec((1, SSD_GROUP_DIM), lambda i, j, s: (0, j)),
                  full((rep, SSD_GROUP_DIM)), full((rep, rep * LANE)), full((SSD_GROUP_DIM, rep)),
                  pl.BlockSpec((1, rep, SSD_HEAD_DIM, SSD_STATE), lambda i, j, s: (i, j, 0, 0))],
        out_specs=[pl.BlockSpec((1, L, SSD_GROUP_DIM), lambda i, j, s: (i, s, j)),
                   pl.BlockSpec((1, rep, SSD_HEAD_DIM, SSD_STATE), lambda i, j, s: (i, j, 0, 0))],
        out_shape=[jax.ShapeDtypeStruct((b, t, SSD_D_INNER), F32),
                   jax.ShapeDtypeStruct((b, SSD_HEADS, SSD_HEAD_DIM, SSD_STATE), F32)],
        scratch_shapes=[pltpu.VMEM((SSD_GROUP_DIM, SSD_STATE), F32)],
        compiler_params=_cparams(3),
        name="ssd_scan",
    )(xbc, xbc, xbc, dtc, dtr, bias_c, bias_r, alog_c, alog_r, dch, e512, e1024, et, s0)


def _gated_norm_kernel(y_ref, z_ref, g_ref, o_ref):
    v = y_ref[0] * _silu(z_ref[0])
    o = v * lax.rsqrt(jnp.mean(v * v, axis=-1, keepdims=True) + EPS) * g_ref[...]
    o_ref[0] = o.astype(o_ref.dtype)


def _gated_norm(y, zx, g):
    b, t, d = y.shape
    tt = _pick_tile(t, 128, SUBLANE)
    return pl.pallas_call(
        _gated_norm_kernel,
        grid=(b, t // tt),
        in_specs=[pl.BlockSpec((1, tt, d), lambda i, j: (i, j, 0)),
                  pl.BlockSpec((1, tt, d), lambda i, j: (i, j, 0)),
                  pl.BlockSpec((1, d), lambda i, j: (0, 0))],
        out_specs=pl.BlockSpec((1, tt, d), lambda i, j: (i, j, 0)),
        out_shape=jax.ShapeDtypeStruct((b, t, d), BF16),
        compiler_params=_cparams(2),
        name="gated_norm",
    )(y, zx, g.reshape(1, d))


def _rope_tables(pos):
    half = NSA_HEAD_DIM // 2
    inv = jnp.exp(-math.log(ROPE_THETA) * jnp.arange(half, dtype=F32) * 2.0 / NSA_HEAD_DIM)
    ang = pos.astype(F32)[:, None] * inv[None, :]
    cos, sin = jnp.cos(ang), jnp.sin(ang)
    return jnp.concatenate([cos, cos], axis=1), jnp.concatenate([-sin, sin], axis=1)


def _rope_slab(x, cos2, sin2):
    return x * cos2 + pltpu.roll(x, NSA_HEAD_DIM // 2, 1) * sin2


def _rope_kernel(q_ref, kc_ref, ks_ref, vs_ref, kw_ref, vw_ref, cos_ref, sin_ref,
                 qo_ref, kco_ref, kso_ref, kwo_ref, ksb_ref, vsb_ref, kwb_ref, vwb_ref):
    cos2, sin2 = cos_ref[...], sin_ref[...]
    hd = NSA_HEAD_DIM
    scale = hd ** -0.5
    for h in range(NSA_HEADS):
        sl = slice(h * hd, (h + 1) * hd)
        qo_ref[0, :, sl] = (_rope_slab(q_ref[0, :, sl], cos2, sin2) * scale).astype(qo_ref.dtype)
    for h in range(NSA_KV_HEADS):
        sl = slice(h * hd, (h + 1) * hd)
        kco_ref[0, :, sl] = _rope_slab(kc_ref[0, :, sl], cos2, sin2)
        ks = _rope_slab(ks_ref[0, :, sl], cos2, sin2)
        kso_ref[0, :, sl] = ks
        ksb_ref[0, :, sl] = ks.astype(ksb_ref.dtype)
        kw = _rope_slab(kw_ref[0, :, sl], cos2, sin2)
        kwo_ref[0, :, sl] = kw
        kwb_ref[0, :, sl] = kw.astype(kwb_ref.dtype)
    vsb_ref[0] = vs_ref[0].astype(vsb_ref.dtype)
    vwb_ref[0] = vw_ref[0].astype(vwb_ref.dtype)


def _rope_split(proj, pos, lowp=BF16):
    b, t, _ = proj.shape
    tt = _pick_tile(t, 256, SUBLANE)
    cos2, sin2 = _rope_tables(pos)
    kvd = NSA_KV_DIM
    qb = NSA_Q_DIM // kvd
    kv_in = lambda k: pl.BlockSpec((1, tt, kvd), lambda i, j: (i, j, qb + k))
    kv_out = pl.BlockSpec((1, tt, kvd), lambda i, j: (i, j, 0))
    tab = pl.BlockSpec((tt, NSA_HEAD_DIM), lambda i, j: (j, 0))
    sds = lambda n, dt: jax.ShapeDtypeStruct((b, t, n), dt)
    return pl.pallas_call(
        _rope_kernel,
        grid=(b, t // tt),
        in_specs=[pl.BlockSpec((1, tt, NSA_Q_DIM), lambda i, j: (i, j, 0))]
                 + [kv_in(k) for k in (0, 2, 3, 4, 5)] + [tab, tab],
        out_specs=[pl.BlockSpec((1, tt, NSA_Q_DIM), lambda i, j: (i, j, 0))] + [kv_out] * 7,
        out_shape=[sds(NSA_Q_DIM, lowp), sds(kvd, F32), sds(kvd, F32), sds(kvd, F32),
                   sds(kvd, lowp), sds(kvd, lowp), sds(kvd, lowp), sds(kvd, lowp)],
        compiler_params=_cparams(2),
        name="rope_split",
    )(proj, proj, proj, proj, proj, proj, cos2, sin2)


def _cmp_finish_kernel(ab_ref, pe_ref, w2_ref, o_ref):
    ab = ab_ref[0]
    n_sub = ab.shape[0]
    pe_term = pe_ref[0:1, :CMP_HIDDEN] + pe_ref[1:2, CMP_HIDDEN:]
    nxt = pltpu.roll(ab[:, CMP_HIDDEN:], n_sub - 1, 0)
    hid = ab[:, :CMP_HIDDEN] + nxt + pe_term
    act = 0.5 * hid * (1.0 + jnp.tanh(math.sqrt(2.0 / math.pi) * (hid + 0.044715 * hid * hid * hid)))
    o_ref[0] = jnp.dot(act.astype(BF16), w2_ref[...].astype(BF16),
                       preferred_element_type=F32).astype(o_ref.dtype)


def _compress(rows, w1, w2, pe):
    b, t, _ = rows.shape
    n_sub = t // CMP_STRIDE
    ratio = CMP_LEN // CMP_STRIDE
    kdim = CMP_STRIDE * NSA_HEAD_DIM
    x = rows.reshape(b, n_sub, CMP_STRIDE, NSA_KV_HEADS, NSA_HEAD_DIM).transpose(0, 3, 1, 2, 4)
    x = x.reshape(b * NSA_KV_HEADS * n_sub, kdim)
    w1cat = w1.reshape(ratio, kdim, CMP_HIDDEN).transpose(1, 0, 2).reshape(kdim, ratio * CMP_HIDDEN)
    ab = _mm(x, w1cat).reshape(b * NSA_KV_HEADS, n_sub, ratio * CMP_HIDDEN)
    return _cmp_finish(ab, w1cat, w2, pe)


def _cmp_finish(ab, w1cat, w2, pe):
    ratio = CMP_LEN // CMP_STRIDE
    kdim = CMP_STRIDE * NSA_HEAD_DIM
    n_sub = ab.shape[1]
    pe_rows = jnp.zeros((SUBLANE, kdim), F32).at[:ratio].set(pe.reshape(ratio, kdim))
    pe_ab = _mm(pe_rows, w1cat)
    return pl.pallas_call(
        _cmp_finish_kernel,
        grid=(ab.shape[0],),
        in_specs=[pl.BlockSpec((1, n_sub, ratio * CMP_HIDDEN), lambda i: (i, 0, 0)),
                  pl.BlockSpec((SUBLANE, ratio * CMP_HIDDEN), lambda i: (0, 0)),
                  pl.BlockSpec((CMP_HIDDEN, NSA_HEAD_DIM), lambda i: (0, 0))],
        out_specs=pl.BlockSpec((1, n_sub, NSA_HEAD_DIM), lambda i: (i, 0, 0)),
        out_shape=jax.ShapeDtypeStruct((ab.shape[0], n_sub, NSA_HEAD_DIM), BF16),
        compiler_params=_cparams(1),
        name="cmp_finish",
    )(ab, pe_ab, w2)


def _masked_softmax_rows(s, allowed):
    s = jnp.where(allowed, s, NEG_INF)
    m = jnp.max(s, axis=1, keepdims=True)
    p = jnp.where(allowed, jnp.exp(s - m), 0.0)
    return p / jnp.maximum(jnp.sum(p, axis=1, keepdims=True), TINY)


def _nsa_attn_kernel(q_ref, kcmp_ref, vcmp_ref, ks_ref, vs_ref, kw_ref, vw_ref, gp_ref, covt_ref, eg_ref,
                     o_ref, *, n_sel):
    tq, hd, rep = NSA_TQ, NSA_HEAD_DIM, NSA_REP
    rows = rep * tq
    i = pl.program_id(2)
    q0 = i * tq
    q = q_ref[0]
    qs = jnp.concatenate([q[:, r * hd:(r + 1) * hd] for r in range(rep)], axis=0)

    def qpos(shape):
        return q0 + (lax.broadcasted_iota(jnp.int32, shape, 0) & (tq - 1))

    n_cmp_pad = kcmp_ref.shape[1]
    s_c = _dot_nt(qs, kcmp_ref[0])
    blk_end = lax.broadcasted_iota(jnp.int32, (rows, n_cmp_pad), 1) * CMP_STRIDE + (CMP_LEN - 1)
    p_c = _masked_softmax_rows(s_c, blk_end <= qpos((rows, n_cmp_pad)))
    o_cmp = jnp.dot(p_c.astype(BF16), vcmp_ref[0], preferred_element_type=F32)

    p_sum = p_c[0:tq]
    for r in range(1, rep):
        p_sum = p_sum + p_c[r * tq:(r + 1) * tq]
    imp_t = _dot_nt(covt_ref[...], p_sum.astype(BF16))
    nb = 32
    assert n_sel <= nb
    imp_t = imp_t[0:nb]
    jb = lax.broadcasted_iota(jnp.int32, (nb, tq), 0)
    sel_shift = SEL_LEN.bit_length() - 1
    cur = (q0 + lax.broadcasted_iota(jnp.int32, (nb, tq), 1)) >> sel_shift
    forced = (jb == 0) | (jb == cur) | (jb == cur - 1)
    score = jnp.where(jb <= cur, imp_t + jnp.where(forced, FORCE_BONUS, 0.0), -1.0)
    score = jnp.where(jb < n_sel, score, -2.0)
    cnt = jnp.zeros((nb, tq), F32)
    for jp in range(n_sel):
        rowv = score[jp:jp + 1, :]
        before = (rowv > score) | ((rowv == score) & (jb > jp))
        cnt = cnt + jnp.where(before, 1.0, 0.0)
    sel_t = jnp.where(cnt < float(min(SEL_TOPK, n_sel)), 1.0, 0.0)
    sel_t = jnp.concatenate([sel_t, jnp.zeros((LANE - nb, tq), F32)], axis=0)
    sel = sel_t.T.astype(BF16)

    kc = NSA_SLC_KC

    def slc_body(c, carry):
        m, l, acc = carry
        k0 = pl.multiple_of(c * kc, kc)
        kblk = ks_ref[0, pl.ds(k0, kc), :]
        vblk = vs_ref[0, pl.ds(k0, kc), :]
        s = _dot_nt(qs, kblk)
        kidx = k0 + lax.broadcasted_iota(jnp.int32, (LANE, kc), 1)
        expand = jnp.where(lax.broadcasted_iota(jnp.int32, (LANE, kc), 0) == (kidx >> sel_shift), 1.0, 0.0)
        mk = jnp.dot(sel, expand.astype(BF16), preferred_element_type=F32)
        mk = jnp.concatenate([mk] * rep, axis=0)
        kpos = k0 + lax.broadcasted_iota(jnp.int32, (rows, kc), 1)
        allowed = (mk > 0.5) & (kpos <= qpos((rows, kc)))
        s = jnp.where(allowed, s, NEG_INF)
        m_new = jnp.maximum(m, jnp.max(s, axis=1, keepdims=True))
        alpha = jnp.exp(m - m_new)
        p = jnp.where(allowed, jnp.exp(s - m_new), 0.0)
        l = alpha * l + jnp.sum(p, axis=1, keepdims=True)
        acc = alpha * acc + jnp.dot(p.astype(BF16), vblk, preferred_element_type=F32)
        return m_new, l, acc

    n_chunks = (q0 + tq + kc - 1) // kc
    init = (jnp.full((rows, 1), NEG_INF, F32), jnp.zeros((rows, 1), F32), jnp.zeros((rows, hd), F32))
    _, l_s, acc_s = lax.fori_loop(0, n_chunks, slc_body, init)
    o_slc = acc_s / jnp.maximum(l_s, TINY)

    span = NSA_WIN_SPAN
    w0 = pl.multiple_of(jnp.maximum(q0 + tq - span, 0), tq)
    s_w = _dot_nt(qs, kw_ref[0, pl.ds(w0, span), :])
    dist = qpos((rows, span)) - (w0 + lax.broadcasted_iota(jnp.int32, (rows, span), 1))
    p_w = _masked_softmax_rows(s_w, (dist >= 0) & (dist < WINDOW))
    o_win = jnp.dot(p_w.astype(BF16), vw_ref[0, pl.ds(w0, span), :], preferred_element_type=F32)

    gexp = _spread(_sigmoid(gp_ref[0]), eg_ref[0])
    unstack = lambda o: jnp.concatenate([o[r * tq:(r + 1) * tq] for r in range(rep)], axis=1)
    w = rep * hd
    out = gexp[:, 0:w] * unstack(o_cmp) + gexp[:, w:2 * w] * unstack(o_slc) + gexp[:, 2 * w:3 * w] * unstack(o_win)
    o_ref[0] = out.astype(o_ref.dtype)


def _block_coverage(n_cmp, n_sel):
    start = np.arange(n_cmp)[:, None] * CMP_STRIDE
    sel_start = np.arange(n_sel)[None, :] * SEL_LEN
    inter = np.minimum(start + CMP_LEN, sel_start + SEL_LEN) - np.maximum(start, sel_start)
    return (np.clip(inter, 0, None) / CMP_LEN).astype(np.float32)


def _nsa_attention(q, kcmp, vcmp, ks, vs, kw, vw, gate_proj):
    b, t, _ = q.shape
    n_cmp_pad = t // CMP_STRIDE
    n_cmp = (t - CMP_LEN) // CMP_STRIDE + 1
    n_sel = -(-t // SEL_LEN)
    assert n_cmp_pad == LANE and t % NSA_SLC_KC == 0 and t >= NSA_WIN_SPAN
    cov_t = np.zeros((LANE, n_cmp_pad), np.float32)
    cov_t[:n_sel, :n_cmp] = _block_coverage(n_cmp, n_sel).T
    n_gate = gate_proj.shape[-1]
    eg = np.zeros((NSA_KV_HEADS, n_gate, 3 * NSA_REP * NSA_HEAD_DIM), np.float32)
    for g in range(NSA_KV_HEADS):
        for br in range(3):
            for r in range(NSA_REP):
                c0 = (br * NSA_REP + r) * NSA_HEAD_DIM
                eg[g, br * NSA_HEADS + g * NSA_REP + r, c0:c0 + NSA_HEAD_DIM] = 1.0
    gw = NSA_REP * NSA_HEAD_DIM
    seq = pl.BlockSpec((1, t, NSA_HEAD_DIM), lambda i, g, s: (i, 0, g))
    cmp_spec = pl.BlockSpec((1, n_cmp_pad, NSA_HEAD_DIM), lambda i, g, s: (i * NSA_KV_HEADS + g, 0, 0))
    return pl.pallas_call(
        functools.partial(_nsa_attn_kernel, n_sel=n_sel),
        grid=(b, NSA_KV_HEADS, t // NSA_TQ),
        in_specs=[pl.BlockSpec((1, NSA_TQ, gw), lambda i, g, s: (i, s, g)),
                  cmp_spec, cmp_spec, seq, seq, seq, seq,
                  pl.BlockSpec((1, NSA_TQ, n_gate), lambda i, g, s: (i, s, 0)),
                  pl.BlockSpec((LANE, n_cmp_pad), lambda i, g, s: (0, 0)),
                  pl.BlockSpec((1, n_gate, 3 * gw), lambda i, g, s: (g, 0, 0))],
        out_specs=pl.BlockSpec((1, NSA_TQ, gw), lambda i, g, s: (i, s, g)),
        out_shape=jax.ShapeDtypeStruct((b, t, NSA_Q_DIM), BF16),
        compiler_params=_cparams(3),
        name="nsa_attention",
    )(q, kcmp, vcmp, ks, vs, kw, vw, gate_proj, jnp.asarray(cov_t, BF16), jnp.asarray(eg, BF16))


_POOL_PAD = 2 * SUBLANE


def _pool_kernel(h_ref, buf_ref, w_ref, x_ref, gate_ref, o_ref, he_ref, wb_ref, *, tt, pos0):
    t = pl.program_id(1)

    @pl.when((pl.program_id(0) == 0) & (t == 0))
    def _():
        wb_ref[...] = w_ref[...].astype(BF16)

    @pl.when(t == 0)
    def _():
        he_ref[0:_POOL_PAD, :] = buf_ref[0]

    @pl.when(t > 0)
    def _():
        he_ref[0:_POOL_PAD, :] = he_ref[tt:tt + _POOL_PAD, :]

    he_ref[_POOL_PAD:_POOL_PAD + tt, :] = h_ref[0]
    gd = POOL_GROUP_DIM
    q_pos = pos0 + t * tt + lax.broadcasted_iota(jnp.int32, (tt, gd), 0)
    for gi, w in enumerate(POOL_WINDOWS):
        sl = slice(gi * gd, (gi + 1) * gd)
        cur = he_ref[_POOL_PAD:_POOL_PAD + tt, sl]
        win = cur
        for k in range(1, w):
            win = win + he_ref[_POOL_PAD - k:_POOL_PAD - k + tt, sl]
        count = jnp.minimum(w, q_pos + 1).astype(F32)
        mixed = (win / count - cur).astype(BF16)
        acc = jnp.dot(mixed, wb_ref[gi], preferred_element_type=F32)
        o_ref[0, :, sl] = x_ref[0, :, sl] + gate_ref[0, :, sl] * acc


def _pool_mixer(h, buf, pool_w, x, gate, pos0):
    b, t, d = h.shape
    tt = _pick_tile(t, 256, SUBLANE)
    buf16 = jnp.concatenate([jnp.zeros((b, _POOL_PAD - POOL_BUF, d), F32), buf], axis=1)
    if gate.shape[1] == 1:
        gate_spec = pl.BlockSpec((1, 1, d), lambda i, j: (i, 0, 0))
    else:
        gate_spec = pl.BlockSpec((1, tt, d), lambda i, j: (i, j, 0))
    tile = pl.BlockSpec((1, tt, d), lambda i, j: (i, j, 0))
    return pl.pallas_call(
        functools.partial(_pool_kernel, tt=tt, pos0=pos0),
        grid=(b, t // tt),
        in_specs=[tile, pl.BlockSpec((1, _POOL_PAD, d), lambda i, j: (i, 0, 0)),
                  pl.BlockSpec(pool_w.shape, lambda i, j: (0, 0, 0)), tile, gate_spec],
        out_specs=tile,
        out_shape=jax.ShapeDtypeStruct((b, t, d), F32),
        scratch_shapes=[pltpu.VMEM((tt + _POOL_PAD, d), F32), pltpu.VMEM(pool_w.shape, BF16)],
        compiler_params=_cparams(2),
        name="pool_mixer",
    )(h, buf16, pool_w, x, gate)


_CMP_PAGES_PER_STEP = 8


def _cmp_pages_kernel(tbl_ref, *refs):
    del tbl_ref
    npg = _CMP_PAGES_PER_STEP
    pages = refs[:npg]
    w_ref, o_ref, x_ref, wb_ref = refs[npg:]
    sub_per_page = PAGE_SIZE // CMP_STRIDE

    @pl.when((pl.program_id(0) == 0) & (pl.program_id(1) == 0))
    def _():
        wb_ref[...] = w_ref[...].astype(BF16)

    for k in range(npg):
        for g in range(NSA_KV_HEADS):
            r0 = (g * npg + k) * sub_per_page
            for s in range(CMP_STRIDE):
                x_ref[r0:r0 + sub_per_page, s * NSA_HEAD_DIM:(s + 1) * NSA_HEAD_DIM] = (
                    pages[k][0, pl.ds(s, sub_per_page, stride=CMP_STRIDE), g, :])
    ab = jnp.dot(x_ref[...].astype(BF16), wb_ref[...], preferred_element_type=F32)
    o_ref[0] = ab.reshape(NSA_KV_HEADS, npg * sub_per_page, ab.shape[-1])


def _cmp_pages(cache, layer, page_table, w1):
    b, n_pages = page_table.shape
    npg = _CMP_PAGES_PER_STEP
    assert n_pages % npg == 0
    ratio = CMP_LEN // CMP_STRIDE
    kdim = CMP_STRIDE * NSA_HEAD_DIM
    w1cat = w1.reshape(ratio, kdim, CMP_HIDDEN).transpose(1, 0, 2).reshape(kdim, ratio * CMP_HIDDEN)
    sub_per_page = PAGE_SIZE // CMP_STRIDE
    rows = NSA_KV_HEADS * npg * sub_per_page
    page_spec = lambda k: pl.BlockSpec((None, 1, PAGE_SIZE, NSA_KV_HEADS, NSA_HEAD_DIM),
                                       lambda i, p, tbl: (layer, tbl[i, p * npg + k], 0, 0, 0))
    out = pl.pallas_call(
        _cmp_pages_kernel,
        grid_spec=pltpu.PrefetchScalarGridSpec(
            num_scalar_prefetch=1,
            grid=(b, n_pages // npg),
            in_specs=[page_spec(k) for k in range(npg)]
                     + [pl.BlockSpec(w1cat.shape, lambda i, p, tbl: (0, 0))],
            out_specs=pl.BlockSpec((1, NSA_KV_HEADS, npg * sub_per_page, ratio * CMP_HIDDEN),
                                   lambda i, p, tbl: (i, 0, p, 0)),
            scratch_shapes=[pltpu.VMEM((rows, kdim), F32), pltpu.VMEM(w1cat.shape, BF16)]),
        out_shape=jax.ShapeDtypeStruct((b, NSA_KV_HEADS, n_pages * sub_per_page, ratio * CMP_HIDDEN), F32),
        compiler_params=_cparams(2),
        name="cmp_pages",
    )(page_table, *([cache] * npg), w1cat)
    return out.reshape(b * NSA_KV_HEADS, n_pages * sub_per_page, ratio * CMP_HIDDEN), w1cat


def _joint_softmax(s1, ok1, s2, ok2):
    s1 = jnp.where(ok1, s1, NEG_INF)
    s2 = jnp.where(ok2, s2, NEG_INF)
    m = jnp.maximum(jnp.max(s1, axis=1, keepdims=True), jnp.max(s2, axis=1, keepdims=True))
    p1 = jnp.where(ok1, jnp.exp(s1 - m), 0.0)
    p2 = jnp.where(ok2, jnp.exp(s2 - m), 0.0)
    den = jnp.maximum(jnp.sum(p1, axis=1, keepdims=True) + jnp.sum(p2, axis=1, keepdims=True), TINY)
    return p1 / den, p2 / den


def _dec_cmp_win_kernel(q_ref, kcmp_ref, vcmp_ref, kwin_ref, vwin_ref, kwn_ref, vwn_ref, cov_ref, rsum_ref,
                        ocmp_ref, owin_ref, imp_ref, *, t, n_cmp, pos0, nbuf):
    q = q_ref[0].astype(BF16)
    rows = q.shape[0]
    tok = lax.broadcasted_iota(jnp.int32, (rows, 1), 0) & (t - 1)
    q_pos = pos0 + tok

    n_pad = kcmp_ref.shape[1]
    s_c = _dot_nt(q, kcmp_ref[0])
    nidx = lax.broadcasted_iota(jnp.int32, (rows, n_pad), 1)
    ok = (nidx < n_cmp) & (nidx * CMP_STRIDE + (CMP_LEN - 1) <= q_pos)
    p_c = _masked_softmax_rows(s_c, ok)
    ocmp_ref[0] = jnp.dot(p_c.astype(BF16), vcmp_ref[0], preferred_element_type=F32)
    p_sum = _dot_exact(rsum_ref[...], p_c)
    imp_ref[0] = jnp.dot(p_sum.astype(BF16), cov_ref[...], preferred_element_type=F32)

    s1 = _dot_nt(q, kwin_ref[0].astype(BF16))
    kp1 = (pos0 - nbuf) + lax.broadcasted_iota(jnp.int32, (rows, nbuf), 1)
    d1 = q_pos - kp1
    kn = kwn_ref[0].astype(BF16)
    s2 = _dot_nt(q, kn)
    i2 = lax.broadcasted_iota(jnp.int32, (rows, kn.shape[0]), 1)
    d2 = tok - i2
    p1, p2 = _joint_softmax(s1, (d1 >= 0) & (d1 < WINDOW), s2, (d2 >= 0) & (i2 < t))
    owin_ref[0] = (jnp.dot(p1.astype(BF16), vwin_ref[0].astype(BF16), preferred_element_type=F32)
                   + jnp.dot(p2.astype(BF16), vwn_ref[0].astype(BF16), preferred_element_type=F32))


def _dec_cmp_win(q_stk, kcmp, vcmp, win_k, win_v, layer, kw_new, vw_new, n_cmp, n_sel, pos0, t):
    bg, rows, hd = q_stk.shape
    n_pad = kcmp.shape[1]
    nbuf = win_k.shape[2]
    sel_pad = -(-n_sel // LANE) * LANE
    cov = np.zeros((n_pad, sel_pad), np.float32)
    cov[:n_cmp, :n_sel] = _block_coverage(n_cmp, n_sel)
    rsum = np.zeros((SUBLANE, rows), np.float32)
    for r in range(NSA_REP):
        for tt in range(t):
            rsum[tt, r * t + tt] = 1.0
    per_bg = lambda n: pl.BlockSpec((1, n, hd), lambda i: (i, 0, 0))
    per_b = lambda n: pl.BlockSpec((1, n, hd), lambda i: (i // NSA_KV_HEADS, 0, i % NSA_KV_HEADS))
    win_spec = per_b(nbuf)
    win_k, win_v = [w[layer].reshape(w.shape[1], nbuf, NSA_KV_DIM) for w in (win_k, win_v)]
    return pl.pallas_call(
        functools.partial(_dec_cmp_win_kernel, t=t, n_cmp=n_cmp, pos0=pos0, nbuf=nbuf),
        grid=(bg,),
        in_specs=[per_bg(rows), per_bg(n_pad), per_bg(n_pad), win_spec, win_spec,
                  per_b(SUBLANE), per_b(SUBLANE),
                  pl.BlockSpec(cov.shape, lambda i: (0, 0)), pl.BlockSpec(rsum.shape, lambda i: (0, 0))],
        out_specs=[per_bg(rows), per_bg(rows), pl.BlockSpec((1, SUBLANE, sel_pad), lambda i: (i, 0, 0))],
        out_shape=[jax.ShapeDtypeStruct((bg, rows, hd), F32), jax.ShapeDtypeStruct((bg, rows, hd), F32),
                   jax.ShapeDtypeStruct((bg, SUBLANE, sel_pad), F32)],
        compiler_params=_cparams(1),
        name="dec_cmp_win",
    )(q_stk, kcmp, vcmp, win_k, win_v, kw_new, vw_new, jnp.asarray(cov, BF16), jnp.asarray(rsum))


def _dec_topk_kernel(imp_ref, o_ref, *, t, n_sel, pos0, top):
    imp_t = imp_ref[...].T
    shape = imp_t.shape
    jb = lax.broadcasted_iota(jnp.int32, shape, 0)
    q_pos = pos0 + (lax.broadcasted_iota(jnp.int32, shape, 1) & (t - 1))
    cur = q_pos >> (SEL_LEN.bit_length() - 1)
    forced = (jb == 0) | (jb == cur) | (jb == cur - 1)
    score = jnp.where(jb <= cur, imp_t + jnp.where(forced, FORCE_BONUS, 0.0), -1.0)
    score = jnp.where(jb < n_sel, score, -2.0)
    jbf = jb.astype(F32)
    for k in range(top):
        m = jnp.max(score, axis=0, keepdims=True)
        idx = jnp.min(jnp.where(score == m, jbf, float(shape[0])), axis=0, keepdims=True)
        o_ref[k:k + 1, :] = idx.astype(jnp.int32)
        score = jnp.where(jbf == idx, -3.0, score)


def _dec_topk(imp, n_sel, pos0, t):
    slots, sel_pad = imp.shape
    top = min(SEL_TOPK, n_sel)
    return pl.pallas_call(
        functools.partial(_dec_topk_kernel, t=t, n_sel=n_sel, pos0=pos0, top=top),
        grid=(1,),
        in_specs=[pl.BlockSpec((slots, sel_pad), lambda i: (0, 0))],
        out_specs=pl.BlockSpec((top, slots), lambda i: (0, 0)),
        out_shape=jax.ShapeDtypeStruct((top, slots), jnp.int32),
        compiler_params=_cparams(1),
        name="dec_topk",
    )(imp)


def _dec_slc_kernel(sel_ref, tbl_ref, *refs, t, n_past_blk, pos0, top):
    del tbl_ref
    kblk, vblk = refs[:top], refs[top:2 * top]
    q_ref, kn_ref, vn_ref, o_ref, kcat_ref, vcat_ref = refs[2 * top:]
    slot = pl.program_id(0)
    tok = slot & (t - 1)
    q_pos = pos0 + tok
    head = (slot // t) % NSA_KV_HEADS
    for g in range(NSA_KV_HEADS):
        @pl.when(head == g)
        def _(g=g):
            for k in range(top):
                kcat_ref[k * SEL_LEN:(k + 1) * SEL_LEN, :] = kblk[k][0, :, g, :].astype(BF16)
                vcat_ref[k * SEL_LEN:(k + 1) * SEL_LEN, :] = vblk[k][0, :, g, :].astype(BF16)
    q = q_ref[0].astype(BF16)
    rows = q.shape[0]
    n_keys = top * SEL_LEN
    s1 = _dot_nt(q, kcat_ref[...])
    col = lax.broadcasted_iota(jnp.int32, (rows, n_keys), 1)
    cblk = col >> (SEL_LEN.bit_length() - 1)
    kpos = col & (SEL_LEN - 1)
    in_past = cblk < 0
    new_lim = jnp.int32(-1)
    for k in range(top):
        blk = sel_ref[slot, k]
        kpos = kpos + jnp.where(cblk == k, blk * SEL_LEN, 0)
        in_past = in_past | (cblk == jnp.where(blk < n_past_blk, k, -1))
        new_lim = jnp.where(blk == n_past_blk, tok, new_lim)
    kn = kn_ref[0].astype(BF16)
    s2 = _dot_nt(q, kn)
    i2 = lax.broadcasted_iota(jnp.int32, (rows, kn.shape[0]), 1)
    ok2 = (i2 <= new_lim) & (i2 < t)
    p1, p2 = _joint_softmax(s1, in_past & (kpos <= q_pos), s2, ok2)
    o_ref[0] = (jnp.dot(p1.astype(BF16), vcat_ref[...], preferred_element_type=F32)
                + jnp.dot(p2.astype(BF16), vn_ref[0].astype(BF16), preferred_element_type=F32))


def _dec_slc(sel, page_table, cache_k, cache_v, layer, q_slot, ks_new, vs_new, pos0, t):
    slots, top = sel.shape
    hd = NSA_HEAD_DIM
    n_past_blk = page_table.shape[1] * PAGE_SIZE // SEL_LEN
    halves = PAGE_SIZE // SEL_LEN
    per_b = NSA_KV_HEADS * t

    def blk_spec(k):
        def index(s, sel_ref, tbl_ref):
            blk = jnp.minimum(sel_ref[s, k], n_past_blk - 1)
            page = tbl_ref[s // per_b, blk // halves]
            return layer, page, blk % halves, 0, 0
        return pl.BlockSpec((None, 1, SEL_LEN, NSA_KV_HEADS, hd), index)

    new_spec = pl.BlockSpec((1, SUBLANE, hd), lambda s, a, b_: (s // per_b, 0, (s // t) % NSA_KV_HEADS))
    slot_spec = pl.BlockSpec((1, SUBLANE, hd), lambda s, a, b_: (s, 0, 0))
    return pl.pallas_call(
        functools.partial(_dec_slc_kernel, t=t, n_past_blk=n_past_blk, pos0=pos0, top=top),
        grid_spec=pltpu.PrefetchScalarGridSpec(
            num_scalar_prefetch=2,
            grid=(slots,),
            in_specs=[blk_spec(k) for k in range(top)] + [blk_spec(k) for k in range(top)]
                     + [slot_spec, new_spec, new_spec],
            out_specs=slot_spec,
            scratch_shapes=[pltpu.VMEM((top * SEL_LEN, hd), BF16), pltpu.VMEM((top * SEL_LEN, hd), BF16)]),
        out_shape=jax.ShapeDtypeStruct((slots, SUBLANE, hd), F32),
        compiler_params=_cparams(1),
        name="dec_slc",
    )(sel, page_table, *([cache_k] * top), *([cache_v] * top), q_slot, ks_new, vs_new)


def _dec_combine_kernel(gp_ref, eg_ref, ocmp_ref, oslc_ref, owin_ref, o_ref):
    gexp = _spread(_sigmoid(gp_ref[...]), eg_ref[...])
    w = NSA_Q_DIM
    out = gexp[:, 0:w] * ocmp_ref[...] + gexp[:, w:2 * w] * oslc_ref[...] + gexp[:, 2 * w:3 * w] * owin_ref[...]
    o_ref[...] = out.astype(o_ref.dtype)


def _dec_combine(gate_proj, o_cmp, o_slc, o_win):
    rows, n_gate = gate_proj.shape
    eg = np.zeros((n_gate, 3 * NSA_Q_DIM), np.float32)
    for br in range(3):
        for h in range(NSA_HEADS):
            c0 = br * NSA_Q_DIM + h * NSA_HEAD_DIM
            eg[br * NSA_HEADS + h, c0:c0 + NSA_HEAD_DIM] = 1.0
    full = lambda a: pl.BlockSpec(a.shape, lambda i: (0, 0))
    args = (gate_proj, jnp.asarray(eg, BF16), o_cmp, o_slc, o_win)
    return pl.pallas_call(
        _dec_combine_kernel,
        grid=(1,),
        in_specs=[full(a) for a in args],
        out_specs=pl.BlockSpec((rows, NSA_Q_DIM), lambda i: (0, 0)),
        out_shape=jax.ShapeDtypeStruct((rows, NSA_Q_DIM), BF16),
        compiler_params=_cparams(1),
        name="dec_combine",
    )(*args)


def _nsa_decode(proj, gate_proj, pos0, P, j, caches, page_table):
    b, t, _ = proj.shape
    cache_ck, cache_cv, cache_sk, cache_sv, win_k, win_v = caches
    assert t & (t - 1) == 0 and t <= SUBLANE
    hd, g, rep = NSA_HEAD_DIM, NSA_KV_HEADS, NSA_REP
    pad_t = lambda a: jnp.pad(a, ((0, 0), (0, SUBLANE - t), (0, 0)))
    q, kc, ks, kw, _, _, _, _ = _rope_split(pad_t(proj), pos0 + jnp.arange(SUBLANE), lowp=F32)
    q, kc, ks, kw = q[:, :t], kc[:, :t], ks[:, :t], kw[:, :t]
    vc, vs, vw = [proj[..., NSA_Q_DIM + k * NSA_KV_DIM:NSA_Q_DIM + (k + 1) * NSA_KV_DIM] for k in (1, 3, 5)]

    tk = page_table.shape[1] * PAGE_SIZE + t
    n_cmp = (tk - CMP_LEN) // CMP_STRIDE + 1
    n_sel = -(-tk // SEL_LEN)
    assert (n_cmp + 1) * CMP_STRIDE <= page_table.shape[1] * PAGE_SIZE
    ab_k, w1k = _cmp_pages(cache_ck, j, page_table, P['nsa_cmpk_w1'][j])
    ab_v, w1v = _cmp_pages(cache_cv, j, page_table, P['nsa_cmpv_w1'][j])
    kcmp = _cmp_finish(ab_k, w1k, P['nsa_cmpk_w2'][j], P['nsa_cmpk_pe'][j])
    vcmp = _cmp_finish(ab_v, w1v, P['nsa_cmpv_w2'][j], P['nsa_cmpv_pe'][j])

    q5 = q.reshape(b, t, g, rep, hd)
    q_stk = q5.transpose(0, 2, 3, 1, 4).reshape(b * g, rep * t, hd)
    o_cmp, o_win, imp = _dec_cmp_win(q_stk, kcmp, vcmp, win_k, win_v, j, pad_t(kw), pad_t(vw),
                                     n_cmp, n_sel, pos0, t)
    sel = _dec_topk(imp[:, :t].reshape(b * g * t, -1), n_sel, pos0, t).T
    q_slot = q5.transpose(0, 2, 1, 3, 4).reshape(b * g * t, rep, hd)
    q_slot = jnp.pad(q_slot, ((0, 0), (0, SUBLANE - rep), (0, 0)))
    o_slc = _dec_slc(sel, page_table, cache_sk, cache_sv, j, q_slot, pad_t(ks), pad_t(vs), pos0, t)

    unstk = lambda o: o.reshape(b, g, rep, t, hd).transpose(0, 3, 1, 2, 4).reshape(b * t, NSA_Q_DIM)
    o_slc = o_slc[:, :rep].reshape(b, g, t, rep, hd).transpose(0, 2, 1, 3, 4).reshape(b * t, NSA_Q_DIM)
    o = _dec_combine(gate_proj.reshape(b * t, -1), unstk(o_cmp), o_slc, unstk(o_win))
    heads = lambda a: a.reshape(b, t, g, hd)
    kw_new = jnp.concatenate([win_k[j][:, t:], heads(kw)], axis=1)
    vw_new = jnp.concatenate([win_v[j][:, t:], heads(vw)], axis=1)
    return o.reshape(b, t, NSA_Q_DIM), kc, vc, ks, vs, kw_new, vw_new


def _run_trunk(x, mods, pos0, past, P):
    b, t, d = x.shape
    fresh = past is None
    per_batch = t % SUBLANE == 0 and t >= LANE
    names = ('ssd_state', 'ssd_conv', 'cmp_k', 'cmp_v', 'slc_k', 'slc_v', 'win_k', 'win_v', 'pool', 'ffn')
    new = {n: [] for n in names}

    bm, tm = (b, t) if per_batch else (1, b * t)

    def mod_rows(v):
        return v[:, None, :] if per_batch else jnp.repeat(v, t, axis=0)[None]

    def mm_rows(a, w, **kw):
        return _mm(a.reshape(b * t, a.shape[-1]), w, **kw)

    def pad_cols(w):
        return jnp.pad(w, ((0, 0), (0, LANE - w.shape[1])))

    def mm_res(a, w, layer, xres, gate):
        out = _mm(a.reshape(b * t, a.shape[-1]), w, layer=layer, res=xres.reshape(b * t, d), gate=gate,
                  rows_per_gate=t if per_batch else None)
        return out.reshape(b, t, d)

    for i in range(DEPTH):
        kind, j = i % N_MIXERS, i // N_MIXERS
        sh1, sc1, g1, sh2, sc2, g2 = [mod_rows(v) for v in jnp.split(mods[i], 6, axis=-1)]
        h_dtype = F32 if kind == 2 else BF16
        h = _normmod(x.reshape(bm, tm, d), P['norm1_g'][i], sh1, sc1, h_dtype).reshape(b, t, d)
        if kind == 0:
            if fresh:
                conv_buf = jnp.zeros((b, SSD_CONV - 1, SSD_CONV_DIM), F32)
                s0 = jnp.zeros((b, SSD_HEADS, SSD_HEAD_DIM, SSD_STATE), F32)
            else:
                conv_buf, s0 = past['ssd_conv'][j], past['ssd_state'][j]
            w_in = P['ssd_w_in']
            nzx = SSD_D_INNER + SSD_CONV_DIM
            dt_raw = mm_rows(h, pad_cols(w_in[j, :, nzx:])).reshape(b, t, LANE)[..., :SSD_HEADS]
            if per_batch:
                zx = mm_rows(h, w_in, layer=j, n_cols=SSD_D_INNER).reshape(b, t, SSD_D_INNER)
                xbc, conv_new = _mm_conv_silu(h.reshape(b * t, d), w_in, j, SSD_D_INNER, SSD_CONV_DIM, t,
                                              conv_buf, P['ssd_conv_w'][j], P['ssd_conv_b'][j])
                xbc = xbc.reshape(b, t, SSD_CONV_DIM)
            else:
                zx = mm_rows(h, w_in, layer=j, n_cols=nzx).reshape(b, t, nzx)
                xbc, conv_new = _conv_silu(zx, SSD_D_INNER, SSD_CONV_DIM, conv_buf,
                                           P['ssd_conv_w'][j], P['ssd_conv_b'][j])
            tp = -(-t // SSD_CHUNK) * SSD_CHUNK
            if tp != t:
                xbc_p = jnp.pad(xbc, ((0, 0), (0, tp - t), (0, 0)))
                dt_p = jnp.pad(dt_raw, ((0, 0), (0, tp - t), (0, 0)))
            else:
                xbc_p, dt_p = xbc, dt_raw
            y, s_new = _ssd_scan(xbc_p, dt_p, P['ssd_dt_bias'][j], P['ssd_a_log'][j], P['ssd_d'][j], s0,
                                 t_valid=min(t, SSD_CHUNK))
            yn = _gated_norm(y[:, :t], zx, P['ssd_norm_g'][j])
            x = mm_res(yn, P['ssd_w_out'], j, x, g1)
            new['ssd_conv'].append(conv_new)
            new['ssd_state'].append(s_new)
        elif kind == 1:
            w_in = P['nsa_w_in']
            npj = NSA_Q_DIM + 6 * NSA_KV_DIM
            proj = mm_rows(h, w_in, layer=j, n_cols=npj).reshape(b, t, npj)
            gate_proj = mm_rows(h, pad_cols(w_in[j, :, npj:])).reshape(b, t, LANE)
            if fresh:
                q_pos = pos0 + jnp.arange(t)
                q, kc, ks, kw, ks_b, vs_b, kw_b, vw_b = _rope_split(proj, q_pos)
                vc, vs, vw = [proj[..., NSA_Q_DIM + k * NSA_KV_DIM:NSA_Q_DIM + (k + 1) * NSA_KV_DIM]
                              for k in (1, 3, 5)]
                kcmp = _compress(kc, P['nsa_cmpk_w1'][j], P['nsa_cmpk_w2'][j], P['nsa_cmpk_pe'][j])
                vcmp = _compress(vc, P['nsa_cmpv_w1'][j], P['nsa_cmpv_w2'][j], P['nsa_cmpv_pe'][j])
                o = _nsa_attention(q, kcmp, vcmp, ks_b, vs_b, kw_b, vw_b, gate_proj)
                keep = min(WINDOW, t)
                kw_new, vw_new = kw[:, t - keep:], vw[:, t - keep:]
            else:
                o, kc, vc, ks, vs, kw_new, vw_new = _nsa_decode(
                    proj, gate_proj, pos0, P, j, past['nsa'][j], past['page_table'])
            x = mm_res(o, P['nsa_w_out'], j, x, g1)
            shp = (b, t, NSA_KV_HEADS, NSA_HEAD_DIM)
            for n, v in (('cmp_k', kc), ('cmp_v', vc), ('slc_k', ks), ('slc_v', vs)):
                new[n].append(v.reshape(shp))
            new['win_k'].append(kw_new.reshape(b, -1, NSA_KV_HEADS, NSA_HEAD_DIM))
            new['win_v'].append(vw_new.reshape(b, -1, NSA_KV_HEADS, NSA_HEAD_DIM))
        else:
            buf = jnp.zeros((b, POOL_BUF, d), F32) if fresh else past['pool'][j]
            gate = (g1 * P['pool_scale'][j]).reshape(b, -1, d)
            x = _pool_mixer(h, buf, P['pool_w'][j], x, gate, pos0)
            new['pool'].append(jnp.concatenate([buf, h], axis=1)[:, -POOL_BUF:])
        h2 = _normmod(x.reshape(bm, tm, d), P['norm2_g'][i], sh2, sc2, BF16)
        fbuf = jnp.zeros((b, FFN_CONV - 1, 2 * D_FF), F32) if fresh else past['ffn'][i]
        if per_batch:
            act, fbuf_new = _mm_conv_glu(h2.reshape(b * t, d), P['ffn_w_up'], i, t, fbuf,
                                         P['ffn_conv_w'][i], P['ffn_conv_b'][i])
        else:
            u = mm_rows(h2, P['ffn_w_up'], layer=i).reshape(b, t, 2 * D_FF)
            act, fbuf_new = _conv_glu(u, fbuf, P['ffn_conv_w'][i], P['ffn_conv_b'][i])
        x = mm_res(act, P['ffn_w_down'], i, x, g2)
        new['ffn'].append(fbuf_new)
    zero = jnp.zeros((bm, 1, d), F32)
    y = _normmod(x.reshape(bm, tm, d), P['final_g'], zero, zero, F32).reshape(b, t, d)
    return y, {n: jnp.stack(v) for n, v in new.items()}


def kernel(x_prompt, x_sample, state_ssd, state_ssd_conv, cache_cmp_k, cache_cmp_v, cache_slc_k, cache_slc_v,
           cache_win_k, cache_win_v, state_pool, state_ffn_conv, page_table, c_prompt, c_sample,
           ada_w, ada_b, norm1_g, norm2_g, final_g,
           ssd_w_in, ssd_conv_w, ssd_conv_b, ssd_dt_bias, ssd_a_log, ssd_d, ssd_norm_g, ssd_w_out,
           nsa_w_in, nsa_cmpk_w1, nsa_cmpk_w2, nsa_cmpk_pe, nsa_cmpv_w1, nsa_cmpv_w2, nsa_cmpv_pe, nsa_w_out,
           pool_w, pool_scale, ffn_w_up, ffn_conv_w, ffn_conv_b, ffn_w_down):
    P = dict(norm1_g=norm1_g, norm2_g=norm2_g, final_g=final_g,
             ssd_w_in=ssd_w_in, ssd_conv_w=ssd_conv_w, ssd_conv_b=ssd_conv_b, ssd_dt_bias=ssd_dt_bias,
             ssd_a_log=ssd_a_log, ssd_d=ssd_d, ssd_norm_g=ssd_norm_g, ssd_w_out=ssd_w_out,
             nsa_w_in=nsa_w_in, nsa_cmpk_w1=nsa_cmpk_w1, nsa_cmpk_w2=nsa_cmpk_w2, nsa_cmpk_pe=nsa_cmpk_pe,
             nsa_cmpv_w1=nsa_cmpv_w1, nsa_cmpv_w2=nsa_cmpv_w2, nsa_cmpv_pe=nsa_cmpv_pe, nsa_w_out=nsa_w_out,
             pool_w=pool_w, pool_scale=pool_scale,
             ffn_w_up=ffn_w_up, ffn_conv_w=ffn_conv_w, ffn_conv_b=ffn_conv_b, ffn_w_down=ffn_w_down)
    nbp, nbs = c_prompt.shape[0], c_sample.shape[0]
    c_rows = -(-(nbp + nbs) // (2 * SUBLANE)) * (2 * SUBLANE)
    c_all = jnp.concatenate([c_prompt, c_sample, jnp.zeros((c_rows - nbp - nbs, D_MODEL), F32)], axis=0)
    mods = [_mm(c_all, ada_w, layer=i, a_silu=True, bias=ada_b[i]) for i in range(DEPTH)]
    mods_p = [m[:nbp] for m in mods]
    mods_s = [m[nbp:nbp + nbs] for m in mods]

    y_prompt, sp = _run_trunk(x_prompt, mods_p, 0, None, P)
    past_len = page_table.shape[1] * PAGE_SIZE
    nsa_past = [(cache_cmp_k, cache_cmp_v, cache_slc_k, cache_slc_v, cache_win_k, cache_win_v)
                for _ in range(cache_cmp_k.shape[0])]
    past = dict(ssd_state=state_ssd, ssd_conv=state_ssd_conv, nsa=nsa_past, page_table=page_table,
                pool=state_pool, ffn=state_ffn_conv)
    y_sample, ss = _run_trunk(x_sample, mods_s, past_len, past, P)
    return (y_prompt, y_sample,
            sp['ssd_state'], ss['ssd_state'], sp['ssd_conv'], ss['ssd_conv'],
            sp['cmp_k'], ss['cmp_k'], sp['cmp_v'], ss['cmp_v'],
            sp['slc_k'], ss['slc_k'], sp['slc_v'], ss['slc_v'],
            sp['win_k'], ss['win_k'], sp['win_v'], ss['win_v'],
            sp['pool'], ss['pool'], sp['ffn'], ss['ffn'])
```

```python
import functools
import math

import jax
import jax.numpy as jnp
import numpy as np
from jax import lax
from jax.experimental import pallas as pl
from jax.experimental.pallas import tpu as pltpu

D_MODEL = 2048
DEPTH = 4
PAGE_SIZE = 128
N_MIXERS = 3

SSD_D_INNER = 2 * D_MODEL
SSD_HEAD_DIM = 64
SSD_HEADS = SSD_D_INNER // SSD_HEAD_DIM
SSD_GROUPS = 8
SSD_REP = SSD_HEADS // SSD_GROUPS
SSD_STATE = 128
SSD_CONV = 4
SSD_CHUNK = 128
SSD_GROUP_DIM = SSD_REP * SSD_HEAD_DIM
SSD_CONV_DIM = SSD_D_INNER + 2 * SSD_GROUPS * SSD_STATE

NSA_HEADS = 16
NSA_KV_HEADS = 4
NSA_HEAD_DIM = D_MODEL // NSA_HEADS
NSA_REP = NSA_HEADS // NSA_KV_HEADS
NSA_Q_DIM = NSA_HEADS * NSA_HEAD_DIM
NSA_KV_DIM = NSA_KV_HEADS * NSA_HEAD_DIM
CMP_LEN = 32
CMP_STRIDE = 16
CMP_HIDDEN = 256
SEL_LEN = 64
SEL_TOPK = 16
WINDOW = 512
WIN_Q_BLOCK = 128
SEL_Q_BLOCK = 64
ROPE_THETA = 10000.0
FORCE_BONUS = 1000.0

POOL_WINDOWS = (2, 4, 8, 16)
POOL_GROUPS = len(POOL_WINDOWS)
POOL_GROUP_DIM = D_MODEL // POOL_GROUPS
POOL_BUF = max(POOL_WINDOWS) - 1

D_FF = 5632
FFN_CONV = 3

EPS = 1e-6
NEG_INF = -1e30
TINY = 1e-30

V7X_VMEM_LIMIT_BYTES = 52 * 1024 * 1024
LANE = 128
SUBLANE = 8

F32 = jnp.float32
BF16 = jnp.bfloat16
HIGHEST = lax.Precision.HIGHEST

NSA_TQ = 256
NSA_SLC_KC = 512
NSA_WIN_SPAN = WINDOW + NSA_TQ


def _cparams(n_axes):
    return pltpu.CompilerParams(dimension_semantics=("arbitrary",) * n_axes,
                                vmem_limit_bytes=V7X_VMEM_LIMIT_BYTES)


def _pick_tile(dim, pref, align):
    t = min(pref, dim)
    t -= t % align
    while t >= align:
        if dim % t == 0:
            return t
        t -= align
    return dim


def _sigmoid(x):
    return 1.0 / (1.0 + jnp.exp(-x))


def _silu(x):
    return x * _sigmoid(x)


def _dot_nt(a, b):
    return lax.dot_general(a, b, (((1,), (1,)), ((), ())), preferred_element_type=F32)


def _dot_tn(a, b):
    return lax.dot_general(a, b, (((0,), (0,)), ((), ())), preferred_element_type=F32)


def _dot_exact(a, b):
    return jnp.dot(a, b, precision=HIGHEST, preferred_element_type=F32)


def _split3(x):
    hi = x.astype(BF16)
    r = x - hi.astype(F32)
    mid = r.astype(BF16)
    lo = (r - mid.astype(F32)).astype(BF16)
    return hi, mid, lo


def _spread(x, e):
    hi, mid, lo = _split3(x)
    dot = lambda p: jnp.dot(p, e, preferred_element_type=F32)
    return dot(hi) + dot(mid) + dot(lo)


def _spread_rows(e, x):
    hi, mid, lo = _split3(x)
    dot = lambda p: jnp.dot(e, p, preferred_element_type=F32)
    return dot(hi) + dot(mid) + dot(lo)


def _normmod_kernel(x_ref, g_ref, sh_ref, sc_ref, o_ref):
    x = x_ref[0].astype(F32)
    y = x * lax.rsqrt(jnp.mean(x * x, axis=-1, keepdims=True) + EPS) * g_ref[...]
    y = y * (1.0 + sc_ref[0]) + sh_ref[0]
    o_ref[0] = y.astype(o_ref.dtype)


def _normmod(x, g, shift, scale, out_dtype):
    b, t, d = x.shape
    r = shift.shape[1]
    tt = _pick_tile(t, 512, SUBLANE)
    if r == 1:
        mod_spec = pl.BlockSpec((1, 1, d), lambda i, j: (i, 0, 0))
    else:
        mod_spec = pl.BlockSpec((1, tt, d), lambda i, j: (i, j, 0))
    return pl.pallas_call(
        _normmod_kernel,
        grid=(b, t // tt),
        in_specs=[pl.BlockSpec((1, tt, d), lambda i, j: (i, j, 0)),
                  pl.BlockSpec((1, d), lambda i, j: (0, 0)),
                  mod_spec, mod_spec],
        out_specs=pl.BlockSpec((1, tt, d), lambda i, j: (i, j, 0)),
        out_shape=jax.ShapeDtypeStruct((b, t, d), out_dtype),
        compiler_params=_cparams(2),
        name="normmod",
    )(x, g.reshape(1, d), shift, scale)


def _mm_kernel(*refs, a_silu, has_bias, has_res):
    it = iter(refs)
    a_ref, w_ref = next(it), next(it)
    bias_ref = next(it) if has_bias else None
    res_ref = next(it) if has_res else None
    gate_ref = next(it) if has_res else None
    o_ref, wb_ref = next(it), next(it)

    @pl.when(pl.program_id(1) == 0)
    def _():
        wb_ref[...] = w_ref[...].astype(BF16)

    a = a_ref[...]
    if a_silu:
        a = _silu(a.astype(F32))
    acc = jnp.dot(a.astype(BF16), wb_ref[...], preferred_element_type=F32)
    if has_bias:
        acc = acc + bias_ref[...]
    if has_res:
        acc = res_ref[...] + gate_ref[0] * acc
    o_ref[...] = acc.astype(o_ref.dtype)


def _w_spec(w, layer, k, tn, col_block):
    if w.ndim == 2:
        return pl.BlockSpec((k, tn), lambda j, i: (0, col_block(j)))
    return pl.BlockSpec((None, k, tn), lambda j, i: (layer, 0, col_block(j)))


def _mm(a, w, *, layer=None, n_cols=None, a_silu=False, bias=None, res=None, gate=None, rows_per_gate=None,
        out_dtype=F32, tm=1024, tn=1024):
    m, k = a.shape
    assert w.shape[-2] == k
    n = w.shape[-1] if n_cols is None else n_cols
    tm = _pick_tile(m if rows_per_gate is None else rows_per_gate, tm, SUBLANE)
    tn = _pick_tile(n, tn, LANE)
    while k * tn * 10 > 30 * 1024 * 1024 and tn % (2 * LANE) == 0:
        tn //= 2
    while 2 * k * tm * a.dtype.itemsize > 16 * 1024 * 1024 and tm % (2 * SUBLANE) == 0:
        tm //= 2
    assert m % tm == 0 and n % tn == 0
    in_specs = [pl.BlockSpec((tm, k), lambda j, i: (i, 0)),
                _w_spec(w, layer, k, tn, lambda j: j)]
    args = [a, w]
    if bias is not None:
        in_specs.append(pl.BlockSpec((1, tn), lambda j, i: (0, j)))
        args.append(bias.reshape(1, -1))
    if res is not None:
        in_specs.append(pl.BlockSpec((tm, tn), lambda j, i: (i, j)))
        args.append(res)
        if gate.shape[1] == 1:
            tiles_per_gate = rows_per_gate // tm
            in_specs.append(pl.BlockSpec((1, 1, tn), lambda j, i: (i // tiles_per_gate, 0, j)))
        else:
            in_specs.append(pl.BlockSpec((1, tm, tn), lambda j, i: (0, i, j)))
        args.append(gate)
    return pl.pallas_call(
        functools.partial(_mm_kernel, a_silu=a_silu, has_bias=bias is not None, has_res=res is not None),
        grid=(n // tn, m // tm),
        in_specs=in_specs,
        out_specs=pl.BlockSpec((tm, tn), lambda j, i: (i, j)),
        out_shape=jax.ShapeDtypeStruct((m, n), out_dtype),
        scratch_shapes=[pltpu.VMEM((k, tn), BF16)],
        compiler_params=_cparams(2),
        name="mm",
    )(*args)


_CONV_PAD = SUBLANE


def _conv_tile(x, first, buf_ref, w_ref, b_ref, st_ref, xe_ref, width, tt):
    lo = _CONV_PAD - (width - 1)

    @pl.when(first)
    def _():
        xe_ref[lo:_CONV_PAD, :] = buf_ref[0]

    @pl.when(jnp.logical_not(first))
    def _():
        xe_ref[0:_CONV_PAD, :] = xe_ref[tt:tt + _CONV_PAD, :]

    xe_ref[_CONV_PAD:_CONV_PAD + tt, :] = x
    acc = b_ref[...] + w_ref[0:1, :] * xe_ref[lo:lo + tt, :]
    for k in range(1, width):
        acc = acc + w_ref[k:k + 1, :] * xe_ref[lo + k:lo + k + tt, :]
    st_ref[0] = xe_ref[tt + lo:tt + _CONV_PAD, :]
    return acc


def _conv_silu_kernel(x_ref, buf_ref, w_ref, b_ref, y_ref, st_ref, xe_ref, *, width, tt):
    first = pl.program_id(2) == 0
    y_ref[0] = _silu(_conv_tile(x_ref[0], first, buf_ref, w_ref, b_ref, st_ref, xe_ref, width, tt))


def _conv_glu_kernel(xa_ref, xg_ref, bufa_ref, bufg_ref, wa_ref, wg_ref, ba_ref, bg_ref,
                     act_ref, sta_ref, stg_ref, xea_ref, xeg_ref, *, width, tt):
    first = pl.program_id(2) == 0
    a = _conv_tile(xa_ref[0], first, bufa_ref, wa_ref, ba_ref, sta_ref, xea_ref, width, tt)
    g = _conv_tile(xg_ref[0], first, bufg_ref, wg_ref, bg_ref, stg_ref, xeg_ref, width, tt)
    act_ref[0] = (_silu(g) * a).astype(act_ref.dtype)


def _mm_conv_silu_kernel(a_ref, w_ref, buf_ref, cw_ref, cb_ref, y_ref, st_ref, wb_ref, xe_ref,
                         *, width, tm, tiles_per_seq):
    i = pl.program_id(1)

    @pl.when(i == 0)
    def _():
        wb_ref[...] = w_ref[...].astype(BF16)

    u = jnp.dot(a_ref[...], wb_ref[...], preferred_element_type=F32)
    first = i % tiles_per_seq == 0
    y_ref[...] = _silu(_conv_tile(u, first, buf_ref, cw_ref, cb_ref, st_ref, xe_ref, width, tm))


def _mm_conv_glu_kernel(a_ref, wa_ref, wg_ref, bufa_ref, bufg_ref, cwa_ref, cwg_ref, cba_ref, cbg_ref,
                        act_ref, sta_ref, stg_ref, wba_ref, wbg_ref, xea_ref, xeg_ref,
                        *, width, tm, tiles_per_seq):
    i = pl.program_id(1)

    @pl.when(i == 0)
    def _():
        wba_ref[...] = wa_ref[...].astype(BF16)
        wbg_ref[...] = wg_ref[...].astype(BF16)

    a = a_ref[...]
    first = i % tiles_per_seq == 0
    ua = jnp.dot(a, wba_ref[...], preferred_element_type=F32)
    ca = _conv_tile(ua, first, bufa_ref, cwa_ref, cba_ref, sta_ref, xea_ref, width, tm)
    ug = jnp.dot(a, wbg_ref[...], preferred_element_type=F32)
    cg = _conv_tile(ug, first, bufg_ref, cwg_ref, cbg_ref, stg_ref, xeg_ref, width, tm)
    act_ref[...] = (_silu(cg) * ca).astype(act_ref.dtype)


def _mm_conv_silu(a, w, layer, col0, n_ch, seq_len, buf, cw, cbias, *, tm=2048, tn=512):
    m, k = a.shape
    width = cw.shape[0]
    nb = m // seq_len
    tm = _pick_tile(seq_len, tm, 2 * SUBLANE)
    tn = _pick_tile(n_ch, tn, LANE)
    assert col0 % tn == 0
    off = col0 // tn
    tps = seq_len // tm
    return pl.pallas_call(
        functools.partial(_mm_conv_silu_kernel, width=width, tm=tm, tiles_per_seq=tps),
        grid=(n_ch // tn, m // tm),
        in_specs=[pl.BlockSpec((tm, k), lambda j, i: (i, 0)),
                  _w_spec(w, layer, k, tn, lambda j: j + off),
                  pl.BlockSpec((1, width - 1, tn), lambda j, i: (i // tps, 0, j)),
                  pl.BlockSpec((width, tn), lambda j, i: (0, j)),
                  pl.BlockSpec((1, tn), lambda j, i: (0, j))],
        out_specs=[pl.BlockSpec((tm, tn), lambda j, i: (i, j)),
                   pl.BlockSpec((1, width - 1, tn), lambda j, i: (i // tps, 0, j))],
        out_shape=[jax.ShapeDtypeStruct((m, n_ch), F32), jax.ShapeDtypeStruct((nb, width - 1, n_ch), F32)],
        scratch_shapes=[pltpu.VMEM((k, tn), BF16), pltpu.VMEM((tm + _CONV_PAD, tn), F32)],
        compiler_params=_cparams(2),
        name="mm_conv_silu",
    )(a, w, buf, cw, cbias.reshape(1, -1))


def _mm_conv_glu(a, w, layer, seq_len, buf, cw, cbias, *, tm=2048, tn=256):
    m, k = a.shape
    half = w.shape[-1] // 2
    width = cw.shape[0]
    nb = m // seq_len
    tm = _pick_tile(seq_len, tm, 2 * SUBLANE)
    tn = _pick_tile(half, tn, LANE)
    hb = half // tn
    tps = seq_len // tm
    cb2 = cbias.reshape(1, -1)
    lo_hi = lambda shape, index: [pl.BlockSpec(shape, lambda j, i: index(j, i)),
                                  pl.BlockSpec(shape, lambda j, i: index(j + hb, i))]
    st_spec = pl.BlockSpec((1, width - 1, tn), lambda j, i: (i // tps, 0, j))
    act, sta, stg = pl.pallas_call(
        functools.partial(_mm_conv_glu_kernel, width=width, tm=tm, tiles_per_seq=tps),
        grid=(hb, m // tm),
        in_specs=[pl.BlockSpec((tm, k), lambda j, i: (i, 0)),
                  _w_spec(w, layer, k, tn, lambda j: j), _w_spec(w, layer, k, tn, lambda j: j + hb)]
                 + lo_hi((1, width - 1, tn), lambda j, i: (i // tps, 0, j))
                 + lo_hi((width, tn), lambda j, i: (0, j))
                 + lo_hi((1, tn), lambda j, i: (0, j)),
        out_specs=[pl.BlockSpec((tm, tn), lambda j, i: (i, j)), st_spec, st_spec],
        out_shape=[jax.ShapeDtypeStruct((m, half), BF16),
                   jax.ShapeDtypeStruct((nb, width - 1, half), F32),
                   jax.ShapeDtypeStruct((nb, width - 1, half), F32)],
        scratch_shapes=[pltpu.VMEM((k, tn), BF16), pltpu.VMEM((k, tn), BF16),
                        pltpu.VMEM((tm + _CONV_PAD, tn), F32), pltpu.VMEM((tm + _CONV_PAD, tn), F32)],
        compiler_params=_cparams(2),
        name="mm_conv_glu",
    )(a, w, w, buf, buf, cw, cw, cb2, cb2)
    return act, jnp.concatenate([sta, stg], axis=-1)


def _conv_silu(x, col0, n_ch, buf, w, bias, *, tc=512, tt=512):
    b, t, _ = x.shape
    width = w.shape[0]
    tt = _pick_tile(t, tt, SUBLANE)
    off = col0 // tc
    return pl.pallas_call(
        functools.partial(_conv_silu_kernel, width=width, tt=tt),
        grid=(b, n_ch // tc, t // tt),
        in_specs=[pl.BlockSpec((1, tt, tc), lambda i, j, s: (i, s, j + off)),
                  pl.BlockSpec((1, width - 1, tc), lambda i, j, s: (i, 0, j)),
                  pl.BlockSpec((width, tc), lambda i, j, s: (0, j)),
                  pl.BlockSpec((1, tc), lambda i, j, s: (0, j))],
        out_specs=[pl.BlockSpec((1, tt, tc), lambda i, j, s: (i, s, j)),
                   pl.BlockSpec((1, width - 1, tc), lambda i, j, s: (i, 0, j))],
        out_shape=[jax.ShapeDtypeStruct((b, t, n_ch), F32),
                   jax.ShapeDtypeStruct((b, width - 1, n_ch), F32)],
        scratch_shapes=[pltpu.VMEM((tt + _CONV_PAD, tc), F32)],
        compiler_params=_cparams(3),
        name="conv_silu",
    )(x, buf, w, bias.reshape(1, -1))


def _conv_glu(u, buf, w, bias, *, tc=512, tt=512):
    b, t, c2 = u.shape
    half = c2 // 2
    width = w.shape[0]
    tt = _pick_tile(t, tt, SUBLANE)
    hb = half // tc
    xa = pl.BlockSpec((1, tt, tc), lambda i, j, s: (i, s, j))
    xg = pl.BlockSpec((1, tt, tc), lambda i, j, s: (i, s, j + hb))
    ba = pl.BlockSpec((1, width - 1, tc), lambda i, j, s: (i, 0, j))
    bg = pl.BlockSpec((1, width - 1, tc), lambda i, j, s: (i, 0, j + hb))
    wa = pl.BlockSpec((width, tc), lambda i, j, s: (0, j))
    wg = pl.BlockSpec((width, tc), lambda i, j, s: (0, j + hb))
    ca = pl.BlockSpec((1, tc), lambda i, j, s: (0, j))
    cg = pl.BlockSpec((1, tc), lambda i, j, s: (0, j + hb))
    st = pl.BlockSpec((1, width - 1, tc), lambda i, j, s: (i, 0, j))
    bias2 = bias.reshape(1, -1)
    act, sta, stg = pl.pallas_call(
        functools.partial(_conv_glu_kernel, width=width, tt=tt),
        grid=(b, hb, t // tt),
        in_specs=[xa, xg, ba, bg, wa, wg, ca, cg],
        out_specs=[pl.BlockSpec((1, tt, tc), lambda i, j, s: (i, s, j)), st, st],
        out_shape=[jax.ShapeDtypeStruct((b, t, half), BF16),
                   jax.ShapeDtypeStruct((b, width - 1, half), F32),
                   jax.ShapeDtypeStruct((b, width - 1, half), F32)],
        scratch_shapes=[pltpu.VMEM((tt + _CONV_PAD, tc), F32), pltpu.VMEM((tt + _CONV_PAD, tc), F32)],
        compiler_params=_cparams(3),
        name="conv_glu",
    )(u, u, buf, buf, w, w, bias2, bias2)
    return act, jnp.concatenate([sta, stg], axis=-1)


def _softplus(x):
    return jnp.maximum(x, 0.0) + jnp.log1p(jnp.exp(-jnp.abs(x)))


def _ssd_kernel(x_ref, b_ref, c_ref, dtc_ref, dtr_ref, biasc_ref, biasr_ref, alogc_ref, alogr_ref,
                dch_ref, e512_ref, e1024_ref, et_ref, s0_ref, y_ref, sf_ref, s_ref, *, t_valid):
    L = SSD_CHUNK
    c = pl.program_id(2)

    @pl.when(c == 0)
    def _():
        s_ref[...] = s0_ref[0].reshape(SSD_GROUP_DIM, SSD_STATE)

    x = x_ref[0]
    bm = b_ref[0].astype(BF16)
    cm = c_ref[0].astype(BF16)
    row = lax.broadcasted_iota(jnp.int32, (L, L), 0)
    col = lax.broadcasted_iota(jnp.int32, (L, L), 1)
    causal = row >= col

    dt_c = _softplus(dtc_ref[0, 0] + biasc_ref[0])
    dt_r = _softplus(dtr_ref[0] + biasr_ref[0])
    if t_valid < L:
        dt_c = jnp.where(lax.broadcasted_iota(jnp.int32, dt_c.shape, 0) < t_valid, dt_c, 0.0)
        dt_r = jnp.where(lax.broadcasted_iota(jnp.int32, dt_r.shape, 1) < t_valid, dt_r, 0.0)
    dta_c = dt_c * (-jnp.exp(alogc_ref[0]))
    dta_r = dt_r * (-jnp.exp(alogr_ref[0]))

    tri = jnp.where(causal, 1.0, 0.0).astype(F32)
    upper = jnp.where(row <= col, 1.0, 0.0).astype(F32)
    acs_c = _dot_exact(tri, dta_c)
    acs_r = _dot_exact(dta_r, upper)
    e512 = e512_ref[...]
    acs_x = _spread(acs_c, e512)
    dt_x = _spread(dt_c, e512)
    acs_full = _spread(acs_c, e1024_ref[...])
    alast_x = acs_x[L - 1:L, :]

    xdt = x * dt_x
    xdt_b = xdt.astype(BF16)
    cb = _dot_nt(cm, bm)
    lane = lax.broadcasted_iota(jnp.int32, (L, LANE), 1)
    pairs = []
    for pr in range(SSD_REP // 2):
        xp = xdt_b[:, pr * LANE:(pr + 1) * LANE]
        halves = []
        for r in (2 * pr, 2 * pr + 1):
            seg = acs_full[:, r * LANE:(r + 1) * LANE] - acs_r[r:r + 1, :]
            lm = jnp.exp(jnp.where(causal, seg, NEG_INF))
            halves.append(jnp.dot((cb * lm).astype(BF16), xp, preferred_element_type=F32))
        pairs.append(jnp.where(lane < SSD_HEAD_DIM, halves[0], halves[1]))
    y_diag = jnp.concatenate(pairs, axis=1)

    s_in = s_ref[...]
    y_off = _dot_nt(cm, s_in.astype(BF16)) * jnp.exp(acs_x)
    y_ref[0] = y_diag + y_off + x * dch_ref[...]

    xw = (xdt * jnp.exp(alast_x - acs_x)).astype(BF16)
    alast_r = jnp.sum(dta_r, axis=1, keepdims=True)
    dec = jnp.exp(_spread_rows(et_ref[...], jnp.broadcast_to(alast_r, (SSD_REP, SSD_STATE))))
    s_new = s_in * dec + _dot_tn(xw, bm)
    s_ref[...] = s_new
    sf_ref[0] = s_new.reshape(SSD_REP, SSD_HEAD_DIM, SSD_STATE)


def _ssd_expanders():
    e512 = np.repeat(np.eye(SSD_REP, dtype=np.float32), SSD_HEAD_DIM, axis=1)
    e1024 = np.repeat(np.eye(SSD_REP, dtype=np.float32), LANE, axis=1)
    return jnp.asarray(e512, BF16), jnp.asarray(e1024, BF16), jnp.asarray(e512.T.copy(), BF16)


def _ssd_scan(xbc, dt_raw, dt_bias, a_log, d_skip, s0, t_valid):
    b, t, _ = xbc.shape
    L = SSD_CHUNK
    assert t % L == 0
    g, rep = SSD_GROUPS, SSD_REP
    dtc = dt_raw.reshape(b, t, g, rep).transpose(0, 2, 1, 3)
    dtr = dt_raw.transpose(0, 2, 1)
    bias_c = dt_bias.reshape(g, 1, rep)
    bias_r = jnp.broadcast_to(dt_bias.reshape(g, rep, 1), (g, rep, L))
    alog_c = a_log.reshape(g, 1, rep)
    alog_r = jnp.broadcast_to(a_log.reshape(g, rep, 1), (g, rep, L))
    dch = jnp.repeat(d_skip, SSD_HEAD_DIM).reshape(1, SSD_D_INNER)
    e512, e1024, et = _ssd_expanders()
    xb = SSD_D_INNER // SSD_STATE
    full = lambda shape: pl.BlockSpec(shape, lambda i, j, s: (0,) * len(shape))
    return pl.pallas_call(
        functools.partial(_ssd_kernel, t_valid=t_valid),
        grid=(b, g, t // L),
        in_specs=[pl.BlockSpec((1, L, SSD_GROUP_DIM), lambda i, j, s: (i, s, j)),
                  pl.BlockSpec((1, L, SSD_STATE), lambda i, j, s: (i, s, xb + j)),
                  pl.BlockSpec((1, L, SSD_STATE), lambda i, j, s: (i, s, xb + g + j)),
                  pl.BlockSpec((1, 1, L, rep), lambda i, j, s: (i, j, s, 0)),
                  pl.BlockSpec((1, rep, L), lambda i, j, s: (i, j, s)),
                  pl.BlockSpec((1, 1, rep), lambda i, j, s: (j, 0, 0)),
                  pl.BlockSpec((1, rep, L), lambda i, j, s: (j, 0, 0)),
                  pl.BlockSpec((1, 1, rep), lambda i, j, s: (j, 0, 0)),
                  pl.BlockSpec((1, rep, L), lambda i, j, s: (j, 0, 0)),
                  pl.BlockSpec((1, SSD_GROUP_DIM), lambda i, j, s: (0, j)),
                  full((rep, SSD_GROUP_DIM)), full((rep, rep * LANE)), full((SSD_GROUP_DIM, rep)),
                  pl.BlockSpec((1, rep, SSD_HEAD_DIM, SSD_STATE), lambda i, j, s: (i, j, 0, 0))],
        out_specs=[pl.BlockSpec((1, L, SSD_GROUP_DIM), lambda i, j, s: (i, s, j)),
                   pl.BlockSpec((1, rep, SSD_HEAD_DIM, SSD_STATE), lambda i, j, s: (i, j, 0, 0))],
        out_shape=[jax.ShapeDtypeStruct((b, t, SSD_D_INNER), F32),
                   jax.ShapeDtypeStruct((b, SSD_HEADS, SSD_HEAD_DIM, SSD_STATE), F32)],
        scratch_shapes=[pltpu.VMEM((SSD_GROUP_DIM, SSD_STATE), F32)],
        compiler_params=_cparams(3),
        name="ssd_scan",
    )(xbc, xbc, xbc, dtc, dtr, bias_c, bias_r, alog_c, alog_r, dch, e512, e1024, et, s0)


def _gated_norm_kernel(y_ref, z_ref, g_ref, o_ref):
    v = y_ref[0] * _silu(z_ref[0])
    o = v * lax.rsqrt(jnp.mean(v * v, axis=-1, keepdims=True) + EPS) * g_ref[...]
    o_ref[0] = o.astype(o_ref.dtype)


def _gated_norm(y, zx, g):
    b, t, d = y.shape
    tt = _pick_tile(t, 128, SUBLANE)
    return pl.pallas_call(
        _gated_norm_kernel,
        grid=(b, t // tt),
        in_specs=[pl.BlockSpec((1, tt, d), lambda i, j: (i, j, 0)),
                  pl.BlockSpec((1, tt, d), lambda i, j: (i, j, 0)),
                  pl.BlockSpec((1, d), lambda i, j: (0, 0))],
        out_specs=pl.BlockSpec((1, tt, d), lambda i, j: (i, j, 0)),
        out_shape=jax.ShapeDtypeStruct((b, t, d), BF16),
        compiler_params=_cparams(2),
        name="gated_norm",
    )(y, zx, g.reshape(1, d))


def _rope_tables(pos):
    half = NSA_HEAD_DIM // 2
    inv = jnp.exp(-math.log(ROPE_THETA) * jnp.arange(half, dtype=F32) * 2.0 / NSA_HEAD_DIM)
    ang = pos.astype(F32)[:, None] * inv[None, :]
    cos, sin = jnp.cos(ang), jnp.sin(ang)
    return jnp.concatenate([cos, cos], axis=1), jnp.concatenate([-sin, sin], axis=1)


def _rope_slab(x, cos2, sin2):
    return x * cos2 + pltpu.roll(x, NSA_HEAD_DIM // 2, 1) * sin2


def _rope_kernel(q_ref, kc_ref, ks_ref, vs_ref, kw_ref, vw_ref, cos_ref, sin_ref,
                 qo_ref, kco_ref, kso_ref, kwo_ref, ksb_ref, vsb_ref, kwb_ref, vwb_ref):
    cos2, sin2 = cos_ref[...], sin_ref[...]
    hd = NSA_HEAD_DIM
    scale = hd ** -0.5
    for h in range(NSA_HEADS):
        sl = slice(h * hd, (h + 1) * hd)
        qo_ref[0, :, sl] = (_rope_slab(q_ref[0, :, sl], cos2, sin2) * scale).astype(qo_ref.dtype)
    for h in range(NSA_KV_HEADS):
        sl = slice(h * hd, (h + 1) * hd)
        kco_ref[0, :, sl] = _rope_slab(kc_ref[0, :, sl], cos2, sin2)
        ks = _rope_slab(ks_ref[0, :, sl], cos2, sin2)
        kso_ref[0, :, sl] = ks
        ksb_ref[0, :, sl] = ks.astype(ksb_ref.dtype)
        kw = _rope_slab(kw_ref[0, :, sl], cos2, sin2)
        kwo_ref[0, :, sl] = kw
        kwb_ref[0, :, sl] = kw.astype(kwb_ref.dtype)
    vsb_ref[0] = vs_ref[0].astype(vsb_ref.dtype)
    vwb_ref[0] = vw_ref[0].astype(vwb_ref.dtype)


def _rope_split(proj, pos, lowp=BF16):
    b, t, _ = proj.shape
    tt = _pick_tile(t, 256, SUBLANE)
    cos2, sin2 = _rope_tables(pos)
    kvd = NSA_KV_DIM
    qb = NSA_Q_DIM // kvd
    kv_in = lambda k: pl.BlockSpec((1, tt, kvd), lambda i, j: (i, j, qb + k))
    kv_out = pl.BlockSpec((1, tt, kvd), lambda i, j: (i, j, 0))
    tab = pl.BlockSpec((tt, NSA_HEAD_DIM), lambda i, j: (j, 0))
    sds = lambda n, dt: jax.ShapeDtypeStruct((b, t, n), dt)
    return pl.pallas_call(
        _rope_kernel,
        grid=(b, t // tt),
        in_specs=[pl.BlockSpec((1, tt, NSA_Q_DIM), lambda i, j: (i, j, 0))]
                 + [kv_in(k) for k in (0, 2, 3, 4, 5)] + [tab, tab],
        out_specs=[pl.BlockSpec((1, tt, NSA_Q_DIM), lambda i, j: (i, j, 0))] + [kv_out] * 7,
        out_shape=[sds(NSA_Q_DIM, lowp), sds(kvd, F32), sds(kvd, F32), sds(kvd, F32),
                   sds(kvd, lowp), sds(kvd, lowp), sds(kvd, lowp), sds(kvd, lowp)],
        compiler_params=_cparams(2),
        name="rope_split",
    )(proj, proj, proj, proj, proj, proj, cos2, sin2)


def _cmp_finish_kernel(ab_ref, pe_ref, w2_ref, o_ref):
    ab = ab_ref[0]
    n_sub = ab.shape[0]
    pe_term = pe_ref[0:1, :CMP_HIDDEN] + pe_ref[1:2, CMP_HIDDEN:]
    nxt = pltpu.roll(ab[:, CMP_HIDDEN:], n_sub - 1, 0)
    hid = ab[:, :CMP_HIDDEN] + nxt + pe_term
    act = 0.5 * hid * (1.0 + jnp.tanh(math.sqrt(2.0 / math.pi) * (hid + 0.044715 * hid * hid * hid)))
    o_ref[0] = jnp.dot(act.astype(BF16), w2_ref[...].astype(BF16),
                       preferred_element_type=F32).astype(o_ref.dtype)


def _compress(rows, w1, w2, pe):
    b, t, _ = rows.shape
    n_sub = t // CMP_STRIDE
    ratio = CMP_LEN // CMP_STRIDE
    kdim = CMP_STRIDE * NSA_HEAD_DIM
    x = rows.reshape(b, n_sub, CMP_STRIDE, NSA_KV_HEADS, NSA_HEAD_DIM).transpose(0, 3, 1, 2, 4)
    x = x.reshape(b * NSA_KV_HEADS * n_sub, kdim)
    w1cat = w1.reshape(ratio, kdim, CMP_HIDDEN).transpose(1, 0, 2).reshape(kdim, ratio * CMP_HIDDEN)
    ab = _mm(x, w1cat).reshape(b * NSA_KV_HEADS, n_sub, ratio * CMP_HIDDEN)
    return _cmp_finish(ab, w1cat, w2, pe)


def _cmp_finish(ab, w1cat, w2, pe):
    ratio = CMP_LEN // CMP_STRIDE
    kdim = CMP_STRIDE * NSA_HEAD_DIM
    n_sub = ab.shape[1]
    pe_rows = jnp.zeros((SUBLANE, kdim), F32).at[:ratio].set(pe.reshape(ratio, kdim))
    pe_ab = _mm(pe_rows, w1cat)
    return pl.pallas_call(
        _cmp_finish_kernel,
        grid=(ab.shape[0],),
        in_specs=[pl.BlockSpec((1, n_sub, ratio * CMP_HIDDEN), lambda i: (i, 0, 0)),
                  pl.BlockSpec((SUBLANE, ratio * CMP_HIDDEN), lambda i: (0, 0)),
                  pl.BlockSpec((CMP_HIDDEN, NSA_HEAD_DIM), lambda i: (0, 0))],
        out_specs=pl.BlockSpec((1, n_sub, NSA_HEAD_DIM), lambda i: (i, 0, 0)),
        out_shape=jax.ShapeDtypeStruct((ab.shape[0], n_sub, NSA_HEAD_DIM), BF16),
        compiler_params=_cparams(1),
        name="cmp_finish",
    )(ab, pe_ab, w2)


def _masked_softmax_rows(s, allowed):
    s = jnp.where(allowed, s, NEG_INF)
    m = jnp.max(s, axis=1, keepdims=True)
    p = jnp.where(allowed, jnp.exp(s - m), 0.0)
    return p / jnp.maximum(jnp.sum(p, axis=1, keepdims=True), TINY)


def _nsa_attn_kernel(q_ref, kcmp_ref, vcmp_ref, ks_ref, vs_ref, kw_ref, vw_ref, gp_ref, covt_ref, eg_ref,
                     o_ref, *, n_sel):
    tq, hd, rep = NSA_TQ, NSA_HEAD_DIM, NSA_REP
    rows = rep * tq
    i = pl.program_id(2)
    q0 = i * tq
    q = q_ref[0]
    qs = jnp.concatenate([q[:, r * hd:(r + 1) * hd] for r in range(rep)], axis=0)

    def qpos(shape):
        return q0 + (lax.broadcasted_iota(jnp.int32, shape, 0) & (tq - 1))

    n_cmp_pad = kcmp_ref.shape[1]
    s_c = _dot_nt(qs, kcmp_ref[0])
    blk_end = lax.broadcasted_iota(jnp.int32, (rows, n_cmp_pad), 1) * CMP_STRIDE + (CMP_LEN - 1)
    p_c = _masked_softmax_rows(s_c, blk_end <= qpos((rows, n_cmp_pad)))
    o_cmp = jnp.dot(p_c.astype(BF16), vcmp_ref[0], preferred_element_type=F32)

    p_sum = p_c[0:tq]
    for r in range(1, rep):
        p_sum = p_sum + p_c[r * tq:(r + 1) * tq]
    imp_t = _dot_nt(covt_ref[...], p_sum.astype(BF16))
    nb = 32
    assert n_sel <= nb
    imp_t = imp_t[0:nb]
    jb = lax.broadcasted_iota(jnp.int32, (nb, tq), 0)
    sel_shift = SEL_LEN.bit_length() - 1
    cur = (q0 + lax.broadcasted_iota(jnp.int32, (nb, tq), 1)) >> sel_shift
    forced = (jb == 0) | (jb == cur) | (jb == cur - 1)
    score = jnp.where(jb <= cur, imp_t + jnp.where(forced, FORCE_BONUS, 0.0), -1.0)
    score = jnp.where(jb < n_sel, score, -2.0)
    cnt = jnp.zeros((nb, tq), F32)
    for jp in range(n_sel):
        rowv = score[jp:jp + 1, :]
        before = (rowv > score) | ((rowv == score) & (jb > jp))
        cnt = cnt + jnp.where(before, 1.0, 0.0)
    sel_t = jnp.where(cnt < float(min(SEL_TOPK, n_sel)), 1.0, 0.0)
    sel_t = jnp.concatenate([sel_t, jnp.zeros((LANE - nb, tq), F32)], axis=0)
    sel = sel_t.T.astype(BF16)

    kc = NSA_SLC_KC

    def slc_body(c, carry):
        m, l, acc = carry
        k0 = pl.multiple_of(c * kc, kc)
        kblk = ks_ref[0, pl.ds(k0, kc), :]
        vblk = vs_ref[0, pl.ds(k0, kc), :]
        s = _dot_nt(qs, kblk)
        kidx = k0 + lax.broadcasted_iota(jnp.int32, (LANE, kc), 1)
        expand = jnp.where(lax.broadcasted_iota(jnp.int32, (LANE, kc), 0) == (kidx >> sel_shift), 1.0, 0.0)
        mk = jnp.dot(sel, expand.astype(BF16), preferred_element_type=F32)
        mk = jnp.concatenate([mk] * rep, axis=0)
        kpos = k0 + lax.broadcasted_iota(jnp.int32, (rows, kc), 1)
        allowed = (mk > 0.5) & (kpos <= qpos((rows, kc)))
        s = jnp.where(allowed, s, NEG_INF)
        m_new = jnp.maximum(m, jnp.max(s, axis=1, keepdims=True))
        alpha = jnp.exp(m - m_new)
        p = jnp.where(allowed, jnp.exp(s - m_new), 0.0)
        l = alpha * l + jnp.sum(p, axis=1, keepdims=True)
        acc = alpha * acc + jnp.dot(p.astype(BF16), vblk, preferred_element_type=F32)
        return m_new, l, acc

    n_chunks = (q0 + tq + kc - 1) // kc
    init = (jnp.full((rows, 1), NEG_INF, F32), jnp.zeros((rows, 1), F32), jnp.zeros((rows, hd), F32))
    _, l_s, acc_s = lax.fori_loop(0, n_chunks, slc_body, init)
    o_slc = acc_s / jnp.maximum(l_s, TINY)

    span = NSA_WIN_SPAN
    w0 = pl.multiple_of(jnp.maximum(q0 + tq - span, 0), tq)
    s_w = _dot_nt(qs, kw_ref[0, pl.ds(w0, span), :])
    dist = qpos((rows, span)) - (w0 + lax.broadcasted_iota(jnp.int32, (rows, span), 1))
    p_w = _masked_softmax_rows(s_w, (dist >= 0) & (dist < WINDOW))
    o_win = jnp.dot(p_w.astype(BF16), vw_ref[0, pl.ds(w0, span), :], preferred_element_type=F32)

    gexp = _spread(_sigmoid(gp_ref[0]), eg_ref[0])
    unstack = lambda o: jnp.concatenate([o[r * tq:(r + 1) * tq] for r in range(rep)], axis=1)
    w = rep * hd
    out = gexp[:, 0:w] * unstack(o_cmp) + gexp[:, w:2 * w] * unstack(o_slc) + gexp[:, 2 * w:3 * w] * unstack(o_win)
    o_ref[0] = out.astype(o_ref.dtype)


def _block_coverage(n_cmp, n_sel):
    start = np.arange(n_cmp)[:, None] * CMP_STRIDE
    sel_start = np.arange(n_sel)[None, :] * SEL_LEN
    inter = np.minimum(start + CMP_LEN, sel_start + SEL_LEN) - np.maximum(start, sel_start)
    return (np.clip(inter, 0, None) / CMP_LEN).astype(np.float32)


def _nsa_attention(q, kcmp, vcmp, ks, vs, kw, vw, gate_proj):
    b, t, _ = q.shape
    n_cmp_pad = t // CMP_STRIDE
    n_cmp = (t - CMP_LEN) // CMP_STRIDE + 1
    n_sel = -(-t // SEL_LEN)
    assert n_cmp_pad == LANE and t % NSA_SLC_KC == 0 and t >= NSA_WIN_SPAN
    cov_t = np.zeros((LANE, n_cmp_pad), np.float32)
    cov_t[:n_sel, :n_cmp] = _block_coverage(n_cmp, n_sel).T
    n_gate = gate_proj.shape[-1]
    eg = np.zeros((NSA_KV_HEADS, n_gate, 3 * NSA_REP * NSA_HEAD_DIM), np.float32)
    for g in range(NSA_KV_HEADS):
        for br in range(3):
            for r in range(NSA_REP):
                c0 = (br * NSA_REP + r) * NSA_HEAD_DIM
                eg[g, br * NSA_HEADS + g * NSA_REP + r, c0:c0 + NSA_HEAD_DIM] = 1.0
    gw = NSA_REP * NSA_HEAD_DIM
    seq = pl.BlockSpec((1, t, NSA_HEAD_DIM), lambda i, g, s: (i, 0, g))
    cmp_spec = pl.BlockSpec((1, n_cmp_pad, NSA_HEAD_DIM), lambda i, g, s: (i * NSA_KV_HEADS + g, 0, 0))
    return pl.pallas_call(
        functools.partial(_nsa_attn_kernel, n_sel=n_sel),
        grid=(b, NSA_KV_HEADS, t // NSA_TQ),
        in_specs=[pl.BlockSpec((1, NSA_TQ, gw), lambda i, g, s: (i, s, g)),
                  cmp_spec, cmp_spec, seq, seq, seq, seq,
                  pl.BlockSpec((1, NSA_TQ, n_gate), lambda i, g, s: (i, s, 0)),
                  pl.BlockSpec((LANE, n_cmp_pad), lambda i, g, s: (0, 0)),
                  pl.BlockSpec((1, n_gate, 3 * gw), lambda i, g, s: (g, 0, 0))],
        out_specs=pl.BlockSpec((1, NSA_TQ, gw), lambda i, g, s: (i, s, g)),
        out_shape=jax.ShapeDtypeStruct((b, t, NSA_Q_DIM), BF16),
        compiler_params=_cparams(3),
        name="nsa_attention",
    )(q, kcmp, vcmp, ks, vs, kw, vw, gate_proj, jnp.asarray(cov_t, BF16), jnp.asarray(eg, BF16))


_POOL_PAD = 2 * SUBLANE


def _pool_kernel(h_ref, buf_ref, w_ref, x_ref, gate_ref, o_ref, he_ref, wb_ref, *, tt, pos0):
    t = pl.program_id(1)

    @pl.when((pl.program_id(0) == 0) & (t == 0))
    def _():
        wb_ref[...] = w_ref[...].astype(BF16)

    @pl.when(t == 0)
    def _():
        he_ref[0:_POOL_PAD, :] = buf_ref[0]

    @pl.when(t > 0)
    def _():
        he_ref[0:_POOL_PAD, :] = he_ref[tt:tt + _POOL_PAD, :]

    he_ref[_POOL_PAD:_POOL_PAD + tt, :] = h_ref[0]
    gd = POOL_GROUP_DIM
    q_pos = pos0 + t * tt + lax.broadcasted_iota(jnp.int32, (tt, gd), 0)
    for gi, w in enumerate(POOL_WINDOWS):
        sl = slice(gi * gd, (gi + 1) * gd)
        cur = he_ref[_POOL_PAD:_POOL_PAD + tt, sl]
        win = cur
        for k in range(1, w):
            win = win + he_ref[_POOL_PAD - k:_POOL_PAD - k + tt, sl]
        count = jnp.minimum(w, q_pos + 1).astype(F32)
        mixed = (win / count - cur).astype(BF16)
        acc = jnp.dot(mixed, wb_ref[gi], preferred_element_type=F32)
        o_ref[0, :, sl] = x_ref[0, :, sl] + gate_ref[0, :, sl] * acc


def _pool_mixer(h, buf, pool_w, x, gate, pos0):
    b, t, d = h.shape
    tt = _pick_tile(t, 256, SUBLANE)
    buf16 = jnp.concatenate([jnp.zeros((b, _POOL_PAD - POOL_BUF, d), F32), buf], axis=1)
    if gate.shape[1] == 1:
        gate_spec = pl.BlockSpec((1, 1, d), lambda i, j: (i, 0, 0))
    else:
        gate_spec = pl.BlockSpec((1, tt, d), lambda i, j: (i, j, 0))
    tile = pl.BlockSpec((1, tt, d), lambda i, j: (i, j, 0))
    return pl.pallas_call(
        functools.partial(_pool_kernel, tt=tt, pos0=pos0),
        grid=(b, t // tt),
        in_specs=[tile, pl.BlockSpec((1, _POOL_PAD, d), lambda i, j: (i, 0, 0)),
                  pl.BlockSpec(pool_w.shape, lambda i, j: (0, 0, 0)), tile, gate_spec],
        out_specs=tile,
        out_shape=jax.ShapeDtypeStruct((b, t, d), F32),
        scratch_shapes=[pltpu.VMEM((tt + _POOL_PAD, d), F32), pltpu.VMEM(pool_w.shape, BF16)],
        compiler_params=_cparams(2),
        name="pool_mixer",
    )(h, buf16, pool_w, x, gate)


_CMP_PAGES_PER_STEP = 8


def _cmp_pages_kernel(tbl_ref, *refs):
    del tbl_ref
    npg = _CMP_PAGES_PER_STEP
    pages = refs[:npg]
    w_ref, o_ref, x_ref, wb_ref = refs[npg:]
    sub_per_page = PAGE_SIZE // CMP_STRIDE

    @pl.when((pl.program_id(0) == 0) & (pl.program_id(1) == 0))
    def _():
        wb_ref[...] = w_ref[...].astype(BF16)

    for k in range(npg):
        for g in range(NSA_KV_HEADS):
            r0 = (g * npg + k) * sub_per_page
            for s in range(CMP_STRIDE):
                x_ref[r0:r0 + sub_per_page, s * NSA_HEAD_DIM:(s + 1) * NSA_HEAD_DIM] = (
                    pages[k][0, pl.ds(s, sub_per_page, stride=CMP_STRIDE), g, :])
    ab = jnp.dot(x_ref[...].astype(BF16), wb_ref[...], preferred_element_type=F32)
    o_ref[0] = ab.reshape(NSA_KV_HEADS, npg * sub_per_page, ab.shape[-1])


def _cmp_pages(cache, layer, page_table, w1):
    b, n_pages = page_table.shape
    npg = _CMP_PAGES_PER_STEP
    assert n_pages % npg == 0
    ratio = CMP_LEN // CMP_STRIDE
    kdim = CMP_STRIDE * NSA_HEAD_DIM
    w1cat = w1.reshape(ratio, kdim, CMP_HIDDEN).transpose(1, 0, 2).reshape(kdim, ratio * CMP_HIDDEN)
    sub_per_page = PAGE_SIZE // CMP_STRIDE
    rows = NSA_KV_HEADS * npg * sub_per_page
    page_spec = lambda k: pl.BlockSpec((None, 1, PAGE_SIZE, NSA_KV_HEADS, NSA_HEAD_DIM),
                                       lambda i, p, tbl: (layer, tbl[i, p * npg + k], 0, 0, 0))
    out = pl.pallas_call(
        _cmp_pages_kernel,
        grid_spec=pltpu.PrefetchScalarGridSpec(
            num_scalar_prefetch=1,
            grid=(b, n_pages // npg),
            in_specs=[page_spec(k) for k in range(npg)]
                     + [pl.BlockSpec(w1cat.shape, lambda i, p, tbl: (0, 0))],
            out_specs=pl.BlockSpec((1, NSA_KV_HEADS, npg * sub_per_page, ratio * CMP_HIDDEN),
                                   lambda i, p, tbl: (i, 0, p, 0)),
            scratch_shapes=[pltpu.VMEM((rows, kdim), F32), pltpu.VMEM(w1cat.shape, BF16)]),
        out_shape=jax.ShapeDtypeStruct((b, NSA_KV_HEADS, n_pages * sub_per_page, ratio * CMP_HIDDEN), F32),
        compiler_params=_cparams(2),
        name="cmp_pages",
    )(page_table, *([cache] * npg), w1cat)
    return out.reshape(b * NSA_KV_HEADS, n_pages * sub_per_page, ratio * CMP_HIDDEN), w1cat


def _joint_softmax(s1, ok1, s2, ok2):
    s1 = jnp.where(ok1, s1, NEG_INF)
    s2 = jnp.where(ok2, s2, NEG_INF)
    m = jnp.maximum(jnp.max(s1, axis=1, keepdims=True), jnp.max(s2, axis=1, keepdims=True))
    p1 = jnp.where(ok1, jnp.exp(s1 - m), 0.0)
    p2 = jnp.where(ok2, jnp.exp(s2 - m), 0.0)
    den = jnp.maximum(jnp.sum(p1, axis=1, keepdims=True) + jnp.sum(p2, axis=1, keepdims=True), TINY)
    return p1 / den, p2 / den


def _dec_cmp_win_kernel(q_ref, kcmp_ref, vcmp_ref, kwin_ref, vwin_ref, kwn_ref, vwn_ref, cov_ref, rsum_ref,
                        ocmp_ref, owin_ref, imp_ref, *, t, n_cmp, pos0, nbuf):
    q = q_ref[0].astype(BF16)
    rows = q.shape[0]
    tok = lax.broadcasted_iota(jnp.int32, (rows, 1), 0) & (t - 1)
    q_pos = pos0 + tok

    n_pad = kcmp_ref.shape[1]
    s_c = _dot_nt(q, kcmp_ref[0])
    nidx = lax.broadcasted_iota(jnp.int32, (rows, n_pad), 1)
    ok = (nidx < n_cmp) & (nidx * CMP_STRIDE + (CMP_LEN - 1) <= q_pos)
    p_c = _masked_softmax_rows(s_c, ok)
    ocmp_ref[0] = jnp.dot(p_c.astype(BF16), vcmp_ref[0], preferred_element_type=F32)
    p_sum = _dot_exact(rsum_ref[...], p_c)
    imp_ref[0] = jnp.dot(p_sum.astype(BF16), cov_ref[...], preferred_element_type=F32)

    s1 = _dot_nt(q, kwin_ref[0].astype(BF16))
    kp1 = (pos0 - nbuf) + lax.broadcasted_iota(jnp.int32, (rows, nbuf), 1)
    d1 = q_pos - kp1
    kn = kwn_ref[0].astype(BF16)
    s2 = _dot_nt(q, kn)
    i2 = lax.broadcasted_iota(jnp.int32, (rows, kn.shape[0]), 1)
    d2 = tok - i2
    p1, p2 = _joint_softmax(s1, (d1 >= 0) & (d1 < WINDOW), s2, (d2 >= 0) & (i2 < t))
    owin_ref[0] = (jnp.dot(p1.astype(BF16), vwin_ref[0].astype(BF16), preferred_element_type=F32)
                   + jnp.dot(p2.astype(BF16), vwn_ref[0].astype(BF16), preferred_element_type=F32))


def _dec_cmp_win(q_stk, kcmp, vcmp, win_k, win_v, layer, kw_new, vw_new, n_cmp, n_sel, pos0, t):
    bg, rows, hd = q_stk.shape
    n_pad = kcmp.shape[1]
    nbuf = win_k.shape[2]
    sel_pad = -(-n_sel // LANE) * LANE
    cov = np.zeros((n_pad, sel_pad), np.float32)
    cov[:n_cmp, :n_sel] = _block_coverage(n_cmp, n_sel)
    rsum = np.zeros((SUBLANE, rows), np.float32)
    for r in range(NSA_REP):
        for tt in range(t):
            rsum[tt, r * t + tt] = 1.0
    per_bg = lambda n: pl.BlockSpec((1, n, hd), lambda i: (i, 0, 0))
    per_b = lambda n: pl.BlockSpec((1, n, hd), lambda i: (i // NSA_KV_HEADS, 0, i % NSA_KV_HEADS))
    win_spec = per_b(nbuf)
    win_k, win_v = [w[layer].reshape(w.shape[1], nbuf, NSA_KV_DIM) for w in (win_k, win_v)]
    return pl.pallas_call(
        functools.partial(_dec_cmp_win_kernel, t=t, n_cmp=n_cmp, pos0=pos0, nbuf=nbuf),
        grid=(bg,),
        in_specs=[per_bg(rows), per_bg(n_pad), per_bg(n_pad), win_spec, win_spec,
                  per_b(SUBLANE), per_b(SUBLANE),
                  pl.BlockSpec(cov.shape, lambda i: (0, 0)), pl.BlockSpec(rsum.shape, lambda i: (0, 0))],
        out_specs=[per_bg(rows), per_bg(rows), pl.BlockSpec((1, SUBLANE, sel_pad), lambda i: (i, 0, 0))],
        out_shape=[jax.ShapeDtypeStruct((bg, rows, hd), F32), jax.ShapeDtypeStruct((bg, rows, hd), F32),
                   jax.ShapeDtypeStruct((bg, SUBLANE, sel_pad), F32)],
        compiler_params=_cparams(1),
        name="dec_cmp_win",
    )(q_stk, kcmp, vcmp, win_k, win_v, kw_new, vw_new, jnp.asarray(cov, BF16), jnp.asarray(rsum))


def _dec_topk_kernel(imp_ref, o_ref, *, t, n_sel, pos0, top):
    imp_t = imp_ref[...].T
    shape = imp_t.shape
    jb = lax.broadcasted_iota(jnp.int32, shape, 0)
    q_pos = pos0 + (lax.broadcasted_iota(jnp.int32, shape, 1) & (t - 1))
    cur = q_pos >> (SEL_LEN.bit_length() - 1)
    forced = (jb == 0) | (jb == cur) | (jb == cur - 1)
    score = jnp.where(jb <= cur, imp_t + jnp.where(forced, FORCE_BONUS, 0.0), -1.0)
    score = jnp.where(jb < n_sel, score, -2.0)
    jbf = jb.astype(F32)
    for k in range(top):
        m = jnp.max(score, axis=0, keepdims=True)
        idx = jnp.min(jnp.where(score == m, jbf, float(shape[0])), axis=0, keepdims=True)
        o_ref[k:k + 1, :] = idx.astype(jnp.int32)
        score = jnp.where(jbf == idx, -3.0, score)


def _dec_topk(imp, n_sel, pos0, t):
    slots, sel_pad = imp.shape
    top = min(SEL_TOPK, n_sel)
    return pl.pallas_call(
        functools.partial(_dec_topk_kernel, t=t, n_sel=n_sel, pos0=pos0, top=top),
        grid=(1,),
        in_specs=[pl.BlockSpec((slots, sel_pad), lambda i: (0, 0))],
        out_specs=pl.BlockSpec((top, slots), lambda i: (0, 0)),
        out_shape=jax.ShapeDtypeStruct((top, slots), jnp.int32),
        compiler_params=_cparams(1),
        name="dec_topk",
    )(imp)


def _dec_slc_kernel(sel_ref, tbl_ref, *refs, t, n_past_blk, pos0, top):
    del tbl_ref
    kblk, vblk = refs[:top], refs[top:2 * top]
    q_ref, kn_ref, vn_ref, o_ref, kcat_ref, vcat_ref = refs[2 * top:]
    slot = pl.program_id(0)
    tok = slot & (t - 1)
    q_pos = pos0 + tok
    head = (slot // t) % NSA_KV_HEADS
    for g in range(NSA_KV_HEADS):
        @pl.when(head == g)
        def _(g=g):
            for k in range(top):
                kcat_ref[k * SEL_LEN:(k + 1) * SEL_LEN, :] = kblk[k][0, :, g, :].astype(BF16)
                vcat_ref[k * SEL_LEN:(k + 1) * SEL_LEN, :] = vblk[k][0, :, g, :].astype(BF16)
    q = q_ref[0].astype(BF16)
    rows = q.shape[0]
    n_keys = top * SEL_LEN
    s1 = _dot_nt(q, kcat_ref[...])
    col = lax.broadcasted_iota(jnp.int32, (rows, n_keys), 1)
    cblk = col >> (SEL_LEN.bit_length() - 1)
    kpos = col & (SEL_LEN - 1)
    in_past = cblk < 0
    new_lim = jnp.int32(-1)
    for k in range(top):
        blk = sel_ref[slot, k]
        kpos = kpos + jnp.where(cblk == k, blk * SEL_LEN, 0)
        in_past = in_past | (cblk == jnp.where(blk < n_past_blk, k, -1))
        new_lim = jnp.where(blk == n_past_blk, tok, new_lim)
    kn = kn_ref[0].astype(BF16)
    s2 = _dot_nt(q, kn)
    i2 = lax.broadcasted_iota(jnp.int32, (rows, kn.shape[0]), 1)
    ok2 = (i2 <= new_lim) & (i2 < t)
    p1, p2 = _joint_softmax(s1, in_past & (kpos <= q_pos), s2, ok2)
    o_ref[0] = (jnp.dot(p1.astype(BF16), vcat_ref[...], preferred_element_type=F32)
                + jnp.dot(p2.astype(BF16), vn_ref[0].astype(BF16), preferred_element_type=F32))


def _dec_slc(sel, page_table, cache_k, cache_v, layer, q_slot, ks_new, vs_new, pos0, t):
    slots, top = sel.shape
    hd = NSA_HEAD_DIM
    n_past_blk = page_table.shape[1] * PAGE_SIZE // SEL_LEN
    halves = PAGE_SIZE // SEL_LEN
    per_b = NSA_KV_HEADS * t

    def blk_spec(k):
        def index(s, sel_ref, tbl_ref):
            blk = jnp.minimum(sel_ref[s, k], n_past_blk - 1)
            page = tbl_ref[s // per_b, blk // halves]
            return layer, page, blk % halves, 0, 0
        return pl.BlockSpec((None, 1, SEL_LEN, NSA_KV_HEADS, hd), index)

    new_spec = pl.BlockSpec((1, SUBLANE, hd), lambda s, a, b_: (s // per_b, 0, (s // t) % NSA_KV_HEADS))
    slot_spec = pl.BlockSpec((1, SUBLANE, hd), lambda s, a, b_: (s, 0, 0))
    return pl.pallas_call(
        functools.partial(_dec_slc_kernel, t=t, n_past_blk=n_past_blk, pos0=pos0, top=top),
        grid_spec=pltpu.PrefetchScalarGridSpec(
            num_scalar_prefetch=2,
            grid=(slots,),
            in_specs=[blk_spec(k) for k in range(top)] + [blk_spec(k) for k in range(top)]
                     + [slot_spec, new_spec, new_spec],
            out_specs=slot_spec,
            scratch_shapes=[pltpu.VMEM((top * SEL_LEN, hd), BF16), pltpu.VMEM((top * SEL_LEN, hd), BF16)]),
        out_shape=jax.ShapeDtypeStruct((slots, SUBLANE, hd), F32),
        compiler_params=_cparams(1),
        name="dec_slc",
    )(sel, page_table, *([cache_k] * top), *([cache_v] * top), q_slot, ks_new, vs_new)


def _dec_combine_kernel(gp_ref, eg_ref, ocmp_ref, oslc_ref, owin_ref, o_ref):
    gexp = _spread(_sigmoid(gp_ref[...]), eg_ref[...])
    w = NSA_Q_DIM
    out = gexp[:, 0:w] * ocmp_ref[...] + gexp[:, w:2 * w] * oslc_ref[...] + gexp[:, 2 * w:3 * w] * owin_ref[...]
    o_ref[...] = out.astype(o_ref.dtype)


def _dec_combine(gate_proj, o_cmp, o_slc, o_win):
    rows, n_gate = gate_proj.shape
    eg = np.zeros((n_gate, 3 * NSA_Q_DIM), np.float32)
    for br in range(3):
        for h in range(NSA_HEADS):
            c0 = br * NSA_Q_DIM + h * NSA_HEAD_DIM
            eg[br * NSA_HEADS + h, c0:c0 + NSA_HEAD_DIM] = 1.0
    full = lambda a: pl.BlockSpec(a.shape, lambda i: (0, 0))
    args = (gate_proj, jnp.asarray(eg, BF16), o_cmp, o_slc, o_win)
    return pl.pallas_call(
        _dec_combine_kernel,
        grid=(1,),
        in_specs=[full(a) for a in args],
        out_specs=pl.BlockSpec((rows, NSA_Q_DIM), lambda i: (0, 0)),
        out_shape=jax.ShapeDtypeStruct((rows, NSA_Q_DIM), BF16),
        compiler_params=_cparams(1),
        name="dec_combine",
    )(*args)


def _nsa_decode(proj, gate_proj, pos0, P, j, caches, page_table):
    b, t, _ = proj.shape
    cache_ck, cache_cv, cache_sk, cache_sv, win_k, win_v = caches
    assert t & (t - 1) == 0 and t <= SUBLANE
    hd, g, rep = NSA_HEAD_DIM, NSA_KV_HEADS, NSA_REP
    pad_t = lambda a: jnp.pad(a, ((0, 0), (0, SUBLANE - t), (0, 0)))
    q, kc, ks, kw, _, _, _, _ = _rope_split(pad_t(proj), pos0 + jnp.arange(SUBLANE), lowp=F32)
    q, kc, ks, kw = q[:, :t], kc[:, :t], ks[:, :t], kw[:, :t]
    vc, vs, vw = [proj[..., NSA_Q_DIM + k * NSA_KV_DIM:NSA_Q_DIM + (k + 1) * NSA_KV_DIM] for k in (1, 3, 5)]

    tk = page_table.shape[1] * PAGE_SIZE + t
    n_cmp = (tk - CMP_LEN) // CMP_STRIDE + 1
    n_sel = -(-tk // SEL_LEN)
    assert (n_cmp + 1) * CMP_STRIDE <= page_table.shape[1] * PAGE_SIZE
    ab_k, w1k = _cmp_pages(cache_ck, j, page_table, P['nsa_cmpk_w1'][j])
    ab_v, w1v = _cmp_pages(cache_cv, j, page_table, P['nsa_cmpv_w1'][j])
    kcmp = _cmp_finish(ab_k, w1k, P['nsa_cmpk_w2'][j], P['nsa_cmpk_pe'][j])
    vcmp = _cmp_finish(ab_v, w1v, P['nsa_cmpv_w2'][j], P['nsa_cmpv_pe'][j])

    q5 = q.reshape(b, t, g, rep, hd)
    q_stk = q5.transpose(0, 2, 3, 1, 4).reshape(b * g, rep * t, hd)
    o_cmp, o_win, imp = _dec_cmp_win(q_stk, kcmp, vcmp, win_k, win_v, j, pad_t(kw), pad_t(vw),
                                     n_cmp, n_sel, pos0, t)
    sel = _dec_topk(imp[:, :t].reshape(b * g * t, -1), n_sel, pos0, t).T
    q_slot = q5.transpose(0, 2, 1, 3, 4).reshape(b * g * t, rep, hd)
    q_slot = jnp.pad(q_slot, ((0, 0), (0, SUBLANE - rep), (0, 0)))
    o_slc = _dec_slc(sel, page_table, cache_sk, cache_sv, j, q_slot, pad_t(ks), pad_t(vs), pos0, t)

    unstk = lambda o: o.reshape(b, g, rep, t, hd).transpose(0, 3, 1, 2, 4).reshape(b * t, NSA_Q_DIM)
    o_slc = o_slc[:, :rep].reshape(b, g, t, rep, hd).transpose(0, 2, 1, 3, 4).reshape(b * t, NSA_Q_DIM)
    o = _dec_combine(gate_proj.reshape(b * t, -1), unstk(o_cmp), o_slc, unstk(o_win))
    heads = lambda a: a.reshape(b, t, g, hd)
    kw_new = jnp.concatenate([win_k[j][:, t:], heads(kw)], axis=1)
    vw_new = jnp.concatenate([win_v[j][:, t:], heads(vw)], axis=1)
    return o.reshape(b, t, NSA_Q_DIM), kc, vc, ks, vs, kw_new, vw_new


def _run_trunk(x, mods, pos0, past, P):
    b, t, d = x.shape
    fresh = past is None
    per_batch = t % SUBLANE == 0 and t >= LANE
    names = ('ssd_state', 'ssd_conv', 'cmp_k', 'cmp_v', 'slc_k', 'slc_v', 'win_k', 'win_v', 'pool', 'ffn')
    new = {n: [] for n in names}

    bm, tm = (b, t) if per_batch else (1, b * t)

    def mod_rows(v):
        return v[:, None, :] if per_batch else jnp.repeat(v, t, axis=0)[None]

    def mm_rows(a, w, **kw):
        return _mm(a.reshape(b * t, a.shape[-1]), w, **kw)

    def pad_cols(w):
        return jnp.pad(w, ((0, 0), (0, LANE - w.shape[1])))

    def mm_res(a, w, layer, xres, gate):
        out = _mm(a.reshape(b * t, a.shape[-1]), w, layer=layer, res=xres.reshape(b * t, d), gate=gate,
                  rows_per_gate=t if per_batch else None)
        return out.reshape(b, t, d)

    for i in range(DEPTH):
        kind, j = i % N_MIXERS, i // N_MIXERS
        sh1, sc1, g1, sh2, sc2, g2 = [mod_rows(v) for v in jnp.split(mods[i], 6, axis=-1)]
        h_dtype = F32 if kind == 2 else BF16
        h = _normmod(x.reshape(bm, tm, d), P['norm1_g'][i], sh1, sc1, h_dtype).reshape(b, t, d)
        if kind == 0:
            if fresh:
                conv_buf = jnp.zeros((b, SSD_CONV - 1, SSD_CONV_DIM), F32)
                s0 = jnp.zeros((b, SSD_HEADS, SSD_HEAD_DIM, SSD_STATE), F32)
            else:
                conv_buf, s0 = past['ssd_conv'][j], past['ssd_state'][j]
            w_in = P['ssd_w_in']
            nzx = SSD_D_INNER + SSD_CONV_DIM
            dt_raw = mm_rows(h, pad_cols(w_in[j, :, nzx:])).reshape(b, t, LANE)[..., :SSD_HEADS]
            if per_batch:
                zx = mm_rows(h, w_in, layer=j, n_cols=SSD_D_INNER).reshape(b, t, SSD_D_INNER)
                xbc, conv_new = _mm_conv_silu(h.reshape(b * t, d), w_in, j, SSD_D_INNER, SSD_CONV_DIM, t,
                                              conv_buf, P['ssd_conv_w'][j], P['ssd_conv_b'][j])
                xbc = xbc.reshape(b, t, SSD_CONV_DIM)
            else:
                zx = mm_rows(h, w_in, layer=j, n_cols=nzx).reshape(b, t, nzx)
                xbc, conv_new = _conv_silu(zx, SSD_D_INNER, SSD_CONV_DIM, conv_buf,
                                           P['ssd_conv_w'][j], P['ssd_conv_b'][j])
            tp = -(-t // SSD_CHUNK) * SSD_CHUNK
            if tp != t:
                xbc_p = jnp.pad(xbc, ((0, 0), (0, tp - t), (0, 0)))
                dt_p = jnp.pad(dt_raw, ((0, 0), (0, tp - t), (0, 0)))
            else:
                xbc_p, dt_p = xbc, dt_raw
            y, s_new = _ssd_scan(xbc_p, dt_p, P['ssd_dt_bias'][j], P['ssd_a_log'][j], P['ssd_d'][j], s0,
                                 t_valid=min(t, SSD_CHUNK))
            yn = _gated_norm(y[:, :t], zx, P['ssd_norm_g'][j])
            x = mm_res(yn, P['ssd_w_out'], j, x, g1)
            new['ssd_conv'].append(conv_new)
            new['ssd_state'].append(s_new)
        elif kind == 1:
            w_in = P['nsa_w_in']
            npj = NSA_Q_DIM + 6 * NSA_KV_DIM
            proj = mm_rows(h, w_in, layer=j, n_cols=npj).reshape(b, t, npj)
            gate_proj = mm_rows(h, pad_cols(w_in[j, :, npj:])).reshape(b, t, LANE)
            if fresh:
                q_pos = pos0 + jnp.arange(t)
                q, kc, ks, kw, ks_b, vs_b, kw_b, vw_b = _rope_split(proj, q_pos)
                vc, vs, vw = [proj[..., NSA_Q_DIM + k * NSA_KV_DIM:NSA_Q_DIM + (k + 1) * NSA_KV_DIM]
                              for k in (1, 3, 5)]
                kcmp = _compress(kc, P['nsa_cmpk_w1'][j], P['nsa_cmpk_w2'][j], P['nsa_cmpk_pe'][j])
                vcmp = _compress(vc, P['nsa_cmpv_w1'][j], P['nsa_cmpv_w2'][j], P['nsa_cmpv_pe'][j])
                o = _nsa_attention(q, kcmp, vcmp, ks_b, vs_b, kw_b, vw_b, gate_proj)
                keep = min(WINDOW, t)
                kw_new, vw_new = kw[:, t - keep:], vw[:, t - keep:]
            else:
                o, kc, vc, ks, vs, kw_new, vw_new = _nsa_decode(
                    proj, gate_proj, pos0, P, j, past['nsa'][j], past['page_table'])
            x = mm_res(o, P['nsa_w_out'], j, x, g1)
            shp = (b, t, NSA_KV_HEADS, NSA_HEAD_DIM)
            for n, v in (('cmp_k', kc), ('cmp_v', vc), ('slc_k', ks), ('slc_v', vs)):
                new[n].append(v.reshape(shp))
            new['win_k'].append(kw_new.reshape(b, -1, NSA_KV_HEADS, NSA_HEAD_DIM))
            new['win_v'].append(vw_new.reshape(b, -1, NSA_KV_HEADS, NSA_HEAD_DIM))
        else:
            buf = jnp.zeros((b, POOL_BUF, d), F32) if fresh else past['pool'][j]
            gate = (g1 * P['pool_scale'][j]).reshape(b, -1, d)
            x = _pool_mixer(h, buf, P['pool_w'][j], x, gate, pos0)
            new['pool'].append(jnp.concatenate([buf, h], axis=1)[:, -POOL_BUF:])
        h2 = _normmod(x.reshape(bm, tm, d), P['norm2_g'][i], sh2, sc2, BF16)
        fbuf = jnp.zeros((b, FFN_CONV - 1, 2 * D_FF), F32) if fresh else past['ffn'][i]
        if per_batch:
            act, fbuf_new = _mm_conv_glu(h2.reshape(b * t, d), P['ffn_w_up'], i, t, fbuf,
                                         P['ffn_conv_w'][i], P['ffn_conv_b'][i])
        else:
            u = mm_rows(h2, P['ffn_w_up'], layer=i).reshape(b, t, 2 * D_FF)
            act, fbuf_new = _conv_glu(u, fbuf, P['ffn_conv_w'][i], P['ffn_conv_b'][i])
        x = mm_res(act, P['ffn_w_down'], i, x, g2)
        new['ffn'].append(fbuf_new)
    zero = jnp.zeros((bm, 1, d), F32)
    y = _normmod(x.reshape(bm, tm, d), P['final_g'], zero, zero, F32).reshape(b, t, d)
    return y, {n: jnp.stack(v) for n, v in new.items()}


def kernel(x_prompt, x_sample, state_ssd, state_ssd_conv, cache_cmp_k, cache_cmp_v, cache_slc_k, cache_slc_v,
           cache_win_k, cache_win_v, state_pool, state_ffn_conv, page_table, c_prompt, c_sample,
           ada_w, ada_b, norm1_g, norm2_g, final_g,
           ssd_w_in, ssd_conv_w, ssd_conv_b, ssd_dt_bias, ssd_a_log, ssd_d, ssd_norm_g, ssd_w_out,
           nsa_w_in, nsa_cmpk_w1, nsa_cmpk_w2, nsa_cmpk_pe, nsa_cmpv_w1, nsa_cmpv_w2, nsa_cmpv_pe, nsa_w_out,
           pool_w, pool_scale, ffn_w_up, ffn_conv_w, ffn_conv_b, ffn_w_down):
    P = dict(norm1_g=norm1_g, norm2_g=norm2_g, final_g=final_g,
             ssd_w_in=ssd_w_in, ssd_conv_w=ssd_conv_w, ssd_conv_b=ssd_conv_b, ssd_dt_bias=ssd_dt_bias,
             ssd_a_log=ssd_a_log, ssd_d=ssd_d, ssd_norm_g=ssd_norm_g, ssd_w_out=ssd_w_out,
             nsa_w_in=nsa_w_in, nsa_cmpk_w1=nsa_cmpk_w1, nsa_cmpk_w2=nsa_cmpk_w2, nsa_cmpk_pe=nsa_cmpk_pe,
             nsa_cmpv_w1=nsa_cmpv_w1, nsa_cmpv_w2=nsa_cmpv_w2, nsa_cmpv_pe=nsa_cmpv_pe, nsa_w_out=nsa_w_out,
             pool_w=pool_w, pool_scale=pool_scale,
             ffn_w_up=ffn_w_up, ffn_conv_w=ffn_conv_w, ffn_conv_b=ffn_conv_b, ffn_w_down=ffn_w_down)
    nbp, nbs = c_prompt.shape[0], c_sample.shape[0]
    c_rows = -(-(nbp + nbs) // (2 * SUBLANE)) * (2 * SUBLANE)
    c_all = jnp.concatenate([c_prompt, c_sample, jnp.zeros((c_rows - nbp - nbs, D_MODEL), F32)], axis=0)
    mods = [_mm(c_all, ada_w, layer=i, a_silu=True, bias=ada_b[i]) for i in range(DEPTH)]
    mods_p = [m[:nbp] for m in mods]
    mods_s = [m[nbp:nbp + nbs] for m in mods]

    y_prompt, sp = _run_trunk(x_prompt, mods_p, 0, None, P)
    past_len = page_table.shape[1] * PAGE_SIZE
    nsa_past = [(cache_cmp_k, cache_cmp_v, cache_slc_k, cache_slc_v, cache_win_k, cache_win_v)
                for _ in range(cache_cmp_k.shape[0])]
    past = dict(ssd_state=state_ssd, ssd_conv=state_ssd_conv, nsa=nsa_past, page_table=page_table,
                pool=state_pool, ffn=state_ffn_conv)
    y_sample, ss = _run_trunk(x_sample, mods_s, past_len, past, P)
    return (y_prompt, y_sample,
            sp['ssd_state'], ss['ssd_state'], sp['ssd_conv'], ss['ssd_conv'],
            sp['cmp_k'], ss['cmp_k'], sp['cmp_v'], ss['cmp_v'],
            sp['slc_k'], ss['slc_k'], sp['slc_v'], ss['slc_v'],
            sp['win_k'], ss['win_k'], sp['win_v'], ss['win_v'],
            sp['pool'], ss['pool'], sp['ffn'], ss['ffn'])
```

```python
import functools
import math

import jax
import jax.numpy as jnp
import numpy as np
from jax import lax
from jax.experimental import pallas as pl
from jax.experimental.pallas import tpu as pltpu

D_MODEL = 2048
DEPTH = 4
PAGE_SIZE = 128
N_MIXERS = 3

SSD_D_INNER = 2 * D_MODEL
SSD_HEAD_DIM = 64
SSD_HEADS = SSD_D_INNER // SSD_HEAD_DIM
SSD_GROUPS = 8
SSD_REP = SSD_HEADS // SSD_GROUPS
SSD_STATE = 128
SSD_CONV = 4
SSD_CHUNK = 128
SSD_GROUP_DIM = SSD_REP * SSD_HEAD_DIM
SSD_CONV_DIM = SSD_D_INNER + 2 * SSD_GROUPS * SSD_STATE

NSA_HEADS = 16
NSA_KV_HEADS = 4
NSA_HEAD_DIM = D_MODEL // NSA_HEADS
NSA_REP = NSA_HEADS // NSA_KV_HEADS
NSA_Q_DIM = NSA_HEADS * NSA_HEAD_DIM
NSA_KV_DIM = NSA_KV_HEADS * NSA_HEAD_DIM
CMP_LEN = 32
CMP_STRIDE = 16
CMP_HIDDEN = 256
SEL_LEN = 64
SEL_TOPK = 16
WINDOW = 512
WIN_Q_BLOCK = 128
SEL_Q_BLOCK = 64
ROPE_THETA = 10000.0
FORCE_BONUS = 1000.0

POOL_WINDOWS = (2, 4, 8, 16)
POOL_GROUPS = len(POOL_WINDOWS)
POOL_GROUP_DIM = D_MODEL // POOL_GROUPS
POOL_BUF = max(POOL_WINDOWS) - 1

D_FF = 5632
FFN_CONV = 3

EPS = 1e-6
NEG_INF = -1e30
TINY = 1e-30

V7X_VMEM_LIMIT_BYTES = 52 * 1024 * 1024
LANE = 128
SUBLANE = 8

F32 = jnp.float32
BF16 = jnp.bfloat16
HIGHEST = lax.Precision.HIGHEST

NSA_TQ = 256
NSA_SLC_KC = 512
NSA_WIN_SPAN = WINDOW + NSA_TQ


def _cparams(n_axes):
    return pltpu.CompilerParams(dimension_semantics=("arbitrary",) * n_axes,
                                vmem_limit_bytes=V7X_VMEM_LIMIT_BYTES)


def _pick_tile(dim, pref, align):
    t = min(pref, dim)
    t -= t % align
    while t >= align:
        if dim % t == 0:
            return t
        t -= align
    return dim


def _sigmoid(x):
    return 1.0 / (1.0 + jnp.exp(-x))


def _silu(x):
    return x * _sigmoid(x)


def _dot_nt(a, b):
    return lax.dot_general(a, b, (((1,), (1,)), ((), ())), preferred_element_type=F32)


def _dot_tn(a, b):
    return lax.dot_general(a, b, (((0,), (0,)), ((), ())), preferred_element_type=F32)


def _dot_exact(a, b):
    return jnp.dot(a, b, precision=HIGHEST, preferred_element_type=F32)


def _split3(x):
    hi = x.astype(BF16)
    r = x - hi.astype(F32)
    mid = r.astype(BF16)
    lo = (r - mid.astype(F32)).astype(BF16)
    return hi, mid, lo


def _spread(x, e):
    hi, mid, lo = _split3(x)
    dot = lambda p: jnp.dot(p, e, preferred_element_type=F32)
    return dot(hi) + dot(mid) + dot(lo)


def _spread_rows(e, x):
    hi, mid, lo = _split3(x)
    dot = lambda p: jnp.dot(e, p, preferred_element_type=F32)
    return dot(hi) + dot(mid) + dot(lo)


def _normmod_kernel(x_ref, g_ref, sh_ref, sc_ref, o_ref):
    x = x_ref[0].astype(F32)
    y = x * lax.rsqrt(jnp.mean(x * x, axis=-1, keepdims=True) + EPS) * g_ref[...]
    y = y * (1.0 + sc_ref[0]) + sh_ref[0]
    o_ref[0] = y.astype(o_ref.dtype)


def _normmod(x, g, shift, scale, out_dtype):
    b, t, d = x.shape
    r = shift.shape[1]
    tt = _pick_tile(t, 512, SUBLANE)
    if r == 1:
        mod_spec = pl.BlockSpec((1, 1, d), lambda i, j: (i, 0, 0))
    else:
        mod_spec = pl.BlockSpec((1, tt, d), lambda i, j: (i, j, 0))
    return pl.pallas_call(
        _normmod_kernel,
        grid=(b, t // tt),
        in_specs=[pl.BlockSpec((1, tt, d), lambda i, j: (i, j, 0)),
                  pl.BlockSpec((1, d), lambda i, j: (0, 0)),
                  mod_spec, mod_spec],
        out_specs=pl.BlockSpec((1, tt, d), lambda i, j: (i, j, 0)),
        out_shape=jax.ShapeDtypeStruct((b, t, d), out_dtype),
        compiler_params=_cparams(2),
        name="normmod",
    )(x, g.reshape(1, d), shift, scale)


def _mm_kernel(*refs, a_silu, has_bias, has_res):
    it = iter(refs)
    a_ref, w_ref = next(it), next(it)
    bias_ref = next(it) if has_bias else None
    res_ref = next(it) if has_res else None
    gate_ref = next(it) if has_res else None
    o_ref, wb_ref = next(it), next(it)

    @pl.when(pl.program_id(1) == 0)
    def _():
        wb_ref[...] = w_ref[...].astype(BF16)

    a = a_ref[...]
    if a_silu:
        a = _silu(a.astype(F32))
    acc = jnp.dot(a.astype(BF16), wb_ref[...], preferred_element_type=F32)
    if has_bias:
        acc = acc + bias_ref[...]
    if has_res:
        acc = res_ref[...] + gate_ref[0] * acc
    o_ref[...] = acc.astype(o_ref.dtype)


_MM_VMEM_BUDGET = 46 * 1024 * 1024


def _w_spec(w, layer, k, tn, col_block):
    if w.ndim == 2:
        return pl.BlockSpec((k, tn), lambda j, i: (0, col_block(j)))
    return pl.BlockSpec((None, k, tn), lambda j, i: (layer, 0, col_block(j)))


def _mm(a, w, *, layer=None, n_cols=None, a_silu=False, bias=None, res=None, gate=None, rows_per_gate=None,
        out_dtype=F32, tm=2048, tn=1024):
    m, k = a.shape
    assert w.shape[-2] == k
    n = w.shape[-1] if n_cols is None else n_cols
    tm = _pick_tile(m if rows_per_gate is None else rows_per_gate, tm, SUBLANE)
    tn = _pick_tile(n, tn, LANE)
    while k * tn * 10 > 30 * 1024 * 1024 and tn % (2 * LANE) == 0:
        tn //= 2
    while 2 * k * tm * a.dtype.itemsize > 16 * 1024 * 1024 and tm % (2 * SUBLANE) == 0:
        tm //= 2

    def vmem_estimate():
        io_tiles = tm * tn * (jnp.dtype(out_dtype).itemsize + (4 if res is not None else 0))
        return 2 * tm * k * a.dtype.itemsize + 2 * k * tn * 4 + k * tn * 2 + 2 * io_tiles

    while vmem_estimate() > _MM_VMEM_BUDGET and tn % (2 * LANE) == 0 and tn > 2 * LANE:
        tn //= 2
    while vmem_estimate() > _MM_VMEM_BUDGET and tm % (2 * SUBLANE) == 0:
        tm //= 2
    assert m % tm == 0 and n % tn == 0
    in_specs = [pl.BlockSpec((tm, k), lambda j, i: (i, 0)),
                _w_spec(w, layer, k, tn, lambda j: j)]
    args = [a, w]
    if bias is not None:
        in_specs.append(pl.BlockSpec((1, tn), lambda j, i: (0, j)))
        args.append(bias.reshape(1, -1))
    if res is not None:
        in_specs.append(pl.BlockSpec((tm, tn), lambda j, i: (i, j)))
        args.append(res)
        if gate.shape[1] == 1:
            tiles_per_gate = rows_per_gate // tm
            in_specs.append(pl.BlockSpec((1, 1, tn), lambda j, i: (i // tiles_per_gate, 0, j)))
        else:
            in_specs.append(pl.BlockSpec((1, tm, tn), lambda j, i: (0, i, j)))
        args.append(gate)
    return pl.pallas_call(
        functools.partial(_mm_kernel, a_silu=a_silu, has_bias=bias is not None, has_res=res is not None),
        grid=(n // tn, m // tm),
        in_specs=in_specs,
        out_specs=pl.BlockSpec((tm, tn), lambda j, i: (i, j)),
        out_shape=jax.ShapeDtypeStruct((m, n), out_dtype),
        scratch_shapes=[pltpu.VMEM((k, tn), BF16)],
        compiler_params=_cparams(2),
        name="mm",
    )(*args)


_CONV_PAD = SUBLANE


def _conv_tile(x, first, buf_ref, w_ref, b_ref, st_ref, xe_ref, width, tt):
    lo = _CONV_PAD - (width - 1)

    @pl.when(first)
    def _():
        xe_ref[lo:_CONV_PAD, :] = buf_ref[0]

    @pl.when(jnp.logical_not(first))
    def _():
        xe_ref[0:_CONV_PAD, :] = xe_ref[tt:tt + _CONV_PAD, :]

    xe_ref[_CONV_PAD:_CONV_PAD + tt, :] = x
    acc = b_ref[...] + w_ref[0:1, :] * xe_ref[lo:lo + tt, :]
    for k in range(1, width):
        acc = acc + w_ref[k:k + 1, :] * xe_ref[lo + k:lo + k + tt, :]
    st_ref[0] = xe_ref[tt + lo:tt + _CONV_PAD, :]
    return acc


def _conv_silu_kernel(x_ref, buf_ref, w_ref, b_ref, y_ref, st_ref, xe_ref, *, width, tt):
    first = pl.program_id(2) == 0
    y_ref[0] = _silu(_conv_tile(x_ref[0], first, buf_ref, w_ref, b_ref, st_ref, xe_ref, width, tt))


def _conv_glu_kernel(xa_ref, xg_ref, bufa_ref, bufg_ref, wa_ref, wg_ref, ba_ref, bg_ref,
                     act_ref, sta_ref, stg_ref, xea_ref, xeg_ref, *, width, tt):
    first = pl.program_id(2) == 0
    a = _conv_tile(xa_ref[0], first, bufa_ref, wa_ref, ba_ref, sta_ref, xea_ref, width, tt)
    g = _conv_tile(xg_ref[0], first, bufg_ref, wg_ref, bg_ref, stg_ref, xeg_ref, width, tt)
    act_ref[0] = (_silu(g) * a).astype(act_ref.dtype)


def _mm_conv_silu_kernel(a_ref, w_ref, buf_ref, cw_ref, cb_ref, y_ref, st_ref, wb_ref, xe_ref,
                         *, width, tm, tiles_per_seq):
    i = pl.program_id(1)

    @pl.when(i == 0)
    def _():
        wb_ref[...] = w_ref[...].astype(BF16)

    u = jnp.dot(a_ref[...], wb_ref[...], preferred_element_type=F32)
    first = i % tiles_per_seq == 0
    y_ref[...] = _silu(_conv_tile(u, first, buf_ref, cw_ref, cb_ref, st_ref, xe_ref, width, tm))


def _mm_conv_glu_kernel(a_ref, wa_ref, wg_ref, bufa_ref, bufg_ref, cwa_ref, cwg_ref, cba_ref, cbg_ref,
                        act_ref, sta_ref, stg_ref, wba_ref, wbg_ref, xea_ref, xeg_ref,
                        *, width, tm, tiles_per_seq):
    i = pl.program_id(1)

    @pl.when(i == 0)
    def _():
        wba_ref[...] = wa_ref[...].astype(BF16)
        wbg_ref[...] = wg_ref[...].astype(BF16)

    a = a_ref[...]
    first = i % tiles_per_seq == 0
    ua = jnp.dot(a, wba_ref[...], preferred_element_type=F32)
    ca = _conv_tile(ua, first, bufa_ref, cwa_ref, cba_ref, sta_ref, xea_ref, width, tm)
    ug = jnp.dot(a, wbg_ref[...], preferred_element_type=F32)
    cg = _conv_tile(ug, first, bufg_ref, cwg_ref, cbg_ref, stg_ref, xeg_ref, width, tm)
    act_ref[...] = (_silu(cg) * ca).astype(act_ref.dtype)


def _mm_conv_silu(a, w, layer, col0, n_ch, seq_len, buf, cw, cbias, *, tm=2048, tn=512):
    m, k = a.shape
    width = cw.shape[0]
    nb = m // seq_len
    tm = _pick_tile(seq_len, tm, 2 * SUBLANE)
    tn = _pick_tile(n_ch, tn, LANE)
    assert col0 % tn == 0
    off = col0 // tn
    tps = seq_len // tm
    return pl.pallas_call(
        functools.partial(_mm_conv_silu_kernel, width=width, tm=tm, tiles_per_seq=tps),
        grid=(n_ch // tn, m // tm),
        in_specs=[pl.BlockSpec((tm, k), lambda j, i: (i, 0)),
                  _w_spec(w, layer, k, tn, lambda j: j + off),
                  pl.BlockSpec((1, width - 1, tn), lambda j, i: (i // tps, 0, j)),
                  pl.BlockSpec((width, tn), lambda j, i: (0, j)),
                  pl.BlockSpec((1, tn), lambda j, i: (0, j))],
        out_specs=[pl.BlockSpec((tm, tn), lambda j, i: (i, j)),
                   pl.BlockSpec((1, width - 1, tn), lambda j, i: (i // tps, 0, j))],
        out_shape=[jax.ShapeDtypeStruct((m, n_ch), F32), jax.ShapeDtypeStruct((nb, width - 1, n_ch), F32)],
        scratch_shapes=[pltpu.VMEM((k, tn), BF16), pltpu.VMEM((tm + _CONV_PAD, tn), F32)],
        compiler_params=_cparams(2),
        name="mm_conv_silu",
    )(a, w, buf, cw, cbias.reshape(1, -1))


def _mm_conv_glu(a, w, layer, seq_len, buf, cw, cbias, *, tm=2048, tn=256):
    m, k = a.shape
    half = w.shape[-1] // 2
    width = cw.shape[0]
    nb = m // seq_len
    tm = _pick_tile(seq_len, tm, 2 * SUBLANE)
    tn = _pick_tile(half, tn, LANE)
    hb = half // tn
    tps = seq_len // tm
    cb2 = cbias.reshape(1, -1)
    lo_hi = lambda shape, index: [pl.BlockSpec(shape, lambda j, i: index(j, i)),
                                  pl.BlockSpec(shape, lambda j, i: index(j + hb, i))]
    st_spec = pl.BlockSpec((1, width - 1, tn), lambda j, i: (i // tps, 0, j))
    act, sta, stg = pl.pallas_call(
        functools.partial(_mm_conv_glu_kernel, width=width, tm=tm, tiles_per_seq=tps),
        grid=(hb, m // tm),
        in_specs=[pl.BlockSpec((tm, k), lambda j, i: (i, 0)),
                  _w_spec(w, layer, k, tn, lambda j: j), _w_spec(w, layer, k, tn, lambda j: j + hb)]
                 + lo_hi((1, width - 1, tn), lambda j, i: (i // tps, 0, j))
                 + lo_hi((width, tn), lambda j, i: (0, j))
                 + lo_hi((1, tn), lambda j, i: (0, j)),
        out_specs=[pl.BlockSpec((tm, tn), lambda j, i: (i, j)), st_spec, st_spec],
        out_shape=[jax.ShapeDtypeStruct((m, half), BF16),
                   jax.ShapeDtypeStruct((nb, width - 1, half), F32),
                   jax.ShapeDtypeStruct((nb, width - 1, half), F32)],
        scratch_shapes=[pltpu.VMEM((k, tn), BF16), pltpu.VMEM((k, tn), BF16),
                        pltpu.VMEM((tm + _CONV_PAD, tn), F32), pltpu.VMEM((tm + _CONV_PAD, tn), F32)],
        compiler_params=_cparams(2),
        name="mm_conv_glu",
    )(a, w, w, buf, buf, cw, cw, cb2, cb2)
    return act, jnp.concatenate([sta, stg], axis=-1)


def _conv_silu(x, col0, n_ch, buf, w, bias, *, tc=512, tt=512):
    b, t, _ = x.shape
    width = w.shape[0]
    tt = _pick_tile(t, tt, SUBLANE)
    off = col0 // tc
    return pl.pallas_call(
        functools.partial(_conv_silu_kernel, width=width, tt=tt),
        grid=(b, n_ch // tc, t // tt),
        in_specs=[pl.BlockSpec((1, tt, tc), lambda i, j, s: (i, s, j + off)),
                  pl.BlockSpec((1, width - 1, tc), lambda i, j, s: (i, 0, j)),
                  pl.BlockSpec((width, tc), lambda i, j, s: (0, j)),
                  pl.BlockSpec((1, tc), lambda i, j, s: (0, j))],
        out_specs=[pl.BlockSpec((1, tt, tc), lambda i, j, s: (i, s, j)),
                   pl.BlockSpec((1, width - 1, tc), lambda i, j, s: (i, 0, j))],
        out_shape=[jax.ShapeDtypeStruct((b, t, n_ch), F32),
                   jax.ShapeDtypeStruct((b, width - 1, n_ch), F32)],
        scratch_shapes=[pltpu.VMEM((tt + _CONV_PAD, tc), F32)],
        compiler_params=_cparams(3),
        name="conv_silu",
    )(x, buf, w, bias.reshape(1, -1))


def _conv_glu(u, buf, w, bias, *, tc=512, tt=512):
    b, t, c2 = u.shape
    half = c2 // 2
    width = w.shape[0]
    tt = _pick_tile(t, tt, SUBLANE)
    hb = half // tc
    xa = pl.BlockSpec((1, tt, tc), lambda i, j, s: (i, s, j))
    xg = pl.BlockSpec((1, tt, tc), lambda i, j, s: (i, s, j + hb))
    ba = pl.BlockSpec((1, width - 1, tc), lambda i, j, s: (i, 0, j))
    bg = pl.BlockSpec((1, width - 1, tc), lambda i, j, s: (i, 0, j + hb))
    wa = pl.BlockSpec((width, tc), lambda i, j, s: (0, j))
    wg = pl.BlockSpec((width, tc), lambda i, j, s: (0, j + hb))
    ca = pl.BlockSpec((1, tc), lambda i, j, s: (0, j))
    cg = pl.BlockSpec((1, tc), lambda i, j, s: (0, j + hb))
    st = pl.BlockSpec((1, width - 1, tc), lambda i, j, s: (i, 0, j))
    bias2 = bias.reshape(1, -1)
    act, sta, stg = pl.pallas_call(
        functools.partial(_conv_glu_kernel, width=width, tt=tt),
        grid=(b, hb, t // tt),
        in_specs=[xa, xg, ba, bg, wa, wg, ca, cg],
        out_specs=[pl.BlockSpec((1, tt, tc), lambda i, j, s: (i, s, j)), st, st],
        out_shape=[jax.ShapeDtypeStruct((b, t, half), BF16),
                   jax.ShapeDtypeStruct((b, width - 1, half), F32),
                   jax.ShapeDtypeStruct((b, width - 1, half), F32)],
        scratch_shapes=[pltpu.VMEM((tt + _CONV_PAD, tc), F32), pltpu.VMEM((tt + _CONV_PAD, tc), F32)],
        compiler_params=_cparams(3),
        name="conv_glu",
    )(u, u, buf, buf, w, w, bias2, bias2)
    return act, jnp.concatenate([sta, stg], axis=-1)


def _softplus(x):
    return jnp.maximum(x, 0.0) + jnp.log1p(jnp.exp(-jnp.abs(x)))


def _ssd_kernel(x_ref, b_ref, c_ref, dtc_ref, dtr_ref, biasc_ref, biasr_ref, alogc_ref, alogr_ref,
                dch_ref, e512_ref, e1024_ref, et_ref, s0_ref, y_ref, sf_ref, s_ref, *, t_valid):
    L = SSD_CHUNK
    c = pl.program_id(2)

    @pl.when(c == 0)
    def _():
        s_ref[...] = s0_ref[0].reshape(SSD_GROUP_DIM, SSD_STATE)

    x = x_ref[0]
    bm = b_ref[0].astype(BF16)
    cm = c_ref[0].astype(BF16)
    row = lax.broadcasted_iota(jnp.int32, (L, L), 0)
    col = lax.broadcasted_iota(jnp.int32, (L, L), 1)
    causal = row >= col

    dt_c = _softplus(dtc_ref[0, 0] + biasc_ref[0])
    dt_r = _softplus(dtr_ref[0] + biasr_ref[0])
    if t_valid < L:
        dt_c = jnp.where(lax.broadcasted_iota(jnp.int32, dt_c.shape, 0) < t_valid, dt_c, 0.0)
        dt_r = jnp.where(lax.broadcasted_iota(jnp.int32, dt_r.shape, 1) < t_valid, dt_r, 0.0)
    dta_c = dt_c * (-jnp.exp(alogc_ref[0]))
    dta_r = dt_r * (-jnp.exp(alogr_ref[0]))

    tri = jnp.where(causal, 1.0, 0.0).astype(F32)
    upper = jnp.where(row <= col, 1.0, 0.0).astype(F32)
    acs_c = _dot_exact(tri, dta_c)
    acs_r = _dot_exact(dta_r, upper)
    e512 = e512_ref[...]
    acs_x = _spread(acs_c, e512)
    dt_x = _spread(dt_c, e512)
    acs_full = _spread(acs_c, e1024_ref[...])
    alast_x = acs_x[L - 1:L, :]

    xdt = x * dt_x
    xdt_b = xdt.astype(BF16)
    cb = _dot_nt(cm, bm)
    lane = lax.broadcasted_iota(jnp.int32, (L, LANE), 1)
    pairs = []
    for pr in range(SSD_REP // 2):
        xp = xdt_b[:, pr * LANE:(pr + 1) * LANE]
        halves = []
        for r in (2 * pr, 2 * pr + 1):
            seg = acs_full[:, r * LANE:(r + 1) * LANE] - acs_r[r:r + 1, :]
            lm = jnp.exp(jnp.where(causal, seg, NEG_INF))
            halves.append(jnp.dot((cb * lm).astype(BF16), xp, preferred_element_type=F32))
        pairs.append(jnp.where(lane < SSD_HEAD_DIM, halves[0], halves[1]))
    y_diag = jnp.concatenate(pairs, axis=1)

    s_in = s_ref[...]
    y_off = _dot_nt(cm, s_in.astype(BF16)) * jnp.exp(acs_x)
    y_ref[0] = y_diag + y_off + x * dch_ref[...]

    xw = (xdt * jnp.exp(alast_x - acs_x)).astype(BF16)
    alast_r = jnp.sum(dta_r, axis=1, keepdims=True)
    dec = jnp.exp(_spread_rows(et_ref[...], jnp.broadcast_to(alast_r, (SSD_REP, SSD_STATE))))
    s_new = s_in * dec + _dot_tn(xw, bm)
    s_ref[...] = s_new
    sf_ref[0] = s_new.reshape(SSD_REP, SSD_HEAD_DIM, SSD_STATE)


def _ssd_expanders():
    e512 = np.repeat(np.eye(SSD_REP, dtype=np.float32), SSD_HEAD_DIM, axis=1)
    e1024 = np.repeat(np.eye(SSD_REP, dtype=np.float32), LANE, axis=1)
    return jnp.asarray(e512, BF16), jnp.asarray(e1024, BF16), jnp.asarray(e512.T.copy(), BF16)


def _ssd_scan(xbc, dt_raw, dt_bias, a_log, d_skip, s0, t_valid):
    b, t, _ = xbc.shape
    L = SSD_CHUNK
    assert t % L == 0
    g, rep = SSD_GROUPS, SSD_REP
    dtc = dt_raw.reshape(b, t, g, rep).transpose(0, 2, 1, 3)
    dtr = dt_raw.transpose(0, 2, 1)
    bias_c = dt_bias.reshape(g, 1, rep)
    bias_r = jnp.broadcast_to(dt_bias.reshape(g, rep, 1), (g, rep, L))
    alog_c = a_log.reshape(g, 1, rep)
    alog_r = jnp.broadcast_to(a_log.reshape(g, rep, 1), (g, rep, L))
    dch = jnp.repeat(d_skip, SSD_HEAD_DIM).reshape(1, SSD_D_INNER)
    e512, e1024, et = _ssd_expanders()
    xb = SSD_D_INNER // SSD_STATE
    full = lambda shape: pl.BlockSpec(shape, lambda i, j, s: (0,) * len(shape))
    return pl.pallas_call(
        functools.partial(_ssd_kernel, t_valid=t_valid),
        grid=(b, g, t // L),
        in_specs=[pl.BlockSpec((1, L, SSD_GROUP_DIM), lambda i, j, s: (i, s, j)),
                  pl.BlockSpec((1, L, SSD_STATE), lambda i, j, s: (i, s, xb + j)),
                  pl.BlockSpec((1, L, SSD_STATE), lambda i, j, s: (i, s, xb + g + j)),
                  pl.BlockSpec((1, 1, L, rep), lambda i, j, s: (i, j, s, 0)),
                  pl.BlockSpec((1, rep, L), lambda i, j, s: (i, j, s)),
                  pl.BlockSpec((1, 1, rep), lambda i, j, s: (j, 0, 0)),
                  pl.BlockSpec((1, rep, L), lambda i, j, s: (j, 0, 0)),
                  pl.BlockSpec((1, 1, rep), lambda i, j, s: (j, 0, 0)),
                  pl.BlockSpec((1, rep, L), lambda i, j, s: (j, 0, 0)),
                  pl.BlockSpec((1, SSD_GROUP_DIM), lambda i, j, s: (0, j)),
                  full((rep, SSD_GROUP_DIM)), full((rep, rep * LANE)), full((SSD_GROUP_DIM, rep)),
                  pl.BlockSpec((1, rep, SSD_HEAD_DIM, SSD_STATE), lambda i, j, s: (i, j, 0, 0))],
        out_specs=[pl.BlockSpec((1, L, SSD_GROUP_DIM), lambda i, j, s: (i, s, j)),
                   pl.BlockSpec((1, rep, SSD_HEAD_DIM, SSD_STATE), lambda i, j, s: (i, j, 0, 0))],
        out_shape=[jax.ShapeDtypeStruct((b, t, SSD_D_INNER), F32),
                   jax.ShapeDtypeStruct((b, SSD_HEADS, SSD_HEAD_DIM, SSD_STATE), F32)],
        scratch_shapes=[pltpu.VMEM((SSD_GROUP_DIM, SSD_STATE), F32)],
        compiler_params=_cparams(3),
        name="ssd_scan",
    )(xbc, xbc, xbc, dtc, dtr, bias_c, bias_r, alog_c, alog_r, dch, e512, e1024, et, s0)


def _gated_norm_kernel(y_ref, z_ref, g_ref, o_ref):
    v = y_ref[0] * _silu(z_ref[0])
    o = v * lax.rsqrt(jnp.mean(v * v, axis=-1, keepdims=True) + EPS) * g_ref[...]
    o_ref[0] = o.astype(o_ref.dtype)


def _gated_norm(y, zx, g):
    b, t, d = y.shape
    tt = _pick_tile(t, 256, SUBLANE)
    return pl.pallas_call(
        _gated_norm_kernel,
        grid=(b, t // tt),
        in_specs=[pl.BlockSpec((1, tt, d), lambda i, j: (i, j, 0)),
                  pl.BlockSpec((1, tt, d), lambda i, j: (i, j, 0)),
                  pl.BlockSpec((1, d), lambda i, j: (0, 0))],
        out_specs=pl.BlockSpec((1, tt, d), lambda i, j: (i, j, 0)),
        out_shape=jax.ShapeDtypeStruct((b, t, d), BF16),
        compiler_params=_cparams(2),
        name="gated_norm",
    )(y, zx, g.reshape(1, d))


def _rope_tables(pos):
    half = NSA_HEAD_DIM // 2
    inv = jnp.exp(-math.log(ROPE_THETA) * jnp.arange(half, dtype=F32) * 2.0 / NSA_HEAD_DIM)
    ang = pos.astype(F32)[:, None] * inv[None, :]
    cos, sin = jnp.cos(ang), jnp.sin(ang)
    return jnp.concatenate([cos, cos], axis=1), jnp.concatenate([-sin, sin], axis=1)


def _rope_slab(x, cos2, sin2):
    return x * cos2 + pltpu.roll(x, NSA_HEAD_DIM // 2, 1) * sin2


def _rope_kernel(q_ref, kc_ref, ks_ref, vs_ref, kw_ref, vw_ref, cos_ref, sin_ref,
                 qo_ref, kco_ref, kso_ref, kwo_ref, ksb_ref, vsb_ref, kwb_ref, vwb_ref):
    cos2, sin2 = cos_ref[...], sin_ref[...]
    hd = NSA_HEAD_DIM
    scale = hd ** -0.5
    for h in range(NSA_HEADS):
        sl = slice(h * hd, (h + 1) * hd)
        qo_ref[0, :, sl] = (_rope_slab(q_ref[0, :, sl], cos2, sin2) * scale).astype(qo_ref.dtype)
    for h in range(NSA_KV_HEADS):
        sl = slice(h * hd, (h + 1) * hd)
        kco_ref[0, :, sl] = _rope_slab(kc_ref[0, :, sl], cos2, sin2)
        ks = _rope_slab(ks_ref[0, :, sl], cos2, sin2)
        kso_ref[0, :, sl] = ks
        ksb_ref[0, :, sl] = ks.astype(ksb_ref.dtype)
        kw = _rope_slab(kw_ref[0, :, sl], cos2, sin2)
        kwo_ref[0, :, sl] = kw
        kwb_ref[0, :, sl] = kw.astype(kwb_ref.dtype)
    vsb_ref[0] = vs_ref[0].astype(vsb_ref.dtype)
    vwb_ref[0] = vw_ref[0].astype(vwb_ref.dtype)


def _rope_split(proj, pos, lowp=BF16):
    b, t, _ = proj.shape
    tt = _pick_tile(t, 256, SUBLANE)
    cos2, sin2 = _rope_tables(pos)
    kvd = NSA_KV_DIM
    qb = NSA_Q_DIM // kvd
    kv_in = lambda k: pl.BlockSpec((1, tt, kvd), lambda i, j: (i, j, qb + k))
    kv_out = pl.BlockSpec((1, tt, kvd), lambda i, j: (i, j, 0))
    tab = pl.BlockSpec((tt, NSA_HEAD_DIM), lambda i, j: (j, 0))
    sds = lambda n, dt: jax.ShapeDtypeStruct((b, t, n), dt)
    return pl.pallas_call(
        _rope_kernel,
        grid=(b, t // tt),
        in_specs=[pl.BlockSpec((1, tt, NSA_Q_DIM), lambda i, j: (i, j, 0))]
                 + [kv_in(k) for k in (0, 2, 3, 4, 5)] + [tab, tab],
        out_specs=[pl.BlockSpec((1, tt, NSA_Q_DIM), lambda i, j: (i, j, 0))] + [kv_out] * 7,
        out_shape=[sds(NSA_Q_DIM, lowp), sds(kvd, F32), sds(kvd, F32), sds(kvd, F32),
                   sds(kvd, lowp), sds(kvd, lowp), sds(kvd, lowp), sds(kvd, lowp)],
        compiler_params=_cparams(2),
        name="rope_split",
    )(proj, proj, proj, proj, proj, proj, cos2, sin2)


def _cmp_finish_kernel(ab_ref, pe_ref, w2_ref, o_ref):
    ab = ab_ref[0]
    n_sub = ab.shape[0]
    pe_term = pe_ref[0:1, :CMP_HIDDEN] + pe_ref[1:2, CMP_HIDDEN:]
    nxt = pltpu.roll(ab[:, CMP_HIDDEN:], n_sub - 1, 0)
    hid = ab[:, :CMP_HIDDEN] + nxt + pe_term
    act = 0.5 * hid * (1.0 + jnp.tanh(math.sqrt(2.0 / math.pi) * (hid + 0.044715 * hid * hid * hid)))
    o_ref[0] = jnp.dot(act.astype(BF16), w2_ref[...].astype(BF16),
                       preferred_element_type=F32).astype(o_ref.dtype)


def _compress(rows, w1, w2, pe):
    b, t, _ = rows.shape
    n_sub = t // CMP_STRIDE
    ratio = CMP_LEN // CMP_STRIDE
    kdim = CMP_STRIDE * NSA_HEAD_DIM
    x = rows.reshape(b, n_sub, CMP_STRIDE, NSA_KV_HEADS, NSA_HEAD_DIM).transpose(0, 3, 1, 2, 4)
    x = x.reshape(b * NSA_KV_HEADS * n_sub, kdim)
    w1cat = w1.reshape(ratio, kdim, CMP_HIDDEN).transpose(1, 0, 2).reshape(kdim, ratio * CMP_HIDDEN)
    ab = _mm(x, w1cat).reshape(b * NSA_KV_HEADS, n_sub, ratio * CMP_HIDDEN)
    return _cmp_finish(ab, w1cat, w2, pe)


def _cmp_finish(ab, w1cat, w2, pe):
    ratio = CMP_LEN // CMP_STRIDE
    kdim = CMP_STRIDE * NSA_HEAD_DIM
    n_sub = ab.shape[1]
    pe_rows = jnp.zeros((SUBLANE, kdim), F32).at[:ratio].set(pe.reshape(ratio, kdim))
    pe_ab = _mm(pe_rows, w1cat)
    return pl.pallas_call(
        _cmp_finish_kernel,
        grid=(ab.shape[0],),
        in_specs=[pl.BlockSpec((1, n_sub, ratio * CMP_HIDDEN), lambda i: (i, 0, 0)),
                  pl.BlockSpec((SUBLANE, ratio * CMP_HIDDEN), lambda i: (0, 0)),
                  pl.BlockSpec((CMP_HIDDEN, NSA_HEAD_DIM), lambda i: (0, 0))],
        out_specs=pl.BlockSpec((1, n_sub, NSA_HEAD_DIM), lambda i: (i, 0, 0)),
        out_shape=jax.ShapeDtypeStruct((ab.shape[0], n_sub, NSA_HEAD_DIM), BF16),
        compiler_params=_cparams(1),
        name="cmp_finish",
    )(ab, pe_ab, w2)


def _masked_softmax_rows(s, allowed):
    s = jnp.where(allowed, s, NEG_INF)
    m = jnp.max(s, axis=1, keepdims=True)
    p = jnp.where(allowed, jnp.exp(s - m), 0.0)
    return p / jnp.maximum(jnp.sum(p, axis=1, keepdims=True), TINY)


def _nsa_attn_kernel(q_ref, kcmp_ref, vcmp_ref, ks_ref, vs_ref, kw_ref, vw_ref, gp_ref, covt_ref, eg_ref,
                     o_ref, *, n_sel):
    tq, hd, rep = NSA_TQ, NSA_HEAD_DIM, NSA_REP
    rows = rep * tq
    i = pl.program_id(2)
    q0 = i * tq
    q = q_ref[0]
    qs = jnp.concatenate([q[:, r * hd:(r + 1) * hd] for r in range(rep)], axis=0)

    def qpos(shape):
        return q0 + (lax.broadcasted_iota(jnp.int32, shape, 0) & (tq - 1))

    n_cmp_pad = kcmp_ref.shape[1]
    s_c = _dot_nt(qs, kcmp_ref[0])
    blk_end = lax.broadcasted_iota(jnp.int32, (rows, n_cmp_pad), 1) * CMP_STRIDE + (CMP_LEN - 1)
    p_c = _masked_softmax_rows(s_c, blk_end <= qpos((rows, n_cmp_pad)))
    o_cmp = jnp.dot(p_c.astype(BF16), vcmp_ref[0], preferred_element_type=F32)

    p_sum = p_c[0:tq]
    for r in range(1, rep):
        p_sum = p_sum + p_c[r * tq:(r + 1) * tq]
    imp_t = _dot_nt(covt_ref[...], p_sum.astype(BF16))
    nb = 32
    assert n_sel <= nb
    imp_t = imp_t[0:nb]
    jb = lax.broadcasted_iota(jnp.int32, (nb, tq), 0)
    sel_shift = SEL_LEN.bit_length() - 1
    cur = (q0 + lax.broadcasted_iota(jnp.int32, (nb, tq), 1)) >> sel_shift
    forced = (jb == 0) | (jb == cur) | (jb == cur - 1)
    score = jnp.where(jb <= cur, imp_t + jnp.where(forced, FORCE_BONUS, 0.0), -1.0)
    score = jnp.where(jb < n_sel, score, -2.0)
    cnt = jnp.zeros((nb, tq), F32)
    for jp in range(n_sel):
        rowv = score[jp:jp + 1, :]
        before = (rowv > score) | ((rowv == score) & (jb > jp))
        cnt = cnt + jnp.where(before, 1.0, 0.0)
    sel_t = jnp.where(cnt < float(min(SEL_TOPK, n_sel)), 1.0, 0.0)
    sel_t = jnp.concatenate([sel_t, jnp.zeros((LANE - nb, tq), F32)], axis=0)
    sel = sel_t.T.astype(BF16)

    kc = NSA_SLC_KC

    def slc_body(c, carry):
        m, l, acc = carry
        k0 = pl.multiple_of(c * kc, kc)
        kblk = ks_ref[0, pl.ds(k0, kc), :]
        vblk = vs_ref[0, pl.ds(k0, kc), :]
        s = _dot_nt(qs, kblk)
        kidx = k0 + lax.broadcasted_iota(jnp.int32, (LANE, kc), 1)
        expand = jnp.where(lax.broadcasted_iota(jnp.int32, (LANE, kc), 0) == (kidx >> sel_shift), 1.0, 0.0)
        mk = jnp.dot(sel, expand.astype(BF16), preferred_element_type=F32)
        mk = jnp.concatenate([mk] * rep, axis=0)
        kpos = k0 + lax.broadcasted_iota(jnp.int32, (rows, kc), 1)
        allowed = (mk > 0.5) & (kpos <= qpos((rows, kc)))
        s = jnp.where(allowed, s, NEG_INF)
        m_new = jnp.maximum(m, jnp.max(s, axis=1, keepdims=True))
        alpha = jnp.exp(m - m_new)
        p = jnp.where(allowed, jnp.exp(s - m_new), 0.0)
        l = alpha * l + jnp.sum(p, axis=1, keepdims=True)
        acc = alpha * acc + jnp.dot(p.astype(BF16), vblk, preferred_element_type=F32)
        return m_new, l, acc

    n_chunks = (q0 + tq + kc - 1) // kc
    init = (jnp.full((rows, 1), NEG_INF, F32), jnp.zeros((rows, 1), F32), jnp.zeros((rows, hd), F32))
    _, l_s, acc_s = lax.fori_loop(0, n_chunks, slc_body, init)
    o_slc = acc_s / jnp.maximum(l_s, TINY)

    span = NSA_WIN_SPAN
    w0 = pl.multiple_of(jnp.maximum(q0 + tq - span, 0), tq)
    s_w = _dot_nt(qs, kw_ref[0, pl.ds(w0, span), :])
    dist = qpos((rows, span)) - (w0 + lax.broadcasted_iota(jnp.int32, (rows, span), 1))
    p_w = _masked_softmax_rows(s_w, (dist >= 0) & (dist < WINDOW))
    o_win = jnp.dot(p_w.astype(BF16), vw_ref[0, pl.ds(w0, span), :], preferred_element_type=F32)

    gexp = _spread(_sigmoid(gp_ref[0]), eg_ref[0])
    unstack = lambda o: jnp.concatenate([o[r * tq:(r + 1) * tq] for r in range(rep)], axis=1)
    w = rep * hd
    out = gexp[:, 0:w] * unstack(o_cmp) + gexp[:, w:2 * w] * unstack(o_slc) + gexp[:, 2 * w:3 * w] * unstack(o_win)
    o_ref[0] = out.astype(o_ref.dtype)


def _block_coverage(n_cmp, n_sel):
    start = np.arange(n_cmp)[:, None] * CMP_STRIDE
    sel_start = np.arange(n_sel)[None, :] * SEL_LEN
    inter = np.minimum(start + CMP_LEN, sel_start + SEL_LEN) - np.maximum(start, sel_start)
    return (np.clip(inter, 0, None) / CMP_LEN).astype(np.float32)


def _nsa_attention(q, kcmp, vcmp, ks, vs, kw, vw, gate_proj):
    b, t, _ = q.shape
    n_cmp_pad = t // CMP_STRIDE
    n_cmp = (t - CMP_LEN) // CMP_STRIDE + 1
    n_sel = -(-t // SEL_LEN)
    assert n_cmp_pad == LANE and t % NSA_SLC_KC == 0 and t >= NSA_WIN_SPAN
    cov_t = np.zeros((LANE, n_cmp_pad), np.float32)
    cov_t[:n_sel, :n_cmp] = _block_coverage(n_cmp, n_sel).T
    n_gate = gate_proj.shape[-1]
    eg = np.zeros((NSA_KV_HEADS, n_gate, 3 * NSA_REP * NSA_HEAD_DIM), np.float32)
    for g in range(NSA_KV_HEADS):
        for br in range(3):
            for r in range(NSA_REP):
                c0 = (br * NSA_REP + r) * NSA_HEAD_DIM
                eg[g, br * NSA_HEADS + g * NSA_REP + r, c0:c0 + NSA_HEAD_DIM] = 1.0
    gw = NSA_REP * NSA_HEAD_DIM
    seq = pl.BlockSpec((1, t, NSA_HEAD_DIM), lambda i, g, s: (i, 0, g))
    cmp_spec = pl.BlockSpec((1, n_cmp_pad, NSA_HEAD_DIM), lambda i, g, s: (i * NSA_KV_HEADS + g, 0, 0))
    return pl.pallas_call(
        functools.partial(_nsa_attn_kernel, n_sel=n_sel),
        grid=(b, NSA_KV_HEADS, t // NSA_TQ),
        in_specs=[pl.BlockSpec((1, NSA_TQ, gw), lambda i, g, s: (i, s, g)),
                  cmp_spec, cmp_spec, seq, seq, seq, seq,
                  pl.BlockSpec((1, NSA_TQ, n_gate), lambda i, g, s: (i, s, 0)),
                  pl.BlockSpec((LANE, n_cmp_pad), lambda i, g, s: (0, 0)),
                  pl.BlockSpec((1, n_gate, 3 * gw), lambda i, g, s: (g, 0, 0))],
        out_specs=pl.BlockSpec((1, NSA_TQ, gw), lambda i, g, s: (i, s, g)),
        out_shape=jax.ShapeDtypeStruct((b, t, NSA_Q_DIM), BF16),
        compiler_params=_cparams(3),
        name="nsa_attention",
    )(q, kcmp, vcmp, ks, vs, kw, vw, gate_proj, jnp.asarray(cov_t, BF16), jnp.asarray(eg, BF16))


_POOL_PAD = 2 * SUBLANE


def _pool_kernel(h_ref, buf_ref, w_ref, x_ref, gate_ref, o_ref, he_ref, wb_ref, *, tt, pos0):
    t = pl.program_id(1)

    @pl.when((pl.program_id(0) == 0) & (t == 0))
    def _():
        wb_ref[...] = w_ref[...].astype(BF16)

    @pl.when(t == 0)
    def _():
        he_ref[0:_POOL_PAD, :] = buf_ref[0]

    @pl.when(t > 0)
    def _():
        he_ref[0:_POOL_PAD, :] = he_ref[tt:tt + _POOL_PAD, :]

    he_ref[_POOL_PAD:_POOL_PAD + tt, :] = h_ref[0]
    gd = POOL_GROUP_DIM
    q_pos = pos0 + t * tt + lax.broadcasted_iota(jnp.int32, (tt, gd), 0)
    for gi, w in enumerate(POOL_WINDOWS):
        sl = slice(gi * gd, (gi + 1) * gd)
        cur = he_ref[_POOL_PAD:_POOL_PAD + tt, sl]
        win = cur
        for k in range(1, w):
            win = win + he_ref[_POOL_PAD - k:_POOL_PAD - k + tt, sl]
        count = jnp.minimum(w, q_pos + 1).astype(F32)
        mixed = (win / count - cur).astype(BF16)
        acc = jnp.dot(mixed, wb_ref[gi], preferred_element_type=F32)
        o_ref[0, :, sl] = x_ref[0, :, sl] + gate_ref[0, :, sl] * acc


def _pool_mixer(h, buf, pool_w, x, gate, pos0):
    b, t, d = h.shape
    tt = _pick_tile(t, 256, SUBLANE)
    buf16 = jnp.concatenate([jnp.zeros((b, _POOL_PAD - POOL_BUF, d), F32), buf], axis=1)
    if gate.shape[1] == 1:
        gate_spec = pl.BlockSpec((1, 1, d), lambda i, j: (i, 0, 0))
    else:
        gate_spec = pl.BlockSpec((1, tt, d), lambda i, j: (i, j, 0))
    tile = pl.BlockSpec((1, tt, d), lambda i, j: (i, j, 0))
    return pl.pallas_call(
        functools.partial(_pool_kernel, tt=tt, pos0=pos0),
        grid=(b, t // tt),
        in_specs=[tile, pl.BlockSpec((1, _POOL_PAD, d), lambda i, j: (i, 0, 0)),
                  pl.BlockSpec(pool_w.shape, lambda i, j: (0, 0, 0)), tile, gate_spec],
        out_specs=tile,
        out_shape=jax.ShapeDtypeStruct((b, t, d), F32),
        scratch_shapes=[pltpu.VMEM((tt + _POOL_PAD, d), F32), pltpu.VMEM(pool_w.shape, BF16)],
        compiler_params=_cparams(2),
        name="pool_mixer",
    )(h, buf16, pool_w, x, gate)


_CMP_PAGES_PER_STEP = 8


def _cmp_pages_kernel(tbl_ref, *refs):
    del tbl_ref
    npg = _CMP_PAGES_PER_STEP
    pages = refs[:npg]
    w_ref, o_ref, x_ref, wb_ref = refs[npg:]
    sub_per_page = PAGE_SIZE // CMP_STRIDE

    @pl.when((pl.program_id(0) == 0) & (pl.program_id(1) == 0))
    def _():
        wb_ref[...] = w_ref[...].astype(BF16)

    for k in range(npg):
        for g in range(NSA_KV_HEADS):
            r0 = (g * npg + k) * sub_per_page
            for s in range(CMP_STRIDE):
                x_ref[r0:r0 + sub_per_page, s * NSA_HEAD_DIM:(s + 1) * NSA_HEAD_DIM] = (
                    pages[k][0, pl.ds(s, sub_per_page, stride=CMP_STRIDE), g, :])
    ab = jnp.dot(x_ref[...].astype(BF16), wb_ref[...], preferred_element_type=F32)
    o_ref[0] = ab.reshape(NSA_KV_HEADS, npg * sub_per_page, ab.shape[-1])


def _cmp_pages(cache, layer, page_table, w1):
    b, n_pages = page_table.shape
    npg = _CMP_PAGES_PER_STEP
    assert n_pages % npg == 0
    ratio = CMP_LEN // CMP_STRIDE
    kdim = CMP_STRIDE * NSA_HEAD_DIM
    w1cat = w1.reshape(ratio, kdim, CMP_HIDDEN).transpose(1, 0, 2).reshape(kdim, ratio * CMP_HIDDEN)
    sub_per_page = PAGE_SIZE // CMP_STRIDE
    rows = NSA_KV_HEADS * npg * sub_per_page
    page_spec = lambda k: pl.BlockSpec((None, 1, PAGE_SIZE, NSA_KV_HEADS, NSA_HEAD_DIM),
                                       lambda i, p, tbl: (layer, tbl[i, p * npg + k], 0, 0, 0))
    out = pl.pallas_call(
        _cmp_pages_kernel,
        grid_spec=pltpu.PrefetchScalarGridSpec(
            num_scalar_prefetch=1,
            grid=(b, n_pages // npg),
            in_specs=[page_spec(k) for k in range(npg)]
                     + [pl.BlockSpec(w1cat.shape, lambda i, p, tbl: (0, 0))],
            out_specs=pl.BlockSpec((1, NSA_KV_HEADS, npg * sub_per_page, ratio * CMP_HIDDEN),
                                   lambda i, p, tbl: (i, 0, p, 0)),
            scratch_shapes=[pltpu.VMEM((rows, kdim), F32), pltpu.VMEM(w1cat.shape, BF16)]),
        out_shape=jax.ShapeDtypeStruct((b, NSA_KV_HEADS, n_pages * sub_per_page, ratio * CMP_HIDDEN), F32),
        compiler_params=_cparams(2),
        name="cmp_pages",
    )(page_table, *([cache] * npg), w1cat)
    return out.reshape(b * NSA_KV_HEADS, n_pages * sub_per_page, ratio * CMP_HIDDEN), w1cat


def _joint_softmax(s1, ok1, s2, ok2):
    s1 = jnp.where(ok1, s1, NEG_INF)
    s2 = jnp.where(ok2, s2, NEG_INF)
    m = jnp.maximum(jnp.max(s1, axis=1, keepdims=True), jnp.max(s2, axis=1, keepdims=True))
    p1 = jnp.where(ok1, jnp.exp(s1 - m), 0.0)
    p2 = jnp.where(ok2, jnp.exp(s2 - m), 0.0)
    den = jnp.maximum(jnp.sum(p1, axis=1, keepdims=True) + jnp.sum(p2, axis=1, keepdims=True), TINY)
    return p1 / den, p2 / den


def _dec_cmp_win_kernel(q_ref, kcmp_ref, vcmp_ref, kwin_ref, vwin_ref, kwn_ref, vwn_ref, cov_ref, rsum_ref,
                        ocmp_ref, owin_ref, imp_ref, *, t, n_cmp, pos0, nbuf):
    q = q_ref[0].astype(BF16)
    rows = q.shape[0]
    tok = lax.broadcasted_iota(jnp.int32, (rows, 1), 0) & (t - 1)
    q_pos = pos0 + tok

    n_pad = kcmp_ref.shape[1]
    s_c = _dot_nt(q, kcmp_ref[0])
    nidx = lax.broadcasted_iota(jnp.int32, (rows, n_pad), 1)
    ok = (nidx < n_cmp) & (nidx * CMP_STRIDE + (CMP_LEN - 1) <= q_pos)
    p_c = _masked_softmax_rows(s_c, ok)
    ocmp_ref[0] = jnp.dot(p_c.astype(BF16), vcmp_ref[0], preferred_element_type=F32)
    p_sum = _dot_exact(rsum_ref[...], p_c)
    imp_ref[0] = jnp.dot(p_sum.astype(BF16), cov_ref[...], preferred_element_type=F32)

    s1 = _dot_nt(q, kwin_ref[0].astype(BF16))
    kp1 = (pos0 - nbuf) + lax.broadcasted_iota(jnp.int32, (rows, nbuf), 1)
    d1 = q_pos - kp1
    kn = kwn_ref[0].astype(BF16)
    s2 = _dot_nt(q, kn)
    i2 = lax.broadcasted_iota(jnp.int32, (rows, kn.shape[0]), 1)
    d2 = tok - i2
    p1, p2 = _joint_softmax(s1, (d1 >= 0) & (d1 < WINDOW), s2, (d2 >= 0) & (i2 < t))
    owin_ref[0] = (jnp.dot(p1.astype(BF16), vwin_ref[0].astype(BF16), preferred_element_type=F32)
                   + jnp.dot(p2.astype(BF16), vwn_ref[0].astype(BF16), preferred_element_type=F32))


def _dec_cmp_win(q_stk, kcmp, vcmp, win_k, win_v, layer, kw_new, vw_new, n_cmp, n_sel, pos0, t):
    bg, rows, hd = q_stk.shape
    n_pad = kcmp.shape[1]
    nbuf = win_k.shape[2]
    sel_pad = -(-n_sel // LANE) * LANE
    cov = np.zeros((n_pad, sel_pad), np.float32)
    cov[:n_cmp, :n_sel] = _block_coverage(n_cmp, n_sel)
    rsum = np.zeros((SUBLANE, rows), np.float32)
    for r in range(NSA_REP):
        for tt in range(t):
            rsum[tt, r * t + tt] = 1.0
    per_bg = lambda n: pl.BlockSpec((1, n, hd), lambda i: (i, 0, 0))
    per_b = lambda n: pl.BlockSpec((1, n, hd), lambda i: (i // NSA_KV_HEADS, 0, i % NSA_KV_HEADS))
    win_spec = per_b(nbuf)
    win_k, win_v = [w[layer].reshape(w.shape[1], nbuf, NSA_KV_DIM) for w in (win_k, win_v)]
    return pl.pallas_call(
        functools.partial(_dec_cmp_win_kernel, t=t, n_cmp=n_cmp, pos0=pos0, nbuf=nbuf),
        grid=(bg,),
        in_specs=[per_bg(rows), per_bg(n_pad), per_bg(n_pad), win_spec, win_spec,
                  per_b(SUBLANE), per_b(SUBLANE),
                  pl.BlockSpec(cov.shape, lambda i: (0, 0)), pl.BlockSpec(rsum.shape, lambda i: (0, 0))],
        out_specs=[per_bg(rows), per_bg(rows), pl.BlockSpec((1, SUBLANE, sel_pad), lambda i: (i, 0, 0))],
        out_shape=[jax.ShapeDtypeStruct((bg, rows, hd), F32), jax.ShapeDtypeStruct((bg, rows, hd), F32),
                   jax.ShapeDtypeStruct((bg, SUBLANE, sel_pad), F32)],
        compiler_params=_cparams(1),
        name="dec_cmp_win",
    )(q_stk, kcmp, vcmp, win_k, win_v, kw_new, vw_new, jnp.asarray(cov, BF16), jnp.asarray(rsum))


def _dec_topk_kernel(imp_ref, o_ref, *, t, n_sel, pos0, top):
    imp_t = imp_ref[...].T
    shape = imp_t.shape
    jb = lax.broadcasted_iota(jnp.int32, shape, 0)
    q_pos = pos0 + (lax.broadcasted_iota(jnp.int32, shape, 1) & (t - 1))
    cur = q_pos >> (SEL_LEN.bit_length() - 1)
    forced = (jb == 0) | (jb == cur) | (jb == cur - 1)
    score = jnp.where(jb <= cur, imp_t + jnp.where(forced, FORCE_BONUS, 0.0), -1.0)
    score = jnp.where(jb < n_sel, score, -2.0)
    jbf = jb.astype(F32)
    for k in range(top):
        m = jnp.max(score, axis=0, keepdims=True)
        idx = jnp.min(jnp.where(score == m, jbf, float(shape[0])), axis=0, keepdims=True)
        o_ref[k:k + 1, :] = idx.astype(jnp.int32)
        score = jnp.where(jbf == idx, -3.0, score)


def _dec_topk(imp, n_sel, pos0, t):
    slots, sel_pad = imp.shape
    top = min(SEL_TOPK, n_sel)
    return pl.pallas_call(
        functools.partial(_dec_topk_kernel, t=t, n_sel=n_sel, pos0=pos0, top=top),
        grid=(1,),
        in_specs=[pl.BlockSpec((slots, sel_pad), lambda i: (0, 0))],
        out_specs=pl.BlockSpec((top, slots), lambda i: (0, 0)),
        out_shape=jax.ShapeDtypeStruct((top, slots), jnp.int32),
        compiler_params=_cparams(1),
        name="dec_topk",
    )(imp)


def _dec_slc_kernel(sel_ref, tbl_ref, *refs, t, n_past_blk, pos0, top):
    del tbl_ref
    kblk, vblk = refs[:top], refs[top:2 * top]
    q_ref, kn_ref, vn_ref, o_ref, kcat_ref, vcat_ref = refs[2 * top:]
    slot = pl.program_id(0)
    tok = slot & (t - 1)
    q_pos = pos0 + tok
    head = (slot // t) % NSA_KV_HEADS
    for g in range(NSA_KV_HEADS):
        @pl.when(head == g)
        def _(g=g):
            for k in range(top):
                kcat_ref[k * SEL_LEN:(k + 1) * SEL_LEN, :] = kblk[k][0, :, g, :].astype(BF16)
                vcat_ref[k * SEL_LEN:(k + 1) * SEL_LEN, :] = vblk[k][0, :, g, :].astype(BF16)
    q = q_ref[0].astype(BF16)
    rows = q.shape[0]
    n_keys = top * SEL_LEN
    s1 = _dot_nt(q, kcat_ref[...])
    col = lax.broadcasted_iota(jnp.int32, (rows, n_keys), 1)
    cblk = col >> (SEL_LEN.bit_length() - 1)
    kpos = col & (SEL_LEN - 1)
    in_past = cblk < 0
    new_lim = jnp.int32(-1)
    for k in range(top):
        blk = sel_ref[slot, k]
        kpos = kpos + jnp.where(cblk == k, blk * SEL_LEN, 0)
        in_past = in_past | (cblk == jnp.where(blk < n_past_blk, k, -1))
        new_lim = jnp.where(blk == n_past_blk, tok, new_lim)
    kn = kn_ref[0].astype(BF16)
    s2 = _dot_nt(q, kn)
    i2 = lax.broadcasted_iota(jnp.int32, (rows, kn.shape[0]), 1)
    ok2 = (i2 <= new_lim) & (i2 < t)
    p1, p2 = _joint_softmax(s1, in_past & (kpos <= q_pos), s2, ok2)
    o_ref[0] = (jnp.dot(p1.astype(BF16), vcat_ref[...], preferred_element_type=F32)
                + jnp.dot(p2.astype(BF16), vn_ref[0].astype(BF16), preferred_element_type=F32))


def _dec_slc(sel, page_table, cache_k, cache_v, layer, q_slot, ks_new, vs_new, pos0, t):
    slots, top = sel.shape
    hd = NSA_HEAD_DIM
    n_past_blk = page_table.shape[1] * PAGE_SIZE // SEL_LEN
    halves = PAGE_SIZE // SEL_LEN
    per_b = NSA_KV_HEADS * t

    def blk_spec(k):
        def index(s, sel_ref, tbl_ref):
            blk = jnp.minimum(sel_ref[s, k], n_past_blk - 1)
            page = tbl_ref[s // per_b, blk // halves]
            return layer, page, blk % halves, 0, 0
        return pl.BlockSpec((None, 1, SEL_LEN, NSA_KV_HEADS, hd), index)

    new_spec = pl.BlockSpec((1, SUBLANE, hd), lambda s, a, b_: (s // per_b, 0, (s // t) % NSA_KV_HEADS))
    slot_spec = pl.BlockSpec((1, SUBLANE, hd), lambda s, a, b_: (s, 0, 0))
    return pl.pallas_call(
        functools.partial(_dec_slc_kernel, t=t, n_past_blk=n_past_blk, pos0=pos0, top=top),
        grid_spec=pltpu.PrefetchScalarGridSpec(
            num_scalar_prefetch=2,
            grid=(slots,),
            in_specs=[blk_spec(k) for k in range(top)] + [blk_spec(k) for k in range(top)]
                     + [slot_spec, new_spec, new_spec],
            out_specs=slot_spec,
            scratch_shapes=[pltpu.VMEM((top * SEL_LEN, hd), BF16), pltpu.VMEM((top * SEL_LEN, hd), BF16)]),
        out_shape=jax.ShapeDtypeStruct((slots, SUBLANE, hd), F32),
        compiler_params=_cparams(1),
        name="dec_slc",
    )(sel, page_table, *([cache_k] * top), *([cache_v] * top), q_slot, ks_new, vs_new)


def _dec_combine_kernel(gp_ref, eg_ref, ocmp_ref, oslc_ref, owin_ref, o_ref):
    gexp = _spread(_sigmoid(gp_ref[...]), eg_ref[...])
    w = NSA_Q_DIM
    out = gexp[:, 0:w] * ocmp_ref[...] + gexp[:, w:2 * w] * oslc_ref[...] + gexp[:, 2 * w:3 * w] * owin_ref[...]
    o_ref[...] = out.astype(o_ref.dtype)


def _dec_combine(gate_proj, o_cmp, o_slc, o_win):
    rows, n_gate = gate_proj.shape
    eg = np.zeros((n_gate, 3 * NSA_Q_DIM), np.float32)
    for br in range(3):
        for h in range(NSA_HEADS):
            c0 = br * NSA_Q_DIM + h * NSA_HEAD_DIM
            eg[br * NSA_HEADS + h, c0:c0 + NSA_HEAD_DIM] = 1.0
    full = lambda a: pl.BlockSpec(a.shape, lambda i: (0, 0))
    args = (gate_proj, jnp.asarray(eg, BF16), o_cmp, o_slc, o_win)
    return pl.pallas_call(
        _dec_combine_kernel,
        grid=(1,),
        in_specs=[full(a) for a in args],
        out_specs=pl.BlockSpec((rows, NSA_Q_DIM), lambda i: (0, 0)),
        out_shape=jax.ShapeDtypeStruct((rows, NSA_Q_DIM), BF16),
        compiler_params=_cparams(1),
        name="dec_combine",
    )(*args)


def _nsa_decode(proj, gate_proj, pos0, P, j, caches, page_table):
    b, t, _ = proj.shape
    cache_ck, cache_cv, cache_sk, cache_sv, win_k, win_v = caches
    assert t & (t - 1) == 0 and t <= SUBLANE
    hd, g, rep = NSA_HEAD_DIM, NSA_KV_HEADS, NSA_REP
    pad_t = lambda a: jnp.pad(a, ((0, 0), (0, SUBLANE - t), (0, 0)))
    q, kc, ks, kw, _, _, _, _ = _rope_split(pad_t(proj), pos0 + jnp.arange(SUBLANE), lowp=F32)
    q, kc, ks, kw = q[:, :t], kc[:, :t], ks[:, :t], kw[:, :t]
    vc, vs, vw = [proj[..., NSA_Q_DIM + k * NSA_KV_DIM:NSA_Q_DIM + (k + 1) * NSA_KV_DIM] for k in (1, 3, 5)]

    tk = page_table.shape[1] * PAGE_SIZE + t
    n_cmp = (tk - CMP_LEN) // CMP_STRIDE + 1
    n_sel = -(-tk // SEL_LEN)
    assert (n_cmp + 1) * CMP_STRIDE <= page_table.shape[1] * PAGE_SIZE
    ab_k, w1k = _cmp_pages(cache_ck, j, page_table, P['nsa_cmpk_w1'][j])
    ab_v, w1v = _cmp_pages(cache_cv, j, page_table, P['nsa_cmpv_w1'][j])
    kcmp = _cmp_finish(ab_k, w1k, P['nsa_cmpk_w2'][j], P['nsa_cmpk_pe'][j])
    vcmp = _cmp_finish(ab_v, w1v, P['nsa_cmpv_w2'][j], P['nsa_cmpv_pe'][j])

    q5 = q.reshape(b, t, g, rep, hd)
    q_stk = q5.transpose(0, 2, 3, 1, 4).reshape(b * g, rep * t, hd)
    o_cmp, o_win, imp = _dec_cmp_win(q_stk, kcmp, vcmp, win_k, win_v, j, pad_t(kw), pad_t(vw),
                                     n_cmp, n_sel, pos0, t)
    sel = _dec_topk(imp[:, :t].reshape(b * g * t, -1), n_sel, pos0, t).T
    q_slot = q5.transpose(0, 2, 1, 3, 4).reshape(b * g * t, rep, hd)
    q_slot = jnp.pad(q_slot, ((0, 0), (0, SUBLANE - rep), (0, 0)))
    o_slc = _dec_slc(sel, page_table, cache_sk, cache_sv, j, q_slot, pad_t(ks), pad_t(vs), pos0, t)

    unstk = lambda o: o.reshape(b, g, rep, t, hd).transpose(0, 3, 1, 2, 4).reshape(b * t, NSA_Q_DIM)
    o_slc = o_slc[:, :rep].reshape(b, g, t, rep, hd).transpose(0, 2, 1, 3, 4).reshape(b * t, NSA_Q_DIM)
    o = _dec_combine(gate_proj.reshape(b * t, -1), unstk(o_cmp), o_slc, unstk(o_win))
    heads = lambda a: a.reshape(b, t, g, hd)
    kw_new = jnp.concatenate([win_k[j][:, t:], heads(kw)], axis=1)
    vw_new = jnp.concatenate([win_v[j][:, t:], heads(vw)], axis=1)
    return o.reshape(b, t, NSA_Q_DIM), kc, vc, ks, vs, kw_new, vw_new


def _run_trunk(x, mods, pos0, past, P):
    b, t, d = x.shape
    fresh = past is None
    per_batch = t % SUBLANE == 0 and t >= LANE
    names = ('ssd_state', 'ssd_conv', 'cmp_k', 'cmp_v', 'slc_k', 'slc_v', 'win_k', 'win_v', 'pool', 'ffn')
    new = {n: [] for n in names}

    bm, tm = (b, t) if per_batch else (1, b * t)

    def mod_rows(v):
        return v[:, None, :] if per_batch else jnp.repeat(v, t, axis=0)[None]

    def mm_rows(a, w, **kw):
        return _mm(a.reshape(b * t, a.shape[-1]), w, **kw)

    def pad_cols(w):
        return jnp.pad(w, ((0, 0), (0, LANE - w.shape[1])))

    def mm_res(a, w, layer, xres, gate):
        out = _mm(a.reshape(b * t, a.shape[-1]), w, layer=layer, res=xres.reshape(b * t, d), gate=gate,
                  rows_per_gate=t if per_batch else None)
        return out.reshape(b, t, d)

    for i in range(DEPTH):
        kind, j = i % N_MIXERS, i // N_MIXERS
        sh1, sc1, g1, sh2, sc2, g2 = [mod_rows(v) for v in jnp.split(mods[i], 6, axis=-1)]
        h_dtype = F32 if kind == 2 else BF16
        h = _normmod(x.reshape(bm, tm, d), P['norm1_g'][i], sh1, sc1, h_dtype).reshape(b, t, d)
        if kind == 0:
            if fresh:
                conv_buf = jnp.zeros((b, SSD_CONV - 1, SSD_CONV_DIM), F32)
                s0 = jnp.zeros((b, SSD_HEADS, SSD_HEAD_DIM, SSD_STATE), F32)
            else:
                conv_buf, s0 = past['ssd_conv'][j], past['ssd_state'][j]
            w_in = P['ssd_w_in']
            nzx = SSD_D_INNER + SSD_CONV_DIM
            dt_raw = mm_rows(h, pad_cols(w_in[j, :, nzx:])).reshape(b, t, LANE)[..., :SSD_HEADS]
            if per_batch:
                zx = mm_rows(h, w_in, layer=j, n_cols=SSD_D_INNER).reshape(b, t, SSD_D_INNER)
                xbc, conv_new = _mm_conv_silu(h.reshape(b * t, d), w_in, j, SSD_D_INNER, SSD_CONV_DIM, t,
                                              conv_buf, P['ssd_conv_w'][j], P['ssd_conv_b'][j])
                xbc = xbc.reshape(b, t, SSD_CONV_DIM)
            else:
                zx = mm_rows(h, w_in, layer=j, n_cols=nzx).reshape(b, t, nzx)
                xbc, conv_new = _conv_silu(zx, SSD_D_INNER, SSD_CONV_DIM, conv_buf,
                                           P['ssd_conv_w'][j], P['ssd_conv_b'][j])
            tp = -(-t // SSD_CHUNK) * SSD_CHUNK
            if tp != t:
                xbc_p = jnp.pad(xbc, ((0, 0), (0, tp - t), (0, 0)))
                dt_p = jnp.pad(dt_raw, ((0, 0), (0, tp - t), (0, 0)))
            else:
                xbc_p, dt_p = xbc, dt_raw
            y, s_new = _ssd_scan(xbc_p, dt_p, P['ssd_dt_bias'][j], P['ssd_a_log'][j], P['ssd_d'][j], s0,
                                 t_valid=min(t, SSD_CHUNK))
            yn = _gated_norm(y[:, :t], zx, P['ssd_norm_g'][j])
            x = mm_res(yn, P['ssd_w_out'], j, x, g1)
            new['ssd_conv'].append(conv_new)
            new['ssd_state'].append(s_new)
        elif kind == 1:
            w_in = P['nsa_w_in']
            npj = NSA_Q_DIM + 6 * NSA_KV_DIM
            proj = mm_rows(h, w_in, layer=j, n_cols=npj).reshape(b, t, npj)
            gate_proj = mm_rows(h, pad_cols(w_in[j, :, npj:])).reshape(b, t, LANE)
            if fresh:
                q_pos = pos0 + jnp.arange(t)
                q, kc, ks, kw, ks_b, vs_b, kw_b, vw_b = _rope_split(proj, q_pos)
                vc, vs, vw = [proj[..., NSA_Q_DIM + k * NSA_KV_DIM:NSA_Q_DIM + (k + 1) * NSA_KV_DIM]
                              for k in (1, 3, 5)]
                kcmp = _compress(kc, P['nsa_cmpk_w1'][j], P['nsa_cmpk_w2'][j], P['nsa_cmpk_pe'][j])
                vcmp = _compress(vc, P['nsa_cmpv_w1'][j], P['nsa_cmpv_w2'][j], P['nsa_cmpv_pe'][j])
                o = _nsa_attention(q, kcmp, vcmp, ks_b, vs_b, kw_b, vw_b, gate_proj)
                keep = min(WINDOW, t)
                kw_new, vw_new = kw[:, t - keep:], vw[:, t - keep:]
            else:
                o, kc, vc, ks, vs, kw_new, vw_new = _nsa_decode(
                    proj, gate_proj, pos0, P, j, past['nsa'][j], past['page_table'])
            x = mm_res(o, P['nsa_w_out'], j, x, g1)
            shp = (b, t, NSA_KV_HEADS, NSA_HEAD_DIM)
            for n, v in (('cmp_k', kc), ('cmp_v', vc), ('slc_k', ks), ('slc_v', vs)):
                new[n].append(v.reshape(shp))
            new['win_k'].append(kw_new.reshape(b, -1, NSA_KV_HEADS, NSA_HEAD_DIM))
            new['win_v'].append(vw_new.reshape(b, -1, NSA_KV_HEADS, NSA_HEAD_DIM))
        else:
            buf = jnp.zeros((b, POOL_BUF, d), F32) if fresh else past['pool'][j]
            gate = (g1 * P['pool_scale'][j]).reshape(b, -1, d)
            x = _pool_mixer(h, buf, P['pool_w'][j], x, gate, pos0)
            new['pool'].append(jnp.concatenate([buf, h], axis=1)[:, -POOL_BUF:])
        h2 = _normmod(x.reshape(bm, tm, d), P['norm2_g'][i], sh2, sc2, BF16)
        fbuf = jnp.zeros((b, FFN_CONV - 1, 2 * D_FF), F32) if fresh else past['ffn'][i]
        if per_batch:
            act, fbuf_new = _mm_conv_glu(h2.reshape(b * t, d), P['ffn_w_up'], i, t, fbuf,
                                         P['ffn_conv_w'][i], P['ffn_conv_b'][i])
        else:
            u = mm_rows(h2, P['ffn_w_up'], layer=i).reshape(b, t, 2 * D_FF)
            act, fbuf_new = _conv_glu(u, fbuf, P['ffn_conv_w'][i], P['ffn_conv_b'][i])
        x = mm_res(act, P['ffn_w_down'], i, x, g2)
        new['ffn'].append(fbuf_new)
    zero = jnp.zeros((bm, 1, d), F32)
    y = _normmod(x.reshape(bm, tm, d), P['final_g'], zero, zero, F32).reshape(b, t, d)
    return y, {n: jnp.stack(v) for n, v in new.items()}


def kernel(x_prompt, x_sample, state_ssd, state_ssd_conv, cache_cmp_k, cache_cmp_v, cache_slc_k, cache_slc_v,
           cache_win_k, cache_win_v, state_pool, state_ffn_conv, page_table, c_prompt, c_sample,
           ada_w, ada_b, norm1_g, norm2_g, final_g,
           ssd_w_in, ssd_conv_w, ssd_conv_b, ssd_dt_bias, ssd_a_log, ssd_d, ssd_norm_g, ssd_w_out,
           nsa_w_in, nsa_cmpk_w1, nsa_cmpk_w2, nsa_cmpk_pe, nsa_cmpv_w1, nsa_cmpv_w2, nsa_cmpv_pe, nsa_w_out,
           pool_w, pool_scale, ffn_w_up, ffn_conv_w, ffn_conv_b, ffn_w_down):
    P = dict(norm1_g=norm1_g, norm2_g=norm2_g, final_g=final_g,
             ssd_w_in=ssd_w_in, ssd_conv_w=ssd_conv_w, ssd_conv_b=ssd_conv_b, ssd_dt_bias=ssd_dt_bias,
             ssd_a_log=ssd_a_log, ssd_d=ssd_d, ssd_norm_g=ssd_norm_g, ssd_w_out=ssd_w_out,
             nsa_w_in=nsa_w_in, nsa_cmpk_w1=nsa_cmpk_w1, nsa_cmpk_w2=nsa_cmpk_w2, nsa_cmpk_pe=nsa_cmpk_pe,
             nsa_cmpv_w1=nsa_cmpv_w1, nsa_cmpv_w2=nsa_cmpv_w2, nsa_cmpv_pe=nsa_cmpv_pe, nsa_w_out=nsa_w_out,
             pool_w=pool_w, pool_scale=pool_scale,
             ffn_w_up=ffn_w_up, ffn_conv_w=ffn_conv_w, ffn_conv_b=ffn_conv_b, ffn_w_down=ffn_w_down)
    nbp, nbs = c_prompt.shape[0], c_sample.shape[0]
    c_rows = -(-(nbp + nbs) // (2 * SUBLANE)) * (2 * SUBLANE)
    c_all = jnp.concatenate([c_prompt, c_sample, jnp.zeros((c_rows - nbp - nbs, D_MODEL), F32)], axis=0)
    mods = [_mm(c_all, ada_w, layer=i, a_silu=True, bias=ada_b[i]) for i in range(DEPTH)]
    mods_p = [m[:nbp] for m in mods]
    mods_s = [m[nbp:nbp + nbs] for m in mods]

    y_prompt, sp = _run_trunk(x_prompt, mods_p, 0, None, P)
    past_len = page_table.shape[1] * PAGE_SIZE
    nsa_past = [(cache_cmp_k, cache_cmp_v, cache_slc_k, cache_slc_v, cache_win_k, cache_win_v)
                for _ in range(cache_cmp_k.shape[0])]
    past = dict(ssd_state=state_ssd, ssd_conv=state_ssd_conv, nsa=nsa_past, page_table=page_table,
                pool=state_pool, ffn=state_ffn_conv)
    y_sample, ss = _run_trunk(x_sample, mods_s, past_len, past, P)
    return (y_prompt, y_sample,
            sp['ssd_state'], ss['ssd_state'], sp['ssd_conv'], ss['ssd_conv'],
            sp['cmp_k'], ss['cmp_k'], sp['cmp_v'], ss['cmp_v'],
            sp['slc_k'], ss['slc_k'], sp['slc_v'], ss['slc_v'],
            sp['win_k'], ss['win_k'], sp['win_v'], ss['win_v'],
            sp['pool'], ss['pool'], sp['ffn'], ss['ffn'])
```
